```python
import jax
import jax.numpy as jnp
from jax import lax
import numpy as np

D_MODEL = 4096
BATCH = 2
SEQ = 8192
DEPTH = 2

HEAD_DIM = 128
MIX_WIDTH = D_MODEL // 4
A_HEADS = MIX_WIDTH // HEAD_DIM
A_WIDTH = A_HEADS * HEAD_DIM
IDX_HEADS = 16
IDX_DIM = 64
IDX_Q_WIDTH = IDX_HEADS * IDX_DIM
DSA_TOPK_MAX = 256
Q_BLOCK = 128
DIL_PATTERNS = ((128, 1), (512, 4), (2048, 16))
N_DIL = 3
B_HEADS = MIX_WIDTH // HEAD_DIM
B_WIDTH = B_HEADS * HEAD_DIM
LRU_WIDTH = MIX_WIDTH
LRU_BLOCKS = 16
LRU_BLOCK = LRU_WIDTH // LRU_BLOCKS
CONV_WIDTH = 4
LRU_C = 8.0
N_EXPERTS = 64
MOE_TOP_K = 8
D_EXPERT = D_MODEL // 16
ROUTED_SCALE = 2.5
MOE_TOKEN_BLOCK = 128
N_MOD = 6
RMS_EPS = 1e-6
A_COLS = 3 * A_WIDTH + IDX_Q_WIDTH + IDX_DIM + IDX_HEADS
B_COLS = N_DIL * 3 * B_WIDTH
C_COLS = 2 * LRU_WIDTH
G_COLS = 3 * D_MODEL
OFF_B = A_COLS
OFF_C = OFF_B + B_COLS
OFF_G = OFF_C + C_COLS
IN_COLS = OFF_G + G_COLS

kernel_name = 'hybrid_dsa_dilated_rglru_moe_adaln'


def rms_norm(x, g):
    xf = x.astype(jnp.float32)
    y = xf * lax.rsqrt(jnp.mean(xf * xf, axis=-1, keepdims=True) + RMS_EPS)
    return (y * g.astype(jnp.float32)).astype(x.dtype)


def dsa_attention(q, k, v, iq, ik, iw, top_k):
    bsz, seq, nh, dh = q.shape
    nb = seq // Q_BLOCK
    key_pos = jnp.arange(seq)

    def blocks(a):
        return jnp.moveaxis(a.reshape(bsz, nb, Q_BLOCK, *a.shape[2:]), 1, 0)

    def one_block(args):
        qb, iqb, iwb, start = args
        q_pos = start + jnp.arange(Q_BLOCK)
        rel = jnp.einsum('bqhd,bsd->bqhs', iqb, ik).astype(jnp.float32) * IDX_DIM ** -0.5
        score = jnp.einsum('bqh,bqhs->bqs', iwb.astype(jnp.float32) * IDX_HEADS ** -0.5,
                           jax.nn.relu(rel))
        causal = key_pos[None, :] <= q_pos[:, None]
        score = jnp.where(causal[None], score, -jnp.inf)
        _, idx = lax.top_k(score, top_k)
        valid = idx <= q_pos[None, :, None]
        kg = jax.vmap(lambda kb, ib: kb[ib])(k, idx)
        vg = jax.vmap(lambda vb, ib: vb[ib])(v, idx)
        s = jnp.einsum('bqhd,bqkhd->bhqk', qb, kg).astype(jnp.float32) * dh ** -0.5
        s = jnp.where(valid[:, None], s, -jnp.inf)
        p = jax.nn.softmax(s, axis=-1)
        return jnp.einsum('bhqk,bqkhd->bqhd', p.astype(vg.dtype), vg)

    out = lax.map(one_block, (blocks(q), blocks(iq), blocks(iw), jnp.arange(nb) * Q_BLOCK))
    return jnp.moveaxis(out, 0, 1).reshape(bsz, seq, nh, dh)


def dilated_window_attention(q, k, v, window, dilation):
    bsz, seq, nh, dh = q.shape
    band = window // dilation
    unit = band * dilation
    seq_p = -(-seq // unit) * unit
    n_blk = seq_p // unit

    def to_classes(a):
        a = jnp.pad(a, ((0, 0), (0, seq_p - seq), (0, 0), (0, 0)))
        a = a.reshape(bsz, seq_p // dilation, dilation, nh, dh).transpose(0, 2, 1, 3, 4)
        return a.reshape(bsz, dilation, n_blk, band, nh, dh)

    def with_prev(a):
        prev = jnp.pad(a, ((0, 0), (0, 0), (1, 0), (0, 0), (0, 0), (0, 0)))[:, :, :-1]
        return jnp.concatenate([prev, a], axis=3)

    qc = to_classes(q)
    kc = with_prev(to_classes(k))
    vc = with_prev(to_classes(v))
    s = jnp.einsum('brnqhd,brnkhd->brnhqk', qc, kc).astype(jnp.float32) * dh ** -0.5
    qi = jnp.arange(band)[:, None]
    ki = jnp.arange(2 * band)[None, :]
    steps_back = qi + band - ki
    in_band = (steps_back >= 0) & (steps_back <= band)
    has_prev = (jnp.arange(n_blk) > 0)[:, None, None] | (ki >= band)[None]
    mask = in_band[None] & has_prev
    s = jnp.where(mask[:, None], s, -jnp.inf)
    lse = jax.nn.logsumexp(s, axis=-1)
    p = jnp.exp(s - lse[..., None])
    o = jnp.einsum('brnhqk,brnkhd->brnqhd', p.astype(v.dtype), vc)

    def from_classes(a):
        tail = a.shape[4:]
        a = a.reshape(bsz, dilation, seq_p // dilation, *tail)
        a = jnp.moveaxis(a, 1, 2).reshape(bsz, seq_p, *tail)
        return a[:, :seq]

    return from_classes(o), from_classes(jnp.moveaxis(lse, 3, 4))


def rg_lru_branch(xr, yr, conv_w, conv_b, w_r, b_r, w_i, b_i, lam):
    bsz, seq, width = xr.shape
    xp = jnp.pad(xr, ((0, 0), (CONV_WIDTH - 1, 0), (0, 0)))
    xc = conv_b + sum(xp[:, j:j + seq] * conv_w[j] for j in range(CONV_WIDTH))
    xb = xc.reshape(bsz, seq, LRU_BLOCKS, LRU_BLOCK)
    r = jax.nn.sigmoid(jnp.einsum('bsgi,gij->bsgj', xb, w_r).reshape(bsz, seq, width) + b_r)
    i = jax.nn.sigmoid(jnp.einsum('bsgi,gij->bsgj', xb, w_i).reshape(bsz, seq, width) + b_i)
    log_a = -LRU_C * jax.nn.softplus(-lam.astype(jnp.float32)) * r.astype(jnp.float32)
    a = jnp.exp(log_a)
    b = jnp.sqrt(-jnp.expm1(2.0 * log_a)) * (i * xc).astype(jnp.float32)

    def combine(left, right):
        a1, b1 = left
        a2, b2 = right
        return a1 * a2, a2 * b1 + b2

    _, h = lax.associative_scan(combine, (a, b), axis=1)
    return h.astype(xr.dtype) * yr


def moe_ffn(t, w_router, b_router, w1, w3, w2, ws1, ws3, ws2):
    n_tok, d = t.shape
    scores = jax.nn.sigmoid((t @ w_router).astype(jnp.float32))
    _, idx = lax.top_k(scores + b_router.astype(jnp.float32), MOE_TOP_K)
    wsel = jnp.take_along_axis(scores, idx, axis=-1)
    wsel = wsel / jnp.sum(wsel, axis=-1, keepdims=True) * ROUTED_SCALE
    gates = jnp.einsum('tk,tke->te', wsel,
                       jax.nn.one_hot(idx, N_EXPERTS, dtype=jnp.float32)).astype(t.dtype)
    nb = n_tok // MOE_TOKEN_BLOCK

    def block(args):
        tb, gb = args
        hid = jax.nn.silu(jnp.einsum('td,edf->tef', tb, w1)) * jnp.einsum('td,edf->tef', tb, w3)
        return jnp.einsum('tef,efd->td', hid * gb[..., None], w2)

    routed = lax.map(block, (t.reshape(nb, MOE_TOKEN_BLOCK, d),
                             gates.reshape(nb, MOE_TOKEN_BLOCK, N_EXPERTS))).reshape(n_tok, d)
    shared = (jax.nn.silu(t @ ws1) * (t @ ws3)) @ ws2
    return routed + shared


def setup_inputs(seed: int = 0) -> dict:
    key = jax.random.key(seed)
    ks = iter(jax.random.split(key, 40))
    L, D = DEPTH, D_MODEL

    def nrm(shape, scale):
        return jax.random.normal(next(ks), shape, jnp.float32) * scale

    u = jax.random.uniform(next(ks), (L, LRU_WIDTH), jnp.float32, minval=0.9, maxval=0.999)
    a0 = u ** (1.0 / LRU_C)
    return {
        'x': nrm((BATCH, SEQ, D), 1.0),
        'c': nrm((BATCH, D), 1.0),
        'w_ada': nrm((D, N_MOD * D), 0.5 * D ** -0.5),
        'b_ada': nrm((N_MOD * D,), 0.02),
        'ada_layer': nrm((L, N_MOD, D), 0.1),
        'norm_mix': 1.0 + nrm((L, D), 0.02),
        'norm_ffn': 1.0 + nrm((L, D), 0.02),
        'w_in': nrm((L, D, IN_COLS), D ** -0.5),
        'q_norm_a': 1.0 + nrm((L, HEAD_DIM), 0.02),
        'k_norm_a': 1.0 + nrm((L, HEAD_DIM), 0.02),
        'q_norm_b': 1.0 + nrm((L, N_DIL, HEAD_DIM), 0.02),
        'k_norm_b': 1.0 + nrm((L, N_DIL, HEAD_DIM), 0.02),
        'conv_w': nrm((L, CONV_WIDTH, LRU_WIDTH), CONV_WIDTH ** -0.5),
        'conv_b': nrm((L, LRU_WIDTH), 0.02),
        'w_rgate': nrm((L, LRU_BLOCKS, LRU_BLOCK, LRU_BLOCK), LRU_BLOCK ** -0.5),
        'b_rgate': nrm((L, LRU_WIDTH), 0.02),
        'w_igate': nrm((L, LRU_BLOCKS, LRU_BLOCK, LRU_BLOCK), LRU_BLOCK ** -0.5),
        'b_igate': nrm((L, LRU_WIDTH), 0.02),
        'lru_lambda': jnp.log(a0) - jnp.log1p(-a0),
        'w_branch_a': nrm((L, A_WIDTH, D), A_WIDTH ** -0.5),
        'w_branch_b': nrm((L, B_WIDTH, D), B_WIDTH ** -0.5),
        'w_branch_c': nrm((L, LRU_WIDTH, D), LRU_WIDTH ** -0.5),
        'w_out': nrm((L, D, D), D ** -0.5),
        'w_router': nrm((L, D, N_EXPERTS), D ** -0.5),
        'b_router': nrm((L, N_EXPERTS), 0.01),
        'w1_exp': nrm((L, N_EXPERTS, D, D_EXPERT), D ** -0.5),
        'w3_exp': nrm((L, N_EXPERTS, D, D_EXPERT), D ** -0.5),
        'w2_exp': nrm((L, N_EXPERTS, D_EXPERT, D), D_EXPERT ** -0.5),
        'w1_shared': nrm((L, D, D_EXPERT), D ** -0.5),
        'w3_shared': nrm((L, D, D_EXPERT), D ** -0.5),
        'w2_shared': nrm((L, D_EXPERT, D), D_EXPERT ** -0.5),
    }


def reference(x, c, w_ada, b_ada, ada_layer, norm_mix, norm_ffn, w_in, q_norm_a, k_norm_a,
              q_norm_b, k_norm_b, conv_w, conv_b, w_rgate, b_rgate, w_igate, b_igate, lru_lambda,
              w_branch_a, w_branch_b, w_branch_c, w_out, w_router, b_router, w1_exp, w3_exp,
              w2_exp, w1_shared, w3_shared, w2_shared):
    bsz, seq, d = x.shape
    top_k_keys = min(DSA_TOPK_MAX, seq // 4)
    mod_shared = (jax.nn.silu(c) @ w_ada + b_ada).reshape(bsz, N_MOD, d)
    for l in range(DEPTH):
        mod = mod_shared + ada_layer[l]
        sh1, sc1, g1, sh2, sc2, g2 = [mod[:, j, None, :] for j in range(N_MOD)]
        h = rms_norm(x, norm_mix[l]) * (1 + sc1) + sh1
        wl = w_in[l]

        pa = h @ wl[:, :A_COLS]
        q, k, v, iq, ik, iw = jnp.split(
            pa, [A_WIDTH, 2 * A_WIDTH, 3 * A_WIDTH, 3 * A_WIDTH + IDX_Q_WIDTH,
                 3 * A_WIDTH + IDX_Q_WIDTH + IDX_DIM], axis=-1)
        q = rms_norm(q.reshape(bsz, seq, A_HEADS, HEAD_DIM), q_norm_a[l])
        k = rms_norm(k.reshape(bsz, seq, A_HEADS, HEAD_DIM), k_norm_a[l])
        v = v.reshape(bsz, seq, A_HEADS, HEAD_DIM)
        iq = iq.reshape(bsz, seq, IDX_HEADS, IDX_DIM)
        o_a = dsa_attention(q, k, v, iq, ik, iw, top_k_keys).reshape(bsz, seq, A_WIDTH)

        pb = (h @ wl[:, OFF_B:OFF_C]).reshape(bsz, seq, N_DIL, 3, B_HEADS, HEAD_DIM)
        outs, lses = [], []
        for g, (window, dilation) in enumerate(DIL_PATTERNS):
            qg = rms_norm(pb[:, :, g, 0], q_norm_b[l, g])
            kg = rms_norm(pb[:, :, g, 1], k_norm_b[l, g])
            og, lg = dilated_window_attention(qg, kg, pb[:, :, g, 2], window, dilation)
            outs.append(og)
            lses.append(lg)
        alpha = jax.nn.softmax(jnp.stack(lses), axis=0)
        o_b = jnp.einsum('gbsh,gbshd->bshd', alpha.astype(x.dtype),
                         jnp.stack(outs)).reshape(bsz, seq, B_WIDTH)

        pc = h @ wl[:, OFF_C:OFF_G]
        o_c = rg_lru_branch(pc[..., :LRU_WIDTH], jax.nn.gelu(pc[..., LRU_WIDTH:]),
                            conv_w[l], conv_b[l], w_rgate[l], b_rgate[l], w_igate[l],
                            b_igate[l], lru_lambda[l])

        mixed = (jax.nn.sigmoid(h @ wl[:, OFF_G:OFF_G + d]) * (o_a @ w_branch_a[l])
                 + jax.nn.sigmoid(h @ wl[:, OFF_G + d:OFF_G + 2 * d]) * (o_b @ w_branch_b[l])
                 + jax.nn.sigmoid(h @ wl[:, OFF_G + 2 * d:OFF_G + 3 * d]) * (o_c @ w_branch_c[l]))
        x = x + g1 * (mixed @ w_out[l])

        h2 = rms_norm(x, norm_ffn[l]) * (1 + sc2) + sh2
        y = moe_ffn(h2.reshape(bsz * seq, d), w_router[l], b_router[l], w1_exp[l], w3_exp[l],
                    w2_exp[l], w1_shared[l], w3_shared[l], w2_shared[l])
        x = x + g2 * y.reshape(bsz, seq, d)
    return x
```

```python
import functools

import jax
import jax.numpy as jnp
from jax import lax
from jax.experimental import pallas as pl
from jax.experimental.pallas import tpu as pltpu

HEAD_DIM = 128
A_HEADS = 8
IDX_HEADS = 16
IDX_DIM = 64
DSA_TOPK_MAX = 256
DIL_PATTERNS = ((128, 1), (512, 4), (2048, 16))
N_DIL = 3
B_HEADS = 8
LRU_BLOCKS = 16
CONV_WIDTH = 4
LRU_C = 8.0
N_EXPERTS = 64
MOE_TOP_K = 8
ROUTED_SCALE = 2.5
N_MOD = 6
RMS_EPS = 1e-6

LANES = 128
SUBLANES = 8
VMEM_LIMIT_BYTES = 56 * 1024 * 1024

NEG_BIG = -1e30
INT_MIN = -(2 ** 31)

BF16 = jnp.bfloat16
F32 = jnp.float32


def _params(n_axes):
    return pltpu.CompilerParams(dimension_semantics=("arbitrary",) * n_axes,
                                vmem_limit_bytes=VMEM_LIMIT_BYTES)


def _dot(a, b):
    return jnp.dot(a, b, preferred_element_type=F32)


def _dot_nt(a, b):
    return lax.dot_general(a, b, (((1,), (1,)), ((), ())), preferred_element_type=F32)


def _tiled_call(body, m, n, tm, tn, ins, outs, *, tiles_per_batch=None, name=None):
    assert m % tm == 0 and n % tn == 0, (m, n, tm, tn)
    grid = (m // tm, n // tn)
    in_specs, arrays = [], []
    for arr, kind in ins:
        arrays.append(arr)
        if kind == 'row':
            in_specs.append(pl.BlockSpec((tm, arr.shape[1]), lambda i, j: (i, 0)))
        elif isinstance(kind, tuple) and kind[0] == 'rowoff':
            _, off, width = kind
            in_specs.append(pl.BlockSpec((tm, width), lambda i, j, off=off: (i, off)))
        elif kind == 'col':
            in_specs.append(pl.BlockSpec((arr.shape[0], tn), lambda i, j: (0, j)))
        elif kind == 'col3':
            in_specs.append(pl.BlockSpec((None, arr.shape[1], arr.shape[2]), lambda i, j: (j, 0, 0)))
        elif kind == 'tile':
            in_specs.append(pl.BlockSpec((tm, tn), lambda i, j: (i, j)))
        elif isinstance(kind, tuple) and kind[0] == 'tileoff':
            in_specs.append(pl.BlockSpec((tm, tn), lambda i, j, off=kind[1]: (i, j + off)))
        elif kind == 'vec':
            in_specs.append(pl.BlockSpec((1, tn), lambda i, j: (0, j)))
        elif kind == 'bvec':
            tpb = tiles_per_batch
            in_specs.append(pl.BlockSpec((None, 1, tn), lambda i, j, tpb=tpb: (i // tpb, 0, j)))
        elif kind == 'full':
            nd = arr.ndim
            in_specs.append(pl.BlockSpec(arr.shape, lambda i, j, nd=nd: (0,) * nd))
        else:
            raise ValueError(kind)
    out_shape, out_specs = [], []
    for n_cols, dtype in outs:
        assert (n_cols * tn) % n == 0
        w = n_cols * tn // n
        out_shape.append(jax.ShapeDtypeStruct((m, n_cols), dtype))
        out_specs.append(pl.BlockSpec((tm, w), lambda i, j: (i, j)))
    single = len(outs) == 1
    res = pl.pallas_call(
        body, grid=grid, in_specs=in_specs,
        out_specs=out_specs[0] if single else out_specs,
        out_shape=out_shape[0] if single else out_shape,
        compiler_params=_params(2), name=name)(*arrays)
    return res


def _mm_body(a_ref, w_ref, o_ref):
    o_ref[...] = _dot(a_ref[...], w_ref[...]).astype(o_ref.dtype)


def _mm_sigmoid_body(a_ref, w_ref, o_ref):
    o_ref[...] = jax.nn.sigmoid(_dot(a_ref[...], w_ref[...])).astype(o_ref.dtype)


def _mm_residual_body(a_ref, w_ref, x_ref, g_ref, o_ref):
    o_ref[...] = x_ref[...] + g_ref[...] * _dot(a_ref[...], w_ref[...])


def _matmul(a, w, out_dtype, *, tm, tn, body=_mm_body, name=None):
    return _tiled_call(body, a.shape[0], w.shape[1], tm, tn, [(a, 'row'), (w, 'col')],
                       [(w.shape[1], out_dtype)], name=name)


def _ada_body(c_ref, w_ref, b_ref, o_ref):
    c = c_ref[...]
    a = (c * jax.nn.sigmoid(c)).astype(BF16)
    o_ref[...] = _dot(a, w_ref[...].astype(BF16)) + b_ref[...]


def _ada_mod(c, w_ada, b_ada):
    bsz, d = c.shape
    n = w_ada.shape[1]
    c_pad = jnp.zeros((SUBLANES, d), F32).at[:bsz].set(c)
    out = _tiled_call(_ada_body, SUBLANES, n, SUBLANES, 512,
                      [(c_pad, 'row'), (w_ada, 'col'), (b_ada.reshape(1, n), 'vec')],
                      [(n, F32)], name='ada_mod')
    return out[:bsz]


def _norm_mod_body(x_ref, g_ref, sc_ref, sh_ref, o_ref):
    x = x_ref[...]
    y = x * lax.rsqrt(jnp.mean(x * x, axis=-1, keepdims=True) + RMS_EPS) * g_ref[...]
    o_ref[...] = (y * (1.0 + sc_ref[...]) + sh_ref[...]).astype(o_ref.dtype)


def _norm_mod(x2, g, sc, sh, seq, tm=256):
    t, d = x2.shape
    bsz = t // seq
    return _tiled_call(_norm_mod_body, t, d, tm, d,
                       [(x2, 'tile'), (g.reshape(1, d), 'vec'), (sc.reshape(bsz, 1, d), 'bvec'),
                        (sh.reshape(bsz, 1, d), 'bvec')],
                       [(d, BF16)], tiles_per_batch=seq // tm, name='norm_mod')


def _head_norm(x, g):
    outs = []
    for h in range(x.shape[1] // HEAD_DIM):
        xh = x[:, h * HEAD_DIM:(h + 1) * HEAD_DIM].astype(F32)
        outs.append(xh * lax.rsqrt(jnp.mean(xh * xh, axis=-1, keepdims=True) + RMS_EPS) * g)
    return jnp.concatenate(outs, axis=1)


def _qk_norm_body(q_ref, k_ref, gq_ref, gk_ref, qo_ref, ko_ref):
    qo_ref[...] = _head_norm(q_ref[...], gq_ref[...]).astype(qo_ref.dtype)
    ko_ref[...] = _head_norm(k_ref[...], gk_ref[...]).astype(ko_ref.dtype)


def _qk_norm(pa, gq, gk, width, tm=512):
    t = pa.shape[0]
    return _tiled_call(_qk_norm_body, t, width, tm, width,
                       [(pa, ('tileoff', 0)), (pa, ('tileoff', 1)),
                        (gq.reshape(1, HEAD_DIM), 'full'), (gk.reshape(1, HEAD_DIM), 'full')],
                       [(width, BF16), (width, BF16)], name='qk_norm_a')


def _dsa_body(q_ref, iq_ref, iw_ref, ik_ref, k_ref, v_ref, o_ref,
              keys_ref, thr_ref, m_ref, l_ref, acc_ref, *, tq, tk, top_k, n_heads):
    qb = pl.program_id(1)
    kb = pl.program_id(2)
    kb_last = ((qb + 1) * tq - 1) // tk
    n_chunks = kb_last + 1
    row = lax.broadcasted_iota(jnp.int32, (tq, tk), 0) + qb * tq
    lane = lax.broadcasted_iota(jnp.int32, (tq, tk), 1)

    @pl.when(kb == 0)
    def _scores_and_threshold():
        w = iw_ref[:, IDX_DIM:IDX_DIM + IDX_HEADS] * (IDX_HEADS ** -0.5 * IDX_DIM ** -0.5)

        def score_chunk(c, carry):
            ik2 = ik_ref[pl.ds(pl.multiple_of(c * tk, tk), tk), :]
            acc = jnp.zeros((tq, tk), F32)
            for p in range(IDX_HEADS // 2):
                iq_pair = iq_ref[:, p * LANES:(p + 1) * LANES]
                for half in range(2):
                    rel = _dot_nt(iq_pair, ik2[:, half * LANES:(half + 1) * LANES])
                    h = 2 * p + half
                    acc = acc + w[:, h:h + 1] * jnp.maximum(rel, 0.0)
            bits = pltpu.bitcast(acc, jnp.int32)
            key = jnp.where(bits < 0, bits ^ 0x7FFFFFFF, bits)
            key = jnp.where(lane + c * tk <= row, key, INT_MIN)
            keys_ref[c] = key
            return carry

        lax.fori_loop(0, n_chunks, score_chunk, 0)

        def count_ge(cand):
            def body(c, cnt):
                blk = keys_ref[c]
                for j in range(tk // LANES):
                    cnt = cnt + jnp.where(blk[:, j * LANES:(j + 1) * LANES] >= cand, 1.0, 0.0)
                return cnt
            cnt = lax.fori_loop(0, n_chunks, body, jnp.zeros((tq, LANES), F32))
            return jnp.sum(cnt, axis=1, keepdims=True)

        k_f = float(top_k)
        t0 = jnp.where(count_ge(jnp.zeros((tq, 1), jnp.int32)) >= k_f, 0, INT_MIN).astype(jnp.int32)

        def bit_body(i, t):
            cand = t | lax.shift_left(jnp.int32(1), 30 - i)
            return jnp.where(count_ge(cand) >= k_f, cand, t)

        t = lax.fori_loop(0, 31, bit_body, t0)
        thr_ref[...] = jnp.broadcast_to(t, (tq, LANES))
        m_ref[...] = jnp.full(m_ref.shape, NEG_BIG, F32)
        l_ref[...] = jnp.zeros(l_ref.shape, F32)
        acc_ref[...] = jnp.zeros(acc_ref.shape, F32)

    @pl.when(kb <= kb_last)
    def _attend():
        sel = (keys_ref[kb] >= thr_ref[:, 0:1]) & (lane + kb * tk <= row)
        scale = HEAD_DIM ** -0.5
        for h in range(n_heads):
            sl = slice(h * HEAD_DIM, (h + 1) * HEAD_DIM)
            s = _dot_nt(q_ref[:, sl], k_ref[:, sl]) * scale
            s = jnp.where(sel, s, NEG_BIG)
            m_old = m_ref[h]
            m_new = jnp.maximum(m_old, jnp.max(s, axis=1, keepdims=True))
            p = jnp.exp(s - m_new[:, 0:1])
            alpha = jnp.exp(m_old - m_new)
            l_ref[h] = alpha * l_ref[h] + jnp.sum(p, axis=1, keepdims=True)
            acc_ref[h] = alpha * acc_ref[h] + _dot(p.astype(BF16), v_ref[:, sl])
            m_ref[h] = m_new

    @pl.when(kb == kb_last)
    def _finish():
        for h in range(n_heads):
            sl = slice(h * HEAD_DIM, (h + 1) * HEAD_DIM)
            o_ref[:, sl] = (acc_ref[h] / l_ref[h]).astype(o_ref.dtype)


def _dsa_attention(qn, kn, pa, ik2, small, seq, top_k, *, tq=256, tk=512):
    t, width = qn.shape
    bsz = t // seq
    nq, nk = seq // tq, seq // tk
    n_heads = width // HEAD_DIM
    assert width == IDX_HEADS * IDX_DIM and pa.shape[1] == 4 * width
    v_off, iq_off = 2, 3

    def kmap(b, qb, kb):
        return (b * nk + jnp.minimum(kb, ((qb + 1) * tq - 1) // tk), 0)

    def vmap_(b, qb, kb):
        return (b * nk + jnp.minimum(kb, ((qb + 1) * tq - 1) // tk), v_off)

    body = functools.partial(_dsa_body, tq=tq, tk=tk, top_k=top_k, n_heads=n_heads)
    return pl.pallas_call(
        body, grid=(bsz, nq, nk),
        in_specs=[
            pl.BlockSpec((tq, width), lambda b, qb, kb: (b * nq + qb, 0)),
            pl.BlockSpec((tq, width), lambda b, qb, kb: (b * nq + qb, iq_off)),
            pl.BlockSpec((tq, LANES), lambda b, qb, kb: (b * nq + qb, 0)),
            pl.BlockSpec((seq, 2 * LANES), lambda b, qb, kb: (b, 0)),
            pl.BlockSpec((tk, width), kmap),
            pl.BlockSpec((tk, width), vmap_),
        ],
        out_specs=pl.BlockSpec((tq, width), lambda b, qb, kb: (b * nq + qb, 0)),
        out_shape=jax.ShapeDtypeStruct((t, width), BF16),
        scratch_shapes=[
            pltpu.VMEM((nk, tq, tk), jnp.int32),
            pltpu.VMEM((tq, LANES), jnp.int32),
            pltpu.VMEM((n_heads, tq, LANES), F32),
            pltpu.VMEM((n_heads, tq, LANES), F32),
            pltpu.VMEM((n_heads, tq, HEAD_DIM), F32),
        ],
        compiler_params=_params(3), name='dsa_attention',
    )(qn, pa, small, ik2, kn, pa)


def _dil_body(q_ref, kc_ref, vc_ref, kp_ref, vp_ref, gq_ref, gk_ref, o_ref, lse_ref,
              kcat_ref, vcat_ref, *, band, rows, n_heads):
    n = pl.program_id(1)
    kcat_ref[0:band, :] = _head_norm(kp_ref[...], gk_ref[...]).astype(BF16)
    kcat_ref[band:band + rows, :] = _head_norm(kc_ref[...], gk_ref[...]).astype(BF16)
    vcat_ref[0:band, :] = vp_ref[...]
    vcat_ref[band:band + rows, :] = vc_ref[...]
    qi = lax.broadcasted_iota(jnp.int32, (band, 2 * band), 0)
    ki = lax.broadcasted_iota(jnp.int32, (band, 2 * band), 1)
    back = qi + band - ki
    in_band = (back >= 0) & (back <= band)
    scale = HEAD_DIM ** -0.5
    gq = gq_ref[...]

    def sub_block(j, carry):
        r0 = pl.multiple_of(j * band, band)
        mask = in_band & ((ki >= band) | (n * (rows // band) + j > 0))
        for h in range(n_heads):
            sl = slice(h * HEAD_DIM, (h + 1) * HEAD_DIM)
            qh = q_ref[pl.ds(r0, band), sl].astype(F32)
            qh = (qh * lax.rsqrt(jnp.mean(qh * qh, axis=-1, keepdims=True) + RMS_EPS) * gq).astype(BF16)
            s = _dot_nt(qh, kcat_ref[pl.ds(r0, 2 * band), sl]) * scale
            s = jnp.where(mask, s, NEG_BIG)
            m = jnp.max(s, axis=1, keepdims=True)
            p = jnp.exp(s - m)
            l = jnp.sum(p, axis=1, keepdims=True)
            o = _dot(p.astype(BF16), vcat_ref[pl.ds(r0, 2 * band), sl]) / l
            o_ref[pl.ds(r0, band), sl] = o.astype(o_ref.dtype)
            lse_ref[pl.ds(r0, band), h:h + 1] = m + jnp.log(l)
        return carry

    lax.fori_loop(0, rows // band, sub_block, 0)


def _dilated_group(qkv, gq, gk, n_seq, len_seq, *, band=128, rows=512):
    t, w3 = qkv.shape
    width = w3 // 3
    n_heads = width // HEAD_DIM
    rows = min(rows, len_seq)
    nblk = len_seq // rows
    sub = rows // band

    def cur(c):
        return lambda s, n: (s * nblk + n, c)

    def prev(c):
        return lambda s, n: (s * nblk * sub + jnp.maximum(n * sub - 1, 0), c)

    body = functools.partial(_dil_body, band=band, rows=rows, n_heads=n_heads)
    return pl.pallas_call(
        body, grid=(n_seq, nblk),
        in_specs=[
            pl.BlockSpec((rows, width), cur(0)), pl.BlockSpec((rows, width), cur(1)),
            pl.BlockSpec((rows, width), cur(2)),
            pl.BlockSpec((band, width), prev(1)), pl.BlockSpec((band, width), prev(2)),
            pl.BlockSpec((1, HEAD_DIM), lambda s, n: (0, 0)), pl.BlockSpec((1, HEAD_DIM), lambda s, n: (0, 0)),
        ],
        out_specs=[pl.BlockSpec((rows, width), lambda s, n: (s * nblk + n, 0)),
                   pl.BlockSpec((rows, n_heads), lambda s, n: (s * nblk + n, 0))],
        out_shape=[jax.ShapeDtypeStruct((t, width), BF16), jax.ShapeDtypeStruct((t, n_heads), F32)],
        scratch_shapes=[pltpu.VMEM((band + rows, width), BF16), pltpu.VMEM((band + rows, width), BF16)],
        compiler_params=_params(2), name='dilated_attention',
    )(qkv, qkv, qkv, qkv, qkv, gq.reshape(1, HEAD_DIM), gk.reshape(1, HEAD_DIM))


def _dil_merge_body(o0_ref, o1_ref, o2_ref, l0_ref, l1_ref, l2_ref, out_ref):
    l0, l1, l2 = l0_ref[...], l1_ref[...], l2_ref[...]
    m = jnp.maximum(jnp.maximum(l0, l1), l2)
    e0, e1, e2 = jnp.exp(l0 - m), jnp.exp(l1 - m), jnp.exp(l2 - m)
    den = e0 + e1 + e2
    a0, a1, a2 = e0 / den, e1 / den, e2 / den
    for h in range(l0.shape[1]):
        sl = slice(h * HEAD_DIM, (h + 1) * HEAD_DIM)
        out_ref[:, sl] = (a0[:, h:h + 1] * o0_ref[:, sl].astype(F32)
                          + a1[:, h:h + 1] * o1_ref[:, sl].astype(F32)
                          + a2[:, h:h + 1] * o2_ref[:, sl].astype(F32)).astype(out_ref.dtype)


def _dil_merge(outs, lses, tm=512):
    t, width = outs[0].shape
    nh = lses[0].shape[1]
    ospec = pl.BlockSpec((tm, width), lambda i: (i, 0))
    lspec = pl.BlockSpec((tm, nh), lambda i: (i, 0))
    return pl.pallas_call(
        _dil_merge_body, grid=(t // tm,), in_specs=[ospec] * 3 + [lspec] * 3, out_specs=ospec,
        out_shape=jax.ShapeDtypeStruct((t, width), BF16), compiler_params=_params(1), name='dilated_merge',
    )(*outs, *lses)


def _to_classes(a, bsz, seq, dil):
    if dil == 1:
        return a
    c = a.shape[1]
    return a.reshape(bsz, seq // dil, dil, c).transpose(0, 2, 1, 3).reshape(bsz * seq, c)


def _from_classes(a, bsz, seq, dil):
    if dil == 1:
        return a
    c = a.shape[1]
    return a.reshape(bsz, dil, seq // dil, c).transpose(0, 2, 1, 3).reshape(bsz * seq, c)


def _lru_body(x_ref, y_ref, xp_ref, cw_ref, cb_ref, wr_ref, br_ref, wi_ref, bi_ref, lam_ref, o_ref,
              carry_ref, *, tm, tiles_per_batch):
    i = pl.program_id(0)
    first = (i % tiles_per_batch) == 0
    x = x_ref[...]
    prev = jnp.where(first, 0.0, xp_ref[...])
    xs = jnp.concatenate([prev, x], axis=0)
    xc = cb_ref[...] + cw_ref[CONV_WIDTH - 1:CONV_WIDTH, :] * x
    for j in range(CONV_WIDTH - 1):
        back = CONV_WIDTH - 1 - j
        xc = xc + cw_ref[j:j + 1, :] * xs[SUBLANES - back:SUBLANES - back + tm, :]
    xcb = xc.astype(BF16)
    r = jax.nn.sigmoid(_dot(xcb, wr_ref[...]) + br_ref[...])
    ig = jax.nn.sigmoid(_dot(xcb, wi_ref[...]) + bi_ref[...])
    lam = lam_ref[...]
    softplus_neg = jnp.maximum(-lam, 0.0) + jnp.log1p(jnp.exp(-jnp.abs(lam)))
    log_a = (-LRU_C * softplus_neg) * r
    a = jnp.exp(log_a)
    b = jnp.sqrt(1.0 - a * a) * (ig * xc)
    rows = lax.broadcasted_iota(jnp.int32, a.shape, 0)
    step = 1
    while step < tm:
        a_sh = pltpu.roll(a, step, 0)
        b_sh = pltpu.roll(b, step, 0)
        valid = rows >= step
        b = jnp.where(valid, a * b_sh + b, b)
        a = jnp.where(valid, a * a_sh, a)
        step *= 2
    h0 = jnp.where(first, 0.0, carry_ref[0:1, :])
    h = a * h0 + b
    carry_ref[...] = jnp.broadcast_to(h[tm - 1:tm, :], carry_ref.shape)
    o_ref[...] = (h * jax.nn.gelu(y_ref[...])).astype(o_ref.dtype)


def _block_diag(w):
    g, n, _ = w.shape
    eye = jnp.eye(g, dtype=w.dtype)
    return (eye[:, None, :, None] * w[:, :, None, :]).reshape(g * n, g * n)


def _rg_lru(pc, conv_w, conv_b, w_r, b_r, w_i, b_i, lam, seq, tm=256):
    t, w2 = pc.shape
    w = w2 // 2
    row = lambda v: v.reshape(1, w)
    full2 = lambda shape: pl.BlockSpec(shape, lambda i: (0, 0))
    body = functools.partial(_lru_body, tm=tm, tiles_per_batch=seq // tm)
    return pl.pallas_call(
        body, grid=(t // tm,),
        in_specs=[
            pl.BlockSpec((tm, w), lambda i: (i, 0)),
            pl.BlockSpec((tm, w), lambda i: (i, 1)),
            pl.BlockSpec((SUBLANES, w), lambda i: (jnp.maximum(i * (tm // SUBLANES) - 1, 0), 0)),
            full2((CONV_WIDTH, w)), full2((1, w)), full2((w, w)), full2((1, w)), full2((w, w)),
            full2((1, w)), full2((1, w)),
        ],
        out_specs=pl.BlockSpec((tm, w), lambda i: (i, 0)),
        out_shape=jax.ShapeDtypeStruct((t, w), BF16),
        scratch_shapes=[pltpu.VMEM((SUBLANES, w), F32)],
        compiler_params=_params(1), name='rg_lru',
    )(pc, pc, pc, conv_w, row(conv_b), _block_diag(w_r).astype(BF16), row(b_r),
      _block_diag(w_i).astype(BF16), row(b_i), row(lam))


def _branch_merge_body(oa_ref, ob_ref, oc_ref, wa_ref, wb_ref, wc_ref, ga_ref, gb_ref, gc_ref, o_ref):
    mixed = (ga_ref[...].astype(F32) * _dot(oa_ref[...], wa_ref[...])
             + gb_ref[...].astype(F32) * _dot(ob_ref[...], wb_ref[...])
             + gc_ref[...].astype(F32) * _dot(oc_ref[...], wc_ref[...]))
    o_ref[...] = mixed.astype(o_ref.dtype)


def _router_body(a_ref, w_ref, b_ref, o_ref, *, n_experts, top_k):
    logits = _dot(a_ref[...], w_ref[...])
    lane = lax.broadcasted_iota(jnp.int32, logits.shape, 1).astype(F32)
    scores = jax.nn.sigmoid(logits)
    cur = jnp.where(lane < n_experts, scores + b_ref[...], -jnp.inf)
    picked = jnp.zeros(logits.shape, jnp.bool_)
    for _ in range(top_k):
        m = jnp.max(cur, axis=1, keepdims=True)
        idx = jnp.min(jnp.where(cur == m, lane, float(LANES)), axis=1, keepdims=True)
        hit = lane == idx
        picked = picked | hit
        cur = jnp.where(hit, -jnp.inf, cur)
    wsel = jnp.where(picked, scores, 0.0)
    gates = wsel / jnp.sum(wsel, axis=1, keepdims=True) * ROUTED_SCALE
    o_ref[...] = jnp.where(lane == n_experts, 1.0, gates)


def _moe_dense_body(a_ref, g_ref, w1_ref, w3_ref, w2_ref, o_ref):
    e = pl.program_id(1)
    a = a_ref[...]
    lane = lax.broadcasted_iota(jnp.int32, g_ref.shape, 1)
    gate = jnp.sum(jnp.where(lane == e, g_ref[...], 0.0), axis=1, keepdims=True)
    h1 = _dot(a, w1_ref[...])
    hid = (h1 * jax.nn.sigmoid(h1)) * _dot(a, w3_ref[...]) * gate
    y = _dot(hid.astype(BF16), w2_ref[...])

    @pl.when(e == 0)
    def _():
        o_ref[...] = y

    @pl.when(e > 0)
    def _():
        o_ref[...] += y


def _moe_dense(h2, gates, w1, w3, w2, tm=512):
    t, d = h2.shape
    e1, _, f = w1.shape
    return pl.pallas_call(
        _moe_dense_body, grid=(t // tm, e1),
        in_specs=[
            pl.BlockSpec((tm, d), lambda i, e: (i, 0)),
            pl.BlockSpec((tm, LANES), lambda i, e: (i, 0)),
            pl.BlockSpec((None, d, f), lambda i, e: (e, 0, 0)),
            pl.BlockSpec((None, d, f), lambda i, e: (e, 0, 0)),
            pl.BlockSpec((None, f, d), lambda i, e: (e, 0, 0)),
        ],
        out_specs=pl.BlockSpec((tm, d), lambda i, e: (i, 0)),
        out_shape=jax.ShapeDtypeStruct((t, d), F32),
        compiler_params=_params(2), name='moe_dense',
    )(h2, gates, w1, w3, w2)


def _residual_body(x_ref, y_ref, g_ref, o_ref):
    o_ref[...] = x_ref[...] + g_ref[...] * y_ref[...]


def kernel(x, c, w_ada, b_ada, ada_layer, norm_mix, norm_ffn, w_in, q_norm_a, k_norm_a, q_norm_b,
           k_norm_b, conv_w, conv_b, w_rgate, b_rgate, w_igate, b_igate, lru_lambda, w_branch_a,
           w_branch_b, w_branch_c, w_out, w_router, b_router, w1_exp, w3_exp, w2_exp, w1_shared,
           w3_shared, w2_shared):
    bsz, seq, d = x.shape
    t = bsz * seq
    depth = w_in.shape[0]
    mixw = d // 4
    a_cols = 3 * mixw + IDX_HEADS * IDX_DIM + IDX_DIM + IDX_HEADS
    off_b = a_cols
    off_c = off_b + N_DIL * 3 * mixw
    off_g = off_c + 2 * mixw
    top_k = min(DSA_TOPK_MAX, seq // 4)
    tm = 1024 if t % 1024 == 0 else 256
    tpb = seq // tm

    mod_shared = _ada_mod(c, w_ada, b_ada).reshape(bsz, N_MOD, d)
    x2 = x.reshape(t, d)
    for l in range(depth):
        mod = mod_shared + ada_layer[l]
        sh1, sc1, g1, sh2, sc2, g2 = [mod[:, j] for j in range(N_MOD)]
        h = _norm_mod(x2, norm_mix[l], sc1, sh1, seq)
        wl = w_in[l]

        n_main = 3 * mixw + IDX_HEADS * IDX_DIM
        pa = _matmul(h, wl[:, :n_main].astype(BF16), BF16, tm=tm, tn=512, name='proj_a')
        w_small = jnp.zeros((d, LANES), BF16).at[:, :IDX_DIM + IDX_HEADS].set(
            wl[:, n_main:a_cols].astype(BF16))
        small = _matmul(h, w_small, F32, tm=tm, tn=LANES, name='proj_a_idx')
        ik = small[:, :IDX_DIM].astype(BF16)
        zeros = jnp.zeros_like(ik)
        ik2 = jnp.concatenate([ik, zeros, zeros, ik], axis=1)
        qn, kn = _qk_norm(pa, q_norm_a[l], k_norm_a[l], mixw)
        o_a = _dsa_attention(qn, kn, pa, ik2, small, seq, top_k)

        pb = _matmul(h, wl[:, off_b:off_c].astype(BF16), BF16, tm=tm, tn=512, name='proj_b')
        outs, lses = [], []
        for g, (window, dil) in enumerate(DIL_PATTERNS):
            qkv = _to_classes(pb[:, g * 3 * mixw:(g + 1) * 3 * mixw], bsz, seq, dil)
            og, lg = _dilated_group(qkv, q_norm_b[l, g], k_norm_b[l, g], bsz * dil, seq // dil,
                                    band=window // dil)
            outs.append(_from_classes(og, bsz, seq, dil))
            lses.append(_from_classes(lg, bsz, seq, dil))
        o_b = _dil_merge(outs, lses)

        pc = _matmul(h, wl[:, off_c:off_g].astype(BF16), F32, tm=tm, tn=512, name='proj_c')
        o_c = _rg_lru(pc, conv_w[l], conv_b[l], w_rgate[l], b_rgate[l], w_igate[l], b_igate[l],
                      lru_lambda[l], seq)

        sg = _matmul(h, wl[:, off_g:].astype(BF16), BF16, tm=tm, tn=512, body=_mm_sigmoid_body,
                     name='proj_gates')
        nd = d // 512
        mixed = _tiled_call(
            _branch_merge_body, t, d, tm, 512,
            [(o_a, 'row'), (o_b, 'row'), (o_c, 'row'),
             (w_branch_a[l].astype(BF16), 'col'), (w_branch_b[l].astype(BF16), 'col'),
             (w_branch_c[l].astype(BF16), 'col'),
             (sg, ('tileoff', 0)), (sg, ('tileoff', nd)), (sg, ('tileoff', 2 * nd))],
            [(d, BF16)], name='branch_merge')
        x2 = _tiled_call(
            _mm_residual_body, t, d, tm, 512,
            [(mixed, 'row'), (w_out[l].astype(BF16), 'col'), (x2, 'tile'), (g1.reshape(bsz, 1, d), 'bvec')],
            [(d, F32)], tiles_per_batch=tpb, name='out_proj')

        h2 = _norm_mod(x2, norm_ffn[l], sc2, sh2, seq)
        w_r = jnp.zeros((d, LANES), BF16).at[:, :N_EXPERTS].set(w_router[l].astype(BF16))
        b_r = jnp.zeros((1, LANES), F32).at[0, :N_EXPERTS].set(b_router[l].astype(F32))
        gates = _tiled_call(
            functools.partial(_router_body, n_experts=N_EXPERTS, top_k=MOE_TOP_K), t, LANES, tm, LANES,
            [(h2, 'row'), (w_r, 'col'), (b_r, 'vec')], [(LANES, F32)], name='router')
        w1 = jnp.concatenate([w1_exp[l], w1_shared[l][None]], axis=0).astype(BF16)
        w3 = jnp.concatenate([w3_exp[l], w3_shared[l][None]], axis=0).astype(BF16)
        w2 = jnp.concatenate([w2_exp[l], w2_shared[l][None]], axis=0).astype(BF16)
        y = _moe_dense(h2, gates, w1, w3, w2)
        x2 = _tiled_call(
            _residual_body, t, d, 256, d,
            [(x2, 'tile'), (y, 'tile'), (g2.reshape(bsz, 1, d), 'bvec')],
            [(d, F32)], tiles_per_batch=seq // 256, name='ffn_residual')
    return x2.reshape(bsz, seq, d)
```

```python
import functools

import jax
import jax.numpy as jnp
from jax import lax
from jax.experimental import pallas as pl
from jax.experimental.pallas import tpu as pltpu

HEAD_DIM = 128
A_HEADS = 8
IDX_HEADS = 16
IDX_DIM = 64
DSA_TOPK_MAX = 256
DIL_PATTERNS = ((128, 1), (512, 4), (2048, 16))
N_DIL = 3
B_HEADS = 8
LRU_BLOCKS = 16
CONV_WIDTH = 4
LRU_C = 8.0
N_EXPERTS = 64
MOE_TOP_K = 8
ROUTED_SCALE = 2.5
N_MOD = 6
RMS_EPS = 1e-6

LANES = 128
SUBLANES = 8
VMEM_LIMIT_BYTES = 56 * 1024 * 1024

NEG_BIG = -1e30
INT_MIN = -(2 ** 31)

BF16 = jnp.bfloat16
F32 = jnp.float32


def _params(n_axes):
    return pltpu.CompilerParams(dimension_semantics=("arbitrary",) * n_axes,
                                vmem_limit_bytes=VMEM_LIMIT_BYTES)


def _dot(a, b):
    return jnp.dot(a, b, preferred_element_type=F32)


def _dot_nt(a, b):
    return lax.dot_general(a, b, (((1,), (1,)), ((), ())), preferred_element_type=F32)


def _tiled_call(body, m, n, tm, tn, ins, outs, *, tiles_per_batch=None, name=None):
    assert m % tm == 0 and n % tn == 0, (m, n, tm, tn)
    grid = (m // tm, n // tn)
    in_specs, arrays = [], []
    for arr, kind in ins:
        arrays.append(arr)
        if kind == 'row':
            in_specs.append(pl.BlockSpec((tm, arr.shape[1]), lambda i, j: (i, 0)))
        elif isinstance(kind, tuple) and kind[0] == 'rowoff':
            _, off, width = kind
            in_specs.append(pl.BlockSpec((tm, width), lambda i, j, off=off: (i, off)))
        elif kind == 'col':
            in_specs.append(pl.BlockSpec((arr.shape[0], tn), lambda i, j: (0, j)))
        elif kind == 'col3':
            in_specs.append(pl.BlockSpec((None, arr.shape[1], arr.shape[2]), lambda i, j: (j, 0, 0)))
        elif kind == 'tile':
            in_specs.append(pl.BlockSpec((tm, tn), lambda i, j: (i, j)))
        elif isinstance(kind, tuple) and kind[0] == 'tileoff':
            in_specs.append(pl.BlockSpec((tm, tn), lambda i, j, off=kind[1]: (i, j + off)))
        elif kind == 'vec':
            in_specs.append(pl.BlockSpec((1, tn), lambda i, j: (0, j)))
        elif kind == 'bvec':
            tpb = tiles_per_batch
            in_specs.append(pl.BlockSpec((None, 1, tn), lambda i, j, tpb=tpb: (i // tpb, 0, j)))
        elif kind == 'full':
            nd = arr.ndim
            in_specs.append(pl.BlockSpec(arr.shape, lambda i, j, nd=nd: (0,) * nd))
        else:
            raise ValueError(kind)
    out_shape, out_specs = [], []
    for n_cols, dtype in outs:
        assert (n_cols * tn) % n == 0
        w = n_cols * tn // n
        out_shape.append(jax.ShapeDtypeStruct((m, n_cols), dtype))
        out_specs.append(pl.BlockSpec((tm, w), lambda i, j: (i, j)))
    single = len(outs) == 1
    res = pl.pallas_call(
        body, grid=grid, in_specs=in_specs,
        out_specs=out_specs[0] if single else out_specs,
        out_shape=out_shape[0] if single else out_shape,
        compiler_params=_params(2), name=name)(*arrays)
    return res


def _mm_body(a_ref, w_ref, o_ref):
    o_ref[...] = _dot(a_ref[...], w_ref[...]).astype(o_ref.dtype)


def _mm_sigmoid_body(a_ref, w_ref, o_ref):
    o_ref[...] = jax.nn.sigmoid(_dot(a_ref[...], w_ref[...])).astype(o_ref.dtype)


def _mm_residual_body(a_ref, w_ref, x_ref, g_ref, o_ref):
    o_ref[...] = x_ref[...] + g_ref[...] * _dot(a_ref[...], w_ref[...])


def _matmul(a, w, out_dtype, *, tm, tn, body=_mm_body, name=None):
    return _tiled_call(body, a.shape[0], w.shape[1], tm, tn, [(a, 'row'), (w, 'col')],
                       [(w.shape[1], out_dtype)], name=name)


def _ada_body(c_ref, w_ref, b_ref, o_ref):
    c = c_ref[...]
    a = (c * jax.nn.sigmoid(c)).astype(BF16)
    o_ref[...] = _dot(a, w_ref[...].astype(BF16)) + b_ref[...]


def _ada_mod(c, w_ada, b_ada):
    bsz, d = c.shape
    n = w_ada.shape[1]
    c_pad = jnp.zeros((SUBLANES, d), F32).at[:bsz].set(c)
    out = _tiled_call(_ada_body, SUBLANES, n, SUBLANES, 512,
                      [(c_pad, 'row'), (w_ada, 'col'), (b_ada.reshape(1, n), 'vec')],
                      [(n, F32)], name='ada_mod')
    return out[:bsz]


def _norm_mod_body(x_ref, g_ref, sc_ref, sh_ref, o_ref):
    x = x_ref[...]
    y = x * lax.rsqrt(jnp.mean(x * x, axis=-1, keepdims=True) + RMS_EPS) * g_ref[...]
    o_ref[...] = (y * (1.0 + sc_ref[...]) + sh_ref[...]).astype(o_ref.dtype)


def _norm_mod(x2, g, sc, sh, seq, tm=256, out_dtype=BF16):
    t, d = x2.shape
    bsz = t // seq
    return _tiled_call(_norm_mod_body, t, d, tm, d,
                       [(x2, 'tile'), (g.reshape(1, d), 'vec'), (sc.reshape(bsz, 1, d), 'bvec'),
                        (sh.reshape(bsz, 1, d), 'bvec')],
                       [(d, out_dtype)], tiles_per_batch=seq // tm, name='norm_mod')


def _head_norm(x, g):
    outs = []
    for h in range(x.shape[1] // HEAD_DIM):
        xh = x[:, h * HEAD_DIM:(h + 1) * HEAD_DIM].astype(F32)
        outs.append(xh * lax.rsqrt(jnp.mean(xh * xh, axis=-1, keepdims=True) + RMS_EPS) * g)
    return jnp.concatenate(outs, axis=1)


def _qk_norm_body(q_ref, k_ref, gq_ref, gk_ref, qo_ref, ko_ref):
    qo_ref[...] = _head_norm(q_ref[...], gq_ref[...]).astype(qo_ref.dtype)
    ko_ref[...] = _head_norm(k_ref[...], gk_ref[...]).astype(ko_ref.dtype)


def _qk_norm(pa, gq, gk, width, tm=512):
    t = pa.shape[0]
    return _tiled_call(_qk_norm_body, t, width, tm, width,
                       [(pa, ('tileoff', 0)), (pa, ('tileoff', 1)),
                        (gq.reshape(1, HEAD_DIM), 'full'), (gk.reshape(1, HEAD_DIM), 'full')],
                       [(width, BF16), (width, BF16)], name='qk_norm_a')


def _dsa_body(q_ref, iq_ref, iw_ref, ik_ref, k_ref, v_ref, o_ref,
              keys_ref, thr_ref, m_ref, l_ref, acc_ref, *, tq, tk, top_k, n_heads):
    qb = pl.program_id(1)
    kb = pl.program_id(2)
    kb_last = ((qb + 1) * tq - 1) // tk
    n_chunks = kb_last + 1
    row = lax.broadcasted_iota(jnp.int32, (tq, tk), 0) + qb * tq
    lane = lax.broadcasted_iota(jnp.int32, (tq, tk), 1)

    @pl.when(kb == 0)
    def _scores_and_threshold():
        w = iw_ref[:, IDX_DIM:IDX_DIM + IDX_HEADS] * (IDX_HEADS ** -0.5 * IDX_DIM ** -0.5)

        def score_chunk(c, carry):
            ik2 = ik_ref[pl.ds(pl.multiple_of(c * tk, tk), tk), :]
            acc = jnp.zeros((tq, tk), F32)
            for p in range(IDX_HEADS // 2):
                iq_pair = iq_ref[:, p * LANES:(p + 1) * LANES]
                for half in range(2):
                    rel = _dot_nt(iq_pair, ik2[:, half * LANES:(half + 1) * LANES])
                    h = 2 * p + half
                    acc = acc + w[:, h:h + 1] * jnp.maximum(rel, 0.0)
            bits = pltpu.bitcast(acc, jnp.int32)
            key = jnp.where(bits < 0, bits ^ 0x7FFFFFFF, bits)
            key = jnp.where(lane + c * tk <= row, key, INT_MIN)
            keys_ref[c] = key
            return carry

        lax.fori_loop(0, n_chunks, score_chunk, 0)

        def count_ge(cand):
            def body(c, cnt):
                blk = keys_ref[c]
                for j in range(tk // LANES):
                    cnt = cnt + jnp.where(blk[:, j * LANES:(j + 1) * LANES] >= cand, 1.0, 0.0)
                return cnt
            cnt = lax.fori_loop(0, n_chunks, body, jnp.zeros((tq, LANES), F32))
            return jnp.sum(cnt, axis=1, keepdims=True)

        k_f = float(top_k)
        t0 = jnp.where(count_ge(jnp.zeros((tq, 1), jnp.int32)) >= k_f, 0, INT_MIN).astype(jnp.int32)

        def bit_body(i, t):
            cand = t | lax.shift_left(jnp.int32(1), 30 - i)
            return jnp.where(count_ge(cand) >= k_f, cand, t)

        t = lax.fori_loop(0, 31, bit_body, t0)
        thr_ref[...] = jnp.broadcast_to(t, (tq, LANES))
        m_ref[...] = jnp.full(m_ref.shape, NEG_BIG, F32)
        l_ref[...] = jnp.zeros(l_ref.shape, F32)
        acc_ref[...] = jnp.zeros(acc_ref.shape, F32)

    @pl.when(kb <= kb_last)
    def _attend():
        sel = (keys_ref[kb] >= thr_ref[:, 0:1]) & (lane + kb * tk <= row)
        scale = HEAD_DIM ** -0.5
        for h in range(n_heads):
            sl = slice(h * HEAD_DIM, (h + 1) * HEAD_DIM)
            s = _dot_nt(q_ref[:, sl], k_ref[:, sl]) * scale
            s = jnp.where(sel, s, NEG_BIG)
            m_old = m_ref[h]
            m_new = jnp.maximum(m_old, jnp.max(s, axis=1, keepdims=True))
            p = jnp.exp(s - m_new[:, 0:1])
            alpha = jnp.exp(m_old - m_new)
            l_ref[h] = alpha * l_ref[h] + jnp.sum(p, axis=1, keepdims=True)
            acc_ref[h] = alpha * acc_ref[h] + _dot(p.astype(BF16), v_ref[:, sl])
            m_ref[h] = m_new

    @pl.when(kb == kb_last)
    def _finish():
        for h in range(n_heads):
            sl = slice(h * HEAD_DIM, (h + 1) * HEAD_DIM)
            o_ref[:, sl] = (acc_ref[h] / l_ref[h]).astype(o_ref.dtype)


def _dsa_attention(qn, kn, pa, ik2, small, seq, top_k, *, tq=256, tk=512):
    t, width = qn.shape
    bsz = t // seq
    nq, nk = seq // tq, seq // tk
    n_heads = width // HEAD_DIM
    assert width == IDX_HEADS * IDX_DIM and pa.shape[1] == 4 * width
    v_off, iq_off = 2, 3

    def kmap(b, qb, kb):
        return (b * nk + jnp.minimum(kb, ((qb + 1) * tq - 1) // tk), 0)

    def vmap_(b, qb, kb):
        return (b * nk + jnp.minimum(kb, ((qb + 1) * tq - 1) // tk), v_off)

    body = functools.partial(_dsa_body, tq=tq, tk=tk, top_k=top_k, n_heads=n_heads)
    return pl.pallas_call(
        body, grid=(bsz, nq, nk),
        in_specs=[
            pl.BlockSpec((tq, width), lambda b, qb, kb: (b * nq + qb, 0)),
            pl.BlockSpec((tq, width), lambda b, qb, kb: (b * nq + qb, iq_off)),
            pl.BlockSpec((tq, LANES), lambda b, qb, kb: (b * nq + qb, 0)),
            pl.BlockSpec((seq, 2 * LANES), lambda b, qb, kb: (b, 0)),
            pl.BlockSpec((tk, width), kmap),
            pl.BlockSpec((tk, width), vmap_),
        ],
        out_specs=pl.BlockSpec((tq, width), lambda b, qb, kb: (b * nq + qb, 0)),
        out_shape=jax.ShapeDtypeStruct((t, width), BF16),
        scratch_shapes=[
            pltpu.VMEM((nk, tq, tk), jnp.int32),
            pltpu.VMEM((tq, LANES), jnp.int32),
            pltpu.VMEM((n_heads, tq, LANES), F32),
            pltpu.VMEM((n_heads, tq, LANES), F32),
            pltpu.VMEM((n_heads, tq, HEAD_DIM), F32),
        ],
        compiler_params=_params(3), name='dsa_attention',
    )(qn, pa, small, ik2, kn, pa)


def _dil_body(q_ref, kc_ref, vc_ref, kp_ref, vp_ref, gq_ref, gk_ref, o_ref, lse_ref,
              kcat_ref, vcat_ref, *, band, rows, n_heads):
    n = pl.program_id(1)
    kcat_ref[0:band, :] = _head_norm(kp_ref[...], gk_ref[...]).astype(BF16)
    kcat_ref[band:band + rows, :] = _head_norm(kc_ref[...], gk_ref[...]).astype(BF16)
    vcat_ref[0:band, :] = vp_ref[...]
    vcat_ref[band:band + rows, :] = vc_ref[...]
    qi = lax.broadcasted_iota(jnp.int32, (band, 2 * band), 0)
    ki = lax.broadcasted_iota(jnp.int32, (band, 2 * band), 1)
    back = qi + band - ki
    in_band = (back >= 0) & (back <= band)
    scale = HEAD_DIM ** -0.5
    gq = gq_ref[...]

    def sub_block(j, carry):
        r0 = pl.multiple_of(j * band, band)
        mask = in_band & ((ki >= band) | (n * (rows // band) + j > 0))
        for h in range(n_heads):
            sl = slice(h * HEAD_DIM, (h + 1) * HEAD_DIM)
            qh = q_ref[pl.ds(r0, band), sl].astype(F32)
            qh = (qh * lax.rsqrt(jnp.mean(qh * qh, axis=-1, keepdims=True) + RMS_EPS) * gq).astype(BF16)
            s = _dot_nt(qh, kcat_ref[pl.ds(r0, 2 * band), sl]) * scale
            s = jnp.where(mask, s, NEG_BIG)
            m = jnp.max(s, axis=1, keepdims=True)
            p = jnp.exp(s - m)
            l = jnp.sum(p, axis=1, keepdims=True)
            o = _dot(p.astype(BF16), vcat_ref[pl.ds(r0, 2 * band), sl]) / l
            o_ref[pl.ds(r0, band), sl] = o.astype(o_ref.dtype)
            lse_ref[pl.ds(r0, band), h:h + 1] = m + jnp.log(l)
        return carry

    lax.fori_loop(0, rows // band, sub_block, 0)


def _dilated_group(qkv, gq, gk, n_seq, len_seq, *, band=128, rows=512):
    t, w3 = qkv.shape
    width = w3 // 3
    n_heads = width // HEAD_DIM
    rows = min(rows, len_seq)
    nblk = len_seq // rows
    sub = rows // band

    def cur(c):
        return lambda s, n: (s * nblk + n, c)

    def prev(c):
        return lambda s, n: (s * nblk * sub + jnp.maximum(n * sub - 1, 0), c)

    body = functools.partial(_dil_body, band=band, rows=rows, n_heads=n_heads)
    return pl.pallas_call(
        body, grid=(n_seq, nblk),
        in_specs=[
            pl.BlockSpec((rows, width), cur(0)), pl.BlockSpec((rows, width), cur(1)),
            pl.BlockSpec((rows, width), cur(2)),
            pl.BlockSpec((band, width), prev(1)), pl.BlockSpec((band, width), prev(2)),
            pl.BlockSpec((1, HEAD_DIM), lambda s, n: (0, 0)), pl.BlockSpec((1, HEAD_DIM), lambda s, n: (0, 0)),
        ],
        out_specs=[pl.BlockSpec((rows, width), lambda s, n: (s * nblk + n, 0)),
                   pl.BlockSpec((rows, n_heads), lambda s, n: (s * nblk + n, 0))],
        out_shape=[jax.ShapeDtypeStruct((t, width), BF16), jax.ShapeDtypeStruct((t, n_heads), F32)],
        scratch_shapes=[pltpu.VMEM((band + rows, width), BF16), pltpu.VMEM((band + rows, width), BF16)],
        compiler_params=_params(2), name='dilated_attention',
    )(qkv, qkv, qkv, qkv, qkv, gq.reshape(1, HEAD_DIM), gk.reshape(1, HEAD_DIM))


def _dil_merge_body(o0_ref, o1_ref, o2_ref, l0_ref, l1_ref, l2_ref, out_ref):
    l0, l1, l2 = l0_ref[...], l1_ref[...], l2_ref[...]
    m = jnp.maximum(jnp.maximum(l0, l1), l2)
    e0, e1, e2 = jnp.exp(l0 - m), jnp.exp(l1 - m), jnp.exp(l2 - m)
    den = e0 + e1 + e2
    a0, a1, a2 = e0 / den, e1 / den, e2 / den
    for h in range(l0.shape[1]):
        sl = slice(h * HEAD_DIM, (h + 1) * HEAD_DIM)
        out_ref[:, sl] = (a0[:, h:h + 1] * o0_ref[:, sl].astype(F32)
                          + a1[:, h:h + 1] * o1_ref[:, sl].astype(F32)
                          + a2[:, h:h + 1] * o2_ref[:, sl].astype(F32)).astype(out_ref.dtype)


def _dil_merge(outs, lses, tm=512):
    t, width = outs[0].shape
    nh = lses[0].shape[1]
    ospec = pl.BlockSpec((tm, width), lambda i: (i, 0))
    lspec = pl.BlockSpec((tm, nh), lambda i: (i, 0))
    return pl.pallas_call(
        _dil_merge_body, grid=(t // tm,), in_specs=[ospec] * 3 + [lspec] * 3, out_specs=ospec,
        out_shape=jax.ShapeDtypeStruct((t, width), BF16), compiler_params=_params(1), name='dilated_merge',
    )(*outs, *lses)


def _to_classes(a, bsz, seq, dil):
    if dil == 1:
        return a
    c = a.shape[1]
    return a.reshape(bsz, seq // dil, dil, c).transpose(0, 2, 1, 3).reshape(bsz * seq, c)


def _from_classes(a, bsz, seq, dil):
    if dil == 1:
        return a
    c = a.shape[1]
    return a.reshape(bsz, dil, seq // dil, c).transpose(0, 2, 1, 3).reshape(bsz * seq, c)


def _lru_body(x_ref, y_ref, xp_ref, cw_ref, cb_ref, wr_ref, br_ref, wi_ref, bi_ref, lam_ref, o_ref,
              carry_ref, *, tm, tiles_per_batch):
    i = pl.program_id(0)
    first = (i % tiles_per_batch) == 0
    x = x_ref[...]
    prev = jnp.where(first, 0.0, xp_ref[...])
    xs = jnp.concatenate([prev, x], axis=0)
    xc = cb_ref[...] + cw_ref[CONV_WIDTH - 1:CONV_WIDTH, :] * x
    for j in range(CONV_WIDTH - 1):
        back = CONV_WIDTH - 1 - j
        xc = xc + cw_ref[j:j + 1, :] * xs[SUBLANES - back:SUBLANES - back + tm, :]
    xcb = xc.astype(BF16)
    r = jax.nn.sigmoid(_dot(xcb, wr_ref[...]) + br_ref[...])
    ig = jax.nn.sigmoid(_dot(xcb, wi_ref[...]) + bi_ref[...])
    lam = lam_ref[...]
    softplus_neg = jnp.maximum(-lam, 0.0) + jnp.log1p(jnp.exp(-jnp.abs(lam)))
    log_a = (-LRU_C * softplus_neg) * r
    a = jnp.exp(log_a)
    b = jnp.sqrt(1.0 - a * a) * (ig * xc)
    rows = lax.broadcasted_iota(jnp.int32, a.shape, 0)
    step = 1
    while step < tm:
        a_sh = pltpu.roll(a, step, 0)
        b_sh = pltpu.roll(b, step, 0)
        valid = rows >= step
        b = jnp.where(valid, a * b_sh + b, b)
        a = jnp.where(valid, a * a_sh, a)
        step *= 2
    h0 = jnp.where(first, 0.0, carry_ref[0:1, :])
    h = a * h0 + b
    carry_ref[...] = jnp.broadcast_to(h[tm - 1:tm, :], carry_ref.shape)
    o_ref[...] = (h * jax.nn.gelu(y_ref[...])).astype(o_ref.dtype)


def _block_diag(w):
    g, n, _ = w.shape
    eye = jnp.eye(g, dtype=w.dtype)
    return (eye[:, None, :, None] * w[:, :, None, :]).reshape(g * n, g * n)


def _rg_lru(pc, conv_w, conv_b, w_r, b_r, w_i, b_i, lam, seq, tm=256):
    t, w2 = pc.shape
    w = w2 // 2
    row = lambda v: v.reshape(1, w)
    full2 = lambda shape: pl.BlockSpec(shape, lambda i: (0, 0))
    body = functools.partial(_lru_body, tm=tm, tiles_per_batch=seq // tm)
    return pl.pallas_call(
        body, grid=(t // tm,),
        in_specs=[
            pl.BlockSpec((tm, w), lambda i: (i, 0)),
            pl.BlockSpec((tm, w), lambda i: (i, 1)),
            pl.BlockSpec((SUBLANES, w), lambda i: (jnp.maximum(i * (tm // SUBLANES) - 1, 0), 0)),
            full2((CONV_WIDTH, w)), full2((1, w)), full2((w, w)), full2((1, w)), full2((w, w)),
            full2((1, w)), full2((1, w)),
        ],
        out_specs=pl.BlockSpec((tm, w), lambda i: (i, 0)),
        out_shape=jax.ShapeDtypeStruct((t, w), BF16),
        scratch_shapes=[pltpu.VMEM((SUBLANES, w), F32)],
        compiler_params=_params(1), name='rg_lru',
    )(pc, pc, pc, conv_w, row(conv_b), _block_diag(w_r).astype(BF16), row(b_r),
      _block_diag(w_i).astype(BF16), row(b_i), row(lam))


def _branch_merge_body(oa_ref, ob_ref, oc_ref, wa_ref, wb_ref, wc_ref, ga_ref, gb_ref, gc_ref, o_ref):
    mixed = (ga_ref[...].astype(F32) * _dot(oa_ref[...], wa_ref[...])
             + gb_ref[...].astype(F32) * _dot(ob_ref[...], wb_ref[...])
             + gc_ref[...].astype(F32) * _dot(oc_ref[...], wc_ref[...]))
    o_ref[...] = mixed.astype(o_ref.dtype)


MOE_ROWS = 256


def _router_body(a_ref, w_ref, b_ref, gsel_ref, eidx_ref, rank_ref, cnt_ref, carry_ref, *, n_experts, top_k, tm):
    i = pl.program_id(0)

    @pl.when(i == 0)
    def _():
        carry_ref[...] = jnp.zeros(carry_ref.shape, F32)

    logits = _dot(a_ref[...].astype(BF16), w_ref[...])
    lane = lax.broadcasted_iota(jnp.int32, logits.shape, 1).astype(F32)
    scores = jax.nn.sigmoid(logits)
    cur = jnp.where(lane < n_experts, scores + b_ref[...], -jnp.inf)
    mask = jnp.zeros(logits.shape, F32)
    hits, picked_scores = [], []
    for r in range(top_k):
        m = jnp.max(cur, axis=1, keepdims=True)
        idx = jnp.min(jnp.where(cur == m, lane, float(LANES)), axis=1, keepdims=True)
        hit = lane == idx
        hits.append(hit)
        picked_scores.append(jnp.sum(jnp.where(hit, scores, 0.0), axis=1, keepdims=True))
        eidx_ref[:, r:r + 1] = idx.astype(jnp.int32)
        cur = jnp.where(hit, -jnp.inf, cur)
        mask = mask + jnp.where(hit, 1.0, 0.0)
    total = picked_scores[0]
    for r in range(1, top_k):
        total = total + picked_scores[r]
    ri = lax.broadcasted_iota(jnp.int32, (tm, tm), 0)
    ci = lax.broadcasted_iota(jnp.int32, (tm, tm), 1)
    tri = jnp.where(ri > ci, 1.0, 0.0).astype(BF16)
    before = _dot(tri, mask.astype(BF16)) + carry_ref[0:1, :]
    for r in range(top_k):
        gsel_ref[:, r:r + 1] = picked_scores[r] / total * ROUTED_SCALE
        rank_ref[:, r:r + 1] = jnp.sum(jnp.where(hits[r], before, 0.0), axis=1, keepdims=True).astype(jnp.int32)
    carry_ref[...] = carry_ref[...] + jnp.sum(mask, axis=0, keepdims=True)
    cnt_ref[...] = carry_ref[...]


def _router(h2, w_router, b_router, tm=1024):
    t, d = h2.shape
    n_exp = w_router.shape[1]
    w_r = jnp.zeros((d, LANES), BF16).at[:, :n_exp].set(w_router.astype(BF16))
    b_r = jnp.zeros((1, LANES), F32).at[0, :n_exp].set(b_router.astype(F32))
    col = lambda dt: jax.ShapeDtypeStruct((t, MOE_TOP_K), dt)
    cspec = pl.BlockSpec((tm, MOE_TOP_K), lambda i: (i, 0))
    return pl.pallas_call(
        functools.partial(_router_body, n_experts=n_exp, top_k=MOE_TOP_K, tm=tm), grid=(t // tm,),
        in_specs=[pl.BlockSpec((tm, d), lambda i: (i, 0)), pl.BlockSpec((d, LANES), lambda i: (0, 0)),
                  pl.BlockSpec((1, LANES), lambda i: (0, 0))],
        out_specs=[cspec, cspec, cspec, pl.BlockSpec((SUBLANES, LANES), lambda i: (0, 0))],
        out_shape=[col(F32), col(jnp.int32), col(jnp.int32), jax.ShapeDtypeStruct((SUBLANES, LANES), F32)],
        scratch_shapes=[pltpu.VMEM((SUBLANES, LANES), F32)],
        compiler_params=_params(1), name='router')(h2, w_r, b_r)


def _moe_group_body(te_ref, nu_ref, idx_hbm, h_hbm, w1_ref, w3_ref, w2_ref, y_hbm,
                    idx_smem, xbuf, ybuf, isem, gsem, ssem, *, tmr, dump_row):
    i = pl.program_id(0)
    n = nu_ref[0]

    def idx_copy(tile, s3):
        return pltpu.make_async_copy(idx_hbm.at[tile], idx_smem.at[s3], isem.at[s3])

    def row_in(tok, r, s):
        return pltpu.make_async_copy(h_hbm.at[pl.ds(tok, 1)], xbuf.at[s, pl.ds(r, 1)], gsem.at[s])

    def row_out(dst, r):
        return pltpu.make_async_copy(ybuf.at[pl.ds(r, 1)], y_hbm.at[pl.ds(dst, 1)], ssem.at[0])

    def gather_start(s3, s):
        def body(r, c):
            row_in(idx_smem[s3, 0, r], r, s).start()
            return c
        lax.fori_loop(0, tmr, body, 0, unroll=8)

    def gather_wait(s):
        def body(r, c):
            row_in(0, r, s).wait()
            return c
        lax.fori_loop(0, tmr, body, 0, unroll=8)

    def scatter_start(s3):
        def body(r, c):
            row_out(idx_smem[s3, 0, tmr + r], r).start()
            return c
        lax.fori_loop(0, tmr, body, 0, unroll=8)

    def scatter_wait():
        def body(r, c):
            row_out(dump_row, r).wait()
            return c
        lax.fori_loop(0, tmr, body, 0, unroll=8)

    @pl.when(i == 0)
    def _prologue():
        ybuf[...] = jnp.zeros(ybuf.shape, ybuf.dtype)
        init = pltpu.make_async_copy(ybuf, y_hbm.at[pl.ds(dump_row, tmr)], ssem.at[0])
        init.start()
        init.wait()
        idx_copy(0, 0).start()
        idx_copy(0, 0).wait()

        @pl.when(1 < n)
        def _():
            idx_copy(1, 1).start()
        gather_start(0, 0)

    @pl.when(i < n)
    def _tile():
        @pl.when(i + 1 < n)
        def _():
            idx_copy(i + 1, (i + 1) % 3).wait()
            gather_start((i + 1) % 3, (i + 1) % 2)

        @pl.when(i + 2 < n)
        def _():
            idx_copy(i + 2, (i + 2) % 3).start()

        gather_wait(i % 2)
        x = xbuf[i % 2].astype(BF16)
        h1 = _dot(x, w1_ref[...].astype(BF16))
        hid = (h1 * jax.nn.sigmoid(h1)) * _dot(x, w3_ref[...].astype(BF16))
        y = _dot(hid.astype(BF16), w2_ref[...].astype(BF16))

        @pl.when(i > 0)
        def _():
            scatter_wait()
        ybuf[...] = y
        scatter_start(i % 3)

    @pl.when(i == n)
    def _drain():
        scatter_wait()


def _moe_grouped(h2, idx, tile_expert, n_used, w1, w3, w2, n_slots):
    t, d = h2.shape
    n_tiles = idx.shape[0]
    tmr = MOE_ROWS
    f = w1.shape[2]
    dump_row = n_slots * t
    wmap = lambda i, te, nu: (te[i], 0, 0)
    grid_spec = pltpu.PrefetchScalarGridSpec(
        num_scalar_prefetch=2, grid=(n_tiles + 1,),
        in_specs=[
            pl.BlockSpec(memory_space=pl.ANY), pl.BlockSpec(memory_space=pl.ANY),
            pl.BlockSpec((None, d, f), wmap), pl.BlockSpec((None, d, f), wmap), pl.BlockSpec((None, f, d), wmap),
        ],
        out_specs=pl.BlockSpec(memory_space=pl.ANY),
        scratch_shapes=[
            pltpu.SMEM((3, 1, 2 * tmr), jnp.int32),
            pltpu.VMEM((2, tmr, d), F32), pltpu.VMEM((tmr, d), F32),
            pltpu.SemaphoreType.DMA((3,)), pltpu.SemaphoreType.DMA((2,)), pltpu.SemaphoreType.DMA((1,)),
        ])
    return pl.pallas_call(
        functools.partial(_moe_group_body, tmr=tmr, dump_row=dump_row), grid_spec=grid_spec,
        out_shape=jax.ShapeDtypeStruct((n_slots * t + tmr, d), F32),
        compiler_params=_params(1), name='moe_grouped')(tile_expert, n_used, idx, h2, w1, w3, w2)


def _moe_plan(eidx, rank, counts, n_experts):
    t, k = eidx.shape
    tmr = MOE_ROWS
    n_tiles = (t * k) // tmr + n_experts
    padded = ((counts + tmr - 1) // tmr) * tmr
    ends = jnp.cumsum(padded)
    offsets = ends - padded
    n_used = (ends[-1] // tmr).astype(jnp.int32)
    pos = (offsets[eidx] + rank).reshape(-1)
    tok = jnp.broadcast_to(jnp.arange(t, dtype=jnp.int32)[:, None], (t, k)).reshape(-1)
    dst = (jnp.arange(k, dtype=jnp.int32)[None, :] * t + jnp.arange(t, dtype=jnp.int32)[:, None]).reshape(-1)
    p = n_tiles * tmr
    tok_sorted = jnp.zeros((p,), jnp.int32).at[pos].set(tok)
    dst_sorted = (k * t + jnp.arange(p, dtype=jnp.int32) % tmr).at[pos].set(dst)
    idx = jnp.concatenate([tok_sorted.reshape(n_tiles, 1, tmr), dst_sorted.reshape(n_tiles, 1, tmr)], axis=2)
    tile_start = jnp.minimum(jnp.arange(n_tiles + 1, dtype=jnp.int32), n_used - 1) * tmr
    tile_expert = jnp.minimum(jnp.searchsorted(ends, tile_start, side='right'), n_experts - 1).astype(jnp.int32)
    return idx, tile_expert, n_used.reshape(1)


def _shared_expert_body(a_ref, w1_ref, w3_ref, w2_ref, o_ref):
    a = a_ref[...].astype(BF16)
    h1 = _dot(a, w1_ref[...])
    hid = (h1 * jax.nn.sigmoid(h1)) * _dot(a, w3_ref[...])
    o_ref[...] = _dot(hid.astype(BF16), w2_ref[...])


def _shared_expert(h2, w1, w3, w2, tm=512):
    t, d = h2.shape
    f = w1.shape[1]
    full = lambda shape: pl.BlockSpec(shape, lambda i: (0, 0))
    return pl.pallas_call(
        _shared_expert_body, grid=(t // tm,),
        in_specs=[pl.BlockSpec((tm, d), lambda i: (i, 0)), full((d, f)), full((d, f)), full((f, d))],
        out_specs=pl.BlockSpec((tm, d), lambda i: (i, 0)),
        out_shape=jax.ShapeDtypeStruct((t, d), F32), compiler_params=_params(1), name='shared_expert',
    )(h2, w1.astype(BF16), w3.astype(BF16), w2.astype(BF16))


def _moe_combine_body(x_ref, g_ref, gsel_ref, ysh_ref, *refs):
    y_refs, o_ref = refs[:-1], refs[-1]
    gs = gsel_ref[...]
    acc = ysh_ref[...]
    for r, y_ref in enumerate(y_refs):
        acc = acc + gs[:, r:r + 1] * y_ref[...]
    o_ref[...] = x_ref[...] + g_ref[...] * acc


def _moe_combine(x2, g2, gsel, y_shared, y, seq, tm=256, tn=1024):
    t, d = x2.shape
    bsz = t // seq
    k = gsel.shape[1]
    tile = pl.BlockSpec((tm, tn), lambda i, j: (i, j))
    slot_specs = [pl.BlockSpec((tm, tn), lambda i, j, r=r: (r * (t // tm) + i, j)) for r in range(k)]
    return pl.pallas_call(
        _moe_combine_body, grid=(t // tm, d // tn),
        in_specs=[tile, pl.BlockSpec((None, 1, tn), lambda i, j: (i // (seq // tm), 0, j)),
                  pl.BlockSpec((tm, k), lambda i, j: (i, 0)), tile] + slot_specs,
        out_specs=tile, out_shape=jax.ShapeDtypeStruct((t, d), F32),
        compiler_params=_params(2), name='moe_combine',
    )(x2, g2.reshape(bsz, 1, d), gsel, y_shared, *([y] * k))


def kernel(x, c, w_ada, b_ada, ada_layer, norm_mix, norm_ffn, w_in, q_norm_a, k_norm_a, q_norm_b,
           k_norm_b, conv_w, conv_b, w_rgate, b_rgate, w_igate, b_igate, lru_lambda, w_branch_a,
           w_branch_b, w_branch_c, w_out, w_router, b_router, w1_exp, w3_exp, w2_exp, w1_shared,
           w3_shared, w2_shared):
    bsz, seq, d = x.shape
    t = bsz * seq
    depth = w_in.shape[0]
    mixw = d // 4
    a_cols = 3 * mixw + IDX_HEADS * IDX_DIM + IDX_DIM + IDX_HEADS
    off_b = a_cols
    off_c = off_b + N_DIL * 3 * mixw
    off_g = off_c + 2 * mixw
    top_k = min(DSA_TOPK_MAX, seq // 4)
    tm = 1024 if t % 1024 == 0 else 256
    tpb = seq // tm

    mod_shared = _ada_mod(c, w_ada, b_ada).reshape(bsz, N_MOD, d)
    x2 = x.reshape(t, d)
    for l in range(depth):
        mod = mod_shared + ada_layer[l]
        sh1, sc1, g1, sh2, sc2, g2 = [mod[:, j] for j in range(N_MOD)]
        h = _norm_mod(x2, norm_mix[l], sc1, sh1, seq)
        wl = w_in[l]

        n_main = 3 * mixw + IDX_HEADS * IDX_DIM
        pa = _matmul(h, wl[:, :n_main].astype(BF16), BF16, tm=tm, tn=512, name='proj_a')
        w_small = jnp.zeros((d, LANES), BF16).at[:, :IDX_DIM + IDX_HEADS].set(
            wl[:, n_main:a_cols].astype(BF16))
        small = _matmul(h, w_small, F32, tm=tm, tn=LANES, name='proj_a_idx')
        ik = small[:, :IDX_DIM].astype(BF16)
        zeros = jnp.zeros_like(ik)
        ik2 = jnp.concatenate([ik, zeros, zeros, ik], axis=1)
        qn, kn = _qk_norm(pa, q_norm_a[l], k_norm_a[l], mixw)
        o_a = _dsa_attention(qn, kn, pa, ik2, small, seq, top_k)

        pb = _matmul(h, wl[:, off_b:off_c].astype(BF16), BF16, tm=tm, tn=512, name='proj_b')
        outs, lses = [], []
        for g, (window, dil) in enumerate(DIL_PATTERNS):
            qkv = _to_classes(pb[:, g * 3 * mixw:(g + 1) * 3 * mixw], bsz, seq, dil)
            og, lg = _dilated_group(qkv, q_norm_b[l, g], k_norm_b[l, g], bsz * dil, seq // dil,
                                    band=window // dil)
            outs.append(_from_classes(og, bsz, seq, dil))
            lses.append(_from_classes(lg, bsz, seq, dil))
        o_b = _dil_merge(outs, lses)

        pc = _matmul(h, wl[:, off_c:off_g].astype(BF16), F32, tm=tm, tn=512, name='proj_c')
        o_c = _rg_lru(pc, conv_w[l], conv_b[l], w_rgate[l], b_rgate[l], w_igate[l], b_igate[l],
                      lru_lambda[l], seq)

        sg = _matmul(h, wl[:, off_g:].astype(BF16), BF16, tm=tm, tn=512, body=_mm_sigmoid_body,
                     name='proj_gates')
        nd = d // 512
        mixed = _tiled_call(
            _branch_merge_body, t, d, tm, 512,
            [(o_a, 'row'), (o_b, 'row'), (o_c, 'row'),
             (w_branch_a[l].astype(BF16), 'col'), (w_branch_b[l].astype(BF16), 'col'),
             (w_branch_c[l].astype(BF16), 'col'),
             (sg, ('tileoff', 0)), (sg, ('tileoff', nd)), (sg, ('tileoff', 2 * nd))],
            [(d, BF16)], name='branch_merge')
        x2 = _tiled_call(
            _mm_residual_body, t, d, tm, 512,
            [(mixed, 'row'), (w_out[l].astype(BF16), 'col'), (x2, 'tile'), (g1.reshape(bsz, 1, d), 'bvec')],
            [(d, F32)], tiles_per_batch=tpb, name='out_proj')

        h2 = _norm_mod(x2, norm_ffn[l], sc2, sh2, seq, out_dtype=F32)
        gsel, eidx, rank, cnt = _router(h2, w_router[l], b_router[l])
        idx, tile_expert, n_used = _moe_plan(eidx, rank, cnt[0, :N_EXPERTS].astype(jnp.int32), N_EXPERTS)
        y = _moe_grouped(h2, idx, tile_expert, n_used, w1_exp[l], w3_exp[l], w2_exp[l], MOE_TOP_K)
        y_shared = _shared_expert(h2, w1_shared[l], w3_shared[l], w2_shared[l])
        x2 = _moe_combine(x2, g2, gsel, y_shared, y, seq)
    return x2.reshape(bsz, seq, d)
```

```python
import functools

import jax
import jax.numpy as jnp
from jax import lax
from jax.experimental import pallas as pl
from jax.experimental.pallas import tpu as pltpu

HEAD_DIM = 128
A_HEADS = 8
IDX_HEADS = 16
IDX_DIM = 64
DSA_TOPK_MAX = 256
DIL_PATTERNS = ((128, 1), (512, 4), (2048, 16))
N_DIL = 3
B_HEADS = 8
LRU_BLOCKS = 16
CONV_WIDTH = 4
LRU_C = 8.0
N_EXPERTS = 64
MOE_TOP_K = 8
ROUTED_SCALE = 2.5
N_MOD = 6
RMS_EPS = 1e-6

LANES = 128
SUBLANES = 8
VMEM_LIMIT_BYTES = 56 * 1024 * 1024

NEG_BIG = -1e30
INT_MIN = -(2 ** 31)

BF16 = jnp.bfloat16
F32 = jnp.float32


def _params(n_axes):
    return pltpu.CompilerParams(dimension_semantics=("arbitrary",) * n_axes,
                                vmem_limit_bytes=VMEM_LIMIT_BYTES)


def _dot(a, b):
    return jnp.dot(a, b, preferred_element_type=F32)


def _dot_nt(a, b):
    return lax.dot_general(a, b, (((1,), (1,)), ((), ())), preferred_element_type=F32)


def _tiled_call(body, m, n, tm, tn, ins, outs, *, tiles_per_batch=None, name=None):
    assert m % tm == 0 and n % tn == 0, (m, n, tm, tn)
    grid = (m // tm, n // tn)
    in_specs, arrays = [], []
    for arr, kind in ins:
        arrays.append(arr)
        if kind == 'row':
            in_specs.append(pl.BlockSpec((tm, arr.shape[1]), lambda i, j: (i, 0)))
        elif isinstance(kind, tuple) and kind[0] == 'rowoff':
            _, off, width = kind
            in_specs.append(pl.BlockSpec((tm, width), lambda i, j, off=off: (i, off)))
        elif kind == 'col':
            in_specs.append(pl.BlockSpec((arr.shape[0], tn), lambda i, j: (0, j)))
        elif kind == 'col3':
            in_specs.append(pl.BlockSpec((None, arr.shape[1], arr.shape[2]), lambda i, j: (j, 0, 0)))
        elif kind == 'tile':
            in_specs.append(pl.BlockSpec((tm, tn), lambda i, j: (i, j)))
        elif isinstance(kind, tuple) and kind[0] == 'tileoff':
            in_specs.append(pl.BlockSpec((tm, tn), lambda i, j, off=kind[1]: (i, j + off)))
        elif kind == 'vec':
            in_specs.append(pl.BlockSpec((1, tn), lambda i, j: (0, j)))
        elif kind == 'bvec':
            tpb = tiles_per_batch
            in_specs.append(pl.BlockSpec((None, 1, tn), lambda i, j, tpb=tpb: (i // tpb, 0, j)))
        elif kind == 'full':
            nd = arr.ndim
            in_specs.append(pl.BlockSpec(arr.shape, lambda i, j, nd=nd: (0,) * nd))
        else:
            raise ValueError(kind)
    out_shape, out_specs = [], []
    for n_cols, dtype in outs:
        assert (n_cols * tn) % n == 0
        w = n_cols * tn // n
        out_shape.append(jax.ShapeDtypeStruct((m, n_cols), dtype))
        out_specs.append(pl.BlockSpec((tm, w), lambda i, j: (i, j)))
    single = len(outs) == 1
    res = pl.pallas_call(
        body, grid=grid, in_specs=in_specs,
        out_specs=out_specs[0] if single else out_specs,
        out_shape=out_shape[0] if single else out_shape,
        compiler_params=_params(2), name=name)(*arrays)
    return res


def _mm_body(a_ref, w_ref, o_ref):
    o_ref[...] = _dot(a_ref[...], w_ref[...]).astype(o_ref.dtype)


def _mm_sigmoid_body(a_ref, w_ref, o_ref):
    o_ref[...] = jax.nn.sigmoid(_dot(a_ref[...], w_ref[...])).astype(o_ref.dtype)


def _mm_residual_body(a_ref, w_ref, x_ref, g_ref, o_ref):
    o_ref[...] = x_ref[...] + g_ref[...] * _dot(a_ref[...], w_ref[...])


def _matmul(a, w, out_dtype, *, tm, tn, body=_mm_body, name=None):
    return _tiled_call(body, a.shape[0], w.shape[1], tm, tn, [(a, 'row'), (w, 'col')],
                       [(w.shape[1], out_dtype)], name=name)


def _ada_body(c_ref, w_ref, b_ref, o_ref):
    c = c_ref[...]
    a = (c * jax.nn.sigmoid(c)).astype(BF16)
    o_ref[...] = _dot(a, w_ref[...].astype(BF16)) + b_ref[...]


def _ada_mod(c, w_ada, b_ada):
    bsz, d = c.shape
    n = w_ada.shape[1]
    c_pad = jnp.zeros((SUBLANES, d), F32).at[:bsz].set(c)
    out = _tiled_call(_ada_body, SUBLANES, n, SUBLANES, 512,
                      [(c_pad, 'row'), (w_ada, 'col'), (b_ada.reshape(1, n), 'vec')],
                      [(n, F32)], name='ada_mod')
    return out[:bsz]


def _norm_mod_body(x_ref, g_ref, sc_ref, sh_ref, o_ref):
    x = x_ref[...]
    y = x * lax.rsqrt(jnp.mean(x * x, axis=-1, keepdims=True) + RMS_EPS) * g_ref[...]
    o_ref[...] = (y * (1.0 + sc_ref[...]) + sh_ref[...]).astype(o_ref.dtype)


def _norm_mod(x2, g, sc, sh, seq, tm=256, out_dtype=BF16):
    t, d = x2.shape
    bsz = t // seq
    return _tiled_call(_norm_mod_body, t, d, tm, d,
                       [(x2, 'tile'), (g.reshape(1, d), 'vec'), (sc.reshape(bsz, 1, d), 'bvec'),
                        (sh.reshape(bsz, 1, d), 'bvec')],
                       [(d, out_dtype)], tiles_per_batch=seq // tm, name='norm_mod')


def _head_norm(x, g):
    outs = []
    for h in range(x.shape[1] // HEAD_DIM):
        xh = x[:, h * HEAD_DIM:(h + 1) * HEAD_DIM].astype(F32)
        outs.append(xh * lax.rsqrt(jnp.mean(xh * xh, axis=-1, keepdims=True) + RMS_EPS) * g)
    return jnp.concatenate(outs, axis=1)


def _qk_norm_body(q_ref, k_ref, gq_ref, gk_ref, qo_ref, ko_ref):
    qo_ref[...] = _head_norm(q_ref[...], gq_ref[...]).astype(qo_ref.dtype)
    ko_ref[...] = _head_norm(k_ref[...], gk_ref[...]).astype(ko_ref.dtype)


def _qk_norm(pa, gq, gk, width, tm=512):
    t = pa.shape[0]
    return _tiled_call(_qk_norm_body, t, width, tm, width,
                       [(pa, ('tileoff', 0)), (pa, ('tileoff', 1)),
                        (gq.reshape(1, HEAD_DIM), 'full'), (gk.reshape(1, HEAD_DIM), 'full')],
                       [(width, BF16), (width, BF16)], name='qk_norm_a')


def _dsa_body(q_ref, iq_ref, iw_ref, ik_ref, k_ref, v_ref, o_ref,
              keys_ref, thr_ref, m_ref, l_ref, acc_ref, *, tq, tk, top_k, n_heads):
    qb = pl.program_id(1)
    kb = pl.program_id(2)
    kb_last = ((qb + 1) * tq - 1) // tk
    n_chunks = kb_last + 1
    row = lax.broadcasted_iota(jnp.int32, (tq, tk), 0) + qb * tq
    lane = lax.broadcasted_iota(jnp.int32, (tq, tk), 1)

    @pl.when(kb == 0)
    def _scores_and_threshold():
        w = iw_ref[:, IDX_DIM:IDX_DIM + IDX_HEADS] * (IDX_HEADS ** -0.5 * IDX_DIM ** -0.5)

        def score_chunk(c, carry):
            ik2 = ik_ref[pl.ds(pl.multiple_of(c * tk, tk), tk), :]
            acc = jnp.zeros((tq, tk), F32)
            for p in range(IDX_HEADS // 2):
                iq_pair = iq_ref[:, p * LANES:(p + 1) * LANES]
                for half in range(2):
                    rel = _dot_nt(iq_pair, ik2[:, half * LANES:(half + 1) * LANES])
                    h = 2 * p + half
                    acc = acc + w[:, h:h + 1] * jnp.maximum(rel, 0.0)
            bits = pltpu.bitcast(acc, jnp.int32)
            key = jnp.where(bits < 0, bits ^ 0x7FFFFFFF, bits)
            key = jnp.where(lane + c * tk <= row, key, INT_MIN)
            keys_ref[c] = key
            return carry

        lax.fori_loop(0, n_chunks, score_chunk, 0)

        def count_ge(cand):
            def body(c, cnt):
                blk = keys_ref[c]
                for j in range(tk // LANES):
                    cnt = cnt + jnp.where(blk[:, j * LANES:(j + 1) * LANES] >= cand, 1.0, 0.0)
                return cnt
            cnt = lax.fori_loop(0, n_chunks, body, jnp.zeros((tq, LANES), F32))
            return jnp.sum(cnt, axis=1, keepdims=True)

        k_f = float(top_k)
        t0 = jnp.where(count_ge(jnp.zeros((tq, 1), jnp.int32)) >= k_f, 0, INT_MIN).astype(jnp.int32)

        def bit_body(i, t):
            cand = t | lax.shift_left(jnp.int32(1), 30 - i)
            return jnp.where(count_ge(cand) >= k_f, cand, t)

        t = lax.fori_loop(0, 31, bit_body, t0)
        thr_ref[...] = jnp.broadcast_to(t, (tq, LANES))
        m_ref[...] = jnp.full(m_ref.shape, NEG_BIG, F32)
        l_ref[...] = jnp.zeros(l_ref.shape, F32)
        acc_ref[...] = jnp.zeros(acc_ref.shape, F32)

    @pl.when(kb <= kb_last)
    def _attend():
        sel = (keys_ref[kb] >= thr_ref[:, 0:1]) & (lane + kb * tk <= row)
        scale = HEAD_DIM ** -0.5
        for h in range(n_heads):
            sl = slice(h * HEAD_DIM, (h + 1) * HEAD_DIM)
            s = _dot_nt(q_ref[:, sl], k_ref[:, sl]) * scale
            s = jnp.where(sel, s, NEG_BIG)
            m_old = m_ref[h]
            m_new = jnp.maximum(m_old, jnp.max(s, axis=1, keepdims=True))
            p = jnp.exp(s - m_new[:, 0:1])
            alpha = jnp.exp(m_old - m_new)
            l_ref[h] = alpha * l_ref[h] + jnp.sum(p, axis=1, keepdims=True)
            acc_ref[h] = alpha * acc_ref[h] + _dot(p.astype(BF16), v_ref[:, sl])
            m_ref[h] = m_new

    @pl.when(kb == kb_last)
    def _finish():
        for h in range(n_heads):
            sl = slice(h * HEAD_DIM, (h + 1) * HEAD_DIM)
            o_ref[:, sl] = (acc_ref[h] / l_ref[h]).astype(o_ref.dtype)


def _dsa_attention(qn, kn, pa, ik2, small, seq, top_k, *, tq=256, tk=512):
    t, width = qn.shape
    bsz = t // seq
    nq, nk = seq // tq, seq // tk
    n_heads = width // HEAD_DIM
    assert width == IDX_HEADS * IDX_DIM and pa.shape[1] == 4 * width
    v_off, iq_off = 2, 3

    def kmap(b, qb, kb):
        return (b * nk + jnp.minimum(kb, ((qb + 1) * tq - 1) // tk), 0)

    def vmap_(b, qb, kb):
        return (b * nk + jnp.minimum(kb, ((qb + 1) * tq - 1) // tk), v_off)

    body = functools.partial(_dsa_body, tq=tq, tk=tk, top_k=top_k, n_heads=n_heads)
    return pl.pallas_call(
        body, grid=(bsz, nq, nk),
        in_specs=[
            pl.BlockSpec((tq, width), lambda b, qb, kb: (b * nq + qb, 0)),
            pl.BlockSpec((tq, width), lambda b, qb, kb: (b * nq + qb, iq_off)),
            pl.BlockSpec((tq, LANES), lambda b, qb, kb: (b * nq + qb, 0)),
            pl.BlockSpec((seq, 2 * LANES), lambda b, qb, kb: (b, 0)),
            pl.BlockSpec((tk, width), kmap),
            pl.BlockSpec((tk, width), vmap_),
        ],
        out_specs=pl.BlockSpec((tq, width), lambda b, qb, kb: (b * nq + qb, 0)),
        out_shape=jax.ShapeDtypeStruct((t, width), BF16),
        scratch_shapes=[
            pltpu.VMEM((nk, tq, tk), jnp.int32),
            pltpu.VMEM((tq, LANES), jnp.int32),
            pltpu.VMEM((n_heads, tq, LANES), F32),
            pltpu.VMEM((n_heads, tq, LANES), F32),
            pltpu.VMEM((n_heads, tq, HEAD_DIM), F32),
        ],
        compiler_params=_params(3), name='dsa_attention',
    )(qn, pa, small, ik2, kn, pa)


def _dil_body(q_ref, kc_ref, vc_ref, kp_ref, vp_ref, gq_ref, gk_ref, o_ref, lse_ref,
              kcat_ref, vcat_ref, *, band, rows, n_heads):
    n = pl.program_id(1)
    kcat_ref[0:band, :] = _head_norm(kp_ref[...], gk_ref[...]).astype(BF16)
    kcat_ref[band:band + rows, :] = _head_norm(kc_ref[...], gk_ref[...]).astype(BF16)
    vcat_ref[0:band, :] = vp_ref[...]
    vcat_ref[band:band + rows, :] = vc_ref[...]
    qi = lax.broadcasted_iota(jnp.int32, (band, 2 * band), 0)
    ki = lax.broadcasted_iota(jnp.int32, (band, 2 * band), 1)
    back = qi + band - ki
    in_band = (back >= 0) & (back <= band)
    scale = HEAD_DIM ** -0.5
    gq = gq_ref[...]

    def sub_block(j, carry):
        r0 = pl.multiple_of(j * band, band)
        mask = in_band & ((ki >= band) | (n * (rows // band) + j > 0))
        for h in range(n_heads):
            sl = slice(h * HEAD_DIM, (h + 1) * HEAD_DIM)
            qh = q_ref[pl.ds(r0, band), sl].astype(F32)
            qh = (qh * lax.rsqrt(jnp.mean(qh * qh, axis=-1, keepdims=True) + RMS_EPS) * gq).astype(BF16)
            s = _dot_nt(qh, kcat_ref[pl.ds(r0, 2 * band), sl]) * scale
            s = jnp.where(mask, s, NEG_BIG)
            m = jnp.max(s, axis=1, keepdims=True)
            p = jnp.exp(s - m)
            l = jnp.sum(p, axis=1, keepdims=True)
            o = _dot(p.astype(BF16), vcat_ref[pl.ds(r0, 2 * band), sl]) / l
            o_ref[pl.ds(r0, band), sl] = o.astype(o_ref.dtype)
            lse_ref[pl.ds(r0, band), h:h + 1] = m + jnp.log(l)
        return carry

    lax.fori_loop(0, rows // band, sub_block, 0)


def _dilated_group(qkv, gq, gk, n_seq, len_seq, *, band=128, rows=512):
    t, w3 = qkv.shape
    width = w3 // 3
    n_heads = width // HEAD_DIM
    rows = min(rows, len_seq)
    nblk = len_seq // rows
    sub = rows // band

    def cur(c):
        return lambda s, n: (s * nblk + n, c)

    def prev(c):
        return lambda s, n: (s * nblk * sub + jnp.maximum(n * sub - 1, 0), c)

    body = functools.partial(_dil_body, band=band, rows=rows, n_heads=n_heads)
    return pl.pallas_call(
        body, grid=(n_seq, nblk),
        in_specs=[
            pl.BlockSpec((rows, width), cur(0)), pl.BlockSpec((rows, width), cur(1)),
            pl.BlockSpec((rows, width), cur(2)),
            pl.BlockSpec((band, width), prev(1)), pl.BlockSpec((band, width), prev(2)),
            pl.BlockSpec((1, HEAD_DIM), lambda s, n: (0, 0)), pl.BlockSpec((1, HEAD_DIM), lambda s, n: (0, 0)),
        ],
        out_specs=[pl.BlockSpec((rows, width), lambda s, n: (s * nblk + n, 0)),
                   pl.BlockSpec((rows, n_heads), lambda s, n: (s * nblk + n, 0))],
        out_shape=[jax.ShapeDtypeStruct((t, width), BF16), jax.ShapeDtypeStruct((t, n_heads), F32)],
        scratch_shapes=[pltpu.VMEM((band + rows, width), BF16), pltpu.VMEM((band + rows, width), BF16)],
        compiler_params=_params(2), name='dilated_attention',
    )(qkv, qkv, qkv, qkv, qkv, gq.reshape(1, HEAD_DIM), gk.reshape(1, HEAD_DIM))


def _dil_merge_body(o0_ref, o1_ref, o2_ref, l0_ref, l1_ref, l2_ref, out_ref):
    l0, l1, l2 = l0_ref[...], l1_ref[...], l2_ref[...]
    m = jnp.maximum(jnp.maximum(l0, l1), l2)
    e0, e1, e2 = jnp.exp(l0 - m), jnp.exp(l1 - m), jnp.exp(l2 - m)
    den = e0 + e1 + e2
    a0, a1, a2 = e0 / den, e1 / den, e2 / den
    for h in range(l0.shape[1]):
        sl = slice(h * HEAD_DIM, (h + 1) * HEAD_DIM)
        out_ref[:, sl] = (a0[:, h:h + 1] * o0_ref[:, sl].astype(F32)
                          + a1[:, h:h + 1] * o1_ref[:, sl].astype(F32)
                          + a2[:, h:h + 1] * o2_ref[:, sl].astype(F32)).astype(out_ref.dtype)


def _dil_merge(outs, lses, tm=512):
    t, width = outs[0].shape
    nh = lses[0].shape[1]
    ospec = pl.BlockSpec((tm, width), lambda i: (i, 0))
    lspec = pl.BlockSpec((tm, nh), lambda i: (i, 0))
    return pl.pallas_call(
        _dil_merge_body, grid=(t // tm,), in_specs=[ospec] * 3 + [lspec] * 3, out_specs=ospec,
        out_shape=jax.ShapeDtypeStruct((t, width), BF16), compiler_params=_params(1), name='dilated_merge',
    )(*outs, *lses)


def _to_classes(a, bsz, seq, dil):
    if dil == 1:
        return a
    c = a.shape[1]
    return a.reshape(bsz, seq // dil, dil, c).transpose(0, 2, 1, 3).reshape(bsz * seq, c)


def _from_classes(a, bsz, seq, dil):
    if dil == 1:
        return a
    c = a.shape[1]
    return a.reshape(bsz, dil, seq // dil, c).transpose(0, 2, 1, 3).reshape(bsz * seq, c)


def _lru_body(x_ref, y_ref, xp_ref, cw_ref, cb_ref, wr_ref, br_ref, wi_ref, bi_ref, lam_ref, o_ref,
              carry_ref, *, tm, tiles_per_batch):
    i = pl.program_id(0)
    first = (i % tiles_per_batch) == 0
    x = x_ref[...]
    prev = jnp.where(first, 0.0, xp_ref[...])
    xs = jnp.concatenate([prev, x], axis=0)
    xc = cb_ref[...] + cw_ref[CONV_WIDTH - 1:CONV_WIDTH, :] * x
    for j in range(CONV_WIDTH - 1):
        back = CONV_WIDTH - 1 - j
        xc = xc + cw_ref[j:j + 1, :] * xs[SUBLANES - back:SUBLANES - back + tm, :]
    xcb = xc.astype(BF16)
    r = jax.nn.sigmoid(_dot(xcb, wr_ref[...]) + br_ref[...])
    ig = jax.nn.sigmoid(_dot(xcb, wi_ref[...]) + bi_ref[...])
    lam = lam_ref[...]
    softplus_neg = jnp.maximum(-lam, 0.0) + jnp.log1p(jnp.exp(-jnp.abs(lam)))
    log_a = (-LRU_C * softplus_neg) * r
    a = jnp.exp(log_a)
    b = jnp.sqrt(1.0 - a * a) * (ig * xc)
    rows = lax.broadcasted_iota(jnp.int32, a.shape, 0)
    step = 1
    while step < tm:
        a_sh = pltpu.roll(a, step, 0)
        b_sh = pltpu.roll(b, step, 0)
        valid = rows >= step
        b = jnp.where(valid, a * b_sh + b, b)
        a = jnp.where(valid, a * a_sh, a)
        step *= 2
    h0 = jnp.where(first, 0.0, carry_ref[0:1, :])
    h = a * h0 + b
    carry_ref[...] = jnp.broadcast_to(h[tm - 1:tm, :], carry_ref.shape)
    o_ref[...] = (h * jax.nn.gelu(y_ref[...])).astype(o_ref.dtype)


def _block_diag(w):
    g, n, _ = w.shape
    eye = jnp.eye(g, dtype=w.dtype)
    return (eye[:, None, :, None] * w[:, :, None, :]).reshape(g * n, g * n)


def _rg_lru(pc, conv_w, conv_b, w_r, b_r, w_i, b_i, lam, seq, tm=256):
    t, w2 = pc.shape
    w = w2 // 2
    row = lambda v: v.reshape(1, w)
    full2 = lambda shape: pl.BlockSpec(shape, lambda i: (0, 0))
    body = functools.partial(_lru_body, tm=tm, tiles_per_batch=seq // tm)
    return pl.pallas_call(
        body, grid=(t // tm,),
        in_specs=[
            pl.BlockSpec((tm, w), lambda i: (i, 0)),
            pl.BlockSpec((tm, w), lambda i: (i, 1)),
            pl.BlockSpec((SUBLANES, w), lambda i: (jnp.maximum(i * (tm // SUBLANES) - 1, 0), 0)),
            full2((CONV_WIDTH, w)), full2((1, w)), full2((w, w)), full2((1, w)), full2((w, w)),
            full2((1, w)), full2((1, w)),
        ],
        out_specs=pl.BlockSpec((tm, w), lambda i: (i, 0)),
        out_shape=jax.ShapeDtypeStruct((t, w), BF16),
        scratch_shapes=[pltpu.VMEM((SUBLANES, w), F32)],
        compiler_params=_params(1), name='rg_lru',
    )(pc, pc, pc, conv_w, row(conv_b), _block_diag(w_r).astype(BF16), row(b_r),
      _block_diag(w_i).astype(BF16), row(b_i), row(lam))


def _branch_merge_body(oa_ref, ob_ref, oc_ref, wa_ref, wb_ref, wc_ref, ga_ref, gb_ref, gc_ref, o_ref):
    mixed = (ga_ref[...].astype(F32) * _dot(oa_ref[...], wa_ref[...])
             + gb_ref[...].astype(F32) * _dot(ob_ref[...], wb_ref[...])
             + gc_ref[...].astype(F32) * _dot(oc_ref[...], wc_ref[...]))
    o_ref[...] = mixed.astype(o_ref.dtype)


MOE_ROWS = 256
IDX_REC = 1024


def _router_body(a_ref, w_ref, b_ref, gsel_ref, eidx_ref, rank_ref, cnt_ref, carry_ref, *, n_experts, top_k, tm):
    i = pl.program_id(0)

    @pl.when(i == 0)
    def _():
        carry_ref[...] = jnp.zeros(carry_ref.shape, F32)

    logits = _dot(a_ref[...].astype(BF16), w_ref[...])
    lane = lax.broadcasted_iota(jnp.int32, logits.shape, 1).astype(F32)
    scores = jax.nn.sigmoid(logits)
    cur = jnp.where(lane < n_experts, scores + b_ref[...], -jnp.inf)
    mask = jnp.zeros(logits.shape, F32)
    hits, picked_scores = [], []
    for r in range(top_k):
        m = jnp.max(cur, axis=1, keepdims=True)
        idx = jnp.min(jnp.where(cur == m, lane, float(LANES)), axis=1, keepdims=True)
        hit = lane == idx
        hits.append(hit)
        picked_scores.append(jnp.sum(jnp.where(hit, scores, 0.0), axis=1, keepdims=True))
        eidx_ref[:, r:r + 1] = idx.astype(jnp.int32)
        cur = jnp.where(hit, -jnp.inf, cur)
        mask = mask + jnp.where(hit, 1.0, 0.0)
    total = picked_scores[0]
    for r in range(1, top_k):
        total = total + picked_scores[r]
    ri = lax.broadcasted_iota(jnp.int32, (tm, tm), 0)
    ci = lax.broadcasted_iota(jnp.int32, (tm, tm), 1)
    tri = jnp.where(ri > ci, 1.0, 0.0).astype(BF16)
    before = _dot(tri, mask.astype(BF16)) + carry_ref[0:1, :]
    for r in range(top_k):
        gsel_ref[:, r:r + 1] = picked_scores[r] / total * ROUTED_SCALE
        rank_ref[:, r:r + 1] = jnp.sum(jnp.where(hits[r], before, 0.0), axis=1, keepdims=True).astype(jnp.int32)
    carry_ref[...] = carry_ref[...] + jnp.sum(mask, axis=0, keepdims=True)
    cnt_ref[...] = carry_ref[...]


def _router(h2, w_router, b_router, tm=1024):
    t, d = h2.shape
    n_exp = w_router.shape[1]
    w_r = jnp.zeros((d, LANES), BF16).at[:, :n_exp].set(w_router.astype(BF16))
    b_r = jnp.zeros((1, LANES), F32).at[0, :n_exp].set(b_router.astype(F32))
    col = lambda dt: jax.ShapeDtypeStruct((t, MOE_TOP_K), dt)
    cspec = pl.BlockSpec((tm, MOE_TOP_K), lambda i: (i, 0))
    return pl.pallas_call(
        functools.partial(_router_body, n_experts=n_exp, top_k=MOE_TOP_K, tm=tm), grid=(t // tm,),
        in_specs=[pl.BlockSpec((tm, d), lambda i: (i, 0)), pl.BlockSpec((d, LANES), lambda i: (0, 0)),
                  pl.BlockSpec((1, LANES), lambda i: (0, 0))],
        out_specs=[cspec, cspec, cspec, pl.BlockSpec((SUBLANES, LANES), lambda i: (0, 0))],
        out_shape=[col(F32), col(jnp.int32), col(jnp.int32), jax.ShapeDtypeStruct((SUBLANES, LANES), F32)],
        scratch_shapes=[pltpu.VMEM((SUBLANES, LANES), F32)],
        compiler_params=_params(1), name='router')(h2, w_r, b_r)


def _moe_group_body(te_ref, nu_ref, idx_hbm, h_hbm, w1_ref, w3_ref, w2_ref, y_hbm,
                    idx_smem, xbuf, ybuf, isem, gsem, ssem, *, tmr, dump_row, n_tiles):
    i = pl.program_id(0)
    n = nu_ref[0]
    d = xbuf.shape[-1]
    n_groups = tmr // SUBLANES
    n_pieces = n_groups // 2
    kc = d // n_pieces

    def idx_copy(tile, s3):
        tile = jnp.minimum(tile, n_tiles - 1)
        return pltpu.make_async_copy(idx_hbm.at[pl.ds(pl.multiple_of(tile * IDX_REC, IDX_REC), IDX_REC)],
                                     idx_smem.at[pl.ds(pl.multiple_of(s3 * IDX_REC, IDX_REC), IDX_REC)],
                                     isem.at[s3])

    def split(row):
        return lax.shift_right_logical(row, 3), row & (SUBLANES - 1)

    def row_in(tok, g, u, s):
        hi, lo = split(tok)
        return pltpu.make_async_copy(h_hbm.at[hi, pl.ds(lo, 1)], xbuf.at[s, g, pl.ds(u, 1)], gsem.at[s])

    def row_out(dst, g, u, s):
        hi, lo = split(dst)
        return pltpu.make_async_copy(ybuf.at[s, g, pl.ds(u, 1)], y_hbm.at[hi, pl.ds(lo, 1)], ssem.at[s])

    def gather_group(g, s3, s):
        for u in range(SUBLANES):
            row_in(idx_smem[s3 * IDX_REC + g * SUBLANES + u], g, u, s).start()

    def scatter_group(g, s3, s):
        for u in range(SUBLANES):
            row_out(idx_smem[s3 * IDX_REC + tmr + g * SUBLANES + u], g, u, s).start()

    def per_group(fn):
        def body(g, c):
            fn(g)
            return c
        lax.fori_loop(0, n_groups, body, 0)

    def gather_wait(s):
        per_group(lambda g: [row_in(jnp.int32(0), g, u, s).wait() for u in range(SUBLANES)])

    def scatter_wait(s):
        per_group(lambda g: [row_out(jnp.int32(dump_row), g, u, s).wait() for u in range(SUBLANES)])

    @pl.when(i == 0)
    def _prologue():
        ybuf[1] = jnp.zeros(ybuf.shape[1:], ybuf.dtype)

        def fill(r, c):
            idx_smem[2 * IDX_REC + tmr + r] = dump_row + r
            return c
        lax.fori_loop(0, tmr, fill, 0)
        idx_copy(0, 0).start()
        idx_copy(0, 0).wait()
        idx_copy(1, 1).start()
        per_group(lambda g: gather_group(g, 0, 0))

    @pl.when((i > 0) & (i < n))
    def _():
        scatter_wait(i % 2)

    @pl.when(i < n)
    def _tile():
        s, s_next, s_prev = i % 2, (i + 1) % 2, (i + 1) % 2
        r_next, r_prev = (i + 1) % 3, (i + 2) % 3
        idx_copy(i + 1, r_next).wait()
        gather_wait(s)
        h1 = jnp.zeros((tmr, w1_ref.shape[-1]), F32)
        h3 = jnp.zeros((tmr, w1_ref.shape[-1]), F32)
        for c in range(n_pieces):
            gather_group(c, r_next, s_next)
            scatter_group(c, r_prev, s_prev)
            xc = xbuf[s, :, :, c * kc:(c + 1) * kc].reshape(tmr, kc).astype(BF16)
            h1 = h1 + _dot(xc, w1_ref[c * kc:(c + 1) * kc, :].astype(BF16))
            h3 = h3 + _dot(xc, w3_ref[c * kc:(c + 1) * kc, :].astype(BF16))
        hid = ((h1 * jax.nn.sigmoid(h1)) * h3).astype(BF16)
        for c in range(n_pieces):
            gather_group(n_pieces + c, r_next, s_next)
            scatter_group(n_pieces + c, r_prev, s_prev)
            yc = _dot(hid, w2_ref[:, c * kc:(c + 1) * kc].astype(BF16))
            ybuf[s, :, :, c * kc:(c + 1) * kc] = yc.reshape(n_groups, SUBLANES, kc)
        idx_copy(i + 2, r_prev).start()

    @pl.when(i == n)
    def _drain():
        scatter_wait(i % 2)
        per_group(lambda g: scatter_group(g, (i + 2) % 3, (i + 1) % 2))
        scatter_wait((i + 1) % 2)
        gather_wait(i % 2)
        idx_copy(i + 1, (i + 1) % 3).wait()


def _moe_grouped(h2, idx, tile_expert, n_used, w1, w3, w2, layer, n_slots):
    t, d = h2.shape
    n_tiles = idx.shape[0] // IDX_REC
    tmr = MOE_ROWS
    f = w1.shape[3]
    dump_row = n_slots * t
    n_rows = n_slots * t + tmr
    assert n_tiles >= 3
    wmap = lambda i, te, nu: (layer, te[i], 0, 0)
    grid_spec = pltpu.PrefetchScalarGridSpec(
        num_scalar_prefetch=2, grid=(n_tiles + 1,),
        in_specs=[
            pl.BlockSpec(memory_space=pl.ANY), pl.BlockSpec(memory_space=pl.ANY),
            pl.BlockSpec((None, None, d, f), wmap), pl.BlockSpec((None, None, d, f), wmap),
            pl.BlockSpec((None, None, f, d), wmap),
        ],
        out_specs=pl.BlockSpec(memory_space=pl.ANY),
        scratch_shapes=[
            pltpu.SMEM((3 * IDX_REC,), jnp.int32),
            pltpu.VMEM((2, tmr // SUBLANES, SUBLANES, d), F32), pltpu.VMEM((2, tmr // SUBLANES, SUBLANES, d), F32),
            pltpu.SemaphoreType.DMA((3,)), pltpu.SemaphoreType.DMA((2,)), pltpu.SemaphoreType.DMA((2,)),
        ])
    y = pl.pallas_call(
        functools.partial(_moe_group_body, tmr=tmr, dump_row=dump_row, n_tiles=n_tiles), grid_spec=grid_spec,
        out_shape=jax.ShapeDtypeStruct((n_rows // SUBLANES, SUBLANES, d), F32),
        compiler_params=_params(1), name='moe_grouped',
    )(tile_expert, n_used, idx, h2.reshape(t // SUBLANES, SUBLANES, d), w1, w3, w2)
    return y.reshape(n_rows, d)


def _moe_plan(eidx, rank, counts, n_experts):
    t, k = eidx.shape
    tmr = MOE_ROWS
    n_tiles = (t * k) // tmr + n_experts
    padded = ((counts + tmr - 1) // tmr) * tmr
    ends = jnp.cumsum(padded)
    offsets = ends - padded
    n_used = (ends[-1] // tmr).astype(jnp.int32)
    pos = (offsets[eidx] + rank).reshape(-1)
    dst = (jnp.arange(k, dtype=jnp.int32)[None, :] * t + jnp.arange(t, dtype=jnp.int32)[:, None]).reshape(-1)
    p = n_tiles * tmr
    dst_sorted = (k * t + jnp.arange(p, dtype=jnp.int32) % tmr).at[pos].set(dst).reshape(n_tiles, tmr)
    tok_sorted = dst_sorted % t
    idx = jnp.concatenate([tok_sorted, dst_sorted, jnp.zeros((n_tiles, IDX_REC - 2 * tmr), jnp.int32)],
                          axis=1).reshape(-1)
    tile_start = jnp.minimum(jnp.arange(n_tiles + 1, dtype=jnp.int32), n_used - 1) * tmr
    tile_expert = jnp.sum((ends[None, :] <= tile_start[:, None]).astype(jnp.int32), axis=1)
    return idx, jnp.minimum(tile_expert, n_experts - 1), n_used.reshape(1)


def _shared_expert_body(a_ref, w1_ref, w3_ref, w2_ref, o_ref):
    a = a_ref[...].astype(BF16)
    h1 = _dot(a, w1_ref[...])
    hid = (h1 * jax.nn.sigmoid(h1)) * _dot(a, w3_ref[...])
    o_ref[...] = _dot(hid.astype(BF16), w2_ref[...])


def _shared_expert(h2, w1, w3, w2, tm=512):
    t, d = h2.shape
    f = w1.shape[1]
    full = lambda shape: pl.BlockSpec(shape, lambda i: (0, 0))
    return pl.pallas_call(
        _shared_expert_body, grid=(t // tm,),
        in_specs=[pl.BlockSpec((tm, d), lambda i: (i, 0)), full((d, f)), full((d, f)), full((f, d))],
        out_specs=pl.BlockSpec((tm, d), lambda i: (i, 0)),
        out_shape=jax.ShapeDtypeStruct((t, d), F32), compiler_params=_params(1), name='shared_expert',
    )(h2, w1.astype(BF16), w3.astype(BF16), w2.astype(BF16))


def _moe_combine_body(x_ref, g_ref, gsel_ref, ysh_ref, *refs):
    y_refs, o_ref = refs[:-1], refs[-1]
    gs = gsel_ref[...]
    acc = ysh_ref[...]
    for r, y_ref in enumerate(y_refs):
        acc = acc + gs[:, r:r + 1] * y_ref[...]
    o_ref[...] = x_ref[...] + g_ref[...] * acc


def _moe_combine(x2, g2, gsel, y_shared, y, seq, tm=256, tn=1024):
    t, d = x2.shape
    bsz = t // seq
    k = gsel.shape[1]
    tile = pl.BlockSpec((tm, tn), lambda i, j: (i, j))
    slot_specs = [pl.BlockSpec((tm, tn), lambda i, j, r=r: (r * (t // tm) + i, j)) for r in range(k)]
    return pl.pallas_call(
        _moe_combine_body, grid=(t // tm, d // tn),
        in_specs=[tile, pl.BlockSpec((None, 1, tn), lambda i, j: (i // (seq // tm), 0, j)),
                  pl.BlockSpec((tm, k), lambda i, j: (i, 0)), tile] + slot_specs,
        out_specs=tile, out_shape=jax.ShapeDtypeStruct((t, d), F32),
        compiler_params=_params(2), name='moe_combine',
    )(x2, g2.reshape(bsz, 1, d), gsel, y_shared, *([y] * k))


def kernel(x, c, w_ada, b_ada, ada_layer, norm_mix, norm_ffn, w_in, q_norm_a, k_norm_a, q_norm_b,
           k_norm_b, conv_w, conv_b, w_rgate, b_rgate, w_igate, b_igate, lru_lambda, w_branch_a,
           w_branch_b, w_branch_c, w_out, w_router, b_router, w1_exp, w3_exp, w2_exp, w1_shared,
           w3_shared, w2_shared):
    bsz, seq, d = x.shape
    t = bsz * seq
    depth = w_in.shape[0]
    mixw = d // 4
    a_cols = 3 * mixw + IDX_HEADS * IDX_DIM + IDX_DIM + IDX_HEADS
    off_b = a_cols
    off_c = off_b + N_DIL * 3 * mixw
    off_g = off_c + 2 * mixw
    top_k = min(DSA_TOPK_MAX, seq // 4)
    tm = 1024 if t % 1024 == 0 else 256
    tpb = seq // tm

    mod_shared = _ada_mod(c, w_ada, b_ada).reshape(bsz, N_MOD, d)
    x2 = x.reshape(t, d)
    for l in range(depth):
        mod = mod_shared + ada_layer[l]
        sh1, sc1, g1, sh2, sc2, g2 = [mod[:, j] for j in range(N_MOD)]
        h = _norm_mod(x2, norm_mix[l], sc1, sh1, seq)
        wl = w_in[l]

        n_main = 3 * mixw + IDX_HEADS * IDX_DIM
        pa = _matmul(h, wl[:, :n_main].astype(BF16), BF16, tm=tm, tn=512, name='proj_a')
        w_small = jnp.zeros((d, LANES), BF16).at[:, :IDX_DIM + IDX_HEADS].set(
            wl[:, n_main:a_cols].astype(BF16))
        small = _matmul(h, w_small, F32, tm=tm, tn=LANES, name='proj_a_idx')
        ik = small[:, :IDX_DIM].astype(BF16)
        zeros = jnp.zeros_like(ik)
        ik2 = jnp.concatenate([ik, zeros, zeros, ik], axis=1)
        qn, kn = _qk_norm(pa, q_norm_a[l], k_norm_a[l], mixw)
        o_a = _dsa_attention(qn, kn, pa, ik2, small, seq, top_k)

        pb = _matmul(h, wl[:, off_b:off_c].astype(BF16), BF16, tm=tm, tn=512, name='proj_b')
        outs, lses = [], []
        for g, (window, dil) in enumerate(DIL_PATTERNS):
            qkv = _to_classes(pb[:, g * 3 * mixw:(g + 1) * 3 * mixw], bsz, seq, dil)
            og, lg = _dilated_group(qkv, q_norm_b[l, g], k_norm_b[l, g], bsz * dil, seq // dil,
                                    band=window // dil)
            outs.append(_from_classes(og, bsz, seq, dil))
            lses.append(_from_classes(lg, bsz, seq, dil))
        o_b = _dil_merge(outs, lses)

        pc = _matmul(h, wl[:, off_c:off_g].astype(BF16), F32, tm=tm, tn=512, name='proj_c')
        o_c = _rg_lru(pc, conv_w[l], conv_b[l], w_rgate[l], b_rgate[l], w_igate[l], b_igate[l],
                      lru_lambda[l], seq)

        sg = _matmul(h, wl[:, off_g:].astype(BF16), BF16, tm=tm, tn=512, body=_mm_sigmoid_body,
                     name='proj_gates')
        nd = d // 512
        mixed = _tiled_call(
            _branch_merge_body, t, d, tm, 512,
            [(o_a, 'row'), (o_b, 'row'), (o_c, 'row'),
             (w_branch_a[l].astype(BF16), 'col'), (w_branch_b[l].astype(BF16), 'col'),
             (w_branch_c[l].astype(BF16), 'col'),
             (sg, ('tileoff', 0)), (sg, ('tileoff', nd)), (sg, ('tileoff', 2 * nd))],
            [(d, BF16)], name='branch_merge')
        x2 = _tiled_call(
            _mm_residual_body, t, d, tm, 512,
            [(mixed, 'row'), (w_out[l].astype(BF16), 'col'), (x2, 'tile'), (g1.reshape(bsz, 1, d), 'bvec')],
            [(d, F32)], tiles_per_batch=tpb, name='out_proj')

        h2 = _norm_mod(x2, norm_ffn[l], sc2, sh2, seq, out_dtype=F32)
        gsel, eidx, rank, cnt = _router(h2, w_router[l], b_router[l])
        idx, tile_expert, n_used = _moe_plan(eidx, rank, cnt[0, :N_EXPERTS].astype(jnp.int32), N_EXPERTS)
        y = _moe_grouped(h2, idx, tile_expert, n_used, w1_exp, w3_exp, w2_exp, l, MOE_TOP_K)
        y_shared = _shared_expert(h2, w1_shared[l], w3_shared[l], w2_shared[l])
        x2 = _moe_combine(x2, g2, gsel, y_shared, y, seq)
    return x2.reshape(bsz, seq, d)
```

```python
import functools

import jax
import jax.numpy as jnp
from jax import lax
from jax.experimental import pallas as pl
from jax.experimental.pallas import tpu as pltpu

HEAD_DIM = 128
A_HEADS = 8
IDX_HEADS = 16
IDX_DIM = 64
DSA_TOPK_MAX = 256
DIL_PATTERNS = ((128, 1), (512, 4), (2048, 16))
N_DIL = 3
B_HEADS = 8
LRU_BLOCKS = 16
CONV_WIDTH = 4
LRU_C = 8.0
N_EXPERTS = 64
MOE_TOP_K = 8
ROUTED_SCALE = 2.5
N_MOD = 6
RMS_EPS = 1e-6

LANES = 128
SUBLANES = 8
VMEM_LIMIT_BYTES = 56 * 1024 * 1024

NEG_BIG = -1e30
LOG2_E = 1.4426950408889634
INT_MIN = -(2 ** 31)

BF16 = jnp.bfloat16
F32 = jnp.float32


def _params(n_axes):
    return pltpu.CompilerParams(dimension_semantics=("arbitrary",) * n_axes,
                                vmem_limit_bytes=VMEM_LIMIT_BYTES)


def _dot(a, b):
    return jnp.dot(a, b, preferred_element_type=F32)


def _dot_nt(a, b):
    return lax.dot_general(a, b, (((1,), (1,)), ((), ())), preferred_element_type=F32)


def _tiled_call(body, m, n, tm, tn, ins, outs, *, tiles_per_batch=None, name=None):
    assert m % tm == 0 and n % tn == 0, (m, n, tm, tn)
    grid = (m // tm, n // tn)
    in_specs, arrays = [], []
    for arr, kind in ins:
        arrays.append(arr)
        if kind == 'row':
            in_specs.append(pl.BlockSpec((tm, arr.shape[1]), lambda i, j: (i, 0)))
        elif isinstance(kind, tuple) and kind[0] == 'rowoff':
            _, off, width = kind
            in_specs.append(pl.BlockSpec((tm, width), lambda i, j, off=off: (i, off)))
        elif kind == 'col':
            in_specs.append(pl.BlockSpec((arr.shape[0], tn), lambda i, j: (0, j)))
        elif kind == 'col3':
            in_specs.append(pl.BlockSpec((None, arr.shape[1], arr.shape[2]), lambda i, j: (j, 0, 0)))
        elif kind == 'tile':
            in_specs.append(pl.BlockSpec((tm, tn), lambda i, j: (i, j)))
        elif isinstance(kind, tuple) and kind[0] == 'tileoff':
            in_specs.append(pl.BlockSpec((tm, tn), lambda i, j, off=kind[1]: (i, j + off)))
        elif kind == 'vec':
            in_specs.append(pl.BlockSpec((1, tn), lambda i, j: (0, j)))
        elif kind == 'bvec':
            tpb = tiles_per_batch
            in_specs.append(pl.BlockSpec((None, 1, tn), lambda i, j, tpb=tpb: (i // tpb, 0, j)))
        elif kind == 'full':
            nd = arr.ndim
            in_specs.append(pl.BlockSpec(arr.shape, lambda i, j, nd=nd: (0,) * nd))
        else:
            raise ValueError(kind)
    out_shape, out_specs = [], []
    for n_cols, dtype in outs:
        assert (n_cols * tn) % n == 0
        w = n_cols * tn // n
        out_shape.append(jax.ShapeDtypeStruct((m, n_cols), dtype))
        out_specs.append(pl.BlockSpec((tm, w), lambda i, j: (i, j)))
    single = len(outs) == 1
    res = pl.pallas_call(
        body, grid=grid, in_specs=in_specs,
        out_specs=out_specs[0] if single else out_specs,
        out_shape=out_shape[0] if single else out_shape,
        compiler_params=_params(2), name=name)(*arrays)
    return res


def _mm_body(a_ref, w_ref, o_ref):
    o_ref[...] = _dot(a_ref[...], w_ref[...]).astype(o_ref.dtype)


def _mm_sigmoid_body(a_ref, w_ref, o_ref):
    o_ref[...] = jax.nn.sigmoid(_dot(a_ref[...], w_ref[...])).astype(o_ref.dtype)


def _mm_residual_body(a_ref, w_ref, x_ref, g_ref, o_ref):
    o_ref[...] = x_ref[...] + g_ref[...] * _dot(a_ref[...], w_ref[...])


def _matmul(a, w, out_dtype, *, tm, tn, body=_mm_body, name=None):
    return _tiled_call(body, a.shape[0], w.shape[1], tm, tn, [(a, 'row'), (w, 'col')],
                       [(w.shape[1], out_dtype)], name=name)


def _ada_body(c_ref, w_ref, b_ref, o_ref):
    c = c_ref[...]
    a = (c * jax.nn.sigmoid(c)).astype(BF16)
    o_ref[...] = _dot(a, w_ref[...].astype(BF16)) + b_ref[...]


def _ada_mod(c, w_ada, b_ada):
    bsz, d = c.shape
    n = w_ada.shape[1]
    c_pad = jnp.zeros((SUBLANES, d), F32).at[:bsz].set(c)
    out = _tiled_call(_ada_body, SUBLANES, n, SUBLANES, 512,
                      [(c_pad, 'row'), (w_ada, 'col'), (b_ada.reshape(1, n), 'vec')],
                      [(n, F32)], name='ada_mod')
    return out[:bsz]


def _norm_mod_body(x_ref, g_ref, sc_ref, sh_ref, o_ref):
    x = x_ref[...]
    y = x * lax.rsqrt(jnp.mean(x * x, axis=-1, keepdims=True) + RMS_EPS) * g_ref[...]
    o_ref[...] = (y * (1.0 + sc_ref[...]) + sh_ref[...]).astype(o_ref.dtype)


def _norm_mod(x2, g, sc, sh, seq, tm=256, out_dtype=BF16):
    t, d = x2.shape
    bsz = t // seq
    return _tiled_call(_norm_mod_body, t, d, tm, d,
                       [(x2, 'tile'), (g.reshape(1, d), 'vec'), (sc.reshape(bsz, 1, d), 'bvec'),
                        (sh.reshape(bsz, 1, d), 'bvec')],
                       [(d, out_dtype)], tiles_per_batch=seq // tm, name='norm_mod')


def _head_norm(x, g):
    outs = []
    for h in range(x.shape[1] // HEAD_DIM):
        xh = x[:, h * HEAD_DIM:(h + 1) * HEAD_DIM].astype(F32)
        outs.append(xh * lax.rsqrt(jnp.mean(xh * xh, axis=-1, keepdims=True) + RMS_EPS) * g)
    return jnp.concatenate(outs, axis=1)


def _qk_norm_body(q_ref, k_ref, gq_ref, gk_ref, qo_ref, ko_ref):
    qo_ref[...] = _head_norm(q_ref[...], gq_ref[...]).astype(qo_ref.dtype)
    ko_ref[...] = _head_norm(k_ref[...], gk_ref[...]).astype(ko_ref.dtype)


def _qk_norm(pa, gq, gk, width, tm=512):
    t = pa.shape[0]
    return _tiled_call(_qk_norm_body, t, width, tm, width,
                       [(pa, ('tileoff', 0)), (pa, ('tileoff', 1)),
                        (gq.reshape(1, HEAD_DIM), 'full'), (gk.reshape(1, HEAD_DIM), 'full')],
                       [(width, BF16), (width, BF16)], name='qk_norm_a')


def _dsa_body(q_ref, iq_ref, iwt_ref, ik_ref, k_ref, vt_ref, o_ref,
              keys_ref, thr_ref, bias_ref, s_ref, p_ref, m_ref, l_ref, acc_ref, *, tq, tk, top_k, n_heads):
    qb = pl.program_id(1)
    kb = pl.program_id(2)
    kb_last = ((qb + 1) * tq - 1) // tk
    n_chunks = kb_last + 1
    kpos = lax.broadcasted_iota(jnp.int32, (tk, tq), 0)
    qpos = lax.broadcasted_iota(jnp.int32, (tk, tq), 1) + qb * tq
    n_acc = 4 * SUBLANES

    @pl.when(kb == 0)
    def _scores_and_threshold():
        w = iwt_ref[...] * (IDX_HEADS ** -0.5 * IDX_DIM ** -0.5)

        def score_chunk(c, carry):
            ik2 = ik_ref[pl.ds(pl.multiple_of(c * tk, tk), tk), :]
            acc = jnp.zeros((tk, tq), F32)
            for p in range(IDX_HEADS // 2):
                iq_pair = iq_ref[:, p * LANES:(p + 1) * LANES]
                for half in range(2):
                    rel = _dot_nt(ik2[:, half * LANES:(half + 1) * LANES], iq_pair)
                    h = 2 * p + half
                    acc = acc + w[h:h + 1, :] * jnp.maximum(rel, 0.0)
            bits = pltpu.bitcast(acc, jnp.int32)
            key = jnp.where(bits < 0, bits ^ 0x7FFFFFFF, bits)
            key = jnp.where(kpos + c * tk <= qpos, key, INT_MIN)
            keys_ref[c] = key
            return carry

        lax.fori_loop(0, n_chunks, score_chunk, 0)

        def count_ge(cand):
            def body(c, cnt):
                for j in range(tk // n_acc):
                    blk = keys_ref[c, j * n_acc:(j + 1) * n_acc, :]
                    cnt = cnt + jnp.where(blk >= cand, 1.0, 0.0)
                return cnt
            cnt = lax.fori_loop(0, n_chunks, body, jnp.zeros((n_acc, tq), F32))
            return jnp.sum(cnt, axis=0, keepdims=True)

        k_f = float(top_k)
        t0 = jnp.where(count_ge(jnp.zeros((1, tq), jnp.int32)) >= k_f, 0, INT_MIN).astype(jnp.int32)

        def bit_body(i, t):
            cand = t | lax.shift_left(jnp.int32(1), 30 - i)
            return jnp.where(count_ge(cand) >= k_f, cand, t)

        t = lax.fori_loop(0, 31, bit_body, t0)
        thr_ref[...] = jnp.broadcast_to(t, thr_ref.shape)
        m_ref[...] = jnp.full(m_ref.shape, NEG_BIG, F32)
        l_ref[...] = jnp.zeros(l_ref.shape, F32)
        acc_ref[...] = jnp.zeros(acc_ref.shape, F32)

    @pl.when(kb <= kb_last)
    def _attend():
        sel = (keys_ref[kb] >= thr_ref[0:1, :]) & (kpos + kb * tk <= qpos)
        bias_ref[...] = jnp.where(sel, 0.0, NEG_BIG)
        c = HEAD_DIM ** -0.5 * LOG2_E

        def raw_scores(h):
            sl = slice(h * HEAD_DIM, (h + 1) * HEAD_DIM)
            s_ref[h % 3] = _dot_nt(k_ref[:, sl], q_ref[:, sl])

        def softmax(h):
            s = s_ref[h % 3] + bias_ref[...]
            m_old = m_ref[h, 0:1, :]
            m_new = jnp.maximum(m_old, jnp.max(s, axis=0, keepdims=True))
            p = jnp.exp2((s - m_new) * c)
            alpha = jnp.exp2((m_old - m_new) * c)
            l_new = alpha * l_ref[h, 0:1, :] + jnp.sum(p, axis=0, keepdims=True)
            p_ref[h % 2] = p.astype(BF16)
            m_ref[h] = jnp.broadcast_to(m_new, (SUBLANES, tq))
            l_ref[h] = jnp.broadcast_to(l_new, (SUBLANES, tq))
            return alpha

        def accumulate(h, alpha):
            acc_ref[h] = alpha * acc_ref[h] + _dot(vt_ref[h], p_ref[h % 2])

        raw_scores(0)
        raw_scores(1)
        alphas = {0: softmax(0)}
        for h in range(n_heads):
            if h + 2 < n_heads:
                raw_scores(h + 2)
            if h + 1 < n_heads:
                alphas[h + 1] = softmax(h + 1)
            accumulate(h, alphas.pop(h))

    @pl.when(kb == kb_last)
    def _finish():
        for h in range(n_heads):
            sl = slice(h * HEAD_DIM, (h + 1) * HEAD_DIM)
            o_ref[:, sl] = (acc_ref[h] / l_ref[h, 0:1, :]).T.astype(o_ref.dtype)


def _dsa_attention(qn, kn, pa, ik2, iwt, vt, seq, top_k, *, tq=256, tk=512):
    t, width = qn.shape
    bsz = t // seq
    nq, nk = seq // tq, seq // tk
    n_heads = width // HEAD_DIM
    assert width == IDX_HEADS * IDX_DIM and pa.shape[1] == 4 * width
    iq_off = 3

    def last(qb):
        return ((qb + 1) * tq - 1) // tk

    body = functools.partial(_dsa_body, tq=tq, tk=tk, top_k=top_k, n_heads=n_heads)
    return pl.pallas_call(
        body, grid=(bsz, nq, nk),
        in_specs=[
            pl.BlockSpec((tq, width), lambda b, qb, kb: (b * nq + qb, 0)),
            pl.BlockSpec((tq, width), lambda b, qb, kb: (b * nq + qb, iq_off)),
            pl.BlockSpec((None, IDX_HEADS, tq), lambda b, qb, kb: (b, 0, qb)),
            pl.BlockSpec((seq, 2 * LANES), lambda b, qb, kb: (b, 0)),
            pl.BlockSpec((tk, width), lambda b, qb, kb: (b * nk + jnp.minimum(kb, last(qb)), 0)),
            pl.BlockSpec((None, n_heads, HEAD_DIM, tk),
                         lambda b, qb, kb: (b, 0, 0, jnp.minimum(kb, last(qb)))),
        ],
        out_specs=pl.BlockSpec((tq, width), lambda b, qb, kb: (b * nq + qb, 0)),
        out_shape=jax.ShapeDtypeStruct((t, width), BF16),
        scratch_shapes=[
            pltpu.VMEM((nk, tk, tq), jnp.int32),
            pltpu.VMEM((SUBLANES, tq), jnp.int32),
            pltpu.VMEM((tk, tq), F32),
            pltpu.VMEM((3, tk, tq), F32),
            pltpu.VMEM((2, tk, tq), BF16),
            pltpu.VMEM((n_heads, SUBLANES, tq), F32),
            pltpu.VMEM((n_heads, SUBLANES, tq), F32),
            pltpu.VMEM((n_heads, HEAD_DIM, tq), F32),
        ],
        compiler_params=_params(3), name='dsa_attention',
    )(qn, pa, iwt, ik2, kn, vt)


def _dil_body(q_ref, kc_ref, vc_ref, kp_ref, vp_ref, gq_ref, gk_ref, o_ref, lse_ref,
              kcat_ref, vcat_ref, *, band, rows, n_heads):
    n = pl.program_id(1)
    kcat_ref[0:band, :] = _head_norm(kp_ref[...], gk_ref[...]).astype(BF16)
    kcat_ref[band:band + rows, :] = _head_norm(kc_ref[...], gk_ref[...]).astype(BF16)
    vcat_ref[0:band, :] = vp_ref[...]
    vcat_ref[band:band + rows, :] = vc_ref[...]
    qi = lax.broadcasted_iota(jnp.int32, (band, 2 * band), 0)
    ki = lax.broadcasted_iota(jnp.int32, (band, 2 * band), 1)
    back = qi + band - ki
    in_band = (back >= 0) & (back <= band)
    scale = HEAD_DIM ** -0.5
    gq = gq_ref[...]

    def sub_block(j, carry):
        r0 = pl.multiple_of(j * band, band)
        mask = in_band & ((ki >= band) | (n * (rows // band) + j > 0))
        for h in range(n_heads):
            sl = slice(h * HEAD_DIM, (h + 1) * HEAD_DIM)
            qh = q_ref[pl.ds(r0, band), sl].astype(F32)
            qh = (qh * lax.rsqrt(jnp.mean(qh * qh, axis=-1, keepdims=True) + RMS_EPS) * gq).astype(BF16)
            s = _dot_nt(qh, kcat_ref[pl.ds(r0, 2 * band), sl]) * scale
            s = jnp.where(mask, s, NEG_BIG)
            m = jnp.max(s, axis=1, keepdims=True)
            p = jnp.exp(s - m)
            l = jnp.sum(p, axis=1, keepdims=True)
            o = _dot(p.astype(BF16), vcat_ref[pl.ds(r0, 2 * band), sl]) / l
            o_ref[pl.ds(r0, band), sl] = o.astype(o_ref.dtype)
            lse_ref[pl.ds(r0, band), h:h + 1] = m + jnp.log(l)
        return carry

    lax.fori_loop(0, rows // band, sub_block, 0)


def _dilated_group(qkv, gq, gk, n_seq, len_seq, *, band=128, rows=512):
    t, w3 = qkv.shape
    width = w3 // 3
    n_heads = width // HEAD_DIM
    rows = min(rows, len_seq)
    nblk = len_seq // rows
    sub = rows // band

    def cur(c):
        return lambda s, n: (s * nblk + n, c)

    def prev(c):
        return lambda s, n: (s * nblk * sub + jnp.maximum(n * sub - 1, 0), c)

    body = functools.partial(_dil_body, band=band, rows=rows, n_heads=n_heads)
    return pl.pallas_call(
        body, grid=(n_seq, nblk),
        in_specs=[
            pl.BlockSpec((rows, width), cur(0)), pl.BlockSpec((rows, width), cur(1)),
            pl.BlockSpec((rows, width), cur(2)),
            pl.BlockSpec((band, width), prev(1)), pl.BlockSpec((band, width), prev(2)),
            pl.BlockSpec((1, HEAD_DIM), lambda s, n: (0, 0)), pl.BlockSpec((1, HEAD_DIM), lambda s, n: (0, 0)),
        ],
        out_specs=[pl.BlockSpec((rows, width), lambda s, n: (s * nblk + n, 0)),
                   pl.BlockSpec((rows, n_heads), lambda s, n: (s * nblk + n, 0))],
        out_shape=[jax.ShapeDtypeStruct((t, width), BF16), jax.ShapeDtypeStruct((t, n_heads), F32)],
        scratch_shapes=[pltpu.VMEM((band + rows, width), BF16), pltpu.VMEM((band + rows, width), BF16)],
        compiler_params=_params(2), name='dilated_attention',
    )(qkv, qkv, qkv, qkv, qkv, gq.reshape(1, HEAD_DIM), gk.reshape(1, HEAD_DIM))


def _dil_merge_body(o0_ref, o1_ref, o2_ref, l0_ref, l1_ref, l2_ref, out_ref):
    l0, l1, l2 = l0_ref[...], l1_ref[...], l2_ref[...]
    m = jnp.maximum(jnp.maximum(l0, l1), l2)
    e0, e1, e2 = jnp.exp(l0 - m), jnp.exp(l1 - m), jnp.exp(l2 - m)
    den = e0 + e1 + e2
    a0, a1, a2 = e0 / den, e1 / den, e2 / den
    for h in range(l0.shape[1]):
        sl = slice(h * HEAD_DIM, (h + 1) * HEAD_DIM)
        out_ref[:, sl] = (a0[:, h:h + 1] * o0_ref[:, sl].astype(F32)
                          + a1[:, h:h + 1] * o1_ref[:, sl].astype(F32)
                          + a2[:, h:h + 1] * o2_ref[:, sl].astype(F32)).astype(out_ref.dtype)


def _dil_merge(outs, lses, tm=512):
    t, width = outs[0].shape
    nh = lses[0].shape[1]
    ospec = pl.BlockSpec((tm, width), lambda i: (i, 0))
    lspec = pl.BlockSpec((tm, nh), lambda i: (i, 0))
    return pl.pallas_call(
        _dil_merge_body, grid=(t // tm,), in_specs=[ospec] * 3 + [lspec] * 3, out_specs=ospec,
        out_shape=jax.ShapeDtypeStruct((t, width), BF16), compiler_params=_params(1), name='dilated_merge',
    )(*outs, *lses)


def _to_classes(a, bsz, seq, dil):
    if dil == 1:
        return a
    c = a.shape[1]
    return a.reshape(bsz, seq // dil, dil, c).transpose(0, 2, 1, 3).reshape(bsz * seq, c)


def _from_classes(a, bsz, seq, dil):
    if dil == 1:
        return a
    c = a.shape[1]
    return a.reshape(bsz, dil, seq // dil, c).transpose(0, 2, 1, 3).reshape(bsz * seq, c)


def _lru_body(x_ref, y_ref, xp_ref, cw_ref, cb_ref, wr_ref, br_ref, wi_ref, bi_ref, lam_ref, o_ref,
              carry_ref, *, tm, tiles_per_batch):
    i = pl.program_id(0)
    first = (i % tiles_per_batch) == 0
    x = x_ref[...]
    prev = jnp.where(first, 0.0, xp_ref[...])
    xs = jnp.concatenate([prev, x], axis=0)
    xc = cb_ref[...] + cw_ref[CONV_WIDTH - 1:CONV_WIDTH, :] * x
    for j in range(CONV_WIDTH - 1):
        back = CONV_WIDTH - 1 - j
        xc = xc + cw_ref[j:j + 1, :] * xs[SUBLANES - back:SUBLANES - back + tm, :]
    xcb = xc.astype(BF16)
    r = jax.nn.sigmoid(_dot(xcb, wr_ref[...]) + br_ref[...])
    ig = jax.nn.sigmoid(_dot(xcb, wi_ref[...]) + bi_ref[...])
    lam = lam_ref[...]
    softplus_neg = jnp.maximum(-lam, 0.0) + jnp.log1p(jnp.exp(-jnp.abs(lam)))
    log_a = (-LRU_C * softplus_neg) * r
    a = jnp.exp(log_a)
    b = jnp.sqrt(1.0 - a * a) * (ig * xc)
    rows = lax.broadcasted_iota(jnp.int32, a.shape, 0)
    step = 1
    while step < tm:
        a_sh = pltpu.roll(a, step, 0)
        b_sh = pltpu.roll(b, step, 0)
        valid = rows >= step
        b = jnp.where(valid, a * b_sh + b, b)
        a = jnp.where(valid, a * a_sh, a)
        step *= 2
    h0 = jnp.where(first, 0.0, carry_ref[0:1, :])
    h = a * h0 + b
    carry_ref[...] = jnp.broadcast_to(h[tm - 1:tm, :], carry_ref.shape)
    o_ref[...] = (h * jax.nn.gelu(y_ref[...])).astype(o_ref.dtype)


def _block_diag(w):
    g, n, _ = w.shape
    eye = jnp.eye(g, dtype=w.dtype)
    return (eye[:, None, :, None] * w[:, :, None, :]).reshape(g * n, g * n)


def _rg_lru(pc, conv_w, conv_b, w_r, b_r, w_i, b_i, lam, seq, tm=256):
    t, w2 = pc.shape
    w = w2 // 2
    row = lambda v: v.reshape(1, w)
    full2 = lambda shape: pl.BlockSpec(shape, lambda i: (0, 0))
    body = functools.partial(_lru_body, tm=tm, tiles_per_batch=seq // tm)
    return pl.pallas_call(
        body, grid=(t // tm,),
        in_specs=[
            pl.BlockSpec((tm, w), lambda i: (i, 0)),
            pl.BlockSpec((tm, w), lambda i: (i, 1)),
            pl.BlockSpec((SUBLANES, w), lambda i: (jnp.maximum(i * (tm // SUBLANES) - 1, 0), 0)),
            full2((CONV_WIDTH, w)), full2((1, w)), full2((w, w)), full2((1, w)), full2((w, w)),
            full2((1, w)), full2((1, w)),
        ],
        out_specs=pl.BlockSpec((tm, w), lambda i: (i, 0)),
        out_shape=jax.ShapeDtypeStruct((t, w), BF16),
        scratch_shapes=[pltpu.VMEM((SUBLANES, w), F32)],
        compiler_params=_params(1), name='rg_lru',
    )(pc, pc, pc, conv_w, row(conv_b), _block_diag(w_r).astype(BF16), row(b_r),
      _block_diag(w_i).astype(BF16), row(b_i), row(lam))


def _branch_merge_body(oa_ref, ob_ref, oc_ref, wa_ref, wb_ref, wc_ref, ga_ref, gb_ref, gc_ref, o_ref):
    mixed = (ga_ref[...].astype(F32) * _dot(oa_ref[...], wa_ref[...])
             + gb_ref[...].astype(F32) * _dot(ob_ref[...], wb_ref[...])
             + gc_ref[...].astype(F32) * _dot(oc_ref[...], wc_ref[...]))
    o_ref[...] = mixed.astype(o_ref.dtype)


MOE_ROWS = 256
IDX_REC = 1024


def _router_body(a_ref, w_ref, b_ref, gsel_ref, eidx_ref, rank_ref, cnt_ref, carry_ref, *, n_experts, top_k, tm):
    i = pl.program_id(0)

    @pl.when(i == 0)
    def _():
        carry_ref[...] = jnp.zeros(carry_ref.shape, F32)

    logits = _dot(a_ref[...].astype(BF16), w_ref[...])
    lane = lax.broadcasted_iota(jnp.int32, logits.shape, 1).astype(F32)
    scores = jax.nn.sigmoid(logits)
    cur = jnp.where(lane < n_experts, scores + b_ref[...], -jnp.inf)
    mask = jnp.zeros(logits.shape, F32)
    hits, picked_scores = [], []
    for r in range(top_k):
        m = jnp.max(cur, axis=1, keepdims=True)
        idx = jnp.min(jnp.where(cur == m, lane, float(LANES)), axis=1, keepdims=True)
        hit = lane == idx
        hits.append(hit)
        picked_scores.append(jnp.sum(jnp.where(hit, scores, 0.0), axis=1, keepdims=True))
        eidx_ref[:, r:r + 1] = idx.astype(jnp.int32)
        cur = jnp.where(hit, -jnp.inf, cur)
        mask = mask + jnp.where(hit, 1.0, 0.0)
    total = picked_scores[0]
    for r in range(1, top_k):
        total = total + picked_scores[r]
    ri = lax.broadcasted_iota(jnp.int32, (tm, tm), 0)
    ci = lax.broadcasted_iota(jnp.int32, (tm, tm), 1)
    tri = jnp.where(ri > ci, 1.0, 0.0).astype(BF16)
    before = _dot(tri, mask.astype(BF16)) + carry_ref[0:1, :]
    for r in range(top_k):
        gsel_ref[:, r:r + 1] = picked_scores[r] / total * ROUTED_SCALE
        rank_ref[:, r:r + 1] = jnp.sum(jnp.where(hits[r], before, 0.0), axis=1, keepdims=True).astype(jnp.int32)
    carry_ref[...] = carry_ref[...] + jnp.sum(mask, axis=0, keepdims=True)
    cnt_ref[...] = carry_ref[...]


def _router(h2, w_router, b_router, tm=1024):
    t, d = h2.shape
    n_exp = w_router.shape[1]
    w_r = jnp.zeros((d, LANES), BF16).at[:, :n_exp].set(w_router.astype(BF16))
    b_r = jnp.zeros((1, LANES), F32).at[0, :n_exp].set(b_router.astype(F32))
    col = lambda dt: jax.ShapeDtypeStruct((t, MOE_TOP_K), dt)
    cspec = pl.BlockSpec((tm, MOE_TOP_K), lambda i: (i, 0))
    return pl.pallas_call(
        functools.partial(_router_body, n_experts=n_exp, top_k=MOE_TOP_K, tm=tm), grid=(t // tm,),
        in_specs=[pl.BlockSpec((tm, d), lambda i: (i, 0)), pl.BlockSpec((d, LANES), lambda i: (0, 0)),
                  pl.BlockSpec((1, LANES), lambda i: (0, 0))],
        out_specs=[cspec, cspec, cspec, pl.BlockSpec((SUBLANES, LANES), lambda i: (0, 0))],
        out_shape=[col(F32), col(jnp.int32), col(jnp.int32), jax.ShapeDtypeStruct((SUBLANES, LANES), F32)],
        scratch_shapes=[pltpu.VMEM((SUBLANES, LANES), F32)],
        compiler_params=_params(1), name='router')(h2, w_r, b_r)


def _moe_group_body(te_ref, nu_ref, idx_hbm, h_hbm, w1_ref, w3_ref, w2_ref, y_hbm,
                    idx_smem, xbuf, ybuf, isem, gsem, ssem, *, tmr, dump_row, n_tiles):
    i = pl.program_id(0)
    n = nu_ref[0]
    d = xbuf.shape[-1]
    n_groups = tmr // SUBLANES
    n_pieces = n_groups // 2
    kc = d // n_pieces

    def idx_copy(tile, s3):
        tile = jnp.minimum(tile, n_tiles - 1)
        return pltpu.make_async_copy(idx_hbm.at[pl.ds(pl.multiple_of(tile * IDX_REC, IDX_REC), IDX_REC)],
                                     idx_smem.at[pl.ds(pl.multiple_of(s3 * IDX_REC, IDX_REC), IDX_REC)],
                                     isem.at[s3])

    def split(row):
        return lax.shift_right_logical(row, 3), row & (SUBLANES - 1)

    def row_in(tok, g, u, s):
        hi, lo = split(tok)
        return pltpu.make_async_copy(h_hbm.at[hi, pl.ds(lo, 1)], xbuf.at[s, g, pl.ds(u, 1)], gsem.at[s])

    def row_out(dst, g, u, s):
        hi, lo = split(dst)
        return pltpu.make_async_copy(ybuf.at[s, g, pl.ds(u, 1)], y_hbm.at[hi, pl.ds(lo, 1)], ssem.at[s])

    def gather_group(g, s3, s):
        for u in range(SUBLANES):
            row_in(idx_smem[s3 * IDX_REC + g * SUBLANES + u], g, u, s).start()

    def scatter_group(g, s3, s):
        for u in range(SUBLANES):
            row_out(idx_smem[s3 * IDX_REC + tmr + g * SUBLANES + u], g, u, s).start()

    def per_group(fn):
        def body(g, c):
            fn(g)
            return c
        lax.fori_loop(0, n_groups, body, 0)

    def gather_wait(s):
        per_group(lambda g: [row_in(jnp.int32(0), g, u, s).wait() for u in range(SUBLANES)])

    def scatter_wait(s):
        per_group(lambda g: [row_out(jnp.int32(dump_row), g, u, s).wait() for u in range(SUBLANES)])

    @pl.when(i == 0)
    def _prologue():
        ybuf[1] = jnp.zeros(ybuf.shape[1:], ybuf.dtype)

        def fill(r, c):
            idx_smem[2 * IDX_REC + tmr + r] = dump_row + r
            return c
        lax.fori_loop(0, tmr, fill, 0)
        idx_copy(0, 0).start()
        idx_copy(0, 0).wait()
        idx_copy(1, 1).start()
        per_group(lambda g: gather_group(g, 0, 0))

    @pl.when((i > 0) & (i < n))
    def _():
        scatter_wait(i % 2)

    @pl.when(i < n)
    def _tile():
        s, s_next, s_prev = i % 2, (i + 1) % 2, (i + 1) % 2
        r_next, r_prev = (i + 1) % 3, (i + 2) % 3
        idx_copy(i + 1, r_next).wait()
        gather_wait(s)
        h1 = jnp.zeros((tmr, w1_ref.shape[-1]), F32)
        h3 = jnp.zeros((tmr, w1_ref.shape[-1]), F32)
        for c in range(n_pieces):
            gather_group(c, r_next, s_next)
            scatter_group(c, r_prev, s_prev)
            xc = xbuf[s, :, :, c * kc:(c + 1) * kc].reshape(tmr, kc).astype(BF16)
            h1 = h1 + _dot(xc, w1_ref[c * kc:(c + 1) * kc, :].astype(BF16))
            h3 = h3 + _dot(xc, w3_ref[c * kc:(c + 1) * kc, :].astype(BF16))
        hid = ((h1 * jax.nn.sigmoid(h1)) * h3).astype(BF16)
        for c in range(n_pieces):
            gather_group(n_pieces + c, r_next, s_next)
            scatter_group(n_pieces + c, r_prev, s_prev)
            yc = _dot(hid, w2_ref[:, c * kc:(c + 1) * kc].astype(BF16))
            ybuf[s, :, :, c * kc:(c + 1) * kc] = yc.reshape(n_groups, SUBLANES, kc)
        idx_copy(i + 2, r_prev).start()

    @pl.when(i == n)
    def _drain():
        scatter_wait(i % 2)
        per_group(lambda g: scatter_group(g, (i + 2) % 3, (i + 1) % 2))
        scatter_wait((i + 1) % 2)
        gather_wait(i % 2)
        idx_copy(i + 1, (i + 1) % 3).wait()


def _moe_grouped(h2, idx, tile_expert, n_used, w1, w3, w2, layer, n_slots):
    t, d = h2.shape
    n_tiles = idx.shape[0] // IDX_REC
    tmr = MOE_ROWS
    f = w1.shape[3]
    dump_row = n_slots * t
    n_rows = n_slots * t + tmr
    assert n_tiles >= 3
    wmap = lambda i, te, nu: (layer, te[i], 0, 0)
    grid_spec = pltpu.PrefetchScalarGridSpec(
        num_scalar_prefetch=2, grid=(n_tiles + 1,),
        in_specs=[
            pl.BlockSpec(memory_space=pl.ANY), pl.BlockSpec(memory_space=pl.ANY),
            pl.BlockSpec((None, None, d, f), wmap), pl.BlockSpec((None, None, d, f), wmap),
            pl.BlockSpec((None, None, f, d), wmap),
        ],
        out_specs=pl.BlockSpec(memory_space=pl.ANY),
        scratch_shapes=[
            pltpu.SMEM((3 * IDX_REC,), jnp.int32),
            pltpu.VMEM((2, tmr // SUBLANES, SUBLANES, d), F32), pltpu.VMEM((2, tmr // SUBLANES, SUBLANES, d), F32),
            pltpu.SemaphoreType.DMA((3,)), pltpu.SemaphoreType.DMA((2,)), pltpu.SemaphoreType.DMA((2,)),
        ])
    y = pl.pallas_call(
        functools.partial(_moe_group_body, tmr=tmr, dump_row=dump_row, n_tiles=n_tiles), grid_spec=grid_spec,
        out_shape=jax.ShapeDtypeStruct((n_rows // SUBLANES, SUBLANES, d), F32),
        compiler_params=_params(1), name='moe_grouped',
    )(tile_expert, n_used, idx, h2.reshape(t // SUBLANES, SUBLANES, d), w1, w3, w2)
    return y.reshape(n_rows, d)


def _moe_plan(eidx, rank, counts, n_experts):
    t, k = eidx.shape
    tmr = MOE_ROWS
    n_tiles = (t * k) // tmr + n_experts
    padded = ((counts + tmr - 1) // tmr) * tmr
    ends = jnp.cumsum(padded)
    offsets = ends - padded
    n_used = (ends[-1] // tmr).astype(jnp.int32)
    pos = (offsets[eidx] + rank).reshape(-1)
    dst = (jnp.arange(k, dtype=jnp.int32)[None, :] * t + jnp.arange(t, dtype=jnp.int32)[:, None]).reshape(-1)
    p = n_tiles * tmr
    dst_sorted = (k * t + jnp.arange(p, dtype=jnp.int32) % tmr).at[pos].set(dst).reshape(n_tiles, tmr)
    tok_sorted = dst_sorted % t
    idx = jnp.concatenate([tok_sorted, dst_sorted, jnp.zeros((n_tiles, IDX_REC - 2 * tmr), jnp.int32)],
                          axis=1).reshape(-1)
    tile_start = jnp.minimum(jnp.arange(n_tiles + 1, dtype=jnp.int32), n_used - 1) * tmr
    tile_expert = jnp.sum((ends[None, :] <= tile_start[:, None]).astype(jnp.int32), axis=1)
    return idx, jnp.minimum(tile_expert, n_experts - 1), n_used.reshape(1)


def _shared_expert_body(a_ref, w1_ref, w3_ref, w2_ref, o_ref):
    a = a_ref[...].astype(BF16)
    h1 = _dot(a, w1_ref[...])
    hid = (h1 * jax.nn.sigmoid(h1)) * _dot(a, w3_ref[...])
    o_ref[...] = _dot(hid.astype(BF16), w2_ref[...])


def _shared_expert(h2, w1, w3, w2, tm=512):
    t, d = h2.shape
    f = w1.shape[1]
    full = lambda shape: pl.BlockSpec(shape, lambda i: (0, 0))
    return pl.pallas_call(
        _shared_expert_body, grid=(t // tm,),
        in_specs=[pl.BlockSpec((tm, d), lambda i: (i, 0)), full((d, f)), full((d, f)), full((f, d))],
        out_specs=pl.BlockSpec((tm, d), lambda i: (i, 0)),
        out_shape=jax.ShapeDtypeStruct((t, d), F32), compiler_params=_params(1), name='shared_expert',
    )(h2, w1.astype(BF16), w3.astype(BF16), w2.astype(BF16))


def _moe_combine_body(x_ref, g_ref, gsel_ref, ysh_ref, *refs):
    y_refs, o_ref = refs[:-1], refs[-1]
    gs = gsel_ref[...]
    acc = ysh_ref[...]
    for r, y_ref in enumerate(y_refs):
        acc = acc + gs[:, r:r + 1] * y_ref[...]
    o_ref[...] = x_ref[...] + g_ref[...] * acc


def _moe_combine(x2, g2, gsel, y_shared, y, seq, tm=256, tn=1024):
    t, d = x2.shape
    bsz = t // seq
    k = gsel.shape[1]
    tile = pl.BlockSpec((tm, tn), lambda i, j: (i, j))
    slot_specs = [pl.BlockSpec((tm, tn), lambda i, j, r=r: (r * (t // tm) + i, j)) for r in range(k)]
    return pl.pallas_call(
        _moe_combine_body, grid=(t // tm, d // tn),
        in_specs=[tile, pl.BlockSpec((None, 1, tn), lambda i, j: (i // (seq // tm), 0, j)),
                  pl.BlockSpec((tm, k), lambda i, j: (i, 0)), tile] + slot_specs,
        out_specs=tile, out_shape=jax.ShapeDtypeStruct((t, d), F32),
        compiler_params=_params(2), name='moe_combine',
    )(x2, g2.reshape(bsz, 1, d), gsel, y_shared, *([y] * k))


def kernel(x, c, w_ada, b_ada, ada_layer, norm_mix, norm_ffn, w_in, q_norm_a, k_norm_a, q_norm_b,
           k_norm_b, conv_w, conv_b, w_rgate, b_rgate, w_igate, b_igate, lru_lambda, w_branch_a,
           w_branch_b, w_branch_c, w_out, w_router, b_router, w1_exp, w3_exp, w2_exp, w1_shared,
           w3_shared, w2_shared):
    bsz, seq, d = x.shape
    t = bsz * seq
    depth = w_in.shape[0]
    mixw = d // 4
    a_cols = 3 * mixw + IDX_HEADS * IDX_DIM + IDX_DIM + IDX_HEADS
    off_b = a_cols
    off_c = off_b + N_DIL * 3 * mixw
    off_g = off_c + 2 * mixw
    top_k = min(DSA_TOPK_MAX, seq // 4)
    tm = 1024 if t % 1024 == 0 else 256
    tpb = seq // tm

    mod_shared = _ada_mod(c, w_ada, b_ada).reshape(bsz, N_MOD, d)
    x2 = x.reshape(t, d)
    for l in range(depth):
        mod = mod_shared + ada_layer[l]
        sh1, sc1, g1, sh2, sc2, g2 = [mod[:, j] for j in range(N_MOD)]
        h = _norm_mod(x2, norm_mix[l], sc1, sh1, seq)
        wl = w_in[l]

        n_main = 3 * mixw + IDX_HEADS * IDX_DIM
        pa = _matmul(h, wl[:, :n_main].astype(BF16), BF16, tm=tm, tn=512, name='proj_a')
        w_small = jnp.zeros((d, LANES), BF16).at[:, :IDX_DIM + IDX_HEADS].set(
            wl[:, n_main:a_cols].astype(BF16))
        small = _matmul(h, w_small, F32, tm=tm, tn=LANES, name='proj_a_idx')
        ik = small[:, :IDX_DIM].astype(BF16)
        zeros = jnp.zeros_like(ik)
        ik2 = jnp.concatenate([ik, zeros, zeros, ik], axis=1)
        qn, kn = _qk_norm(pa, q_norm_a[l], k_norm_a[l], mixw)
        iwt = small[:, IDX_DIM:IDX_DIM + IDX_HEADS].reshape(bsz, seq, IDX_HEADS).transpose(0, 2, 1)
        vt = pa[:, 2 * mixw:3 * mixw].reshape(bsz, seq, mixw // HEAD_DIM, HEAD_DIM).transpose(0, 2, 3, 1)
        o_a = _dsa_attention(qn, kn, pa, ik2, iwt, vt, seq, top_k)

        pb = _matmul(h, wl[:, off_b:off_c].astype(BF16), BF16, tm=tm, tn=512, name='proj_b')
        outs, lses = [], []
        for g, (window, dil) in enumerate(DIL_PATTERNS):
            qkv = _to_classes(pb[:, g * 3 * mixw:(g + 1) * 3 * mixw], bsz, seq, dil)
            og, lg = _dilated_group(qkv, q_norm_b[l, g], k_norm_b[l, g], bsz * dil, seq // dil,
                                    band=window // dil)
            outs.append(_from_classes(og, bsz, seq, dil))
            lses.append(_from_classes(lg, bsz, seq, dil))
        o_b = _dil_merge(outs, lses)

        pc = _matmul(h, wl[:, off_c:off_g].astype(BF16), F32, tm=tm, tn=512, name='proj_c')
        o_c = _rg_lru(pc, conv_w[l], conv_b[l], w_rgate[l], b_rgate[l], w_igate[l], b_igate[l],
                      lru_lambda[l], seq)

        sg = _matmul(h, wl[:, off_g:].astype(BF16), BF16, tm=tm, tn=512, body=_mm_sigmoid_body,
                     name='proj_gates')
        nd = d // 512
        mixed = _tiled_call(
            _branch_merge_body, t, d, tm, 512,
            [(o_a, 'row'), (o_b, 'row'), (o_c, 'row'),
             (w_branch_a[l].astype(BF16), 'col'), (w_branch_b[l].astype(BF16), 'col'),
             (w_branch_c[l].astype(BF16), 'col'),
             (sg, ('tileoff', 0)), (sg, ('tileoff', nd)), (sg, ('tileoff', 2 * nd))],
            [(d, BF16)], name='branch_merge')
        x2 = _tiled_call(
            _mm_residual_body, t, d, tm, 512,
            [(mixed, 'row'), (w_out[l].astype(BF16), 'col'), (x2, 'tile'), (g1.reshape(bsz, 1, d), 'bvec')],
            [(d, F32)], tiles_per_batch=tpb, name='out_proj')

        h2 = _norm_mod(x2, norm_ffn[l], sc2, sh2, seq, out_dtype=F32)
        gsel, eidx, rank, cnt = _router(h2, w_router[l], b_router[l])
        idx, tile_expert, n_used = _moe_plan(eidx, rank, cnt[0, :N_EXPERTS].astype(jnp.int32), N_EXPERTS)
        y = _moe_grouped(h2, idx, tile_expert, n_used, w1_exp, w3_exp, w2_exp, l, MOE_TOP_K)
        y_shared = _shared_expert(h2, w1_shared[l], w3_shared[l], w2_shared[l])
        x2 = _moe_combine(x2, g2, gsel, y_shared, y, seq)
    return x2.reshape(bsz, seq, d)
```

```python
import functools

import jax
import jax.numpy as jnp
from jax import lax
from jax.experimental import pallas as pl
from jax.experimental.pallas import tpu as pltpu

HEAD_DIM = 128
A_HEADS = 8
IDX_HEADS = 16
IDX_DIM = 64
DSA_TOPK_MAX = 256
DIL_PATTERNS = ((128, 1), (512, 4), (2048, 16))
N_DIL = 3
B_HEADS = 8
LRU_BLOCKS = 16
CONV_WIDTH = 4
LRU_C = 8.0
N_EXPERTS = 64
MOE_TOP_K = 8
ROUTED_SCALE = 2.5
N_MOD = 6
RMS_EPS = 1e-6

LANES = 128
SUBLANES = 8
VMEM_LIMIT_BYTES = 56 * 1024 * 1024

NEG_BIG = -1e30
LOG2_E = 1.4426950408889634
INT_MIN = -(2 ** 31)

BF16 = jnp.bfloat16
F32 = jnp.float32


def _params(n_axes):
    return pltpu.CompilerParams(dimension_semantics=("arbitrary",) * n_axes,
                                vmem_limit_bytes=VMEM_LIMIT_BYTES)


def _dot(a, b):
    return jnp.dot(a, b, preferred_element_type=F32)


def _dot_nt(a, b):
    return lax.dot_general(a, b, (((1,), (1,)), ((), ())), preferred_element_type=F32)


def _tiled_call(body, m, n, tm, tn, ins, outs, *, tiles_per_batch=None, name=None):
    assert m % tm == 0 and n % tn == 0, (m, n, tm, tn)
    grid = (m // tm, n // tn)
    in_specs, arrays = [], []
    for arr, kind in ins:
        arrays.append(arr)
        if kind == 'row':
            in_specs.append(pl.BlockSpec((tm, arr.shape[1]), lambda i, j: (i, 0)))
        elif isinstance(kind, tuple) and kind[0] == 'rowoff':
            _, off, width = kind
            in_specs.append(pl.BlockSpec((tm, width), lambda i, j, off=off: (i, off)))
        elif kind == 'col':
            in_specs.append(pl.BlockSpec((arr.shape[0], tn), lambda i, j: (0, j)))
        elif kind == 'col3':
            in_specs.append(pl.BlockSpec((None, arr.shape[1], arr.shape[2]), lambda i, j: (j, 0, 0)))
        elif kind == 'tile':
            in_specs.append(pl.BlockSpec((tm, tn), lambda i, j: (i, j)))
        elif isinstance(kind, tuple) and kind[0] == 'tileoff':
            in_specs.append(pl.BlockSpec((tm, tn), lambda i, j, off=kind[1]: (i, j + off)))
        elif kind == 'vec':
            in_specs.append(pl.BlockSpec((1, tn), lambda i, j: (0, j)))
        elif kind == 'bvec':
            tpb = tiles_per_batch
            in_specs.append(pl.BlockSpec((None, 1, tn), lambda i, j, tpb=tpb: (i // tpb, 0, j)))
        elif kind == 'full':
            nd = arr.ndim
            in_specs.append(pl.BlockSpec(arr.shape, lambda i, j, nd=nd: (0,) * nd))
        else:
            raise ValueError(kind)
    out_shape, out_specs = [], []
    for n_cols, dtype in outs:
        assert (n_cols * tn) % n == 0
        w = n_cols * tn // n
        out_shape.append(jax.ShapeDtypeStruct((m, n_cols), dtype))
        out_specs.append(pl.BlockSpec((tm, w), lambda i, j: (i, j)))
    single = len(outs) == 1
    res = pl.pallas_call(
        body, grid=grid, in_specs=in_specs,
        out_specs=out_specs[0] if single else out_specs,
        out_shape=out_shape[0] if single else out_shape,
        compiler_params=_params(2), name=name)(*arrays)
    return res


def _mm_body(a_ref, w_ref, o_ref):
    o_ref[...] = _dot(a_ref[...], w_ref[...]).astype(o_ref.dtype)


def _mm_sigmoid_body(a_ref, w_ref, o_ref):
    o_ref[...] = jax.nn.sigmoid(_dot(a_ref[...], w_ref[...])).astype(o_ref.dtype)


def _mm_residual_body(a_ref, w_ref, x_ref, g_ref, o_ref):
    o_ref[...] = x_ref[...] + g_ref[...] * _dot(a_ref[...], w_ref[...])


def _matmul(a, w, out_dtype, *, tm, tn, body=_mm_body, name=None):
    return _tiled_call(body, a.shape[0], w.shape[1], tm, tn, [(a, 'row'), (w, 'col')],
                       [(w.shape[1], out_dtype)], name=name)


def _ada_body(c_ref, w_ref, b_ref, o_ref):
    c = c_ref[...]
    a = (c * jax.nn.sigmoid(c)).astype(BF16)
    o_ref[...] = _dot(a, w_ref[...].astype(BF16)) + b_ref[...]


def _ada_mod(c, w_ada, b_ada):
    bsz, d = c.shape
    n = w_ada.shape[1]
    c_pad = jnp.zeros((SUBLANES, d), F32).at[:bsz].set(c)
    out = _tiled_call(_ada_body, SUBLANES, n, SUBLANES, 512,
                      [(c_pad, 'row'), (w_ada, 'col'), (b_ada.reshape(1, n), 'vec')],
                      [(n, F32)], name='ada_mod')
    return out[:bsz]


def _norm_mod_body(x_ref, g_ref, sc_ref, sh_ref, o_ref):
    x = x_ref[...]
    y = x * lax.rsqrt(jnp.mean(x * x, axis=-1, keepdims=True) + RMS_EPS) * g_ref[...]
    o_ref[...] = (y * (1.0 + sc_ref[...]) + sh_ref[...]).astype(o_ref.dtype)


def _norm_mod(x2, g, sc, sh, seq, tm=256):
    t, d = x2.shape
    bsz = t // seq
    return _tiled_call(_norm_mod_body, t, d, tm, d,
                       [(x2, 'tile'), (g.reshape(1, d), 'vec'), (sc.reshape(bsz, 1, d), 'bvec'),
                        (sh.reshape(bsz, 1, d), 'bvec')],
                       [(d, BF16)], tiles_per_batch=seq // tm, name='norm_mod')


def _head_norm(x, g):
    outs = []
    for h in range(x.shape[1] // HEAD_DIM):
        xh = x[:, h * HEAD_DIM:(h + 1) * HEAD_DIM].astype(F32)
        outs.append(xh * lax.rsqrt(jnp.mean(xh * xh, axis=-1, keepdims=True) + RMS_EPS) * g)
    return jnp.concatenate(outs, axis=1)


def _qk_norm_body(q_ref, k_ref, gq_ref, gk_ref, qo_ref, ko_ref):
    qo_ref[...] = _head_norm(q_ref[...], gq_ref[...]).astype(qo_ref.dtype)
    ko_ref[...] = _head_norm(k_ref[...], gk_ref[...]).astype(ko_ref.dtype)


def _qk_norm(pa, gq, gk, width, tm=512):
    t = pa.shape[0]
    return _tiled_call(_qk_norm_body, t, width, tm, width,
                       [(pa, ('tileoff', 0)), (pa, ('tileoff', 1)),
                        (gq.reshape(1, HEAD_DIM), 'full'), (gk.reshape(1, HEAD_DIM), 'full')],
                       [(width, BF16), (width, BF16)], name='qk_norm_a')


def _dsa_body(q_ref, iq_ref, iwt_ref, ik_ref, k_ref, vt_ref, o_ref,
              keys_ref, thr_ref, bias_ref, s_ref, p_ref, m_ref, l_ref, acc_ref, *, tq, tk, top_k, n_heads):
    qb = pl.program_id(1)
    kb = pl.program_id(2)
    kb_last = ((qb + 1) * tq - 1) // tk
    n_chunks = kb_last + 1
    kpos = lax.broadcasted_iota(jnp.int32, (tk, tq), 0)
    qpos = lax.broadcasted_iota(jnp.int32, (tk, tq), 1) + qb * tq
    n_acc = 4 * SUBLANES

    @pl.when(kb == 0)
    def _scores_and_threshold():
        w = iwt_ref[...] * (IDX_HEADS ** -0.5 * IDX_DIM ** -0.5)

        def score_chunk(c, carry):
            ik2 = ik_ref[pl.ds(pl.multiple_of(c * tk, tk), tk), :]
            acc = jnp.zeros((tk, tq), F32)
            for p in range(IDX_HEADS // 2):
                iq_pair = iq_ref[:, p * LANES:(p + 1) * LANES]
                for half in range(2):
                    rel = _dot_nt(ik2[:, half * LANES:(half + 1) * LANES], iq_pair)
                    h = 2 * p + half
                    acc = acc + w[h:h + 1, :] * jnp.maximum(rel, 0.0)
            bits = pltpu.bitcast(acc, jnp.int32)
            key = jnp.where(bits < 0, bits ^ 0x7FFFFFFF, bits)
            key = jnp.where(kpos + c * tk <= qpos, key, INT_MIN)
            keys_ref[c] = key
            return carry

        lax.fori_loop(0, n_chunks, score_chunk, 0)

        def count_ge(cand):
            def body(c, cnt):
                for j in range(tk // n_acc):
                    blk = keys_ref[c, j * n_acc:(j + 1) * n_acc, :]
                    cnt = cnt + jnp.where(blk >= cand, 1.0, 0.0)
                return cnt
            cnt = lax.fori_loop(0, n_chunks, body, jnp.zeros((n_acc, tq), F32))
            return jnp.sum(cnt, axis=0, keepdims=True)

        k_f = float(top_k)
        t0 = jnp.where(count_ge(jnp.zeros((1, tq), jnp.int32)) >= k_f, 0, INT_MIN).astype(jnp.int32)

        def bit_body(i, t):
            cand = t | lax.shift_left(jnp.int32(1), 30 - i)
            return jnp.where(count_ge(cand) >= k_f, cand, t)

        t = lax.fori_loop(0, 31, bit_body, t0)
        thr_ref[...] = jnp.broadcast_to(t, thr_ref.shape)
        m_ref[...] = jnp.full(m_ref.shape, NEG_BIG, F32)
        l_ref[...] = jnp.zeros(l_ref.shape, F32)
        acc_ref[...] = jnp.zeros(acc_ref.shape, F32)

    @pl.when(kb <= kb_last)
    def _attend():
        sel = (keys_ref[kb] >= thr_ref[0:1, :]) & (kpos + kb * tk <= qpos)
        bias_ref[...] = jnp.where(sel, 0.0, NEG_BIG)
        c = HEAD_DIM ** -0.5 * LOG2_E

        def raw_scores(h):
            sl = slice(h * HEAD_DIM, (h + 1) * HEAD_DIM)
            s_ref[h % 3] = _dot_nt(k_ref[:, sl], q_ref[:, sl])

        def softmax(h):
            s = s_ref[h % 3] + bias_ref[...]
            m_old = m_ref[h, 0:1, :]
            m_new = jnp.maximum(m_old, jnp.max(s, axis=0, keepdims=True))
            p = jnp.exp2((s - m_new) * c)
            alpha = jnp.exp2((m_old - m_new) * c)
            l_new = alpha * l_ref[h, 0:1, :] + jnp.sum(p, axis=0, keepdims=True)
            p_ref[h % 2] = p.astype(BF16)
            m_ref[h] = jnp.broadcast_to(m_new, (SUBLANES, tq))
            l_ref[h] = jnp.broadcast_to(l_new, (SUBLANES, tq))
            return alpha

        def accumulate(h, alpha):
            acc_ref[h] = alpha * acc_ref[h] + _dot(vt_ref[h], p_ref[h % 2])

        raw_scores(0)
        raw_scores(1)
        alphas = {0: softmax(0)}
        for h in range(n_heads):
            if h + 2 < n_heads:
                raw_scores(h + 2)
            if h + 1 < n_heads:
                alphas[h + 1] = softmax(h + 1)
            accumulate(h, alphas.pop(h))

    @pl.when(kb == kb_last)
    def _finish():
        for h in range(n_heads):
            sl = slice(h * HEAD_DIM, (h + 1) * HEAD_DIM)
            o_ref[:, sl] = (acc_ref[h] / l_ref[h, 0:1, :]).T.astype(o_ref.dtype)


def _dsa_attention(qn, kn, pa, ik2, iwt, vt, seq, top_k, *, tq=256, tk=512):
    t, width = qn.shape
    bsz = t // seq
    nq, nk = seq // tq, seq // tk
    n_heads = width // HEAD_DIM
    assert width == IDX_HEADS * IDX_DIM and pa.shape[1] == 4 * width
    iq_off = 3

    def last(qb):
        return ((qb + 1) * tq - 1) // tk

    body = functools.partial(_dsa_body, tq=tq, tk=tk, top_k=top_k, n_heads=n_heads)
    return pl.pallas_call(
        body, grid=(bsz, nq, nk),
        in_specs=[
            pl.BlockSpec((tq, width), lambda b, qb, kb: (b * nq + qb, 0)),
            pl.BlockSpec((tq, width), lambda b, qb, kb: (b * nq + qb, iq_off)),
            pl.BlockSpec((None, IDX_HEADS, tq), lambda b, qb, kb: (b, 0, qb)),
            pl.BlockSpec((seq, 2 * LANES), lambda b, qb, kb: (b, 0)),
            pl.BlockSpec((tk, width), lambda b, qb, kb: (b * nk + jnp.minimum(kb, last(qb)), 0)),
            pl.BlockSpec((None, n_heads, HEAD_DIM, tk),
                         lambda b, qb, kb: (b, 0, 0, jnp.minimum(kb, last(qb)))),
        ],
        out_specs=pl.BlockSpec((tq, width), lambda b, qb, kb: (b * nq + qb, 0)),
        out_shape=jax.ShapeDtypeStruct((t, width), BF16),
        scratch_shapes=[
            pltpu.VMEM((nk, tk, tq), jnp.int32),
            pltpu.VMEM((SUBLANES, tq), jnp.int32),
            pltpu.VMEM((tk, tq), F32),
            pltpu.VMEM((3, tk, tq), F32),
            pltpu.VMEM((2, tk, tq), BF16),
            pltpu.VMEM((n_heads, SUBLANES, tq), F32),
            pltpu.VMEM((n_heads, SUBLANES, tq), F32),
            pltpu.VMEM((n_heads, HEAD_DIM, tq), F32),
        ],
        compiler_params=_params(3), name='dsa_attention',
    )(qn, pa, iwt, ik2, kn, vt)


def _dil_body(q_ref, kc_ref, vc_ref, kp_ref, vp_ref, gq_ref, gk_ref, o_ref, lse_ref,
              kcat_ref, vcat_ref, *, band, rows, n_heads):
    n = pl.program_id(1)
    kcat_ref[0:band, :] = _head_norm(kp_ref[...], gk_ref[...]).astype(BF16)
    kcat_ref[band:band + rows, :] = _head_norm(kc_ref[...], gk_ref[...]).astype(BF16)
    vcat_ref[0:band, :] = vp_ref[...]
    vcat_ref[band:band + rows, :] = vc_ref[...]
    qi = lax.broadcasted_iota(jnp.int32, (band, 2 * band), 0)
    ki = lax.broadcasted_iota(jnp.int32, (band, 2 * band), 1)
    back = qi + band - ki
    in_band = (back >= 0) & (back <= band)
    scale = HEAD_DIM ** -0.5
    gq = gq_ref[...]

    def sub_block(j, carry):
        r0 = pl.multiple_of(j * band, band)
        mask = in_band & ((ki >= band) | (n * (rows // band) + j > 0))
        for h in range(n_heads):
            sl = slice(h * HEAD_DIM, (h + 1) * HEAD_DIM)
            qh = q_ref[pl.ds(r0, band), sl].astype(F32)
            qh = (qh * lax.rsqrt(jnp.mean(qh * qh, axis=-1, keepdims=True) + RMS_EPS) * gq).astype(BF16)
            s = _dot_nt(qh, kcat_ref[pl.ds(r0, 2 * band), sl]) * scale
            s = jnp.where(mask, s, NEG_BIG)
            m = jnp.max(s, axis=1, keepdims=True)
            p = jnp.exp(s - m)
            l = jnp.sum(p, axis=1, keepdims=True)
            o = _dot(p.astype(BF16), vcat_ref[pl.ds(r0, 2 * band), sl]) / l
            o_ref[pl.ds(r0, band), sl] = o.astype(o_ref.dtype)
            lse_ref[pl.ds(r0, band), h:h + 1] = m + jnp.log(l)
        return carry

    lax.fori_loop(0, rows // band, sub_block, 0)


def _dilated_group(qkv, gq, gk, n_seq, len_seq, *, band=128, rows=512):
    t, w3 = qkv.shape
    width = w3 // 3
    n_heads = width // HEAD_DIM
    rows = min(rows, len_seq)
    nblk = len_seq // rows
    sub = rows // band

    def cur(c):
        return lambda s, n: (s * nblk + n, c)

    def prev(c):
        return lambda s, n: (s * nblk * sub + jnp.maximum(n * sub - 1, 0), c)

    body = functools.partial(_dil_body, band=band, rows=rows, n_heads=n_heads)
    return pl.pallas_call(
        body, grid=(n_seq, nblk),
        in_specs=[
            pl.BlockSpec((rows, width), cur(0)), pl.BlockSpec((rows, width), cur(1)),
            pl.BlockSpec((rows, width), cur(2)),
            pl.BlockSpec((band, width), prev(1)), pl.BlockSpec((band, width), prev(2)),
            pl.BlockSpec((1, HEAD_DIM), lambda s, n: (0, 0)), pl.BlockSpec((1, HEAD_DIM), lambda s, n: (0, 0)),
        ],
        out_specs=[pl.BlockSpec((rows, width), lambda s, n: (s * nblk + n, 0)),
                   pl.BlockSpec((rows, n_heads), lambda s, n: (s * nblk + n, 0))],
        out_shape=[jax.ShapeDtypeStruct((t, width), BF16), jax.ShapeDtypeStruct((t, n_heads), F32)],
        scratch_shapes=[pltpu.VMEM((band + rows, width), BF16), pltpu.VMEM((band + rows, width), BF16)],
        compiler_params=_params(2), name='dilated_attention',
    )(qkv, qkv, qkv, qkv, qkv, gq.reshape(1, HEAD_DIM), gk.reshape(1, HEAD_DIM))


def _dil_merge_body(o0_ref, o1_ref, o2_ref, l0_ref, l1_ref, l2_ref, out_ref):
    l0, l1, l2 = l0_ref[...], l1_ref[...], l2_ref[...]
    m = jnp.maximum(jnp.maximum(l0, l1), l2)
    e0, e1, e2 = jnp.exp(l0 - m), jnp.exp(l1 - m), jnp.exp(l2 - m)
    den = e0 + e1 + e2
    a0, a1, a2 = e0 / den, e1 / den, e2 / den
    for h in range(l0.shape[1]):
        sl = slice(h * HEAD_DIM, (h + 1) * HEAD_DIM)
        out_ref[:, sl] = (a0[:, h:h + 1] * o0_ref[:, sl].astype(F32)
                          + a1[:, h:h + 1] * o1_ref[:, sl].astype(F32)
                          + a2[:, h:h + 1] * o2_ref[:, sl].astype(F32)).astype(out_ref.dtype)


def _dil_merge(outs, lses, tm=512):
    t, width = outs[0].shape
    nh = lses[0].shape[1]
    ospec = pl.BlockSpec((tm, width), lambda i: (i, 0))
    lspec = pl.BlockSpec((tm, nh), lambda i: (i, 0))
    return pl.pallas_call(
        _dil_merge_body, grid=(t // tm,), in_specs=[ospec] * 3 + [lspec] * 3, out_specs=ospec,
        out_shape=jax.ShapeDtypeStruct((t, width), BF16), compiler_params=_params(1), name='dilated_merge',
    )(*outs, *lses)


def _to_classes(a, bsz, seq, dil):
    if dil == 1:
        return a
    c = a.shape[1]
    return a.reshape(bsz, seq // dil, dil, c).transpose(0, 2, 1, 3).reshape(bsz * seq, c)


def _from_classes(a, bsz, seq, dil):
    if dil == 1:
        return a
    c = a.shape[1]
    return a.reshape(bsz, dil, seq // dil, c).transpose(0, 2, 1, 3).reshape(bsz * seq, c)


def _lru_body(x_ref, y_ref, xp_ref, cw_ref, cb_ref, wr_ref, br_ref, wi_ref, bi_ref, lam_ref, o_ref,
              carry_ref, *, tm, tiles_per_batch):
    i = pl.program_id(0)
    first = (i % tiles_per_batch) == 0
    x = x_ref[...]
    prev = jnp.where(first, 0.0, xp_ref[...])
    xs = jnp.concatenate([prev, x], axis=0)
    xc = cb_ref[...] + cw_ref[CONV_WIDTH - 1:CONV_WIDTH, :] * x
    for j in range(CONV_WIDTH - 1):
        back = CONV_WIDTH - 1 - j
        xc = xc + cw_ref[j:j + 1, :] * xs[SUBLANES - back:SUBLANES - back + tm, :]
    xcb = xc.astype(BF16)
    r = jax.nn.sigmoid(_dot(xcb, wr_ref[...]) + br_ref[...])
    ig = jax.nn.sigmoid(_dot(xcb, wi_ref[...]) + bi_ref[...])
    lam = lam_ref[...]
    softplus_neg = jnp.maximum(-lam, 0.0) + jnp.log1p(jnp.exp(-jnp.abs(lam)))
    log_a = (-LRU_C * softplus_neg) * r
    a = jnp.exp(log_a)
    b = jnp.sqrt(1.0 - a * a) * (ig * xc)
    rows = lax.broadcasted_iota(jnp.int32, a.shape, 0)
    step = 1
    while step < tm:
        a_sh = pltpu.roll(a, step, 0)
        b_sh = pltpu.roll(b, step, 0)
        valid = rows >= step
        b = jnp.where(valid, a * b_sh + b, b)
        a = jnp.where(valid, a * a_sh, a)
        step *= 2
    h0 = jnp.where(first, 0.0, carry_ref[0:1, :])
    h = a * h0 + b
    carry_ref[...] = jnp.broadcast_to(h[tm - 1:tm, :], carry_ref.shape)
    o_ref[...] = (h * jax.nn.gelu(y_ref[...])).astype(o_ref.dtype)


def _block_diag(w):
    g, n, _ = w.shape
    eye = jnp.eye(g, dtype=w.dtype)
    return (eye[:, None, :, None] * w[:, :, None, :]).reshape(g * n, g * n)


def _rg_lru(pc, conv_w, conv_b, w_r, b_r, w_i, b_i, lam, seq, tm=256):
    t, w2 = pc.shape
    w = w2 // 2
    row = lambda v: v.reshape(1, w)
    full2 = lambda shape: pl.BlockSpec(shape, lambda i: (0, 0))
    body = functools.partial(_lru_body, tm=tm, tiles_per_batch=seq // tm)
    return pl.pallas_call(
        body, grid=(t // tm,),
        in_specs=[
            pl.BlockSpec((tm, w), lambda i: (i, 0)),
            pl.BlockSpec((tm, w), lambda i: (i, 1)),
            pl.BlockSpec((SUBLANES, w), lambda i: (jnp.maximum(i * (tm // SUBLANES) - 1, 0), 0)),
            full2((CONV_WIDTH, w)), full2((1, w)), full2((w, w)), full2((1, w)), full2((w, w)),
            full2((1, w)), full2((1, w)),
        ],
        out_specs=pl.BlockSpec((tm, w), lambda i: (i, 0)),
        out_shape=jax.ShapeDtypeStruct((t, w), BF16),
        scratch_shapes=[pltpu.VMEM((SUBLANES, w), F32)],
        compiler_params=_params(1), name='rg_lru',
    )(pc, pc, pc, conv_w, row(conv_b), _block_diag(w_r).astype(BF16), row(b_r),
      _block_diag(w_i).astype(BF16), row(b_i), row(lam))


def _branch_merge_body(oa_ref, ob_ref, oc_ref, wa_ref, wb_ref, wc_ref, ga_ref, gb_ref, gc_ref, o_ref):
    mixed = (ga_ref[...].astype(F32) * _dot(oa_ref[...], wa_ref[...])
             + gb_ref[...].astype(F32) * _dot(ob_ref[...], wb_ref[...])
             + gc_ref[...].astype(F32) * _dot(oc_ref[...], wc_ref[...]))
    o_ref[...] = mixed.astype(o_ref.dtype)


MOE_ROWS = 256
IDX_REC = 1024


def _router_body(a_ref, w_ref, b_ref, gsel_ref, eidx_ref, rank_ref, cnt_ref, carry_ref, *, n_experts, top_k, tm):
    i = pl.program_id(0)

    @pl.when(i == 0)
    def _():
        carry_ref[...] = jnp.zeros(carry_ref.shape, F32)

    logits = _dot(a_ref[...], w_ref[...])
    lane = lax.broadcasted_iota(jnp.int32, logits.shape, 1).astype(F32)
    scores = jax.nn.sigmoid(logits)
    cur = jnp.where(lane < n_experts, scores + b_ref[...], -jnp.inf)
    mask = jnp.zeros(logits.shape, F32)
    hits, picked_scores = [], []
    for r in range(top_k):
        m = jnp.max(cur, axis=1, keepdims=True)
        idx = jnp.min(jnp.where(cur == m, lane, float(LANES)), axis=1, keepdims=True)
        hit = lane == idx
        hits.append(hit)
        picked_scores.append(jnp.sum(jnp.where(hit, scores, 0.0), axis=1, keepdims=True))
        eidx_ref[:, r:r + 1] = idx.astype(jnp.int32)
        cur = jnp.where(hit, -jnp.inf, cur)
        mask = mask + jnp.where(hit, 1.0, 0.0)
    total = picked_scores[0]
    for r in range(1, top_k):
        total = total + picked_scores[r]
    ri = lax.broadcasted_iota(jnp.int32, (tm, tm), 0)
    ci = lax.broadcasted_iota(jnp.int32, (tm, tm), 1)
    tri = jnp.where(ri > ci, 1.0, 0.0).astype(BF16)
    before = _dot(tri, mask.astype(BF16)) + carry_ref[0:1, :]
    for r in range(top_k):
        gsel_ref[:, r:r + 1] = picked_scores[r] / total * ROUTED_SCALE
        rank_ref[:, r:r + 1] = jnp.sum(jnp.where(hits[r], before, 0.0), axis=1, keepdims=True).astype(jnp.int32)
    carry_ref[...] = carry_ref[...] + jnp.sum(mask, axis=0, keepdims=True)
    cnt_ref[...] = carry_ref[...]


def _router(h2, w_router, b_router, tm=1024):
    t, d = h2.shape
    n_exp = w_router.shape[1]
    w_r = jnp.zeros((d, LANES), BF16).at[:, :n_exp].set(w_router.astype(BF16))
    b_r = jnp.zeros((1, LANES), F32).at[0, :n_exp].set(b_router.astype(F32))
    col = lambda dt: jax.ShapeDtypeStruct((t, MOE_TOP_K), dt)
    cspec = pl.BlockSpec((tm, MOE_TOP_K), lambda i: (i, 0))
    return pl.pallas_call(
        functools.partial(_router_body, n_experts=n_exp, top_k=MOE_TOP_K, tm=tm), grid=(t // tm,),
        in_specs=[pl.BlockSpec((tm, d), lambda i: (i, 0)), pl.BlockSpec((d, LANES), lambda i: (0, 0)),
                  pl.BlockSpec((1, LANES), lambda i: (0, 0))],
        out_specs=[cspec, cspec, cspec, pl.BlockSpec((SUBLANES, LANES), lambda i: (0, 0))],
        out_shape=[col(F32), col(jnp.int32), col(jnp.int32), jax.ShapeDtypeStruct((SUBLANES, LANES), F32)],
        scratch_shapes=[pltpu.VMEM((SUBLANES, LANES), F32)],
        compiler_params=_params(1), name='router')(h2, w_r, b_r)


def _pack_bf16_pairs(lo, hi):
    lo_bits = pltpu.bitcast(lo.astype(BF16).astype(F32), jnp.uint32)
    hi_bits = pltpu.bitcast(hi.astype(BF16).astype(F32), jnp.uint32)
    return (hi_bits & jnp.uint32(0xFFFF0000)) | lax.shift_right_logical(lo_bits, jnp.uint32(16))


def _unpack_lo(u):
    return pltpu.bitcast(lax.shift_left(u, jnp.uint32(16)), F32)


def _unpack_hi(u):
    return pltpu.bitcast(u & jnp.uint32(0xFFFF0000), F32)


def _norm_mod_pack_body(x_ref, g_ref, sc_ref, sh_ref, o_ref, p_ref):
    x = x_ref[...]
    y = x * lax.rsqrt(jnp.mean(x * x, axis=-1, keepdims=True) + RMS_EPS) * g_ref[...]
    y = y * (1.0 + sc_ref[...]) + sh_ref[...]
    half = y.shape[1] // 2
    o_ref[...] = y.astype(o_ref.dtype)
    p_ref[...] = _pack_bf16_pairs(y[:, :half], y[:, half:])


def _norm_mod_pack(x2, g, sc, sh, seq, tm=256):
    t, d = x2.shape
    bsz = t // seq
    row = pl.BlockSpec((tm, d), lambda i: (i, 0))
    vec = pl.BlockSpec((1, d), lambda i: (0, 0))
    bvec = pl.BlockSpec((None, 1, d), lambda i: (i // (seq // tm), 0, 0))
    return pl.pallas_call(
        _norm_mod_pack_body, grid=(t // tm,), in_specs=[row, vec, bvec, bvec],
        out_specs=[row, pl.BlockSpec((tm, d // 2), lambda i: (i, 0))],
        out_shape=[jax.ShapeDtypeStruct((t, d), BF16), jax.ShapeDtypeStruct((t, d // 2), jnp.uint32)],
        compiler_params=_params(1), name='norm_mod_pack',
    )(x2, g.reshape(1, d), sc.reshape(bsz, 1, d), sh.reshape(bsz, 1, d))


def _moe_group_body(te_ref, nu_ref, idx_hbm, h_hbm, w1_ref, w3_ref, w2_ref, y_hbm,
                    idx_smem, xbuf, ybuf, isem, gsem, ssem, *, tmr, dump_row, n_tiles):
    i = pl.program_id(0)
    n = nu_ref[0]
    dh = xbuf.shape[-1]
    n_groups = tmr // SUBLANES
    n_up = n_groups // 2
    n_down = n_up // 2
    kc = 2 * dh // n_up

    def idx_copy(tile, rec):
        tile = jnp.minimum(tile, n_tiles - 1)
        return pltpu.make_async_copy(idx_hbm.at[pl.ds(pl.multiple_of(tile * IDX_REC, IDX_REC), IDX_REC)],
                                     idx_smem.at[pl.ds(pl.multiple_of(rec * IDX_REC, IDX_REC), IDX_REC)],
                                     isem.at[rec])

    def split(row):
        return lax.shift_right_logical(row, 3), row & (SUBLANES - 1)

    def row_in(tok, g, u, s):
        hi, lo = split(tok)
        return pltpu.make_async_copy(h_hbm.at[hi, pl.ds(lo, 1)], xbuf.at[s, g, pl.ds(u, 1)], gsem.at[s])

    def row_out(dst, g, u, s):
        hi, lo = split(dst)
        return pltpu.make_async_copy(ybuf.at[s, g, pl.ds(u, 1)], y_hbm.at[hi, pl.ds(lo, 1)], ssem.at[s])

    def gather_group(g, rec, s):
        for u in range(SUBLANES):
            row_in(idx_smem[rec * IDX_REC + g * SUBLANES + u], g, u, s).start()

    def scatter_group(g, rec, s):
        for u in range(SUBLANES):
            row_out(idx_smem[rec * IDX_REC + tmr + g * SUBLANES + u], g, u, s).start()

    def per_group(fn):
        def body(g, c):
            fn(g)
            return c
        lax.fori_loop(0, n_groups, body, 0)

    def gather_wait(s):
        per_group(lambda g: [row_in(jnp.int32(0), g, u, s).wait() for u in range(SUBLANES)])

    def scatter_wait(s):
        per_group(lambda g: [row_out(jnp.int32(dump_row), g, u, s).wait() for u in range(SUBLANES)])

    @pl.when(i == 0)
    def _prologue():
        ybuf[2] = jnp.zeros(ybuf.shape[1:], ybuf.dtype)
        for region in range(2):
            init = pltpu.make_async_copy(
                ybuf.at[2], y_hbm.at[pl.ds((dump_row + region * tmr) // SUBLANES, n_groups)], ssem.at[region])
            init.start()
            init.wait()

        def fill(r, c):
            idx_smem[4 * IDX_REC + tmr + r] = dump_row + 2 * tmr + r
            return c
        lax.fori_loop(0, tmr, fill, 0)
        for k in range(2):
            idx_copy(k, k).start()
            idx_copy(k, k).wait()
        idx_copy(2, 2).start()
        per_group(lambda g: gather_group(g, 0, 0))
        per_group(lambda g: gather_group(g, 1, 1))

    @pl.when((i >= 2) & (i < n))
    def _():
        scatter_wait(i % 3)

    @pl.when(i < n)
    def _tile():
        s, s_in, s_out = i % 3, (i + 2) % 3, (i + 2) % 3
        r_in, r_out = (i + 2) % 5, (i + 4) % 5
        idx_copy(i + 2, r_in).wait()
        gather_wait(s)
        h1 = jnp.zeros((tmr, w1_ref.shape[-1]), F32)
        h3 = jnp.zeros((tmr, w1_ref.shape[-1]), F32)
        for c in range(n_up):
            gather_group(2 * c, r_in, s_in)
            gather_group(2 * c + 1, r_in, s_in)
            pc = c % (n_up // 2)
            u = xbuf[s, :, :, pc * kc:(pc + 1) * kc].reshape(tmr, kc)
            xc = (_unpack_lo(u) if c < n_up // 2 else _unpack_hi(u)).astype(BF16)
            h1 = h1 + _dot(xc, w1_ref[c * kc:(c + 1) * kc, :].astype(BF16))
            h3 = h3 + _dot(xc, w3_ref[c * kc:(c + 1) * kc, :].astype(BF16))
        hid = ((h1 * jax.nn.sigmoid(h1)) * h3).astype(BF16)
        for c in range(n_down):
            for j in range(n_groups // n_down):
                scatter_group(c * (n_groups // n_down) + j, r_out, s_out)
            y_lo = _dot(hid, w2_ref[:, c * kc:(c + 1) * kc].astype(BF16))
            y_hi = _dot(hid, w2_ref[:, dh + c * kc:dh + (c + 1) * kc].astype(BF16))
            ybuf[s, :, :, c * kc:(c + 1) * kc] = _pack_bf16_pairs(y_lo, y_hi).reshape(n_groups, SUBLANES, kc)
        idx_copy(i + 3, (i + 3) % 5).start()

    @pl.when(i == n)
    def _drain():
        per_group(lambda g: scatter_group(g, (i + 4) % 5, (i + 2) % 3))
        scatter_wait(i % 3)
        scatter_wait((i + 1) % 3)
        scatter_wait((i + 2) % 3)
        gather_wait(i % 3)
        gather_wait((i + 1) % 3)
        idx_copy(i + 2, (i + 2) % 5).wait()


def _moe_grouped(h2p, idx, tile_expert, n_used, w1, w3, w2, layer, n_slots):
    t, dh = h2p.shape
    d = 2 * dh
    n_tiles = idx.shape[0] // IDX_REC
    tmr = MOE_ROWS
    f = w1.shape[3]
    dump_row = n_slots * t
    n_rows = n_slots * t + 3 * tmr
    assert n_tiles >= 4
    wmap = lambda i, te, nu: (layer, te[i], 0, 0)
    rows = lambda n_buf: pltpu.VMEM((n_buf, tmr // SUBLANES, SUBLANES, dh), jnp.uint32)
    grid_spec = pltpu.PrefetchScalarGridSpec(
        num_scalar_prefetch=2, grid=(n_tiles + 1,),
        in_specs=[
            pl.BlockSpec(memory_space=pl.ANY), pl.BlockSpec(memory_space=pl.ANY),
            pl.BlockSpec((None, None, d, f), wmap), pl.BlockSpec((None, None, d, f), wmap),
            pl.BlockSpec((None, None, f, d), wmap),
        ],
        out_specs=pl.BlockSpec(memory_space=pl.ANY),
        scratch_shapes=[
            pltpu.SMEM((5 * IDX_REC,), jnp.int32), rows(3), rows(3),
            pltpu.SemaphoreType.DMA((5,)), pltpu.SemaphoreType.DMA((3,)), pltpu.SemaphoreType.DMA((3,)),
        ])
    y = pl.pallas_call(
        functools.partial(_moe_group_body, tmr=tmr, dump_row=dump_row, n_tiles=n_tiles), grid_spec=grid_spec,
        out_shape=jax.ShapeDtypeStruct((n_rows // SUBLANES, SUBLANES, dh), jnp.uint32),
        compiler_params=_params(1), name='moe_grouped',
    )(tile_expert, n_used, idx, h2p.reshape(t // SUBLANES, SUBLANES, dh), w1, w3, w2)
    return y.reshape(n_rows, dh)


def _moe_plan(eidx, rank, counts, n_experts):
    t, k = eidx.shape
    tmr = MOE_ROWS
    n_tiles = (t * k) // tmr + n_experts
    padded = ((counts + tmr - 1) // tmr) * tmr
    ends = jnp.cumsum(padded)
    offsets = ends - padded
    n_used = (ends[-1] // tmr).astype(jnp.int32)
    pos = (offsets[eidx] + rank).reshape(-1)
    dst = (jnp.arange(k, dtype=jnp.int32)[None, :] * t + jnp.arange(t, dtype=jnp.int32)[:, None]).reshape(-1)
    p = jnp.arange(n_tiles * tmr, dtype=jnp.int32)
    pad_dst = k * t + ((p // tmr) % 3) * tmr + p % tmr
    dst_sorted = pad_dst.at[pos].set(dst).reshape(n_tiles, tmr)
    tok_sorted = jnp.where(dst_sorted < k * t, dst_sorted % t, dst_sorted % tmr)
    idx = jnp.concatenate([tok_sorted, dst_sorted, jnp.zeros((n_tiles, IDX_REC - 2 * tmr), jnp.int32)],
                          axis=1).reshape(-1)
    tile_start = jnp.minimum(jnp.arange(n_tiles + 1, dtype=jnp.int32), n_used - 1) * tmr
    tile_expert = jnp.sum((ends[None, :] <= tile_start[:, None]).astype(jnp.int32), axis=1)
    return idx, jnp.minimum(tile_expert, n_experts - 1), n_used.reshape(1)


def _shared_expert_body(a_ref, w1_ref, w3_ref, w2_ref, o_ref):
    a = a_ref[...]
    h1 = _dot(a, w1_ref[...])
    hid = (h1 * jax.nn.sigmoid(h1)) * _dot(a, w3_ref[...])
    o_ref[...] = _dot(hid.astype(BF16), w2_ref[...])


def _shared_expert(h2, w1, w3, w2, tm=512):
    t, d = h2.shape
    f = w1.shape[1]
    full = lambda shape: pl.BlockSpec(shape, lambda i: (0, 0))
    return pl.pallas_call(
        _shared_expert_body, grid=(t // tm,),
        in_specs=[pl.BlockSpec((tm, d), lambda i: (i, 0)), full((d, f)), full((d, f)), full((f, d))],
        out_specs=pl.BlockSpec((tm, d), lambda i: (i, 0)),
        out_shape=jax.ShapeDtypeStruct((t, d), F32), compiler_params=_params(1), name='shared_expert',
    )(h2, w1.astype(BF16), w3.astype(BF16), w2.astype(BF16))


def _moe_combine_body(x_ref, g_ref, gsel_ref, ysh_ref, *refs):
    y_refs, o_ref = refs[:-1], refs[-1]
    half = x_ref.shape[1] // 2
    gs = gsel_ref[...]
    acc_lo = ysh_ref[:, :half]
    acc_hi = ysh_ref[:, half:]
    for r, y_ref in enumerate(y_refs):
        u = y_ref[...]
        acc_lo = acc_lo + gs[:, r:r + 1] * _unpack_lo(u)
        acc_hi = acc_hi + gs[:, r:r + 1] * _unpack_hi(u)
    o_ref[:, :half] = x_ref[:, :half] + g_ref[:, :half] * acc_lo
    o_ref[:, half:] = x_ref[:, half:] + g_ref[:, half:] * acc_hi


def _moe_combine(x2, g2, gsel, y_shared, y, seq, tm=128):
    t, d = x2.shape
    bsz = t // seq
    k = gsel.shape[1]
    row = pl.BlockSpec((tm, d), lambda i: (i, 0))
    slot_specs = [pl.BlockSpec((tm, d // 2), lambda i, r=r: (r * (t // tm) + i, 0)) for r in range(k)]
    return pl.pallas_call(
        _moe_combine_body, grid=(t // tm,),
        in_specs=[row, pl.BlockSpec((None, 1, d), lambda i: (i // (seq // tm), 0, 0)),
                  pl.BlockSpec((tm, k), lambda i: (i, 0)), row] + slot_specs,
        out_specs=row, out_shape=jax.ShapeDtypeStruct((t, d), F32),
        compiler_params=_params(1), name='moe_combine',
    )(x2, g2.reshape(bsz, 1, d), gsel, y_shared, *([y] * k))


def kernel(x, c, w_ada, b_ada, ada_layer, norm_mix, norm_ffn, w_in, q_norm_a, k_norm_a, q_norm_b,
           k_norm_b, conv_w, conv_b, w_rgate, b_rgate, w_igate, b_igate, lru_lambda, w_branch_a,
           w_branch_b, w_branch_c, w_out, w_router, b_router, w1_exp, w3_exp, w2_exp, w1_shared,
           w3_shared, w2_shared):
    bsz, seq, d = x.shape
    t = bsz * seq
    depth = w_in.shape[0]
    mixw = d // 4
    a_cols = 3 * mixw + IDX_HEADS * IDX_DIM + IDX_DIM + IDX_HEADS
    off_b = a_cols
    off_c = off_b + N_DIL * 3 * mixw
    off_g = off_c + 2 * mixw
    top_k = min(DSA_TOPK_MAX, seq // 4)
    tm = 1024 if t % 1024 == 0 else 256
    tpb = seq // tm

    mod_shared = _ada_mod(c, w_ada, b_ada).reshape(bsz, N_MOD, d)
    x2 = x.reshape(t, d)
    for l in range(depth):
        mod = mod_shared + ada_layer[l]
        sh1, sc1, g1, sh2, sc2, g2 = [mod[:, j] for j in range(N_MOD)]
        h = _norm_mod(x2, norm_mix[l], sc1, sh1, seq)
        wl = w_in[l]

        n_main = 3 * mixw + IDX_HEADS * IDX_DIM
        pa = _matmul(h, wl[:, :n_main].astype(BF16), BF16, tm=tm, tn=512, name='proj_a')
        w_small = jnp.zeros((d, LANES), BF16).at[:, :IDX_DIM + IDX_HEADS].set(
            wl[:, n_main:a_cols].astype(BF16))
        small = _matmul(h, w_small, F32, tm=tm, tn=LANES, name='proj_a_idx')
        ik = small[:, :IDX_DIM].astype(BF16)
        zeros = jnp.zeros_like(ik)
        ik2 = jnp.concatenate([ik, zeros, zeros, ik], axis=1)
        qn, kn = _qk_norm(pa, q_norm_a[l], k_norm_a[l], mixw)
        iwt = small[:, IDX_DIM:IDX_DIM + IDX_HEADS].reshape(bsz, seq, IDX_HEADS).transpose(0, 2, 1)
        vt = pa[:, 2 * mixw:3 * mixw].reshape(bsz, seq, mixw // HEAD_DIM, HEAD_DIM).transpose(0, 2, 3, 1)
        o_a = _dsa_attention(qn, kn, pa, ik2, iwt, vt, seq, top_k)

        pb = _matmul(h, wl[:, off_b:off_c].astype(BF16), BF16, tm=tm, tn=512, name='proj_b')
        outs, lses = [], []
        for g, (window, dil) in enumerate(DIL_PATTERNS):
            qkv = _to_classes(pb[:, g * 3 * mixw:(g + 1) * 3 * mixw], bsz, seq, dil)
            og, lg = _dilated_group(qkv, q_norm_b[l, g], k_norm_b[l, g], bsz * dil, seq // dil,
                                    band=window // dil)
            outs.append(_from_classes(og, bsz, seq, dil))
            lses.append(_from_classes(lg, bsz, seq, dil))
        o_b = _dil_merge(outs, lses)

        pc = _matmul(h, wl[:, off_c:off_g].astype(BF16), F32, tm=tm, tn=512, name='proj_c')
        o_c = _rg_lru(pc, conv_w[l], conv_b[l], w_rgate[l], b_rgate[l], w_igate[l], b_igate[l],
                      lru_lambda[l], seq)

        sg = _matmul(h, wl[:, off_g:].astype(BF16), BF16, tm=tm, tn=512, body=_mm_sigmoid_body,
                     name='proj_gates')
        nd = d // 512
        mixed = _tiled_call(
            _branch_merge_body, t, d, tm, 512,
            [(o_a, 'row'), (o_b, 'row'), (o_c, 'row'),
             (w_branch_a[l].astype(BF16), 'col'), (w_branch_b[l].astype(BF16), 'col'),
             (w_branch_c[l].astype(BF16), 'col'),
             (sg, ('tileoff', 0)), (sg, ('tileoff', nd)), (sg, ('tileoff', 2 * nd))],
            [(d, BF16)], name='branch_merge')
        x2 = _tiled_call(
            _mm_residual_body, t, d, tm, 512,
            [(mixed, 'row'), (w_out[l].astype(BF16), 'col'), (x2, 'tile'), (g1.reshape(bsz, 1, d), 'bvec')],
            [(d, F32)], tiles_per_batch=tpb, name='out_proj')

        h2, h2p = _norm_mod_pack(x2, norm_ffn[l], sc2, sh2, seq)
        gsel, eidx, rank, cnt = _router(h2, w_router[l], b_router[l])
        idx, tile_expert, n_used = _moe_plan(eidx, rank, cnt[0, :N_EXPERTS].astype(jnp.int32), N_EXPERTS)
        y = _moe_grouped(h2p, idx, tile_expert, n_used, w1_exp, w3_exp, w2_exp, l, MOE_TOP_K)
        y_shared = _shared_expert(h2, w1_shared[l], w3_shared[l], w2_shared[l])
        x2 = _moe_combine(x2, g2, gsel, y_shared, y, seq)
    return x2.reshape(bsz, seq, d)
```

```python
import functools

import jax
import jax.numpy as jnp
from jax import lax
from jax.experimental import pallas as pl
from jax.experimental.pallas import tpu as pltpu

HEAD_DIM = 128
A_HEADS = 8
IDX_HEADS = 16
IDX_DIM = 64
DSA_TOPK_MAX = 256
DIL_PATTERNS = ((128, 1), (512, 4), (2048, 16))
N_DIL = 3
B_HEADS = 8
LRU_BLOCKS = 16
CONV_WIDTH = 4
LRU_C = 8.0
N_EXPERTS = 64
MOE_TOP_K = 8
ROUTED_SCALE = 2.5
N_MOD = 6
RMS_EPS = 1e-6

LANES = 128
SUBLANES = 8
VMEM_LIMIT_BYTES = 56 * 1024 * 1024

NEG_BIG = -1e30
LOG2_E = 1.4426950408889634
INT_MIN = -(2 ** 31)

BF16 = jnp.bfloat16
F32 = jnp.float32


def _params(n_axes):
    return pltpu.CompilerParams(dimension_semantics=("arbitrary",) * n_axes,
                                vmem_limit_bytes=VMEM_LIMIT_BYTES)


def _dot(a, b):
    return jnp.dot(a, b, preferred_element_type=F32)


def _dot_nt(a, b):
    return lax.dot_general(a, b, (((1,), (1,)), ((), ())), preferred_element_type=F32)


def _tiled_call(body, m, n, tm, tn, ins, outs, *, tiles_per_batch=None, name=None):
    assert m % tm == 0 and n % tn == 0, (m, n, tm, tn)
    grid = (m // tm, n // tn)
    in_specs, arrays = [], []
    for arr, kind in ins:
        arrays.append(arr)
        if kind == 'row':
            in_specs.append(pl.BlockSpec((tm, arr.shape[1]), lambda i, j: (i, 0)))
        elif isinstance(kind, tuple) and kind[0] == 'rowoff':
            _, off, width = kind
            in_specs.append(pl.BlockSpec((tm, width), lambda i, j, off=off: (i, off)))
        elif kind == 'col':
            in_specs.append(pl.BlockSpec((arr.shape[0], tn), lambda i, j: (0, j)))
        elif isinstance(kind, tuple) and kind[0] == 'coloff':
            in_specs.append(pl.BlockSpec((arr.shape[0], tn), lambda i, j, off=kind[1]: (0, j + off)))
        elif kind == 'col3':
            in_specs.append(pl.BlockSpec((None, arr.shape[1], arr.shape[2]), lambda i, j: (j, 0, 0)))
        elif kind == 'tile':
            in_specs.append(pl.BlockSpec((tm, tn), lambda i, j: (i, j)))
        elif isinstance(kind, tuple) and kind[0] == 'tileoff':
            in_specs.append(pl.BlockSpec((tm, tn), lambda i, j, off=kind[1]: (i, j + off)))
        elif kind == 'vec':
            in_specs.append(pl.BlockSpec((1, tn), lambda i, j: (0, j)))
        elif kind == 'bvec':
            tpb = tiles_per_batch
            in_specs.append(pl.BlockSpec((None, 1, tn), lambda i, j, tpb=tpb: (i // tpb, 0, j)))
        elif kind == 'full':
            nd = arr.ndim
            in_specs.append(pl.BlockSpec(arr.shape, lambda i, j, nd=nd: (0,) * nd))
        else:
            raise ValueError(kind)
    out_shape, out_specs = [], []
    for n_cols, dtype in outs:
        assert (n_cols * tn) % n == 0
        w = n_cols * tn // n
        out_shape.append(jax.ShapeDtypeStruct((m, n_cols), dtype))
        out_specs.append(pl.BlockSpec((tm, w), lambda i, j: (i, j)))
    single = len(outs) == 1
    res = pl.pallas_call(
        body, grid=grid, in_specs=in_specs,
        out_specs=out_specs[0] if single else out_specs,
        out_shape=out_shape[0] if single else out_shape,
        compiler_params=_params(2), name=name)(*arrays)
    return res


def _mm_body(a_ref, w_ref, o_ref):
    o_ref[...] = _dot(a_ref[...], w_ref[...]).astype(o_ref.dtype)


def _mm_residual_body(a_ref, w_ref, x_ref, g_ref, o_ref):
    o_ref[...] = x_ref[...] + g_ref[...] * _dot(a_ref[...], w_ref[...])


def _matmul(a, w, out_dtype, *, tm, tn, body=_mm_body, name=None):
    return _tiled_call(body, a.shape[0], w.shape[1], tm, tn, [(a, 'row'), (w, 'col')],
                       [(w.shape[1], out_dtype)], name=name)


def _ada_body(c_ref, w_ref, b_ref, o_ref):
    c = c_ref[...]
    a = (c * jax.nn.sigmoid(c)).astype(BF16)
    o_ref[...] = _dot(a, w_ref[...].astype(BF16)) + b_ref[...]


def _ada_mod(c, w_ada, b_ada):
    bsz, d = c.shape
    n = w_ada.shape[1]
    c_pad = jnp.zeros((SUBLANES, d), F32).at[:bsz].set(c)
    out = _tiled_call(_ada_body, SUBLANES, n, SUBLANES, 512,
                      [(c_pad, 'row'), (w_ada, 'col'), (b_ada.reshape(1, n), 'vec')],
                      [(n, F32)], name='ada_mod')
    return out[:bsz]


def _norm_mod_body(x_ref, g_ref, sc_ref, sh_ref, o_ref):
    x = x_ref[...]
    y = x * lax.rsqrt(jnp.mean(x * x, axis=-1, keepdims=True) + RMS_EPS) * g_ref[...]
    o_ref[...] = (y * (1.0 + sc_ref[...]) + sh_ref[...]).astype(o_ref.dtype)


def _norm_mod(x2, g, sc, sh, seq, tm=256):
    t, d = x2.shape
    bsz = t // seq
    return _tiled_call(_norm_mod_body, t, d, tm, d,
                       [(x2, 'tile'), (g.reshape(1, d), 'vec'), (sc.reshape(bsz, 1, d), 'bvec'),
                        (sh.reshape(bsz, 1, d), 'bvec')],
                       [(d, BF16)], tiles_per_batch=seq // tm, name='norm_mod')


def _head_norm(x, g):
    outs = []
    for h in range(x.shape[1] // HEAD_DIM):
        xh = x[:, h * HEAD_DIM:(h + 1) * HEAD_DIM].astype(F32)
        outs.append(xh * lax.rsqrt(jnp.mean(xh * xh, axis=-1, keepdims=True) + RMS_EPS) * g)
    return jnp.concatenate(outs, axis=1)


def _qk_norm_body(q_ref, k_ref, gq_ref, gk_ref, qo_ref, ko_ref):
    qo_ref[...] = _head_norm(q_ref[...], gq_ref[...]).astype(qo_ref.dtype)
    ko_ref[...] = _head_norm(k_ref[...], gk_ref[...]).astype(ko_ref.dtype)


def _qk_norm(pa, gq, gk, width, tm=512):
    t = pa.shape[0]
    return _tiled_call(_qk_norm_body, t, width, tm, width,
                       [(pa, ('tileoff', 0)), (pa, ('tileoff', 1)),
                        (gq.reshape(1, HEAD_DIM), 'full'), (gk.reshape(1, HEAD_DIM), 'full')],
                       [(width, BF16), (width, BF16)], name='qk_norm_a')


def _dsa_body(q_ref, iq_ref, iwt_ref, ik_ref, k_ref, vt_ref, o_ref,
              keys_ref, thr_ref, bias_ref, s_ref, p_ref, m_ref, l_ref, acc_ref, *, tq, tk, top_k, n_heads):
    qb = pl.program_id(1)
    kb = pl.program_id(2)
    kb_last = ((qb + 1) * tq - 1) // tk
    n_chunks = kb_last + 1
    kpos = lax.broadcasted_iota(jnp.int32, (tk, tq), 0)
    qpos = lax.broadcasted_iota(jnp.int32, (tk, tq), 1) + qb * tq
    n_acc = 4 * SUBLANES

    @pl.when(kb == 0)
    def _scores_and_threshold():
        w = iwt_ref[...] * (IDX_HEADS ** -0.5 * IDX_DIM ** -0.5)

        def score_chunk(c, carry):
            ik2 = ik_ref[pl.ds(pl.multiple_of(c * tk, tk), tk), :]
            acc = jnp.zeros((tk, tq), F32)
            for p in range(IDX_HEADS // 2):
                iq_pair = iq_ref[:, p * LANES:(p + 1) * LANES]
                for half in range(2):
                    rel = _dot_nt(ik2[:, half * LANES:(half + 1) * LANES], iq_pair)
                    h = 2 * p + half
                    acc = acc + w[h:h + 1, :] * jnp.maximum(rel, 0.0)
            bits = pltpu.bitcast(acc, jnp.int32)
            key = jnp.where(bits < 0, bits ^ 0x7FFFFFFF, bits)
            key = jnp.where(kpos + c * tk <= qpos, key, INT_MIN)
            keys_ref[c] = key
            return carry

        lax.fori_loop(0, n_chunks, score_chunk, 0)

        def count_ge(cand):
            def body(c, cnt):
                for j in range(tk // n_acc):
                    blk = keys_ref[c, j * n_acc:(j + 1) * n_acc, :]
                    cnt = cnt + jnp.where(blk >= cand, 1.0, 0.0)
                return cnt
            cnt = lax.fori_loop(0, n_chunks, body, jnp.zeros((n_acc, tq), F32))
            return jnp.sum(cnt, axis=0, keepdims=True)

        k_f = float(top_k)
        t0 = jnp.where(count_ge(jnp.zeros((1, tq), jnp.int32)) >= k_f, 0, INT_MIN).astype(jnp.int32)

        def bit_body(i, t):
            cand = t | lax.shift_left(jnp.int32(1), 30 - i)
            return jnp.where(count_ge(cand) >= k_f, cand, t)

        t = lax.fori_loop(0, 31, bit_body, t0)
        thr_ref[...] = jnp.broadcast_to(t, thr_ref.shape)
        m_ref[...] = jnp.full(m_ref.shape, NEG_BIG, F32)
        l_ref[...] = jnp.zeros(l_ref.shape, F32)
        acc_ref[...] = jnp.zeros(acc_ref.shape, F32)

    @pl.when(kb <= kb_last)
    def _attend():
        sel = (keys_ref[kb] >= thr_ref[0:1, :]) & (kpos + kb * tk <= qpos)
        bias_ref[...] = jnp.where(sel, 0.0, NEG_BIG)
        c = HEAD_DIM ** -0.5 * LOG2_E

        def raw_scores(h):
            sl = slice(h * HEAD_DIM, (h + 1) * HEAD_DIM)
            s_ref[h % 3] = _dot_nt(k_ref[:, sl], q_ref[:, sl])

        def softmax(h):
            s = s_ref[h % 3] + bias_ref[...]
            m_old = m_ref[h, 0:1, :]
            m_new = jnp.maximum(m_old, jnp.max(s, axis=0, keepdims=True))
            p = jnp.exp2((s - m_new) * c)
            alpha = jnp.exp2((m_old - m_new) * c)
            l_new = alpha * l_ref[h, 0:1, :] + jnp.sum(p, axis=0, keepdims=True)
            p_ref[h % 2] = p.astype(BF16)
            m_ref[h] = jnp.broadcast_to(m_new, (SUBLANES, tq))
            l_ref[h] = jnp.broadcast_to(l_new, (SUBLANES, tq))
            return alpha

        def accumulate(h, alpha):
            acc_ref[h] = alpha * acc_ref[h] + _dot(vt_ref[h], p_ref[h % 2])

        raw_scores(0)
        raw_scores(1)
        alphas = {0: softmax(0)}
        for h in range(n_heads):
            if h + 2 < n_heads:
                raw_scores(h + 2)
            if h + 1 < n_heads:
                alphas[h + 1] = softmax(h + 1)
            accumulate(h, alphas.pop(h))

    @pl.when(kb == kb_last)
    def _finish():
        for h in range(n_heads):
            sl = slice(h * HEAD_DIM, (h + 1) * HEAD_DIM)
            o_ref[:, sl] = (acc_ref[h] / l_ref[h, 0:1, :]).T.astype(o_ref.dtype)


def _dsa_attention(qn, kn, pa, ik2, iwt, vt, seq, top_k, *, tq=256, tk=512):
    t, width = qn.shape
    bsz = t // seq
    nq, nk = seq // tq, seq // tk
    n_heads = width // HEAD_DIM
    assert width == IDX_HEADS * IDX_DIM and pa.shape[1] == 4 * width
    iq_off = 3

    def last(qb):
        return ((qb + 1) * tq - 1) // tk

    body = functools.partial(_dsa_body, tq=tq, tk=tk, top_k=top_k, n_heads=n_heads)
    return pl.pallas_call(
        body, grid=(bsz, nq, nk),
        in_specs=[
            pl.BlockSpec((tq, width), lambda b, qb, kb: (b * nq + qb, 0)),
            pl.BlockSpec((tq, width), lambda b, qb, kb: (b * nq + qb, iq_off)),
            pl.BlockSpec((None, IDX_HEADS, tq), lambda b, qb, kb: (b, 0, qb)),
            pl.BlockSpec((seq, 2 * LANES), lambda b, qb, kb: (b, 0)),
            pl.BlockSpec((tk, width), lambda b, qb, kb: (b * nk + jnp.minimum(kb, last(qb)), 0)),
            pl.BlockSpec((None, n_heads, HEAD_DIM, tk),
                         lambda b, qb, kb: (b, 0, 0, jnp.minimum(kb, last(qb)))),
        ],
        out_specs=pl.BlockSpec((tq, width), lambda b, qb, kb: (b * nq + qb, 0)),
        out_shape=jax.ShapeDtypeStruct((t, width), BF16),
        scratch_shapes=[
            pltpu.VMEM((nk, tk, tq), jnp.int32),
            pltpu.VMEM((SUBLANES, tq), jnp.int32),
            pltpu.VMEM((tk, tq), F32),
            pltpu.VMEM((3, tk, tq), F32),
            pltpu.VMEM((2, tk, tq), BF16),
            pltpu.VMEM((n_heads, SUBLANES, tq), F32),
            pltpu.VMEM((n_heads, SUBLANES, tq), F32),
            pltpu.VMEM((n_heads, HEAD_DIM, tq), F32),
        ],
        compiler_params=_params(3), name='dsa_attention',
    )(qn, pa, iwt, ik2, kn, vt)


def _dil_body(q_ref, kc_ref, vc_ref, kp_ref, vp_ref, gq_ref, gk_ref, o_ref, lse_ref,
              kcat_ref, vcat_ref, *, band, rows, n_heads):
    n = pl.program_id(1)
    kcat_ref[0:band, :] = _head_norm(kp_ref[...], gk_ref[...]).astype(BF16)
    kcat_ref[band:band + rows, :] = _head_norm(kc_ref[...], gk_ref[...]).astype(BF16)
    vcat_ref[0:band, :] = vp_ref[...]
    vcat_ref[band:band + rows, :] = vc_ref[...]
    qi = lax.broadcasted_iota(jnp.int32, (band, 2 * band), 0)
    ki = lax.broadcasted_iota(jnp.int32, (band, 2 * band), 1)
    back = qi + band - ki
    in_band = (back >= 0) & (back <= band)
    scale = HEAD_DIM ** -0.5
    gq = gq_ref[...]

    def sub_block(j, carry):
        r0 = pl.multiple_of(j * band, band)
        mask = in_band & ((ki >= band) | (n * (rows // band) + j > 0))
        for h in range(n_heads):
            sl = slice(h * HEAD_DIM, (h + 1) * HEAD_DIM)
            qh = q_ref[pl.ds(r0, band), sl].astype(F32)
            qh = (qh * lax.rsqrt(jnp.mean(qh * qh, axis=-1, keepdims=True) + RMS_EPS) * gq).astype(BF16)
            s = _dot_nt(qh, kcat_ref[pl.ds(r0, 2 * band), sl]) * scale
            s = jnp.where(mask, s, NEG_BIG)
            m = jnp.max(s, axis=1, keepdims=True)
            p = jnp.exp(s - m)
            l = jnp.sum(p, axis=1, keepdims=True)
            o = _dot(p.astype(BF16), vcat_ref[pl.ds(r0, 2 * band), sl]) / l
            o_ref[pl.ds(r0, band), sl] = o.astype(o_ref.dtype)
            lse_ref[pl.ds(r0, band), h:h + 1] = m + jnp.log(l)
        return carry

    lax.fori_loop(0, rows // band, sub_block, 0)


def _dilated_group(qkv, gq, gk, n_seq, len_seq, *, band=128, rows=512):
    t, w3 = qkv.shape
    width = w3 // 3
    n_heads = width // HEAD_DIM
    rows = min(rows, len_seq)
    nblk = len_seq // rows
    sub = rows // band

    def cur(c):
        return lambda s, n: (s * nblk + n, c)

    def prev(c):
        return lambda s, n: (s * nblk * sub + jnp.maximum(n * sub - 1, 0), c)

    body = functools.partial(_dil_body, band=band, rows=rows, n_heads=n_heads)
    return pl.pallas_call(
        body, grid=(n_seq, nblk),
        in_specs=[
            pl.BlockSpec((rows, width), cur(0)), pl.BlockSpec((rows, width), cur(1)),
            pl.BlockSpec((rows, width), cur(2)),
            pl.BlockSpec((band, width), prev(1)), pl.BlockSpec((band, width), prev(2)),
            pl.BlockSpec((1, HEAD_DIM), lambda s, n: (0, 0)), pl.BlockSpec((1, HEAD_DIM), lambda s, n: (0, 0)),
        ],
        out_specs=[pl.BlockSpec((rows, width), lambda s, n: (s * nblk + n, 0)),
                   pl.BlockSpec((rows, n_heads), lambda s, n: (s * nblk + n, 0))],
        out_shape=[jax.ShapeDtypeStruct((t, width), BF16), jax.ShapeDtypeStruct((t, n_heads), F32)],
        scratch_shapes=[pltpu.VMEM((band + rows, width), BF16), pltpu.VMEM((band + rows, width), BF16)],
        compiler_params=_params(2), name='dilated_attention',
    )(qkv, qkv, qkv, qkv, qkv, gq.reshape(1, HEAD_DIM), gk.reshape(1, HEAD_DIM))


def _dil_merge_body(o0_ref, o1_ref, o2_ref, l0_ref, l1_ref, l2_ref, out_ref):
    l0, l1, l2 = l0_ref[...], l1_ref[...], l2_ref[...]
    m = jnp.maximum(jnp.maximum(l0, l1), l2)
    e0, e1, e2 = jnp.exp(l0 - m), jnp.exp(l1 - m), jnp.exp(l2 - m)
    den = e0 + e1 + e2
    a0, a1, a2 = e0 / den, e1 / den, e2 / den
    for h in range(l0.shape[1]):
        sl = slice(h * HEAD_DIM, (h + 1) * HEAD_DIM)
        out_ref[:, sl] = (a0[:, h:h + 1] * o0_ref[:, sl].astype(F32)
                          + a1[:, h:h + 1] * o1_ref[:, sl].astype(F32)
                          + a2[:, h:h + 1] * o2_ref[:, sl].astype(F32)).astype(out_ref.dtype)


def _dil_merge(outs, lses, tm=512):
    t, width = outs[0].shape
    nh = lses[0].shape[1]
    ospec = pl.BlockSpec((tm, width), lambda i: (i, 0))
    lspec = pl.BlockSpec((tm, nh), lambda i: (i, 0))
    return pl.pallas_call(
        _dil_merge_body, grid=(t // tm,), in_specs=[ospec] * 3 + [lspec] * 3, out_specs=ospec,
        out_shape=jax.ShapeDtypeStruct((t, width), BF16), compiler_params=_params(1), name='dilated_merge',
    )(*outs, *lses)


def _to_classes(a, bsz, seq, dil):
    if dil == 1:
        return a
    c = a.shape[1]
    return a.reshape(bsz, seq // dil, dil, c).transpose(0, 2, 1, 3).reshape(bsz * seq, c)


def _from_classes(a, bsz, seq, dil):
    if dil == 1:
        return a
    c = a.shape[1]
    return a.reshape(bsz, dil, seq // dil, c).transpose(0, 2, 1, 3).reshape(bsz * seq, c)


def _lru_body(x_ref, y_ref, xp_ref, cw_ref, cb_ref, wr_ref, br_ref, wi_ref, bi_ref, lam_ref, o_ref,
              carry_ref, *, tm, tiles_per_batch):
    i = pl.program_id(0)
    first = (i % tiles_per_batch) == 0
    x = x_ref[...]
    prev = jnp.where(first, 0.0, xp_ref[...])
    xs = jnp.concatenate([prev, x], axis=0)
    xc = cb_ref[...] + cw_ref[CONV_WIDTH - 1:CONV_WIDTH, :] * x
    for j in range(CONV_WIDTH - 1):
        back = CONV_WIDTH - 1 - j
        xc = xc + cw_ref[j:j + 1, :] * xs[SUBLANES - back:SUBLANES - back + tm, :]
    xcb = xc.astype(BF16)
    r = jax.nn.sigmoid(_dot(xcb, wr_ref[...]) + br_ref[...])
    ig = jax.nn.sigmoid(_dot(xcb, wi_ref[...]) + bi_ref[...])
    lam = lam_ref[...]
    softplus_neg = jnp.maximum(-lam, 0.0) + jnp.log1p(jnp.exp(-jnp.abs(lam)))
    log_a = (-LRU_C * softplus_neg) * r
    a = jnp.exp(log_a)
    b = jnp.sqrt(1.0 - a * a) * (ig * xc)
    rows = lax.broadcasted_iota(jnp.int32, a.shape, 0)
    step = 1
    while step < tm:
        a_sh = pltpu.roll(a, step, 0)
        b_sh = pltpu.roll(b, step, 0)
        valid = rows >= step
        b = jnp.where(valid, a * b_sh + b, b)
        a = jnp.where(valid, a * a_sh, a)
        step *= 2
    h0 = jnp.where(first, 0.0, carry_ref[0:1, :])
    h = a * h0 + b
    carry_ref[...] = jnp.broadcast_to(h[tm - 1:tm, :], carry_ref.shape)
    o_ref[...] = (h * jax.nn.gelu(y_ref[...])).astype(o_ref.dtype)


def _block_diag(w):
    g, n, _ = w.shape
    eye = jnp.eye(g, dtype=w.dtype)
    return (eye[:, None, :, None] * w[:, :, None, :]).reshape(g * n, g * n)


def _rg_lru(pc, conv_w, conv_b, w_r, b_r, w_i, b_i, lam, seq, tm=256):
    t, w2 = pc.shape
    w = w2 // 2
    row = lambda v: v.reshape(1, w)
    full2 = lambda shape: pl.BlockSpec(shape, lambda i: (0, 0))
    body = functools.partial(_lru_body, tm=tm, tiles_per_batch=seq // tm)
    return pl.pallas_call(
        body, grid=(t // tm,),
        in_specs=[
            pl.BlockSpec((tm, w), lambda i: (i, 0)),
            pl.BlockSpec((tm, w), lambda i: (i, 1)),
            pl.BlockSpec((SUBLANES, w), lambda i: (jnp.maximum(i * (tm // SUBLANES) - 1, 0), 0)),
            full2((CONV_WIDTH, w)), full2((1, w)), full2((w, w)), full2((1, w)), full2((w, w)),
            full2((1, w)), full2((1, w)),
        ],
        out_specs=pl.BlockSpec((tm, w), lambda i: (i, 0)),
        out_shape=jax.ShapeDtypeStruct((t, w), BF16),
        scratch_shapes=[pltpu.VMEM((SUBLANES, w), F32)],
        compiler_params=_params(1), name='rg_lru',
    )(pc, pc, pc, conv_w, row(conv_b), _block_diag(w_r).astype(BF16), row(b_r),
      _block_diag(w_i).astype(BF16), row(b_i), row(lam))


def _branch_merge_body(h_ref, oa_ref, ob_ref, oc_ref, wga_ref, wgb_ref, wgc_ref, wa_ref, wb_ref, wc_ref, o_ref):
    h = h_ref[...]
    mixed = (jax.nn.sigmoid(_dot(h, wga_ref[...])) * _dot(oa_ref[...], wa_ref[...])
             + jax.nn.sigmoid(_dot(h, wgb_ref[...])) * _dot(ob_ref[...], wb_ref[...])
             + jax.nn.sigmoid(_dot(h, wgc_ref[...])) * _dot(oc_ref[...], wc_ref[...]))
    o_ref[...] = mixed.astype(o_ref.dtype)


MOE_ROWS = 256
IDX_REC = 1024


def _router_body(a_ref, w_ref, b_ref, gsel_ref, eidx_ref, rank_ref, cnt_ref, carry_ref, *, n_experts, top_k, tm):
    i = pl.program_id(0)

    @pl.when(i == 0)
    def _():
        carry_ref[...] = jnp.zeros(carry_ref.shape, F32)

    logits = _dot(a_ref[...], w_ref[...])
    lane = lax.broadcasted_iota(jnp.int32, logits.shape, 1).astype(F32)
    scores = jax.nn.sigmoid(logits)
    cur = jnp.where(lane < n_experts, scores + b_ref[...], -jnp.inf)
    mask = jnp.zeros(logits.shape, F32)
    hits, picked_scores = [], []
    for r in range(top_k):
        m = jnp.max(cur, axis=1, keepdims=True)
        idx = jnp.min(jnp.where(cur == m, lane, float(LANES)), axis=1, keepdims=True)
        hit = lane == idx
        hits.append(hit)
        picked_scores.append(jnp.sum(jnp.where(hit, scores, 0.0), axis=1, keepdims=True))
        eidx_ref[:, r:r + 1] = idx.astype(jnp.int32)
        cur = jnp.where(hit, -jnp.inf, cur)
        mask = mask + jnp.where(hit, 1.0, 0.0)
    total = picked_scores[0]
    for r in range(1, top_k):
        total = total + picked_scores[r]
    ri = lax.broadcasted_iota(jnp.int32, (tm, tm), 0)
    ci = lax.broadcasted_iota(jnp.int32, (tm, tm), 1)
    tri = jnp.where(ri > ci, 1.0, 0.0).astype(BF16)
    before = _dot(tri, mask.astype(BF16)) + carry_ref[0:1, :]
    for r in range(top_k):
        gsel_ref[:, r:r + 1] = picked_scores[r] / total * ROUTED_SCALE
        rank_ref[:, r:r + 1] = jnp.sum(jnp.where(hits[r], before, 0.0), axis=1, keepdims=True).astype(jnp.int32)
    carry_ref[...] = carry_ref[...] + jnp.sum(mask, axis=0, keepdims=True)
    cnt_ref[...] = carry_ref[...]


def _router(h2, w_router, b_router, tm=1024):
    t, d = h2.shape
    n_exp = w_router.shape[1]
    w_r = jnp.zeros((d, LANES), BF16).at[:, :n_exp].set(w_router.astype(BF16))
    b_r = jnp.zeros((1, LANES), F32).at[0, :n_exp].set(b_router.astype(F32))
    col = lambda dt: jax.ShapeDtypeStruct((t, MOE_TOP_K), dt)
    cspec = pl.BlockSpec((tm, MOE_TOP_K), lambda i: (i, 0))
    return pl.pallas_call(
        functools.partial(_router_body, n_experts=n_exp, top_k=MOE_TOP_K, tm=tm), grid=(t // tm,),
        in_specs=[pl.BlockSpec((tm, d), lambda i: (i, 0)), pl.BlockSpec((d, LANES), lambda i: (0, 0)),
                  pl.BlockSpec((1, LANES), lambda i: (0, 0))],
        out_specs=[cspec, cspec, cspec, pl.BlockSpec((SUBLANES, LANES), lambda i: (0, 0))],
        out_shape=[col(F32), col(jnp.int32), col(jnp.int32), jax.ShapeDtypeStruct((SUBLANES, LANES), F32)],
        scratch_shapes=[pltpu.VMEM((SUBLANES, LANES), F32)],
        compiler_params=_params(1), name='router')(h2, w_r, b_r)


def _pack_bf16_pairs(lo, hi):
    lo_bits = pltpu.bitcast(lo.astype(BF16).astype(F32), jnp.uint32)
    hi_bits = pltpu.bitcast(hi.astype(BF16).astype(F32), jnp.uint32)
    return (hi_bits & jnp.uint32(0xFFFF0000)) | lax.shift_right_logical(lo_bits, jnp.uint32(16))


def _unpack_lo(u):
    return pltpu.bitcast(lax.shift_left(u, jnp.uint32(16)), F32)


def _unpack_hi(u):
    return pltpu.bitcast(u & jnp.uint32(0xFFFF0000), F32)


def _norm_mod_pack_body(x_ref, g_ref, sc_ref, sh_ref, o_ref, p_ref):
    x = x_ref[...]
    y = x * lax.rsqrt(jnp.mean(x * x, axis=-1, keepdims=True) + RMS_EPS) * g_ref[...]
    y = y * (1.0 + sc_ref[...]) + sh_ref[...]
    half = y.shape[1] // 2
    o_ref[...] = y.astype(o_ref.dtype)
    p_ref[...] = _pack_bf16_pairs(y[:, :half], y[:, half:])


def _norm_mod_pack(x2, g, sc, sh, seq, tm=256):
    t, d = x2.shape
    bsz = t // seq
    row = pl.BlockSpec((tm, d), lambda i: (i, 0))
    vec = pl.BlockSpec((1, d), lambda i: (0, 0))
    bvec = pl.BlockSpec((None, 1, d), lambda i: (i // (seq // tm), 0, 0))
    return pl.pallas_call(
        _norm_mod_pack_body, grid=(t // tm,), in_specs=[row, vec, bvec, bvec],
        out_specs=[row, pl.BlockSpec((tm, d // 2), lambda i: (i, 0))],
        out_shape=[jax.ShapeDtypeStruct((t, d), BF16), jax.ShapeDtypeStruct((t, d // 2), jnp.uint32)],
        compiler_params=_params(1), name='norm_mod_pack',
    )(x2, g.reshape(1, d), sc.reshape(bsz, 1, d), sh.reshape(bsz, 1, d))


def _moe_group_body(te_ref, nu_ref, idx_hbm, h_hbm, w1_ref, w3_ref, w2_ref, y_hbm,
                    idx_smem, xbuf, ybuf, isem, gsem, ssem, *, tmr, dump_row, n_tiles):
    i = pl.program_id(0)
    n = nu_ref[0]
    dh = xbuf.shape[-1]
    n_groups = tmr // SUBLANES
    n_up = n_groups // 2
    n_down = n_up // 2
    kc = 2 * dh // n_up

    def idx_copy(tile, rec):
        tile = jnp.minimum(tile, n_tiles - 1)
        return pltpu.make_async_copy(idx_hbm.at[pl.ds(pl.multiple_of(tile * IDX_REC, IDX_REC), IDX_REC)],
                                     idx_smem.at[pl.ds(pl.multiple_of(rec * IDX_REC, IDX_REC), IDX_REC)],
                                     isem.at[rec])

    def split(row):
        return lax.shift_right_logical(row, 3), row & (SUBLANES - 1)

    def row_in(tok, g, u, s):
        hi, lo = split(tok)
        return pltpu.make_async_copy(h_hbm.at[hi, pl.ds(lo, 1)], xbuf.at[s, g, pl.ds(u, 1)], gsem.at[s])

    def row_out(dst, g, u, s):
        hi, lo = split(dst)
        return pltpu.make_async_copy(ybuf.at[s, g, pl.ds(u, 1)], y_hbm.at[hi, pl.ds(lo, 1)], ssem.at[s])

    def gather_group(g, rec, s):
        for u in range(SUBLANES):
            row_in(idx_smem[rec * IDX_REC + g * SUBLANES + u], g, u, s).start(priority=u % 2)

    def scatter_group(g, rec, s):
        for u in range(SUBLANES):
            row_out(idx_smem[rec * IDX_REC + tmr + g * SUBLANES + u], g, u, s).start(priority=u % 2)

    def per_group(fn):
        def body(g, c):
            fn(g)
            return c
        lax.fori_loop(0, n_groups, body, 0)

    def gather_wait(s):
        per_group(lambda g: [row_in(jnp.int32(0), g, u, s).wait() for u in range(SUBLANES)])

    def scatter_wait(s):
        per_group(lambda g: [row_out(jnp.int32(dump_row), g, u, s).wait() for u in range(SUBLANES)])

    @pl.when(i == 0)
    def _prologue():
        ybuf[2] = jnp.zeros(ybuf.shape[1:], ybuf.dtype)
        for region in range(2):
            init = pltpu.make_async_copy(
                ybuf.at[2], y_hbm.at[pl.ds((dump_row + region * tmr) // SUBLANES, n_groups)], ssem.at[region])
            init.start()
            init.wait()

        def fill(r, c):
            idx_smem[4 * IDX_REC + tmr + r] = dump_row + 2 * tmr + r
            return c
        lax.fori_loop(0, tmr, fill, 0)
        for k in range(2):
            idx_copy(k, k).start()
            idx_copy(k, k).wait()
        idx_copy(2, 2).start()
        per_group(lambda g: gather_group(g, 0, 0))
        per_group(lambda g: gather_group(g, 1, 1))

    @pl.when((i >= 2) & (i < n))
    def _():
        scatter_wait(i % 3)

    @pl.when(i < n)
    def _tile():
        s, s_in, s_out = i % 3, (i + 2) % 3, (i + 2) % 3
        r_in, r_out = (i + 2) % 5, (i + 4) % 5
        idx_copy(i + 2, r_in).wait()
        gather_wait(s)
        h1 = jnp.zeros((tmr, w1_ref.shape[-1]), F32)
        h3 = jnp.zeros((tmr, w1_ref.shape[-1]), F32)
        for c in range(n_up):
            gather_group(2 * c, r_in, s_in)
            gather_group(2 * c + 1, r_in, s_in)
            pc = c % (n_up // 2)
            u = xbuf[s, :, :, pc * kc:(pc + 1) * kc].reshape(tmr, kc)
            xc = (_unpack_lo(u) if c < n_up // 2 else _unpack_hi(u)).astype(BF16)
            h1 = h1 + _dot(xc, w1_ref[c * kc:(c + 1) * kc, :].astype(BF16))
            h3 = h3 + _dot(xc, w3_ref[c * kc:(c + 1) * kc, :].astype(BF16))
        hid = ((h1 * jax.nn.sigmoid(h1)) * h3).astype(BF16)
        for c in range(n_down):
            for j in range(n_groups // n_down):
                scatter_group(c * (n_groups // n_down) + j, r_out, s_out)
            y_lo = _dot(hid, w2_ref[:, c * kc:(c + 1) * kc].astype(BF16))
            y_hi = _dot(hid, w2_ref[:, dh + c * kc:dh + (c + 1) * kc].astype(BF16))
            ybuf[s, :, :, c * kc:(c + 1) * kc] = _pack_bf16_pairs(y_lo, y_hi).reshape(n_groups, SUBLANES, kc)
        idx_copy(i + 3, (i + 3) % 5).start()

    @pl.when(i == n)
    def _drain():
        per_group(lambda g: scatter_group(g, (i + 4) % 5, (i + 2) % 3))
        scatter_wait(i % 3)
        scatter_wait((i + 1) % 3)
        scatter_wait((i + 2) % 3)
        gather_wait(i % 3)
        gather_wait((i + 1) % 3)
        idx_copy(i + 2, (i + 2) % 5).wait()


def _moe_grouped(h2p, idx, tile_expert, n_used, w1, w3, w2, layer, n_slots):
    t, dh = h2p.shape
    d = 2 * dh
    n_tiles = idx.shape[0] // IDX_REC
    tmr = MOE_ROWS
    f = w1.shape[3]
    dump_row = n_slots * t
    n_rows = n_slots * t + 3 * tmr
    assert n_tiles >= 4
    wmap = lambda i, te, nu: (layer, te[i], 0, 0)
    rows = lambda n_buf: pltpu.VMEM((n_buf, tmr // SUBLANES, SUBLANES, dh), jnp.uint32)
    grid_spec = pltpu.PrefetchScalarGridSpec(
        num_scalar_prefetch=2, grid=(n_tiles + 1,),
        in_specs=[
            pl.BlockSpec(memory_space=pl.ANY), pl.BlockSpec(memory_space=pl.ANY),
            pl.BlockSpec((None, None, d, f), wmap), pl.BlockSpec((None, None, d, f), wmap),
            pl.BlockSpec((None, None, f, d), wmap),
        ],
        out_specs=pl.BlockSpec(memory_space=pl.ANY),
        scratch_shapes=[
            pltpu.SMEM((5 * IDX_REC,), jnp.int32), rows(3), rows(3),
            pltpu.SemaphoreType.DMA((5,)), pltpu.SemaphoreType.DMA((3,)), pltpu.SemaphoreType.DMA((3,)),
        ])
    y = pl.pallas_call(
        functools.partial(_moe_group_body, tmr=tmr, dump_row=dump_row, n_tiles=n_tiles), grid_spec=grid_spec,
        out_shape=jax.ShapeDtypeStruct((n_rows // SUBLANES, SUBLANES, dh), jnp.uint32),
        compiler_params=_params(1), name='moe_grouped',
    )(tile_expert, n_used, idx, h2p.reshape(t // SUBLANES, SUBLANES, dh), w1, w3, w2)
    return y.reshape(n_rows, dh)


def _moe_plan(eidx, rank, counts, n_experts):
    t, k = eidx.shape
    tmr = MOE_ROWS
    n_tiles = (t * k) // tmr + n_experts
    padded = ((counts + tmr - 1) // tmr) * tmr
    ends = jnp.cumsum(padded)
    offsets = ends - padded
    n_used = (ends[-1] // tmr).astype(jnp.int32)
    pos = (offsets[eidx] + rank).reshape(-1)
    dst = (jnp.arange(k, dtype=jnp.int32)[None, :] * t + jnp.arange(t, dtype=jnp.int32)[:, None]).reshape(-1)
    p = jnp.arange(n_tiles * tmr, dtype=jnp.int32)
    pad_dst = k * t + ((p // tmr) % 3) * tmr + p % tmr
    dst_sorted = pad_dst.at[pos].set(dst).reshape(n_tiles, tmr)
    tok_sorted = jnp.where(dst_sorted < k * t, dst_sorted % t, dst_sorted % tmr)
    idx = jnp.concatenate([tok_sorted, dst_sorted, jnp.zeros((n_tiles, IDX_REC - 2 * tmr), jnp.int32)],
                          axis=1).reshape(-1)
    tile_start = jnp.minimum(jnp.arange(n_tiles + 1, dtype=jnp.int32), n_used - 1) * tmr
    tile_expert = jnp.sum((ends[None, :] <= tile_start[:, None]).astype(jnp.int32), axis=1)
    return idx, jnp.minimum(tile_expert, n_experts - 1), n_used.reshape(1)


def _shared_expert_body(a_ref, w1_ref, w3_ref, w2_ref, o_ref):
    a = a_ref[...]
    h1 = _dot(a, w1_ref[...])
    hid = (h1 * jax.nn.sigmoid(h1)) * _dot(a, w3_ref[...])
    o_ref[...] = _dot(hid.astype(BF16), w2_ref[...])


def _shared_expert(h2, w1, w3, w2, tm=512):
    t, d = h2.shape
    f = w1.shape[1]
    full = lambda shape: pl.BlockSpec(shape, lambda i: (0, 0))
    return pl.pallas_call(
        _shared_expert_body, grid=(t // tm,),
        in_specs=[pl.BlockSpec((tm, d), lambda i: (i, 0)), full((d, f)), full((d, f)), full((f, d))],
        out_specs=pl.BlockSpec((tm, d), lambda i: (i, 0)),
        out_shape=jax.ShapeDtypeStruct((t, d), F32), compiler_params=_params(1), name='shared_expert',
    )(h2, w1.astype(BF16), w3.astype(BF16), w2.astype(BF16))


def _moe_combine_body(x_ref, g_ref, gsel_ref, ysh_ref, *refs):
    y_refs, o_ref = refs[:-1], refs[-1]
    half = x_ref.shape[1] // 2
    gs = gsel_ref[...]
    acc_lo = ysh_ref[:, :half]
    acc_hi = ysh_ref[:, half:]
    for r, y_ref in enumerate(y_refs):
        u = y_ref[...]
        acc_lo = acc_lo + gs[:, r:r + 1] * _unpack_lo(u)
        acc_hi = acc_hi + gs[:, r:r + 1] * _unpack_hi(u)
    o_ref[:, :half] = x_ref[:, :half] + g_ref[:, :half] * acc_lo
    o_ref[:, half:] = x_ref[:, half:] + g_ref[:, half:] * acc_hi


def _moe_combine(x2, g2, gsel, y_shared, y, seq, tm=128):
    t, d = x2.shape
    bsz = t // seq
    k = gsel.shape[1]
    row = pl.BlockSpec((tm, d), lambda i: (i, 0))
    slot_specs = [pl.BlockSpec((tm, d // 2), lambda i, r=r: (r * (t // tm) + i, 0)) for r in range(k)]
    return pl.pallas_call(
        _moe_combine_body, grid=(t // tm,),
        in_specs=[row, pl.BlockSpec((None, 1, d), lambda i: (i // (seq // tm), 0, 0)),
                  pl.BlockSpec((tm, k), lambda i: (i, 0)), row] + slot_specs,
        out_specs=row, out_shape=jax.ShapeDtypeStruct((t, d), F32),
        compiler_params=_params(1), name='moe_combine',
    )(x2, g2.reshape(bsz, 1, d), gsel, y_shared, *([y] * k))


def kernel(x, c, w_ada, b_ada, ada_layer, norm_mix, norm_ffn, w_in, q_norm_a, k_norm_a, q_norm_b,
           k_norm_b, conv_w, conv_b, w_rgate, b_rgate, w_igate, b_igate, lru_lambda, w_branch_a,
           w_branch_b, w_branch_c, w_out, w_router, b_router, w1_exp, w3_exp, w2_exp, w1_shared,
           w3_shared, w2_shared):
    bsz, seq, d = x.shape
    t = bsz * seq
    depth = w_in.shape[0]
    mixw = d // 4
    a_cols = 3 * mixw + IDX_HEADS * IDX_DIM + IDX_DIM + IDX_HEADS
    off_b = a_cols
    off_c = off_b + N_DIL * 3 * mixw
    off_g = off_c + 2 * mixw
    top_k = min(DSA_TOPK_MAX, seq // 4)
    tm = 1024 if t % 1024 == 0 else 256
    tpb = seq // tm

    mod_shared = _ada_mod(c, w_ada, b_ada).reshape(bsz, N_MOD, d)
    x2 = x.reshape(t, d)
    for l in range(depth):
        mod = mod_shared + ada_layer[l]
        sh1, sc1, g1, sh2, sc2, g2 = [mod[:, j] for j in range(N_MOD)]
        h = _norm_mod(x2, norm_mix[l], sc1, sh1, seq)
        wl = w_in[l]

        n_main = 3 * mixw + IDX_HEADS * IDX_DIM
        pa = _matmul(h, wl[:, :n_main].astype(BF16), BF16, tm=tm, tn=512, name='proj_a')
        w_small = jnp.zeros((d, LANES), BF16).at[:, :IDX_DIM + IDX_HEADS].set(
            wl[:, n_main:a_cols].astype(BF16))
        small = _matmul(h, w_small, F32, tm=tm, tn=LANES, name='proj_a_idx')
        ik = small[:, :IDX_DIM].astype(BF16)
        zeros = jnp.zeros_like(ik)
        ik2 = jnp.concatenate([ik, zeros, zeros, ik], axis=1)
        qn, kn = _qk_norm(pa, q_norm_a[l], k_norm_a[l], mixw)
        iwt = small[:, IDX_DIM:IDX_DIM + IDX_HEADS].reshape(bsz, seq, IDX_HEADS).transpose(0, 2, 1)
        vt = pa[:, 2 * mixw:3 * mixw].reshape(bsz, seq, mixw // HEAD_DIM, HEAD_DIM).transpose(0, 2, 3, 1)
        o_a = _dsa_attention(qn, kn, pa, ik2, iwt, vt, seq, top_k)

        outs, lses = [], []
        for g, (window, dil) in enumerate(DIL_PATTERNS):
            w_g = wl[:, off_b + g * 3 * mixw:off_b + (g + 1) * 3 * mixw].astype(BF16)
            qkv = _to_classes(_matmul(h, w_g, BF16, tm=tm, tn=512, name='proj_b'), bsz, seq, dil)
            og, lg = _dilated_group(qkv, q_norm_b[l, g], k_norm_b[l, g], bsz * dil, seq // dil,
                                    band=window // dil)
            outs.append(_from_classes(og, bsz, seq, dil))
            lses.append(_from_classes(lg, bsz, seq, dil))
        o_b = _dil_merge(outs, lses)

        pc = _matmul(h, wl[:, off_c:off_g].astype(BF16), F32, tm=tm, tn=512, name='proj_c')
        o_c = _rg_lru(pc, conv_w[l], conv_b[l], w_rgate[l], b_rgate[l], w_igate[l], b_igate[l],
                      lru_lambda[l], seq)

        w_gates = wl[:, off_g:].astype(BF16)
        tn_merge = 256
        nd = d // tn_merge
        mixed = _tiled_call(
            _branch_merge_body, t, d, tm, tn_merge,
            [(h, 'row'), (o_a, 'row'), (o_b, 'row'), (o_c, 'row'),
             (w_gates, ('coloff', 0)), (w_gates, ('coloff', nd)), (w_gates, ('coloff', 2 * nd)),
             (w_branch_a[l].astype(BF16), 'col'), (w_branch_b[l].astype(BF16), 'col'),
             (w_branch_c[l].astype(BF16), 'col')],
            [(d, BF16)], name='branch_merge')
        x2 = _tiled_call(
            _mm_residual_body, t, d, tm, 512,
            [(mixed, 'row'), (w_out[l].astype(BF16), 'col'), (x2, 'tile'), (g1.reshape(bsz, 1, d), 'bvec')],
            [(d, F32)], tiles_per_batch=tpb, name='out_proj')

        h2, h2p = _norm_mod_pack(x2, norm_ffn[l], sc2, sh2, seq)
        gsel, eidx, rank, cnt = _router(h2, w_router[l], b_router[l])
        idx, tile_expert, n_used = _moe_plan(eidx, rank, cnt[0, :N_EXPERTS].astype(jnp.int32), N_EXPERTS)
        y = _moe_grouped(h2p, idx, tile_expert, n_used, w1_exp, w3_exp, w2_exp, l, MOE_TOP_K)
        y_shared = _shared_expert(h2, w1_shared[l], w3_shared[l], w2_shared[l])
        x2 = _moe_combine(x2, g2, gsel, y_shared, y, seq)
    return x2.reshape(bsz, seq, d)
```

```python
import functools

import jax
import jax.numpy as jnp
from jax import lax
from jax.experimental import pallas as pl
from jax.experimental.pallas import tpu as pltpu

HEAD_DIM = 128
A_HEADS = 8
IDX_HEADS = 16
IDX_DIM = 64
DSA_TOPK_MAX = 256
DIL_PATTERNS = ((128, 1), (512, 4), (2048, 16))
N_DIL = 3
B_HEADS = 8
LRU_BLOCKS = 16
CONV_WIDTH = 4
LRU_C = 8.0
N_EXPERTS = 64
MOE_TOP_K = 8
ROUTED_SCALE = 2.5
N_MOD = 6
RMS_EPS = 1e-6

LANES = 128
SUBLANES = 8
VMEM_LIMIT_BYTES = 56 * 1024 * 1024

NEG_BIG = -1e30
LOG2_E = 1.4426950408889634
INT_MIN = -(2 ** 31)

BF16 = jnp.bfloat16
F32 = jnp.float32


def _params(n_axes):
    return pltpu.CompilerParams(dimension_semantics=("arbitrary",) * n_axes,
                                vmem_limit_bytes=VMEM_LIMIT_BYTES)


def _dot(a, b):
    return jnp.dot(a, b, preferred_element_type=F32)


def _dot_nt(a, b):
    return lax.dot_general(a, b, (((1,), (1,)), ((), ())), preferred_element_type=F32)


def _tiled_call(body, m, n, tm, tn, ins, outs, *, tiles_per_batch=None, name=None):
    assert m % tm == 0 and n % tn == 0, (m, n, tm, tn)
    grid = (m // tm, n // tn)
    in_specs, arrays = [], []
    for arr, kind in ins:
        arrays.append(arr)
        if kind == 'row':
            in_specs.append(pl.BlockSpec((tm, arr.shape[1]), lambda i, j: (i, 0)))
        elif isinstance(kind, tuple) and kind[0] == 'rowoff':
            _, off, width = kind
            in_specs.append(pl.BlockSpec((tm, width), lambda i, j, off=off: (i, off)))
        elif kind == 'col':
            in_specs.append(pl.BlockSpec((arr.shape[0], tn), lambda i, j: (0, j)))
        elif isinstance(kind, tuple) and kind[0] == 'coloff':
            in_specs.append(pl.BlockSpec((arr.shape[0], tn), lambda i, j, off=kind[1]: (0, j + off)))
        elif kind == 'col3':
            in_specs.append(pl.BlockSpec((None, arr.shape[1], arr.shape[2]), lambda i, j: (j, 0, 0)))
        elif kind == 'tile':
            in_specs.append(pl.BlockSpec((tm, tn), lambda i, j: (i, j)))
        elif isinstance(kind, tuple) and kind[0] == 'tileoff':
            in_specs.append(pl.BlockSpec((tm, tn), lambda i, j, off=kind[1]: (i, j + off)))
        elif kind == 'vec':
            in_specs.append(pl.BlockSpec((1, tn), lambda i, j: (0, j)))
        elif kind == 'bvec':
            tpb = tiles_per_batch
            in_specs.append(pl.BlockSpec((None, 1, tn), lambda i, j, tpb=tpb: (i // tpb, 0, j)))
        elif kind == 'full':
            nd = arr.ndim
            in_specs.append(pl.BlockSpec(arr.shape, lambda i, j, nd=nd: (0,) * nd))
        else:
            raise ValueError(kind)
    out_shape, out_specs = [], []
    for n_cols, dtype in outs:
        assert (n_cols * tn) % n == 0
        w = n_cols * tn // n
        out_shape.append(jax.ShapeDtypeStruct((m, n_cols), dtype))
        out_specs.append(pl.BlockSpec((tm, w), lambda i, j: (i, j)))
    single = len(outs) == 1
    res = pl.pallas_call(
        body, grid=grid, in_specs=in_specs,
        out_specs=out_specs[0] if single else out_specs,
        out_shape=out_shape[0] if single else out_shape,
        compiler_params=_params(2), name=name)(*arrays)
    return res


def _mm_body(a_ref, w_ref, o_ref):
    o_ref[...] = _dot(a_ref[...], w_ref[...]).astype(o_ref.dtype)


def _mm_residual_body(a_ref, w_ref, x_ref, g_ref, o_ref):
    o_ref[...] = x_ref[...] + g_ref[...] * _dot(a_ref[...], w_ref[...])


def _matmul(a, w, out_dtype, *, tm, tn, body=_mm_body, name=None):
    return _tiled_call(body, a.shape[0], w.shape[1], tm, tn, [(a, 'row'), (w, 'col')],
                       [(w.shape[1], out_dtype)], name=name)


def _ada_body(c_ref, w_ref, b_ref, o_ref):
    c = c_ref[...]
    a = (c * jax.nn.sigmoid(c)).astype(BF16)
    o_ref[...] = _dot(a, w_ref[...].astype(BF16)) + b_ref[...]


def _ada_mod(c, w_ada, b_ada):
    bsz, d = c.shape
    n = w_ada.shape[1]
    c_pad = jnp.zeros((SUBLANES, d), F32).at[:bsz].set(c)
    out = _tiled_call(_ada_body, SUBLANES, n, SUBLANES, 512,
                      [(c_pad, 'row'), (w_ada, 'col'), (b_ada.reshape(1, n), 'vec')],
                      [(n, F32)], name='ada_mod')
    return out[:bsz]


def _norm_mod_body(x_ref, g_ref, sc_ref, sh_ref, o_ref):
    x = x_ref[...]
    y = x * lax.rsqrt(jnp.mean(x * x, axis=-1, keepdims=True) + RMS_EPS) * g_ref[...]
    o_ref[...] = (y * (1.0 + sc_ref[...]) + sh_ref[...]).astype(o_ref.dtype)


def _norm_mod(x2, g, sc, sh, seq, tm=256):
    t, d = x2.shape
    bsz = t // seq
    return _tiled_call(_norm_mod_body, t, d, tm, d,
                       [(x2, 'tile'), (g.reshape(1, d), 'vec'), (sc.reshape(bsz, 1, d), 'bvec'),
                        (sh.reshape(bsz, 1, d), 'bvec')],
                       [(d, BF16)], tiles_per_batch=seq // tm, name='norm_mod')


def _head_norm(x, g):
    outs = []
    for h in range(x.shape[1] // HEAD_DIM):
        xh = x[:, h * HEAD_DIM:(h + 1) * HEAD_DIM].astype(F32)
        outs.append(xh * lax.rsqrt(jnp.mean(xh * xh, axis=-1, keepdims=True) + RMS_EPS) * g)
    return jnp.concatenate(outs, axis=1)


def _qk_norm_body(q_ref, k_ref, gq_ref, gk_ref, qo_ref, ko_ref):
    qo_ref[...] = _head_norm(q_ref[...], gq_ref[...]).astype(qo_ref.dtype)
    ko_ref[...] = _head_norm(k_ref[...], gk_ref[...]).astype(ko_ref.dtype)


def _qk_norm(pa, gq, gk, width, tm=512):
    t = pa.shape[0]
    return _tiled_call(_qk_norm_body, t, width, tm, width,
                       [(pa, ('tileoff', 0)), (pa, ('tileoff', 1)),
                        (gq.reshape(1, HEAD_DIM), 'full'), (gk.reshape(1, HEAD_DIM), 'full')],
                       [(width, BF16), (width, BF16)], name='qk_norm_a')


def _dsa_body(q_ref, iq_ref, iwt_ref, ik_ref, k_ref, vt_ref, o_ref,
              keys_ref, thr_ref, bias_ref, s_ref, p_ref, m_ref, l_ref, acc_ref, *, tq, tk, top_k, n_heads):
    qb = pl.program_id(1)
    kb = pl.program_id(2)
    kb_last = ((qb + 1) * tq - 1) // tk
    n_chunks = kb_last + 1
    kpos = lax.broadcasted_iota(jnp.int32, (tk, tq), 0)
    qpos = lax.broadcasted_iota(jnp.int32, (tk, tq), 1) + qb * tq
    n_acc = 4 * SUBLANES

    @pl.when(kb == 0)
    def _scores_and_threshold():
        w = iwt_ref[...] * (IDX_HEADS ** -0.5 * IDX_DIM ** -0.5)

        def score_chunk(c, carry):
            ik2 = ik_ref[pl.ds(pl.multiple_of(c * tk, tk), tk), :]
            acc = jnp.zeros((tk, tq), F32)
            for p in range(IDX_HEADS // 2):
                iq_pair = iq_ref[:, p * LANES:(p + 1) * LANES]
                for half in range(2):
                    rel = _dot_nt(ik2[:, half * LANES:(half + 1) * LANES], iq_pair)
                    h = 2 * p + half
                    acc = acc + w[h:h + 1, :] * jnp.maximum(rel, 0.0)
            bits = pltpu.bitcast(acc, jnp.int32)
            key = jnp.where(bits < 0, bits ^ 0x7FFFFFFF, bits)
            key = jnp.where(kpos + c * tk <= qpos, key, INT_MIN)
            keys_ref[c] = key
            return carry

        lax.fori_loop(0, n_chunks, score_chunk, 0)

        def count_ge(cand):
            def body(c, cnt):
                for j in range(tk // n_acc):
                    blk = keys_ref[c, j * n_acc:(j + 1) * n_acc, :]
                    cnt = cnt + jnp.where(blk >= cand, 1.0, 0.0)
                return cnt
            cnt = lax.fori_loop(0, n_chunks, body, jnp.zeros((n_acc, tq), F32))
            return jnp.sum(cnt, axis=0, keepdims=True)

        k_f = float(top_k)
        t0 = jnp.where(count_ge(jnp.zeros((1, tq), jnp.int32)) >= k_f, 0, INT_MIN).astype(jnp.int32)

        def bit_body(i, t):
            cand = t | lax.shift_left(jnp.int32(1), 30 - i)
            return jnp.where(count_ge(cand) >= k_f, cand, t)

        t = lax.fori_loop(0, 31, bit_body, t0)
        thr_ref[...] = jnp.broadcast_to(t, thr_ref.shape)
        m_ref[...] = jnp.full(m_ref.shape, NEG_BIG, F32)
        l_ref[...] = jnp.zeros(l_ref.shape, F32)
        acc_ref[...] = jnp.zeros(acc_ref.shape, F32)

    @pl.when(kb <= kb_last)
    def _attend():
        sel = (keys_ref[kb] >= thr_ref[0:1, :]) & (kpos + kb * tk <= qpos)
        bias_ref[...] = jnp.where(sel, 0.0, NEG_BIG)
        c = HEAD_DIM ** -0.5 * LOG2_E

        def raw_scores(h):
            sl = slice(h * HEAD_DIM, (h + 1) * HEAD_DIM)
            s_ref[h % 4] = _dot_nt(k_ref[:, sl], q_ref[:, sl])

        def softmax(h):
            s = s_ref[h % 4] + bias_ref[...]
            m_old = m_ref[h, 0:1, :]
            m_new = jnp.maximum(m_old, jnp.max(s, axis=0, keepdims=True))
            p = jnp.exp2((s - m_new) * c)
            alpha = jnp.exp2((m_old - m_new) * c)
            l_new = alpha * l_ref[h, 0:1, :] + jnp.sum(p, axis=0, keepdims=True)
            p_ref[h % 3] = p.astype(BF16)
            m_ref[h] = jnp.broadcast_to(m_new, (SUBLANES, tq))
            l_ref[h] = jnp.broadcast_to(l_new, (SUBLANES, tq))
            return alpha

        def accumulate(h, alpha):
            acc_ref[h] = alpha * acc_ref[h] + _dot(vt_ref[h], p_ref[h % 3])

        s_ahead, p_ahead = 3, 2
        for h in range(s_ahead):
            raw_scores(h)
        alphas = {h: softmax(h) for h in range(p_ahead)}
        for h in range(n_heads):
            if h + s_ahead < n_heads:
                raw_scores(h + s_ahead)
            if h + p_ahead < n_heads:
                alphas[h + p_ahead] = softmax(h + p_ahead)
            accumulate(h, alphas.pop(h))

    @pl.when(kb == kb_last)
    def _finish():
        for h in range(n_heads):
            sl = slice(h * HEAD_DIM, (h + 1) * HEAD_DIM)
            o_ref[:, sl] = (acc_ref[h] / l_ref[h, 0:1, :]).T.astype(o_ref.dtype)


def _dsa_attention(qn, kn, pa, ik2, iwt, vt, seq, top_k, *, tq=256, tk=512):
    t, width = qn.shape
    bsz = t // seq
    nq, nk = seq // tq, seq // tk
    n_heads = width // HEAD_DIM
    assert width == IDX_HEADS * IDX_DIM and pa.shape[1] == 4 * width
    iq_off = 3

    def last(qb):
        return ((qb + 1) * tq - 1) // tk

    body = functools.partial(_dsa_body, tq=tq, tk=tk, top_k=top_k, n_heads=n_heads)
    return pl.pallas_call(
        body, grid=(bsz, nq, nk),
        in_specs=[
            pl.BlockSpec((tq, width), lambda b, qb, kb: (b * nq + qb, 0)),
            pl.BlockSpec((tq, width), lambda b, qb, kb: (b * nq + qb, iq_off)),
            pl.BlockSpec((None, IDX_HEADS, tq), lambda b, qb, kb: (b, 0, qb)),
            pl.BlockSpec((seq, 2 * LANES), lambda b, qb, kb: (b, 0)),
            pl.BlockSpec((tk, width), lambda b, qb, kb: (b * nk + jnp.minimum(kb, last(qb)), 0)),
            pl.BlockSpec((None, n_heads, HEAD_DIM, tk),
                         lambda b, qb, kb: (b, 0, 0, jnp.minimum(kb, last(qb)))),
        ],
        out_specs=pl.BlockSpec((tq, width), lambda b, qb, kb: (b * nq + qb, 0)),
        out_shape=jax.ShapeDtypeStruct((t, width), BF16),
        scratch_shapes=[
            pltpu.VMEM((nk, tk, tq), jnp.int32),
            pltpu.VMEM((SUBLANES, tq), jnp.int32),
            pltpu.VMEM((tk, tq), F32),
            pltpu.VMEM((4, tk, tq), F32),
            pltpu.VMEM((3, tk, tq), BF16),
            pltpu.VMEM((n_heads, SUBLANES, tq), F32),
            pltpu.VMEM((n_heads, SUBLANES, tq), F32),
            pltpu.VMEM((n_heads, HEAD_DIM, tq), F32),
        ],
        compiler_params=_params(3), name='dsa_attention',
    )(qn, pa, iwt, ik2, kn, vt)


def _dil_body(q_ref, kc_ref, vc_ref, kp_ref, vp_ref, gq_ref, gk_ref, o_ref, lse_ref,
              kcat_ref, vcat_ref, *, band, rows, n_heads):
    n = pl.program_id(1)
    kcat_ref[0:band, :] = _head_norm(kp_ref[...], gk_ref[...]).astype(BF16)
    kcat_ref[band:band + rows, :] = _head_norm(kc_ref[...], gk_ref[...]).astype(BF16)
    vcat_ref[0:band, :] = vp_ref[...]
    vcat_ref[band:band + rows, :] = vc_ref[...]
    qi = lax.broadcasted_iota(jnp.int32, (band, 2 * band), 0)
    ki = lax.broadcasted_iota(jnp.int32, (band, 2 * band), 1)
    back = qi + band - ki
    in_band = (back >= 0) & (back <= band)
    scale = HEAD_DIM ** -0.5
    gq = gq_ref[...]

    def sub_block(j, carry):
        r0 = pl.multiple_of(j * band, band)
        mask = in_band & ((ki >= band) | (n * (rows // band) + j > 0))
        for h in range(n_heads):
            sl = slice(h * HEAD_DIM, (h + 1) * HEAD_DIM)
            qh = q_ref[pl.ds(r0, band), sl].astype(F32)
            qh = (qh * lax.rsqrt(jnp.mean(qh * qh, axis=-1, keepdims=True) + RMS_EPS) * gq).astype(BF16)
            s = _dot_nt(qh, kcat_ref[pl.ds(r0, 2 * band), sl]) * scale
            s = jnp.where(mask, s, NEG_BIG)
            m = jnp.max(s, axis=1, keepdims=True)
            p = jnp.exp(s - m)
            l = jnp.sum(p, axis=1, keepdims=True)
            o = _dot(p.astype(BF16), vcat_ref[pl.ds(r0, 2 * band), sl]) / l
            o_ref[pl.ds(r0, band), sl] = o.astype(o_ref.dtype)
            lse_ref[pl.ds(r0, band), h:h + 1] = m + jnp.log(l)
        return carry

    lax.fori_loop(0, rows // band, sub_block, 0)


def _dilated_group(qkv, gq, gk, n_seq, len_seq, *, band=128, rows=512):
    t, w3 = qkv.shape
    width = w3 // 3
    n_heads = width // HEAD_DIM
    rows = min(rows, len_seq)
    nblk = len_seq // rows
    sub = rows // band

    def cur(c):
        return lambda s, n: (s * nblk + n, c)

    def prev(c):
        return lambda s, n: (s * nblk * sub + jnp.maximum(n * sub - 1, 0), c)

    body = functools.partial(_dil_body, band=band, rows=rows, n_heads=n_heads)
    return pl.pallas_call(
        body, grid=(n_seq, nblk),
        in_specs=[
            pl.BlockSpec((rows, width), cur(0)), pl.BlockSpec((rows, width), cur(1)),
            pl.BlockSpec((rows, width), cur(2)),
            pl.BlockSpec((band, width), prev(1)), pl.BlockSpec((band, width), prev(2)),
            pl.BlockSpec((1, HEAD_DIM), lambda s, n: (0, 0)), pl.BlockSpec((1, HEAD_DIM), lambda s, n: (0, 0)),
        ],
        out_specs=[pl.BlockSpec((rows, width), lambda s, n: (s * nblk + n, 0)),
                   pl.BlockSpec((rows, n_heads), lambda s, n: (s * nblk + n, 0))],
        out_shape=[jax.ShapeDtypeStruct((t, width), BF16), jax.ShapeDtypeStruct((t, n_heads), F32)],
        scratch_shapes=[pltpu.VMEM((band + rows, width), BF16), pltpu.VMEM((band + rows, width), BF16)],
        compiler_params=_params(2), name='dilated_attention',
    )(qkv, qkv, qkv, qkv, qkv, gq.reshape(1, HEAD_DIM), gk.reshape(1, HEAD_DIM))


def _dil_merge_body(o0_ref, o1_ref, o2_ref, l0_ref, l1_ref, l2_ref, out_ref):
    l0, l1, l2 = l0_ref[...], l1_ref[...], l2_ref[...]
    m = jnp.maximum(jnp.maximum(l0, l1), l2)
    e0, e1, e2 = jnp.exp(l0 - m), jnp.exp(l1 - m), jnp.exp(l2 - m)
    den = e0 + e1 + e2
    a0, a1, a2 = e0 / den, e1 / den, e2 / den
    for h in range(l0.shape[1]):
        sl = slice(h * HEAD_DIM, (h + 1) * HEAD_DIM)
        out_ref[:, sl] = (a0[:, h:h + 1] * o0_ref[:, sl].astype(F32)
                          + a1[:, h:h + 1] * o1_ref[:, sl].astype(F32)
                          + a2[:, h:h + 1] * o2_ref[:, sl].astype(F32)).astype(out_ref.dtype)


def _dil_merge(outs, lses, tm=512):
    t, width = outs[0].shape
    nh = lses[0].shape[1]
    ospec = pl.BlockSpec((tm, width), lambda i: (i, 0))
    lspec = pl.BlockSpec((tm, nh), lambda i: (i, 0))
    return pl.pallas_call(
        _dil_merge_body, grid=(t // tm,), in_specs=[ospec] * 3 + [lspec] * 3, out_specs=ospec,
        out_shape=jax.ShapeDtypeStruct((t, width), BF16), compiler_params=_params(1), name='dilated_merge',
    )(*outs, *lses)


def _to_classes(a, bsz, seq, dil):
    if dil == 1:
        return a
    c = a.shape[1]
    return a.reshape(bsz, seq // dil, dil, c).transpose(0, 2, 1, 3).reshape(bsz * seq, c)


def _from_classes(a, bsz, seq, dil):
    if dil == 1:
        return a
    c = a.shape[1]
    return a.reshape(bsz, dil, seq // dil, c).transpose(0, 2, 1, 3).reshape(bsz * seq, c)


def _lru_body(x_ref, y_ref, xp_ref, cw_ref, cb_ref, wr_ref, br_ref, wi_ref, bi_ref, lam_ref, o_ref,
              carry_ref, *, tm, tiles_per_batch):
    i = pl.program_id(0)
    first = (i % tiles_per_batch) == 0
    x = x_ref[...]
    prev = jnp.where(first, 0.0, xp_ref[...])
    xs = jnp.concatenate([prev, x], axis=0)
    xc = cb_ref[...] + cw_ref[CONV_WIDTH - 1:CONV_WIDTH, :] * x
    for j in range(CONV_WIDTH - 1):
        back = CONV_WIDTH - 1 - j
        xc = xc + cw_ref[j:j + 1, :] * xs[SUBLANES - back:SUBLANES - back + tm, :]
    xcb = xc.astype(BF16)
    r = jax.nn.sigmoid(_dot(xcb, wr_ref[...]) + br_ref[...])
    ig = jax.nn.sigmoid(_dot(xcb, wi_ref[...]) + bi_ref[...])
    lam = lam_ref[...]
    softplus_neg = jnp.maximum(-lam, 0.0) + jnp.log1p(jnp.exp(-jnp.abs(lam)))
    log_a = (-LRU_C * softplus_neg) * r
    a = jnp.exp(log_a)
    b = jnp.sqrt(1.0 - a * a) * (ig * xc)
    rows = lax.broadcasted_iota(jnp.int32, a.shape, 0)
    step = 1
    while step < tm:
        a_sh = pltpu.roll(a, step, 0)
        b_sh = pltpu.roll(b, step, 0)
        valid = rows >= step
        b = jnp.where(valid, a * b_sh + b, b)
        a = jnp.where(valid, a * a_sh, a)
        step *= 2
    h0 = jnp.where(first, 0.0, carry_ref[0:1, :])
    h = a * h0 + b
    carry_ref[...] = jnp.broadcast_to(h[tm - 1:tm, :], carry_ref.shape)
    o_ref[...] = (h * jax.nn.gelu(y_ref[...])).astype(o_ref.dtype)


def _block_diag(w):
    g, n, _ = w.shape
    eye = jnp.eye(g, dtype=w.dtype)
    return (eye[:, None, :, None] * w[:, :, None, :]).reshape(g * n, g * n)


def _rg_lru(pc, conv_w, conv_b, w_r, b_r, w_i, b_i, lam, seq, tm=256):
    t, w2 = pc.shape
    w = w2 // 2
    row = lambda v: v.reshape(1, w)
    full2 = lambda shape: pl.BlockSpec(shape, lambda i: (0, 0))
    body = functools.partial(_lru_body, tm=tm, tiles_per_batch=seq // tm)
    return pl.pallas_call(
        body, grid=(t // tm,),
        in_specs=[
            pl.BlockSpec((tm, w), lambda i: (i, 0)),
            pl.BlockSpec((tm, w), lambda i: (i, 1)),
            pl.BlockSpec((SUBLANES, w), lambda i: (jnp.maximum(i * (tm // SUBLANES) - 1, 0), 0)),
            full2((CONV_WIDTH, w)), full2((1, w)), full2((w, w)), full2((1, w)), full2((w, w)),
            full2((1, w)), full2((1, w)),
        ],
        out_specs=pl.BlockSpec((tm, w), lambda i: (i, 0)),
        out_shape=jax.ShapeDtypeStruct((t, w), BF16),
        scratch_shapes=[pltpu.VMEM((SUBLANES, w), F32)],
        compiler_params=_params(1), name='rg_lru',
    )(pc, pc, pc, conv_w, row(conv_b), _block_diag(w_r).astype(BF16), row(b_r),
      _block_diag(w_i).astype(BF16), row(b_i), row(lam))


def _branch_merge_body(h_ref, oa_ref, ob_ref, oc_ref, wga_ref, wgb_ref, wgc_ref, wa_ref, wb_ref, wc_ref, o_ref):
    h = h_ref[...]
    mixed = (jax.nn.sigmoid(_dot(h, wga_ref[...])) * _dot(oa_ref[...], wa_ref[...])
             + jax.nn.sigmoid(_dot(h, wgb_ref[...])) * _dot(ob_ref[...], wb_ref[...])
             + jax.nn.sigmoid(_dot(h, wgc_ref[...])) * _dot(oc_ref[...], wc_ref[...]))
    o_ref[...] = mixed.astype(o_ref.dtype)


MOE_ROWS = 256
IDX_REC = 1024


def _router_body(a_ref, w_ref, b_ref, gsel_ref, eidx_ref, rank_ref, cnt_ref, carry_ref, *, n_experts, top_k, tm):
    i = pl.program_id(0)

    @pl.when(i == 0)
    def _():
        carry_ref[...] = jnp.zeros(carry_ref.shape, F32)

    logits = _dot(a_ref[...], w_ref[...])
    lane = lax.broadcasted_iota(jnp.int32, logits.shape, 1).astype(F32)
    scores = jax.nn.sigmoid(logits)
    cur = jnp.where(lane < n_experts, scores + b_ref[...], -jnp.inf)
    mask = jnp.zeros(logits.shape, F32)
    hits, picked_scores = [], []
    for r in range(top_k):
        m = jnp.max(cur, axis=1, keepdims=True)
        idx = jnp.min(jnp.where(cur == m, lane, float(LANES)), axis=1, keepdims=True)
        hit = lane == idx
        hits.append(hit)
        picked_scores.append(jnp.sum(jnp.where(hit, scores, 0.0), axis=1, keepdims=True))
        eidx_ref[:, r:r + 1] = idx.astype(jnp.int32)
        cur = jnp.where(hit, -jnp.inf, cur)
        mask = mask + jnp.where(hit, 1.0, 0.0)
    total = picked_scores[0]
    for r in range(1, top_k):
        total = total + picked_scores[r]
    ri = lax.broadcasted_iota(jnp.int32, (tm, tm), 0)
    ci = lax.broadcasted_iota(jnp.int32, (tm, tm), 1)
    tri = jnp.where(ri > ci, 1.0, 0.0).astype(BF16)
    before = _dot(tri, mask.astype(BF16)) + carry_ref[0:1, :]
    for r in range(top_k):
        gsel_ref[:, r:r + 1] = picked_scores[r] / total * ROUTED_SCALE
        rank_ref[:, r:r + 1] = jnp.sum(jnp.where(hits[r], before, 0.0), axis=1, keepdims=True).astype(jnp.int32)
    carry_ref[...] = carry_ref[...] + jnp.sum(mask, axis=0, keepdims=True)
    cnt_ref[...] = carry_ref[...]


def _router(h2, w_router, b_router, tm=1024):
    t, d = h2.shape
    n_exp = w_router.shape[1]
    w_r = jnp.zeros((d, LANES), BF16).at[:, :n_exp].set(w_router.astype(BF16))
    b_r = jnp.zeros((1, LANES), F32).at[0, :n_exp].set(b_router.astype(F32))
    col = lambda dt: jax.ShapeDtypeStruct((t, MOE_TOP_K), dt)
    cspec = pl.BlockSpec((tm, MOE_TOP_K), lambda i: (i, 0))
    return pl.pallas_call(
        functools.partial(_router_body, n_experts=n_exp, top_k=MOE_TOP_K, tm=tm), grid=(t // tm,),
        in_specs=[pl.BlockSpec((tm, d), lambda i: (i, 0)), pl.BlockSpec((d, LANES), lambda i: (0, 0)),
                  pl.BlockSpec((1, LANES), lambda i: (0, 0))],
        out_specs=[cspec, cspec, cspec, pl.BlockSpec((SUBLANES, LANES), lambda i: (0, 0))],
        out_shape=[col(F32), col(jnp.int32), col(jnp.int32), jax.ShapeDtypeStruct((SUBLANES, LANES), F32)],
        scratch_shapes=[pltpu.VMEM((SUBLANES, LANES), F32)],
        compiler_params=_params(1), name='router')(h2, w_r, b_r)


def _pack_bf16_pairs(lo, hi):
    lo_bits = pltpu.bitcast(lo.astype(BF16).astype(F32), jnp.uint32)
    hi_bits = pltpu.bitcast(hi.astype(BF16).astype(F32), jnp.uint32)
    return (hi_bits & jnp.uint32(0xFFFF0000)) | lax.shift_right_logical(lo_bits, jnp.uint32(16))


def _unpack_lo(u):
    return pltpu.bitcast(lax.shift_left(u, jnp.uint32(16)), F32)


def _unpack_hi(u):
    return pltpu.bitcast(u & jnp.uint32(0xFFFF0000), F32)


def _norm_mod_pack_body(x_ref, g_ref, sc_ref, sh_ref, o_ref, p_ref):
    x = x_ref[...]
    y = x * lax.rsqrt(jnp.mean(x * x, axis=-1, keepdims=True) + RMS_EPS) * g_ref[...]
    y = y * (1.0 + sc_ref[...]) + sh_ref[...]
    half = y.shape[1] // 2
    o_ref[...] = y.astype(o_ref.dtype)
    p_ref[...] = _pack_bf16_pairs(y[:, :half], y[:, half:])


def _norm_mod_pack(x2, g, sc, sh, seq, tm=256):
    t, d = x2.shape
    bsz = t // seq
    row = pl.BlockSpec((tm, d), lambda i: (i, 0))
    vec = pl.BlockSpec((1, d), lambda i: (0, 0))
    bvec = pl.BlockSpec((None, 1, d), lambda i: (i // (seq // tm), 0, 0))
    return pl.pallas_call(
        _norm_mod_pack_body, grid=(t // tm,), in_specs=[row, vec, bvec, bvec],
        out_specs=[row, pl.BlockSpec((tm, d // 2), lambda i: (i, 0))],
        out_shape=[jax.ShapeDtypeStruct((t, d), BF16), jax.ShapeDtypeStruct((t, d // 2), jnp.uint32)],
        compiler_params=_params(1), name='norm_mod_pack',
    )(x2, g.reshape(1, d), sc.reshape(bsz, 1, d), sh.reshape(bsz, 1, d))


def _moe_group_body(te_ref, nu_ref, idx_hbm, h_hbm, w1_ref, w3_ref, w2_ref, y_hbm,
                    idx_smem, xbuf, ybuf, isem, gsem, ssem, *, tmr, dump_row, n_tiles):
    i = pl.program_id(0)
    n = nu_ref[0]
    dh = xbuf.shape[-1]
    n_groups = tmr // SUBLANES
    n_up = n_groups // 2
    n_down = n_up // 2
    kc = 2 * dh // n_up

    def idx_copy(tile, rec):
        tile = jnp.minimum(tile, n_tiles - 1)
        return pltpu.make_async_copy(idx_hbm.at[pl.ds(pl.multiple_of(tile * IDX_REC, IDX_REC), IDX_REC)],
                                     idx_smem.at[pl.ds(pl.multiple_of(rec * IDX_REC, IDX_REC), IDX_REC)],
                                     isem.at[rec])

    def split(row):
        return lax.shift_right_logical(row, 3), row & (SUBLANES - 1)

    def row_in(tok, g, u, s):
        hi, lo = split(tok)
        return pltpu.make_async_copy(h_hbm.at[hi, pl.ds(lo, 1)], xbuf.at[s, g, pl.ds(u, 1)], gsem.at[s])

    def row_out(dst, g, u, s):
        hi, lo = split(dst)
        return pltpu.make_async_copy(ybuf.at[s, g, pl.ds(u, 1)], y_hbm.at[hi, pl.ds(lo, 1)], ssem.at[s])

    def gather_group(g, rec, s):
        for u in range(SUBLANES):
            row_in(idx_smem[rec * IDX_REC + g * SUBLANES + u], g, u, s).start(priority=u % 2)

    def scatter_group(g, rec, s):
        for u in range(SUBLANES):
            row_out(idx_smem[rec * IDX_REC + tmr + g * SUBLANES + u], g, u, s).start(priority=u % 2)

    def per_group(fn):
        def body(g, c):
            fn(g)
            return c
        lax.fori_loop(0, n_groups, body, 0)

    def gather_wait(s):
        per_group(lambda g: [row_in(jnp.int32(0), g, u, s).wait() for u in range(SUBLANES)])

    def scatter_wait(s):
        per_group(lambda g: [row_out(jnp.int32(dump_row), g, u, s).wait() for u in range(SUBLANES)])

    @pl.when(i == 0)
    def _prologue():
        ybuf[2] = jnp.zeros(ybuf.shape[1:], ybuf.dtype)
        for region in range(2):
            init = pltpu.make_async_copy(
                ybuf.at[2], y_hbm.at[pl.ds((dump_row + region * tmr) // SUBLANES, n_groups)], ssem.at[region])
            init.start()
            init.wait()

        def fill(r, c):
            idx_smem[4 * IDX_REC + tmr + r] = dump_row + 2 * tmr + r
            return c
        lax.fori_loop(0, tmr, fill, 0)
        for k in range(2):
            idx_copy(k, k).start()
            idx_copy(k, k).wait()
        idx_copy(2, 2).start()
        per_group(lambda g: gather_group(g, 0, 0))
        per_group(lambda g: gather_group(g, 1, 1))

    @pl.when((i >= 2) & (i < n))
    def _():
        scatter_wait(i % 3)

    @pl.when(i < n)
    def _tile():
        s, s_in, s_out = i % 3, (i + 2) % 3, (i + 2) % 3
        r_in, r_out = (i + 2) % 5, (i + 4) % 5
        idx_copy(i + 2, r_in).wait()
        gather_wait(s)
        h1 = jnp.zeros((tmr, w1_ref.shape[-1]), F32)
        h3 = jnp.zeros((tmr, w1_ref.shape[-1]), F32)
        for c in range(n_up):
            gather_group(2 * c, r_in, s_in)
            gather_group(2 * c + 1, r_in, s_in)
            pc = c % (n_up // 2)
            u = xbuf[s, :, :, pc * kc:(pc + 1) * kc].reshape(tmr, kc)
            xc = (_unpack_lo(u) if c < n_up // 2 else _unpack_hi(u)).astype(BF16)
            h1 = h1 + _dot(xc, w1_ref[c * kc:(c + 1) * kc, :].astype(BF16))
            h3 = h3 + _dot(xc, w3_ref[c * kc:(c + 1) * kc, :].astype(BF16))
        hid = ((h1 * jax.nn.sigmoid(h1)) * h3).astype(BF16)
        for c in range(n_down):
            for j in range(n_groups // n_down):
                scatter_group(c * (n_groups // n_down) + j, r_out, s_out)
            y_lo = _dot(hid, w2_ref[:, c * kc:(c + 1) * kc].astype(BF16))
            y_hi = _dot(hid, w2_ref[:, dh + c * kc:dh + (c + 1) * kc].astype(BF16))
            ybuf[s, :, :, c * kc:(c + 1) * kc] = _pack_bf16_pairs(y_lo, y_hi).reshape(n_groups, SUBLANES, kc)
        idx_copy(i + 3, (i + 3) % 5).start()

    @pl.when(i == n)
    def _drain():
        per_group(lambda g: scatter_group(g, (i + 4) % 5, (i + 2) % 3))
        scatter_wait(i % 3)
        scatter_wait((i + 1) % 3)
        scatter_wait((i + 2) % 3)
        gather_wait(i % 3)
        gather_wait((i + 1) % 3)
        idx_copy(i + 2, (i + 2) % 5).wait()


def _moe_grouped(h2p, idx, tile_expert, n_used, w1, w3, w2, layer, n_slots):
    t, dh = h2p.shape
    d = 2 * dh
    n_tiles = idx.shape[0] // IDX_REC
    tmr = MOE_ROWS
    f = w1.shape[3]
    dump_row = n_slots * t
    n_rows = n_slots * t + 3 * tmr
    assert n_tiles >= 4
    wmap = lambda i, te, nu: (layer, te[i], 0, 0)
    rows = lambda n_buf: pltpu.VMEM((n_buf, tmr // SUBLANES, SUBLANES, dh), jnp.uint32)
    grid_spec = pltpu.PrefetchScalarGridSpec(
        num_scalar_prefetch=2, grid=(n_tiles + 1,),
        in_specs=[
            pl.BlockSpec(memory_space=pl.ANY), pl.BlockSpec(memory_space=pl.ANY),
            pl.BlockSpec((None, None, d, f), wmap), pl.BlockSpec((None, None, d, f), wmap),
            pl.BlockSpec((None, None, f, d), wmap),
        ],
        out_specs=pl.BlockSpec(memory_space=pl.ANY),
        scratch_shapes=[
            pltpu.SMEM((5 * IDX_REC,), jnp.int32), rows(3), rows(3),
            pltpu.SemaphoreType.DMA((5,)), pltpu.SemaphoreType.DMA((3,)), pltpu.SemaphoreType.DMA((3,)),
        ])
    y = pl.pallas_call(
        functools.partial(_moe_group_body, tmr=tmr, dump_row=dump_row, n_tiles=n_tiles), grid_spec=grid_spec,
        out_shape=jax.ShapeDtypeStruct((n_rows // SUBLANES, SUBLANES, dh), jnp.uint32),
        compiler_params=_params(1), name='moe_grouped',
    )(tile_expert, n_used, idx, h2p.reshape(t // SUBLANES, SUBLANES, dh), w1, w3, w2)
    return y.reshape(n_rows, dh)


def _moe_plan(eidx, rank, counts, n_experts):
    t, k = eidx.shape
    tmr = MOE_ROWS
    n_tiles = (t * k) // tmr + n_experts
    padded = ((counts + tmr - 1) // tmr) * tmr
    ends = jnp.cumsum(padded)
    offsets = ends - padded
    n_used = (ends[-1] // tmr).astype(jnp.int32)
    pos = (offsets[eidx] + rank).reshape(-1)
    dst = (jnp.arange(k, dtype=jnp.int32)[None, :] * t + jnp.arange(t, dtype=jnp.int32)[:, None]).reshape(-1)
    p = jnp.arange(n_tiles * tmr, dtype=jnp.int32)
    pad_dst = k * t + ((p // tmr) % 3) * tmr + p % tmr
    dst_sorted = pad_dst.at[pos].set(dst).reshape(n_tiles, tmr)
    tok_sorted = jnp.where(dst_sorted < k * t, dst_sorted % t, dst_sorted % tmr)
    idx = jnp.concatenate([tok_sorted, dst_sorted, jnp.zeros((n_tiles, IDX_REC - 2 * tmr), jnp.int32)],
                          axis=1).reshape(-1)
    tile_start = jnp.minimum(jnp.arange(n_tiles + 1, dtype=jnp.int32), n_used - 1) * tmr
    tile_expert = jnp.sum((ends[None, :] <= tile_start[:, None]).astype(jnp.int32), axis=1)
    return idx, jnp.minimum(tile_expert, n_experts - 1), n_used.reshape(1)


def _moe_combine_body(x_ref, g_ref, gsel_ref, h_ref, w1_ref, w3_ref, w2_ref, *refs):
    y_refs, o_ref = refs[:-1], refs[-1]
    half = x_ref.shape[1] // 2
    a = h_ref[...]
    h1 = _dot(a, w1_ref[...])
    hid = ((h1 * jax.nn.sigmoid(h1)) * _dot(a, w3_ref[...])).astype(BF16)
    acc_lo = _dot(hid, w2_ref[:, :half])
    acc_hi = _dot(hid, w2_ref[:, half:])
    gs = gsel_ref[...]
    for r, y_ref in enumerate(y_refs):
        u = y_ref[...]
        acc_lo = acc_lo + gs[:, r:r + 1] * _unpack_lo(u)
        acc_hi = acc_hi + gs[:, r:r + 1] * _unpack_hi(u)
    o_ref[:, :half] = x_ref[:, :half] + g_ref[:, :half] * acc_lo
    o_ref[:, half:] = x_ref[:, half:] + g_ref[:, half:] * acc_hi


def _moe_combine(x2, g2, gsel, h2, ws1, ws3, ws2, y, seq, tm=128):
    t, d = x2.shape
    bsz = t // seq
    k = gsel.shape[1]
    f = ws1.shape[1]
    row = pl.BlockSpec((tm, d), lambda i: (i, 0))
    full = lambda shape: pl.BlockSpec(shape, lambda i: (0, 0))
    slot_specs = [pl.BlockSpec((tm, d // 2), lambda i, r=r: (r * (t // tm) + i, 0)) for r in range(k)]
    return pl.pallas_call(
        _moe_combine_body, grid=(t // tm,),
        in_specs=[row, pl.BlockSpec((None, 1, d), lambda i: (i // (seq // tm), 0, 0)),
                  pl.BlockSpec((tm, k), lambda i: (i, 0)), row, full((d, f)), full((d, f)), full((f, d))] + slot_specs,
        out_specs=row, out_shape=jax.ShapeDtypeStruct((t, d), F32),
        compiler_params=_params(1), name='moe_combine',
    )(x2, g2.reshape(bsz, 1, d), gsel, h2, ws1.astype(BF16), ws3.astype(BF16), ws2.astype(BF16), *([y] * k))


def kernel(x, c, w_ada, b_ada, ada_layer, norm_mix, norm_ffn, w_in, q_norm_a, k_norm_a, q_norm_b,
           k_norm_b, conv_w, conv_b, w_rgate, b_rgate, w_igate, b_igate, lru_lambda, w_branch_a,
           w_branch_b, w_branch_c, w_out, w_router, b_router, w1_exp, w3_exp, w2_exp, w1_shared,
           w3_shared, w2_shared):
    bsz, seq, d = x.shape
    t = bsz * seq
    depth = w_in.shape[0]
    mixw = d // 4
    a_cols = 3 * mixw + IDX_HEADS * IDX_DIM + IDX_DIM + IDX_HEADS
    off_b = a_cols
    off_c = off_b + N_DIL * 3 * mixw
    off_g = off_c + 2 * mixw
    top_k = min(DSA_TOPK_MAX, seq // 4)
    tm = 1024 if t % 1024 == 0 else 256
    tpb = seq // tm

    mod_shared = _ada_mod(c, w_ada, b_ada).reshape(bsz, N_MOD, d)
    x2 = x.reshape(t, d)
    for l in range(depth):
        mod = mod_shared + ada_layer[l]
        sh1, sc1, g1, sh2, sc2, g2 = [mod[:, j] for j in range(N_MOD)]
        h = _norm_mod(x2, norm_mix[l], sc1, sh1, seq)
        wl = w_in[l]

        n_main = 3 * mixw + IDX_HEADS * IDX_DIM
        pa = _matmul(h, wl[:, :n_main].astype(BF16), BF16, tm=tm, tn=512, name='proj_a')
        w_small = jnp.zeros((d, LANES), BF16).at[:, :IDX_DIM + IDX_HEADS].set(
            wl[:, n_main:a_cols].astype(BF16))
        small = _matmul(h, w_small, F32, tm=tm, tn=LANES, name='proj_a_idx')
        ik = small[:, :IDX_DIM].astype(BF16)
        zeros = jnp.zeros_like(ik)
        ik2 = jnp.concatenate([ik, zeros, zeros, ik], axis=1)
        qn, kn = _qk_norm(pa, q_norm_a[l], k_norm_a[l], mixw)
        iwt = small[:, IDX_DIM:IDX_DIM + IDX_HEADS].reshape(bsz, seq, IDX_HEADS).transpose(0, 2, 1)
        vt = pa[:, 2 * mixw:3 * mixw].reshape(bsz, seq, mixw // HEAD_DIM, HEAD_DIM).transpose(0, 2, 3, 1)
        o_a = _dsa_attention(qn, kn, pa, ik2, iwt, vt, seq, top_k)

        outs, lses = [], []
        for g, (window, dil) in enumerate(DIL_PATTERNS):
            w_g = wl[:, off_b + g * 3 * mixw:off_b + (g + 1) * 3 * mixw].astype(BF16)
            qkv = _to_classes(_matmul(h, w_g, BF16, tm=tm, tn=512, name='proj_b'), bsz, seq, dil)
            og, lg = _dilated_group(qkv, q_norm_b[l, g], k_norm_b[l, g], bsz * dil, seq // dil,
                                    band=window // dil)
            outs.append(_from_classes(og, bsz, seq, dil))
            lses.append(_from_classes(lg, bsz, seq, dil))
        o_b = _dil_merge(outs, lses)

        pc = _matmul(h, wl[:, off_c:off_g].astype(BF16), F32, tm=tm, tn=512, name='proj_c')
        o_c = _rg_lru(pc, conv_w[l], conv_b[l], w_rgate[l], b_rgate[l], w_igate[l], b_igate[l],
                      lru_lambda[l], seq)

        w_gates = wl[:, off_g:].astype(BF16)
        tn_merge = 256
        nd = d // tn_merge
        mixed = _tiled_call(
            _branch_merge_body, t, d, tm, tn_merge,
            [(h, 'row'), (o_a, 'row'), (o_b, 'row'), (o_c, 'row'),
             (w_gates, ('coloff', 0)), (w_gates, ('coloff', nd)), (w_gates, ('coloff', 2 * nd)),
             (w_branch_a[l].astype(BF16), 'col'), (w_branch_b[l].astype(BF16), 'col'),
             (w_branch_c[l].astype(BF16), 'col')],
            [(d, BF16)], name='branch_merge')
        x2 = _tiled_call(
            _mm_residual_body, t, d, tm, 512,
            [(mixed, 'row'), (w_out[l].astype(BF16), 'col'), (x2, 'tile'), (g1.reshape(bsz, 1, d), 'bvec')],
            [(d, F32)], tiles_per_batch=tpb, name='out_proj')

        h2, h2p = _norm_mod_pack(x2, norm_ffn[l], sc2, sh2, seq)
        gsel, eidx, rank, cnt = _router(h2, w_router[l], b_router[l])
        idx, tile_expert, n_used = _moe_plan(eidx, rank, cnt[0, :N_EXPERTS].astype(jnp.int32), N_EXPERTS)
        y = _moe_grouped(h2p, idx, tile_expert, n_used, w1_exp, w3_exp, w2_exp, l, MOE_TOP_K)
        x2 = _moe_combine(x2, g2, gsel, h2, w1_shared[l], w3_shared[l], w2_shared[l], y, seq)
    return x2.reshape(bsz, seq, d)
```

```python
import functools

import jax
import jax.numpy as jnp
from jax import lax
from jax.experimental import pallas as pl
from jax.experimental.pallas import tpu as pltpu

HEAD_DIM = 128
A_HEADS = 8
IDX_HEADS = 16
IDX_DIM = 64
DSA_TOPK_MAX = 256
DIL_PATTERNS = ((128, 1), (512, 4), (2048, 16))
N_DIL = 3
B_HEADS = 8
LRU_BLOCKS = 16
CONV_WIDTH = 4
LRU_C = 8.0
N_EXPERTS = 64
MOE_TOP_K = 8
ROUTED_SCALE = 2.5
N_MOD = 6
RMS_EPS = 1e-6

LANES = 128
SUBLANES = 8
VMEM_LIMIT_BYTES = 56 * 1024 * 1024

NEG_BIG = -1e30
LOG2_E = 1.4426950408889634
INT_MIN = -(2 ** 31)

BF16 = jnp.bfloat16
F32 = jnp.float32


def _params(n_axes):
    return pltpu.CompilerParams(dimension_semantics=("arbitrary",) * n_axes,
                                vmem_limit_bytes=VMEM_LIMIT_BYTES)


def _dot(a, b):
    return jnp.dot(a, b, preferred_element_type=F32)


def _dot_nt(a, b):
    return lax.dot_general(a, b, (((1,), (1,)), ((), ())), preferred_element_type=F32)


def _tiled_call(body, m, n, tm, tn, ins, outs, *, tiles_per_batch=None, name=None):
    assert m % tm == 0 and n % tn == 0, (m, n, tm, tn)
    grid = (m // tm, n // tn)
    in_specs, arrays = [], []
    for arr, kind in ins:
        arrays.append(arr)
        if kind == 'row':
            in_specs.append(pl.BlockSpec((tm, arr.shape[1]), lambda i, j: (i, 0)))
        elif isinstance(kind, tuple) and kind[0] == 'rowoff':
            _, off, width = kind
            in_specs.append(pl.BlockSpec((tm, width), lambda i, j, off=off: (i, off)))
        elif kind == 'col':
            in_specs.append(pl.BlockSpec((arr.shape[0], tn), lambda i, j: (0, j)))
        elif isinstance(kind, tuple) and kind[0] == 'coloff':
            in_specs.append(pl.BlockSpec((arr.shape[0], tn), lambda i, j, off=kind[1]: (0, j + off)))
        elif kind == 'col3':
            in_specs.append(pl.BlockSpec((None, arr.shape[1], arr.shape[2]), lambda i, j: (j, 0, 0)))
        elif kind == 'tile':
            in_specs.append(pl.BlockSpec((tm, tn), lambda i, j: (i, j)))
        elif isinstance(kind, tuple) and kind[0] == 'tileoff':
            in_specs.append(pl.BlockSpec((tm, tn), lambda i, j, off=kind[1]: (i, j + off)))
        elif kind == 'vec':
            in_specs.append(pl.BlockSpec((1, tn), lambda i, j: (0, j)))
        elif kind == 'bvec':
            tpb = tiles_per_batch
            in_specs.append(pl.BlockSpec((None, 1, tn), lambda i, j, tpb=tpb: (i // tpb, 0, j)))
        elif kind == 'full':
            nd = arr.ndim
            in_specs.append(pl.BlockSpec(arr.shape, lambda i, j, nd=nd: (0,) * nd))
        else:
            raise ValueError(kind)
    out_shape, out_specs = [], []
    for n_cols, dtype in outs:
        assert (n_cols * tn) % n == 0
        w = n_cols * tn // n
        out_shape.append(jax.ShapeDtypeStruct((m, n_cols), dtype))
        out_specs.append(pl.BlockSpec((tm, w), lambda i, j: (i, j)))
    single = len(outs) == 1
    res = pl.pallas_call(
        body, grid=grid, in_specs=in_specs,
        out_specs=out_specs[0] if single else out_specs,
        out_shape=out_shape[0] if single else out_shape,
        compiler_params=_params(2), name=name)(*arrays)
    return res


def _mm_body(a_ref, w_ref, o_ref):
    o_ref[...] = _dot(a_ref[...], w_ref[...]).astype(o_ref.dtype)


def _mm_residual_body(a_ref, w_ref, x_ref, g_ref, o_ref):
    o_ref[...] = x_ref[...] + g_ref[...] * _dot(a_ref[...], w_ref[...])


def _matmul(a, w, out_dtype, *, tm, tn, body=_mm_body, name=None):
    return _tiled_call(body, a.shape[0], w.shape[1], tm, tn, [(a, 'row'), (w, 'col')],
                       [(w.shape[1], out_dtype)], name=name)


def _ada_body(c_ref, w_ref, b_ref, o_ref):
    c = c_ref[...]
    a = (c * jax.nn.sigmoid(c)).astype(BF16)
    o_ref[...] = _dot(a, w_ref[...].astype(BF16)) + b_ref[...]


def _ada_mod(c, w_ada, b_ada):
    bsz, d = c.shape
    n = w_ada.shape[1]
    c_pad = jnp.zeros((SUBLANES, d), F32).at[:bsz].set(c)
    out = _tiled_call(_ada_body, SUBLANES, n, SUBLANES, 1024,
                      [(c_pad, 'row'), (w_ada, 'col'), (b_ada.reshape(1, n), 'vec')],
                      [(n, F32)], name='ada_mod')
    return out[:bsz]


def _norm_mod_body(x_ref, g_ref, sc_ref, sh_ref, o_ref):
    x = x_ref[...]
    y = x * lax.rsqrt(jnp.mean(x * x, axis=-1, keepdims=True) + RMS_EPS) * g_ref[...]
    o_ref[...] = (y * (1.0 + sc_ref[...]) + sh_ref[...]).astype(o_ref.dtype)


def _norm_mod(x2, g, sc, sh, seq, tm=256):
    t, d = x2.shape
    bsz = t // seq
    return _tiled_call(_norm_mod_body, t, d, tm, d,
                       [(x2, 'tile'), (g.reshape(1, d), 'vec'), (sc.reshape(bsz, 1, d), 'bvec'),
                        (sh.reshape(bsz, 1, d), 'bvec')],
                       [(d, BF16)], tiles_per_batch=seq // tm, name='norm_mod')


def _head_norm(x, g):
    outs = []
    for h in range(x.shape[1] // HEAD_DIM):
        xh = x[:, h * HEAD_DIM:(h + 1) * HEAD_DIM].astype(F32)
        outs.append(xh * lax.rsqrt(jnp.mean(xh * xh, axis=-1, keepdims=True) + RMS_EPS) * g)
    return jnp.concatenate(outs, axis=1)


def _qk_norm_body(q_ref, k_ref, gq_ref, gk_ref, qo_ref, ko_ref):
    qo_ref[...] = _head_norm(q_ref[...], gq_ref[...]).astype(qo_ref.dtype)
    ko_ref[...] = _head_norm(k_ref[...], gk_ref[...]).astype(ko_ref.dtype)


def _qk_norm(pa, gq, gk, width, tm=512):
    t = pa.shape[0]
    return _tiled_call(_qk_norm_body, t, width, tm, width,
                       [(pa, ('tileoff', 0)), (pa, ('tileoff', 1)),
                        (gq.reshape(1, HEAD_DIM), 'full'), (gk.reshape(1, HEAD_DIM), 'full')],
                       [(width, BF16), (width, BF16)], name='qk_norm_a')


def _dsa_body(qb_ref, kb_ref, q_ref, iq_ref, iwt_ref, ik_ref, k_ref, vt_ref, o_ref,
              keys_ref, thr_ref, bias_ref, s_ref, p_ref, m_ref, l_ref, acc_ref, *, tq, tk, top_k, n_heads):
    qb = qb_ref[pl.program_id(1)]
    kb = kb_ref[pl.program_id(1)]
    kb_last = ((qb + 1) * tq - 1) // tk
    n_chunks = kb_last + 1
    kpos = lax.broadcasted_iota(jnp.int32, (tk, tq), 0)
    qpos = lax.broadcasted_iota(jnp.int32, (tk, tq), 1) + qb * tq
    n_acc = 4 * SUBLANES

    @pl.when(kb == 0)
    def _scores_and_threshold():
        w = iwt_ref[...] * (IDX_HEADS ** -0.5 * IDX_DIM ** -0.5)

        def score_chunk(c, carry):
            ik2 = ik_ref[pl.ds(pl.multiple_of(c * tk, tk), tk), :]
            acc = jnp.zeros((tk, tq), F32)
            for p in range(IDX_HEADS // 2):
                iq_pair = iq_ref[:, p * LANES:(p + 1) * LANES]
                for half in range(2):
                    rel = _dot_nt(ik2[:, half * LANES:(half + 1) * LANES], iq_pair)
                    h = 2 * p + half
                    acc = acc + w[h:h + 1, :] * jnp.maximum(rel, 0.0)
            bits = pltpu.bitcast(acc, jnp.int32)
            key = jnp.where(bits < 0, bits ^ 0x7FFFFFFF, bits)
            key = jnp.where(kpos + c * tk <= qpos, key, INT_MIN)
            keys_ref[c] = key
            return carry

        lax.fori_loop(0, n_chunks, score_chunk, 0)

        def count_ge(cand):
            def body(c, cnt):
                for j in range(tk // n_acc):
                    blk = keys_ref[c, j * n_acc:(j + 1) * n_acc, :]
                    cnt = cnt + jnp.where(blk >= cand, 1.0, 0.0)
                return cnt
            cnt = lax.fori_loop(0, n_chunks, body, jnp.zeros((n_acc, tq), F32))
            return jnp.sum(cnt, axis=0, keepdims=True)

        k_f = float(top_k)
        t0 = jnp.where(count_ge(jnp.zeros((1, tq), jnp.int32)) >= k_f, 0, INT_MIN).astype(jnp.int32)

        def bit_body(i, t):
            cand = t | lax.shift_left(jnp.int32(1), 30 - i)
            return jnp.where(count_ge(cand) >= k_f, cand, t)

        t = lax.fori_loop(0, 31, bit_body, t0)
        thr_ref[...] = jnp.broadcast_to(t, thr_ref.shape)
        m_ref[...] = jnp.full(m_ref.shape, NEG_BIG, F32)
        l_ref[...] = jnp.zeros(l_ref.shape, F32)
        acc_ref[...] = jnp.zeros(acc_ref.shape, F32)

    @pl.when(kb <= kb_last)
    def _attend():
        sel = (keys_ref[kb] >= thr_ref[0:1, :]) & (kpos + kb * tk <= qpos)
        bias_ref[...] = jnp.where(sel, 0.0, NEG_BIG)
        c = HEAD_DIM ** -0.5 * LOG2_E

        def raw_scores(h):
            sl = slice(h * HEAD_DIM, (h + 1) * HEAD_DIM)
            s_ref[h % 4] = _dot_nt(k_ref[:, sl], q_ref[:, sl])

        def softmax(h):
            s = s_ref[h % 4] + bias_ref[...]
            m_old = m_ref[h, 0:1, :]
            m_new = jnp.maximum(m_old, jnp.max(s, axis=0, keepdims=True))
            p = jnp.exp2((s - m_new) * c)
            alpha = jnp.exp2((m_old - m_new) * c)
            l_new = alpha * l_ref[h, 0:1, :] + jnp.sum(p, axis=0, keepdims=True)
            p_ref[h % 3] = p.astype(BF16)
            m_ref[h] = jnp.broadcast_to(m_new, (SUBLANES, tq))
            l_ref[h] = jnp.broadcast_to(l_new, (SUBLANES, tq))
            return alpha

        def accumulate(h, alpha):
            acc_ref[h] = alpha * acc_ref[h] + _dot(vt_ref[h], p_ref[h % 3])

        s_ahead, p_ahead = 3, 2
        for h in range(s_ahead):
            raw_scores(h)
        alphas = {h: softmax(h) for h in range(p_ahead)}
        for h in range(n_heads):
            if h + s_ahead < n_heads:
                raw_scores(h + s_ahead)
            if h + p_ahead < n_heads:
                alphas[h + p_ahead] = softmax(h + p_ahead)
            accumulate(h, alphas.pop(h))

    @pl.when(kb == kb_last)
    def _finish():
        for h in range(n_heads):
            sl = slice(h * HEAD_DIM, (h + 1) * HEAD_DIM)
            o_ref[:, sl] = (acc_ref[h] / l_ref[h, 0:1, :]).T.astype(o_ref.dtype)


def _dsa_attention(qn, kn, pa, ik2, iwt, vt, seq, top_k, *, tq=256, tk=512):
    t, width = qn.shape
    bsz = t // seq
    nq, nk = seq // tq, seq // tk
    n_heads = width // HEAD_DIM
    assert width == IDX_HEADS * IDX_DIM and pa.shape[1] == 4 * width
    iq_off = 3

    pairs = [(qb, kb) for qb in range(nq) for kb in range(((qb + 1) * tq - 1) // tk + 1)]
    qb_tab = jnp.asarray([p[0] for p in pairs], jnp.int32)
    kb_tab = jnp.asarray([p[1] for p in pairs], jnp.int32)

    body = functools.partial(_dsa_body, tq=tq, tk=tk, top_k=top_k, n_heads=n_heads)
    grid_spec = pltpu.PrefetchScalarGridSpec(
        num_scalar_prefetch=2, grid=(bsz, len(pairs)),
        in_specs=[
            pl.BlockSpec((tq, width), lambda b, s, qt, kt: (b * nq + qt[s], 0)),
            pl.BlockSpec((tq, width), lambda b, s, qt, kt: (b * nq + qt[s], iq_off)),
            pl.BlockSpec((None, IDX_HEADS, tq), lambda b, s, qt, kt: (b, 0, qt[s])),
            pl.BlockSpec((seq, 2 * LANES), lambda b, s, qt, kt: (b, 0)),
            pl.BlockSpec((tk, width), lambda b, s, qt, kt: (b * nk + kt[s], 0)),
            pl.BlockSpec((None, n_heads, HEAD_DIM, tk), lambda b, s, qt, kt: (b, 0, 0, kt[s])),
        ],
        out_specs=pl.BlockSpec((tq, width), lambda b, s, qt, kt: (b * nq + qt[s], 0)),
        scratch_shapes=[
            pltpu.VMEM((nk, tk, tq), jnp.int32),
            pltpu.VMEM((SUBLANES, tq), jnp.int32),
            pltpu.VMEM((tk, tq), F32),
            pltpu.VMEM((4, tk, tq), F32),
            pltpu.VMEM((3, tk, tq), BF16),
            pltpu.VMEM((n_heads, SUBLANES, tq), F32),
            pltpu.VMEM((n_heads, SUBLANES, tq), F32),
            pltpu.VMEM((n_heads, HEAD_DIM, tq), F32),
        ])
    return pl.pallas_call(
        body, grid_spec=grid_spec, out_shape=jax.ShapeDtypeStruct((t, width), BF16),
        compiler_params=_params(2), name='dsa_attention',
    )(qb_tab, kb_tab, qn, pa, iwt, ik2, kn, vt)


def _dil_body(q_ref, kc_ref, vc_ref, kp_ref, vp_ref, gq_ref, gk_ref, o_ref, lse_ref,
              kcat_ref, vcat_ref, *, band, rows, n_heads):
    n = pl.program_id(1)
    kcat_ref[0:band, :] = _head_norm(kp_ref[...], gk_ref[...]).astype(BF16)
    kcat_ref[band:band + rows, :] = _head_norm(kc_ref[...], gk_ref[...]).astype(BF16)
    vcat_ref[0:band, :] = vp_ref[...]
    vcat_ref[band:band + rows, :] = vc_ref[...]
    qi = lax.broadcasted_iota(jnp.int32, (band, 2 * band), 0)
    ki = lax.broadcasted_iota(jnp.int32, (band, 2 * band), 1)
    back = qi + band - ki
    in_band = (back >= 0) & (back <= band)
    scale = HEAD_DIM ** -0.5
    gq = gq_ref[...]

    def sub_block(j, carry):
        r0 = pl.multiple_of(j * band, band)
        bias = jnp.where(in_band & ((ki >= band) | (n * (rows // band) + j > 0)), 0.0, NEG_BIG)
        heads = [slice(h * HEAD_DIM, (h + 1) * HEAD_DIM) for h in range(n_heads)]

        def scores(sl):
            qh = q_ref[pl.ds(r0, band), sl].astype(F32)
            qh = (qh * lax.rsqrt(jnp.mean(qh * qh, axis=-1, keepdims=True) + RMS_EPS) * gq).astype(BF16)
            return _dot_nt(qh, kcat_ref[pl.ds(r0, 2 * band), sl]) * scale + bias

        s_all = [scores(sl) for sl in heads]
        m_all = [jnp.max(s, axis=1, keepdims=True) for s in s_all]
        p_all = [jnp.exp(s - m) for s, m in zip(s_all, m_all)]
        l_all = [jnp.sum(p, axis=1, keepdims=True) for p in p_all]
        for h, sl in enumerate(heads):
            o = _dot(p_all[h].astype(BF16), vcat_ref[pl.ds(r0, 2 * band), sl]) / l_all[h]
            o_ref[pl.ds(r0, band), sl] = o.astype(o_ref.dtype)
            lse_ref[pl.ds(r0, band), h:h + 1] = m_all[h] + jnp.log(l_all[h])
        return carry

    lax.fori_loop(0, rows // band, sub_block, 0)


def _dilated_group(qkv, gq, gk, n_seq, len_seq, *, band=128, rows=512):
    t, w3 = qkv.shape
    width = w3 // 3
    n_heads = width // HEAD_DIM
    rows = min(rows, len_seq)
    nblk = len_seq // rows
    sub = rows // band

    def cur(c):
        return lambda s, n: (s * nblk + n, c)

    def prev(c):
        return lambda s, n: (s * nblk * sub + jnp.maximum(n * sub - 1, 0), c)

    body = functools.partial(_dil_body, band=band, rows=rows, n_heads=n_heads)
    return pl.pallas_call(
        body, grid=(n_seq, nblk),
        in_specs=[
            pl.BlockSpec((rows, width), cur(0)), pl.BlockSpec((rows, width), cur(1)),
            pl.BlockSpec((rows, width), cur(2)),
            pl.BlockSpec((band, width), prev(1)), pl.BlockSpec((band, width), prev(2)),
            pl.BlockSpec((1, HEAD_DIM), lambda s, n: (0, 0)), pl.BlockSpec((1, HEAD_DIM), lambda s, n: (0, 0)),
        ],
        out_specs=[pl.BlockSpec((rows, width), lambda s, n: (s * nblk + n, 0)),
                   pl.BlockSpec((rows, n_heads), lambda s, n: (s * nblk + n, 0))],
        out_shape=[jax.ShapeDtypeStruct((t, width), BF16), jax.ShapeDtypeStruct((t, n_heads), F32)],
        scratch_shapes=[pltpu.VMEM((band + rows, width), BF16), pltpu.VMEM((band + rows, width), BF16)],
        compiler_params=_params(2), name='dilated_attention',
    )(qkv, qkv, qkv, qkv, qkv, gq.reshape(1, HEAD_DIM), gk.reshape(1, HEAD_DIM))


def _dil_merge_body(o0_ref, o1_ref, o2_ref, l0_ref, l1_ref, l2_ref, out_ref):
    l0, l1, l2 = l0_ref[...], l1_ref[...], l2_ref[...]
    m = jnp.maximum(jnp.maximum(l0, l1), l2)
    e0, e1, e2 = jnp.exp(l0 - m), jnp.exp(l1 - m), jnp.exp(l2 - m)
    den = e0 + e1 + e2
    a0, a1, a2 = e0 / den, e1 / den, e2 / den
    for h in range(l0.shape[1]):
        sl = slice(h * HEAD_DIM, (h + 1) * HEAD_DIM)
        out_ref[:, sl] = (a0[:, h:h + 1] * o0_ref[:, sl].astype(F32)
                          + a1[:, h:h + 1] * o1_ref[:, sl].astype(F32)
                          + a2[:, h:h + 1] * o2_ref[:, sl].astype(F32)).astype(out_ref.dtype)


def _dil_merge(outs, lses, tm=512):
    t, width = outs[0].shape
    nh = lses[0].shape[1]
    ospec = pl.BlockSpec((tm, width), lambda i: (i, 0))
    lspec = pl.BlockSpec((tm, nh), lambda i: (i, 0))
    return pl.pallas_call(
        _dil_merge_body, grid=(t // tm,), in_specs=[ospec] * 3 + [lspec] * 3, out_specs=ospec,
        out_shape=jax.ShapeDtypeStruct((t, width), BF16), compiler_params=_params(1), name='dilated_merge',
    )(*outs, *lses)


def _to_classes(a, bsz, seq, dil):
    if dil == 1:
        return a
    c = a.shape[1]
    return a.reshape(bsz, seq // dil, dil, c).transpose(0, 2, 1, 3).reshape(bsz * seq, c)


def _from_classes(a, bsz, seq, dil):
    if dil == 1:
        return a
    c = a.shape[1]
    return a.reshape(bsz, dil, seq // dil, c).transpose(0, 2, 1, 3).reshape(bsz * seq, c)


def _lru_body(x_ref, y_ref, xp_ref, cw_ref, cb_ref, wr_ref, br_ref, wi_ref, bi_ref, lam_ref, o_ref,
              carry_ref, *, tm, tiles_per_batch):
    i = pl.program_id(0)
    first = (i % tiles_per_batch) == 0
    x = x_ref[...]
    prev = jnp.where(first, 0.0, xp_ref[...])
    xs = jnp.concatenate([prev, x], axis=0)
    xc = cb_ref[...] + cw_ref[CONV_WIDTH - 1:CONV_WIDTH, :] * x
    for j in range(CONV_WIDTH - 1):
        back = CONV_WIDTH - 1 - j
        xc = xc + cw_ref[j:j + 1, :] * xs[SUBLANES - back:SUBLANES - back + tm, :]
    xcb = xc.astype(BF16)
    r = jax.nn.sigmoid(_dot(xcb, wr_ref[...]) + br_ref[...])
    ig = jax.nn.sigmoid(_dot(xcb, wi_ref[...]) + bi_ref[...])
    lam = lam_ref[...]
    softplus_neg = jnp.maximum(-lam, 0.0) + jnp.log1p(jnp.exp(-jnp.abs(lam)))
    log_a = (-LRU_C * softplus_neg) * r
    a = jnp.exp(log_a)
    b = jnp.sqrt(1.0 - a * a) * (ig * xc)
    rows = lax.broadcasted_iota(jnp.int32, a.shape, 0)
    step = 1
    while step < tm:
        a_sh = pltpu.roll(a, step, 0)
        b_sh = pltpu.roll(b, step, 0)
        valid = rows >= step
        b = jnp.where(valid, a * b_sh + b, b)
        a = jnp.where(valid, a * a_sh, a)
        step *= 2
    h0 = jnp.where(first, 0.0, carry_ref[0:1, :])
    h = a * h0 + b
    carry_ref[...] = jnp.broadcast_to(h[tm - 1:tm, :], carry_ref.shape)
    o_ref[...] = (h * jax.nn.gelu(y_ref[...])).astype(o_ref.dtype)


def _block_diag(w):
    g, n, _ = w.shape
    eye = jnp.eye(g, dtype=w.dtype)
    return (eye[:, None, :, None] * w[:, :, None, :]).reshape(g * n, g * n)


def _rg_lru(pc, conv_w, conv_b, w_r, b_r, w_i, b_i, lam, seq, tm=256):
    t, w2 = pc.shape
    w = w2 // 2
    row = lambda v: v.reshape(1, w)
    full2 = lambda shape: pl.BlockSpec(shape, lambda i: (0, 0))
    body = functools.partial(_lru_body, tm=tm, tiles_per_batch=seq // tm)
    return pl.pallas_call(
        body, grid=(t // tm,),
        in_specs=[
            pl.BlockSpec((tm, w), lambda i: (i, 0)),
            pl.BlockSpec((tm, w), lambda i: (i, 1)),
            pl.BlockSpec((SUBLANES, w), lambda i: (jnp.maximum(i * (tm // SUBLANES) - 1, 0), 0)),
            full2((CONV_WIDTH, w)), full2((1, w)), full2((w, w)), full2((1, w)), full2((w, w)),
            full2((1, w)), full2((1, w)),
        ],
        out_specs=pl.BlockSpec((tm, w), lambda i: (i, 0)),
        out_shape=jax.ShapeDtypeStruct((t, w), BF16),
        scratch_shapes=[pltpu.VMEM((SUBLANES, w), F32)],
        compiler_params=_params(1), name='rg_lru',
    )(pc, pc, pc, conv_w, row(conv_b), _block_diag(w_r).astype(BF16), row(b_r),
      _block_diag(w_i).astype(BF16), row(b_i), row(lam))


def _branch_merge_body(h_ref, oa_ref, ob_ref, oc_ref, wga_ref, wgb_ref, wgc_ref, wa_ref, wb_ref, wc_ref, o_ref):
    h = h_ref[...]
    mixed = (jax.nn.sigmoid(_dot(h, wga_ref[...])) * _dot(oa_ref[...], wa_ref[...])
             + jax.nn.sigmoid(_dot(h, wgb_ref[...])) * _dot(ob_ref[...], wb_ref[...])
             + jax.nn.sigmoid(_dot(h, wgc_ref[...])) * _dot(oc_ref[...], wc_ref[...]))
    o_ref[...] = mixed.astype(o_ref.dtype)


MOE_ROWS = 256
IDX_REC = 1024


def _router_body(a_ref, w_ref, b_ref, gsel_ref, eidx_ref, rank_ref, cnt_ref, carry_ref, *, n_experts, top_k, tm):
    i = pl.program_id(0)

    @pl.when(i == 0)
    def _():
        carry_ref[...] = jnp.zeros(carry_ref.shape, F32)

    logits = _dot(a_ref[...], w_ref[...])
    lane = lax.broadcasted_iota(jnp.int32, logits.shape, 1).astype(F32)
    scores = jax.nn.sigmoid(logits)
    cur = jnp.where(lane < n_experts, scores + b_ref[...], -jnp.inf)
    mask = jnp.zeros(logits.shape, F32)
    hits, picked_scores = [], []
    for r in range(top_k):
        m = jnp.max(cur, axis=1, keepdims=True)
        idx = jnp.min(jnp.where(cur == m, lane, float(LANES)), axis=1, keepdims=True)
        hit = lane == idx
        hits.append(hit)
        picked_scores.append(jnp.sum(jnp.where(hit, scores, 0.0), axis=1, keepdims=True))
        eidx_ref[:, r:r + 1] = idx.astype(jnp.int32)
        cur = jnp.where(hit, -jnp.inf, cur)
        mask = mask + jnp.where(hit, 1.0, 0.0)
    total = picked_scores[0]
    for r in range(1, top_k):
        total = total + picked_scores[r]
    ri = lax.broadcasted_iota(jnp.int32, (tm, tm), 0)
    ci = lax.broadcasted_iota(jnp.int32, (tm, tm), 1)
    tri = jnp.where(ri > ci, 1.0, 0.0).astype(BF16)
    before = _dot(tri, mask.astype(BF16)) + carry_ref[0:1, :]
    for r in range(top_k):
        gsel_ref[:, r:r + 1] = picked_scores[r] / total * ROUTED_SCALE
        rank_ref[:, r:r + 1] = jnp.sum(jnp.where(hits[r], before, 0.0), axis=1, keepdims=True).astype(jnp.int32)
    carry_ref[...] = carry_ref[...] + jnp.sum(mask, axis=0, keepdims=True)
    cnt_ref[...] = carry_ref[...]


def _router(h2, w_router, b_router, tm=1024):
    t, d = h2.shape
    n_exp = w_router.shape[1]
    w_r = jnp.zeros((d, LANES), BF16).at[:, :n_exp].set(w_router.astype(BF16))
    b_r = jnp.zeros((1, LANES), F32).at[0, :n_exp].set(b_router.astype(F32))
    col = lambda dt: jax.ShapeDtypeStruct((t, MOE_TOP_K), dt)
    cspec = pl.BlockSpec((tm, MOE_TOP_K), lambda i: (i, 0))
    return pl.pallas_call(
        functools.partial(_router_body, n_experts=n_exp, top_k=MOE_TOP_K, tm=tm), grid=(t // tm,),
        in_specs=[pl.BlockSpec((tm, d), lambda i: (i, 0)), pl.BlockSpec((d, LANES), lambda i: (0, 0)),
                  pl.BlockSpec((1, LANES), lambda i: (0, 0))],
        out_specs=[cspec, cspec, cspec, pl.BlockSpec((SUBLANES, LANES), lambda i: (0, 0))],
        out_shape=[col(F32), col(jnp.int32), col(jnp.int32), jax.ShapeDtypeStruct((SUBLANES, LANES), F32)],
        scratch_shapes=[pltpu.VMEM((SUBLANES, LANES), F32)],
        compiler_params=_params(1), name='router')(h2, w_r, b_r)


def _pack_bf16_pairs(lo, hi):
    lo_bits = pltpu.bitcast(lo.astype(BF16).astype(F32), jnp.uint32)
    hi_bits = pltpu.bitcast(hi.astype(BF16).astype(F32), jnp.uint32)
    return (hi_bits & jnp.uint32(0xFFFF0000)) | lax.shift_right_logical(lo_bits, jnp.uint32(16))


def _unpack_lo(u):
    return pltpu.bitcast(lax.shift_left(u, jnp.uint32(16)), F32)


def _unpack_hi(u):
    return pltpu.bitcast(u & jnp.uint32(0xFFFF0000), F32)


def _norm_mod_pack_body(x_ref, g_ref, sc_ref, sh_ref, o_ref, p_ref):
    x = x_ref[...]
    y = x * lax.rsqrt(jnp.mean(x * x, axis=-1, keepdims=True) + RMS_EPS) * g_ref[...]
    y = y * (1.0 + sc_ref[...]) + sh_ref[...]
    half = y.shape[1] // 2
    o_ref[...] = y.astype(o_ref.dtype)
    p_ref[...] = _pack_bf16_pairs(y[:, :half], y[:, half:])


def _norm_mod_pack(x2, g, sc, sh, seq, tm=256):
    t, d = x2.shape
    bsz = t // seq
    row = pl.BlockSpec((tm, d), lambda i: (i, 0))
    vec = pl.BlockSpec((1, d), lambda i: (0, 0))
    bvec = pl.BlockSpec((None, 1, d), lambda i: (i // (seq // tm), 0, 0))
    return pl.pallas_call(
        _norm_mod_pack_body, grid=(t // tm,), in_specs=[row, vec, bvec, bvec],
        out_specs=[row, pl.BlockSpec((tm, d // 2), lambda i: (i, 0))],
        out_shape=[jax.ShapeDtypeStruct((t, d), BF16), jax.ShapeDtypeStruct((t, d // 2), jnp.uint32)],
        compiler_params=_params(1), name='norm_mod_pack',
    )(x2, g.reshape(1, d), sc.reshape(bsz, 1, d), sh.reshape(bsz, 1, d))


def _moe_group_body(te_ref, nu_ref, idx_hbm, h_hbm, w1_ref, w3_ref, w2_ref, y_hbm,
                    idx_smem, xbuf, ybuf, isem, gsem, ssem, *, tmr, dump_row, n_tiles):
    i = pl.program_id(0)
    n = nu_ref[0]
    dh = xbuf.shape[-1]
    n_groups = tmr // SUBLANES
    n_up = n_groups // 2
    n_down = n_up // 2
    kc = 2 * dh // n_up

    def idx_copy(tile, rec):
        tile = jnp.minimum(tile, n_tiles - 1)
        return pltpu.make_async_copy(idx_hbm.at[pl.ds(pl.multiple_of(tile * IDX_REC, IDX_REC), IDX_REC)],
                                     idx_smem.at[pl.ds(pl.multiple_of(rec * IDX_REC, IDX_REC), IDX_REC)],
                                     isem.at[rec])

    def split(row):
        return lax.shift_right_logical(row, 3), row & (SUBLANES - 1)

    def row_in(tok, g, u, s):
        hi, lo = split(tok)
        return pltpu.make_async_copy(h_hbm.at[hi, pl.ds(lo, 1)], xbuf.at[s, g, pl.ds(u, 1)], gsem.at[s])

    def row_out(dst, g, u, s):
        hi, lo = split(dst)
        return pltpu.make_async_copy(ybuf.at[s, g, pl.ds(u, 1)], y_hbm.at[hi, pl.ds(lo, 1)], ssem.at[s])

    def gather_group(g, rec, s):
        for u in range(SUBLANES):
            row_in(idx_smem[rec * IDX_REC + g * SUBLANES + u], g, u, s).start(priority=u % 2)

    def scatter_group(g, rec, s):
        for u in range(SUBLANES):
            row_out(idx_smem[rec * IDX_REC + tmr + g * SUBLANES + u], g, u, s).start(priority=u % 2)

    def per_group(fn):
        def body(g, c):
            fn(g)
            return c
        lax.fori_loop(0, n_groups, body, 0)

    def gather_wait(s):
        per_group(lambda g: [row_in(jnp.int32(0), g, u, s).wait() for u in range(SUBLANES)])

    def scatter_wait(s):
        per_group(lambda g: [row_out(jnp.int32(dump_row), g, u, s).wait() for u in range(SUBLANES)])

    @pl.when(i == 0)
    def _prologue():
        ybuf[2] = jnp.zeros(ybuf.shape[1:], ybuf.dtype)
        for region in range(2):
            init = pltpu.make_async_copy(
                ybuf.at[2], y_hbm.at[pl.ds((dump_row + region * tmr) // SUBLANES, n_groups)], ssem.at[region])
            init.start()
            init.wait()

        def fill(r, c):
            idx_smem[4 * IDX_REC + tmr + r] = dump_row + 2 * tmr + r
            return c
        lax.fori_loop(0, tmr, fill, 0)
        for k in range(2):
            idx_copy(k, k).start()
            idx_copy(k, k).wait()
        idx_copy(2, 2).start()
        per_group(lambda g: gather_group(g, 0, 0))
        per_group(lambda g: gather_group(g, 1, 1))

    @pl.when((i >= 2) & (i < n))
    def _():
        scatter_wait(i % 3)

    @pl.when(i < n)
    def _tile():
        s, s_in, s_out = i % 3, (i + 2) % 3, (i + 2) % 3
        r_in, r_out = (i + 2) % 5, (i + 4) % 5
        idx_copy(i + 2, r_in).wait()
        gather_wait(s)
        h1 = jnp.zeros((tmr, w1_ref.shape[-1]), F32)
        h3 = jnp.zeros((tmr, w1_ref.shape[-1]), F32)
        for c in range(n_up):
            gather_group(2 * c, r_in, s_in)
            gather_group(2 * c + 1, r_in, s_in)
            pc = c % (n_up // 2)
            u = xbuf[s, :, :, pc * kc:(pc + 1) * kc].reshape(tmr, kc)
            xc = (_unpack_lo(u) if c < n_up // 2 else _unpack_hi(u)).astype(BF16)
            h1 = h1 + _dot(xc, w1_ref[c * kc:(c + 1) * kc, :].astype(BF16))
            h3 = h3 + _dot(xc, w3_ref[c * kc:(c + 1) * kc, :].astype(BF16))
        hid = ((h1 * jax.nn.sigmoid(h1)) * h3).astype(BF16)
        for c in range(n_down):
            for j in range(n_groups // n_down):
                scatter_group(c * (n_groups // n_down) + j, r_out, s_out)
            y_lo = _dot(hid, w2_ref[:, c * kc:(c + 1) * kc].astype(BF16))
            y_hi = _dot(hid, w2_ref[:, dh + c * kc:dh + (c + 1) * kc].astype(BF16))
            ybuf[s, :, :, c * kc:(c + 1) * kc] = _pack_bf16_pairs(y_lo, y_hi).reshape(n_groups, SUBLANES, kc)
        idx_copy(i + 3, (i + 3) % 5).start()

    @pl.when(i == n)
    def _drain():
        per_group(lambda g: scatter_group(g, (i + 4) % 5, (i + 2) % 3))
        scatter_wait(i % 3)
        scatter_wait((i + 1) % 3)
        scatter_wait((i + 2) % 3)
        gather_wait(i % 3)
        gather_wait((i + 1) % 3)
        idx_copy(i + 2, (i + 2) % 5).wait()


def _moe_grouped(h2p, idx, tile_expert, n_used, w1, w3, w2, layer, n_slots):
    t, dh = h2p.shape
    d = 2 * dh
    n_tiles = idx.shape[0] // IDX_REC
    tmr = MOE_ROWS
    f = w1.shape[3]
    dump_row = n_slots * t
    n_rows = n_slots * t + 3 * tmr
    assert n_tiles >= 4
    wmap = lambda i, te, nu: (layer, te[i], 0, 0)
    rows = lambda n_buf: pltpu.VMEM((n_buf, tmr // SUBLANES, SUBLANES, dh), jnp.uint32)
    grid_spec = pltpu.PrefetchScalarGridSpec(
        num_scalar_prefetch=2, grid=(n_tiles + 1,),
        in_specs=[
            pl.BlockSpec(memory_space=pl.ANY), pl.BlockSpec(memory_space=pl.ANY),
            pl.BlockSpec((None, None, d, f), wmap), pl.BlockSpec((None, None, d, f), wmap),
            pl.BlockSpec((None, None, f, d), wmap),
        ],
        out_specs=pl.BlockSpec(memory_space=pl.ANY),
        scratch_shapes=[
            pltpu.SMEM((5 * IDX_REC,), jnp.int32), rows(3), rows(3),
            pltpu.SemaphoreType.DMA((5,)), pltpu.SemaphoreType.DMA((3,)), pltpu.SemaphoreType.DMA((3,)),
        ])
    y = pl.pallas_call(
        functools.partial(_moe_group_body, tmr=tmr, dump_row=dump_row, n_tiles=n_tiles), grid_spec=grid_spec,
        out_shape=jax.ShapeDtypeStruct((n_rows // SUBLANES, SUBLANES, dh), jnp.uint32),
        compiler_params=_params(1), name='moe_grouped',
    )(tile_expert, n_used, idx, h2p.reshape(t // SUBLANES, SUBLANES, dh), w1, w3, w2)
    return y.reshape(n_rows, dh)


def _moe_plan(eidx, rank, counts, n_experts):
    t, k = eidx.shape
    tmr = MOE_ROWS
    n_tiles = (t * k) // tmr + n_experts
    padded = ((counts + tmr - 1) // tmr) * tmr
    ends = jnp.cumsum(padded)
    offsets = ends - padded
    n_used = (ends[-1] // tmr).astype(jnp.int32)
    pos = (offsets[eidx] + rank).reshape(-1)
    dst = (jnp.arange(k, dtype=jnp.int32)[None, :] * t + jnp.arange(t, dtype=jnp.int32)[:, None]).reshape(-1)
    p = jnp.arange(n_tiles * tmr, dtype=jnp.int32)
    pad_dst = k * t + ((p // tmr) % 3) * tmr + p % tmr
    dst_sorted = pad_dst.at[pos].set(dst).reshape(n_tiles, tmr)
    tok_sorted = jnp.where(dst_sorted < k * t, dst_sorted % t, dst_sorted % tmr)
    idx = jnp.concatenate([tok_sorted, dst_sorted, jnp.zeros((n_tiles, IDX_REC - 2 * tmr), jnp.int32)],
                          axis=1).reshape(-1)
    tile_start = jnp.minimum(jnp.arange(n_tiles + 1, dtype=jnp.int32), n_used - 1) * tmr
    tile_expert = jnp.sum((ends[None, :] <= tile_start[:, None]).astype(jnp.int32), axis=1)
    return idx, jnp.minimum(tile_expert, n_experts - 1), n_used.reshape(1)


def _moe_combine_body(x_ref, g_ref, gsel_ref, h_ref, w1_ref, w3_ref, w2_ref, *refs):
    y_refs, o_ref = refs[:-1], refs[-1]
    half = x_ref.shape[1] // 2
    a = h_ref[...]
    h1 = _dot(a, w1_ref[...])
    hid = ((h1 * jax.nn.sigmoid(h1)) * _dot(a, w3_ref[...])).astype(BF16)
    acc_lo = _dot(hid, w2_ref[:, :half])
    acc_hi = _dot(hid, w2_ref[:, half:])
    gs = gsel_ref[...]
    for r, y_ref in enumerate(y_refs):
        u = y_ref[...]
        acc_lo = acc_lo + gs[:, r:r + 1] * _unpack_lo(u)
        acc_hi = acc_hi + gs[:, r:r + 1] * _unpack_hi(u)
    o_ref[:, :half] = x_ref[:, :half] + g_ref[:, :half] * acc_lo
    o_ref[:, half:] = x_ref[:, half:] + g_ref[:, half:] * acc_hi


def _moe_combine(x2, g2, gsel, h2, ws1, ws3, ws2, y, seq, tm=128):
    t, d = x2.shape
    bsz = t // seq
    k = gsel.shape[1]
    f = ws1.shape[1]
    row = pl.BlockSpec((tm, d), lambda i: (i, 0))
    full = lambda shape: pl.BlockSpec(shape, lambda i: (0, 0))
    slot_specs = [pl.BlockSpec((tm, d // 2), lambda i, r=r: (r * (t // tm) + i, 0)) for r in range(k)]
    return pl.pallas_call(
        _moe_combine_body, grid=(t // tm,),
        in_specs=[row, pl.BlockSpec((None, 1, d), lambda i: (i // (seq // tm), 0, 0)),
                  pl.BlockSpec((tm, k), lambda i: (i, 0)), row, full((d, f)), full((d, f)), full((f, d))] + slot_specs,
        out_specs=row, out_shape=jax.ShapeDtypeStruct((t, d), F32),
        compiler_params=_params(1), name='moe_combine',
    )(x2, g2.reshape(bsz, 1, d), gsel, h2, ws1.astype(BF16), ws3.astype(BF16), ws2.astype(BF16), *([y] * k))


def kernel(x, c, w_ada, b_ada, ada_layer, norm_mix, norm_ffn, w_in, q_norm_a, k_norm_a, q_norm_b,
           k_norm_b, conv_w, conv_b, w_rgate, b_rgate, w_igate, b_igate, lru_lambda, w_branch_a,
           w_branch_b, w_branch_c, w_out, w_router, b_router, w1_exp, w3_exp, w2_exp, w1_shared,
           w3_shared, w2_shared):
    bsz, seq, d = x.shape
    t = bsz * seq
    depth = w_in.shape[0]
    mixw = d // 4
    a_cols = 3 * mixw + IDX_HEADS * IDX_DIM + IDX_DIM + IDX_HEADS
    off_b = a_cols
    off_c = off_b + N_DIL * 3 * mixw
    off_g = off_c + 2 * mixw
    top_k = min(DSA_TOPK_MAX, seq // 4)
    tm = 1024 if t % 1024 == 0 else 256
    tpb = seq // tm

    mod_shared = _ada_mod(c, w_ada, b_ada).reshape(bsz, N_MOD, d)
    x2 = x.reshape(t, d)
    for l in range(depth):
        mod = mod_shared + ada_layer[l]
        sh1, sc1, g1, sh2, sc2, g2 = [mod[:, j] for j in range(N_MOD)]
        h = _norm_mod(x2, norm_mix[l], sc1, sh1, seq)
        wl = w_in[l]

        n_main = 3 * mixw + IDX_HEADS * IDX_DIM
        pa = _matmul(h, wl[:, :n_main].astype(BF16), BF16, tm=tm, tn=512, name='proj_a')
        w_small = jnp.zeros((d, LANES), BF16).at[:, :IDX_DIM + IDX_HEADS].set(
            wl[:, n_main:a_cols].astype(BF16))
        small = _matmul(h, w_small, F32, tm=tm, tn=LANES, name='proj_a_idx')
        ik = small[:, :IDX_DIM].astype(BF16)
        zeros = jnp.zeros_like(ik)
        ik2 = jnp.concatenate([ik, zeros, zeros, ik], axis=1)
        qn, kn = _qk_norm(pa, q_norm_a[l], k_norm_a[l], mixw)
        iwt = small[:, IDX_DIM:IDX_DIM + IDX_HEADS].reshape(bsz, seq, IDX_HEADS).transpose(0, 2, 1)
        vt = pa[:, 2 * mixw:3 * mixw].reshape(bsz, seq, mixw // HEAD_DIM, HEAD_DIM).transpose(0, 2, 3, 1)
        o_a = _dsa_attention(qn, kn, pa, ik2, iwt, vt, seq, top_k)

        outs, lses = [], []
        for g, (window, dil) in enumerate(DIL_PATTERNS):
            w_g = wl[:, off_b + g * 3 * mixw:off_b + (g + 1) * 3 * mixw].astype(BF16)
            qkv = _to_classes(_matmul(h, w_g, BF16, tm=tm, tn=512, name='proj_b'), bsz, seq, dil)
            og, lg = _dilated_group(qkv, q_norm_b[l, g], k_norm_b[l, g], bsz * dil, seq // dil,
                                    band=window // dil)
            outs.append(_from_classes(og, bsz, seq, dil))
            lses.append(_from_classes(lg, bsz, seq, dil))
        o_b = _dil_merge(outs, lses)

        pc = _matmul(h, wl[:, off_c:off_g].astype(BF16), F32, tm=tm, tn=512, name='proj_c')
        o_c = _rg_lru(pc, conv_w[l], conv_b[l], w_rgate[l], b_rgate[l], w_igate[l], b_igate[l],
                      lru_lambda[l], seq)

        w_gates = wl[:, off_g:].astype(BF16)
        tn_merge = 256
        nd = d // tn_merge
        mixed = _tiled_call(
            _branch_merge_body, t, d, tm, tn_merge,
            [(h, 'row'), (o_a, 'row'), (o_b, 'row'), (o_c, 'row'),
             (w_gates, ('coloff', 0)), (w_gates, ('coloff', nd)), (w_gates, ('coloff', 2 * nd)),
             (w_branch_a[l].astype(BF16), 'col'), (w_branch_b[l].astype(BF16), 'col'),
             (w_branch_c[l].astype(BF16), 'col')],
            [(d, BF16)], name='branch_merge')
        x2 = _tiled_call(
            _mm_residual_body, t, d, tm, 512,
            [(mixed, 'row'), (w_out[l].astype(BF16), 'col'), (x2, 'tile'), (g1.reshape(bsz, 1, d), 'bvec')],
            [(d, F32)], tiles_per_batch=tpb, name='out_proj')

        h2, h2p = _norm_mod_pack(x2, norm_ffn[l], sc2, sh2, seq)
        gsel, eidx, rank, cnt = _router(h2, w_router[l], b_router[l])
        idx, tile_expert, n_used = _moe_plan(eidx, rank, cnt[0, :N_EXPERTS].astype(jnp.int32), N_EXPERTS)
        y = _moe_grouped(h2p, idx, tile_expert, n_used, w1_exp, w3_exp, w2_exp, l, MOE_TOP_K)
        x2 = _moe_combine(x2, g2, gsel, h2, w1_shared[l], w3_shared[l], w2_shared[l], y, seq)
    return x2.reshape(bsz, seq, d)
```

```python
import functools

import jax
import jax.numpy as jnp
from jax import lax
from jax.experimental import pallas as pl
from jax.experimental.pallas import tpu as pltpu

HEAD_DIM = 128
A_HEADS = 8
IDX_HEADS = 16
IDX_DIM = 64
DSA_TOPK_MAX = 256
DIL_PATTERNS = ((128, 1), (512, 4), (2048, 16))
N_DIL = 3
B_HEADS = 8
LRU_BLOCKS = 16
CONV_WIDTH = 4
LRU_C = 8.0
N_EXPERTS = 64
MOE_TOP_K = 8
ROUTED_SCALE = 2.5
N_MOD = 6
RMS_EPS = 1e-6

LANES = 128
SUBLANES = 8
VMEM_LIMIT_BYTES = 56 * 1024 * 1024

NEG_BIG = -1e30
LOG2_E = 1.4426950408889634
INT_MIN = -(2 ** 31)

BF16 = jnp.bfloat16
F32 = jnp.float32


def _params(n_axes):
    return pltpu.CompilerParams(dimension_semantics=("arbitrary",) * n_axes,
                                vmem_limit_bytes=VMEM_LIMIT_BYTES)


def _dot(a, b):
    return jnp.dot(a, b, preferred_element_type=F32)


def _dot_nt(a, b):
    return lax.dot_general(a, b, (((1,), (1,)), ((), ())), preferred_element_type=F32)


def _tiled_call(body, m, n, tm, tn, ins, outs, *, tiles_per_batch=None, name=None):
    assert m % tm == 0 and n % tn == 0, (m, n, tm, tn)
    grid = (m // tm, n // tn)
    in_specs, arrays = [], []
    for arr, kind in ins:
        arrays.append(arr)
        if kind == 'row':
            in_specs.append(pl.BlockSpec((tm, arr.shape[1]), lambda i, j: (i, 0)))
        elif isinstance(kind, tuple) and kind[0] == 'rowoff':
            _, off, width = kind
            in_specs.append(pl.BlockSpec((tm, width), lambda i, j, off=off: (i, off)))
        elif kind == 'col':
            in_specs.append(pl.BlockSpec((arr.shape[0], tn), lambda i, j: (0, j)))
        elif isinstance(kind, tuple) and kind[0] == 'coloff':
            in_specs.append(pl.BlockSpec((arr.shape[0], tn), lambda i, j, off=kind[1]: (0, j + off)))
        elif kind == 'col3':
            in_specs.append(pl.BlockSpec((None, arr.shape[1], arr.shape[2]), lambda i, j: (j, 0, 0)))
        elif kind == 'tile':
            in_specs.append(pl.BlockSpec((tm, tn), lambda i, j: (i, j)))
        elif isinstance(kind, tuple) and kind[0] == 'tileoff':
            in_specs.append(pl.BlockSpec((tm, tn), lambda i, j, off=kind[1]: (i, j + off)))
        elif kind == 'vec':
            in_specs.append(pl.BlockSpec((1, tn), lambda i, j: (0, j)))
        elif kind == 'bvec':
            tpb = tiles_per_batch
            in_specs.append(pl.BlockSpec((None, 1, tn), lambda i, j, tpb=tpb: (i // tpb, 0, j)))
        elif kind == 'full':
            nd = arr.ndim
            in_specs.append(pl.BlockSpec(arr.shape, lambda i, j, nd=nd: (0,) * nd))
        else:
            raise ValueError(kind)
    out_shape, out_specs = [], []
    for n_cols, dtype in outs:
        assert (n_cols * tn) % n == 0
        w = n_cols * tn // n
        out_shape.append(jax.ShapeDtypeStruct((m, n_cols), dtype))
        out_specs.append(pl.BlockSpec((tm, w), lambda i, j: (i, j)))
    single = len(outs) == 1
    res = pl.pallas_call(
        body, grid=grid, in_specs=in_specs,
        out_specs=out_specs[0] if single else out_specs,
        out_shape=out_shape[0] if single else out_shape,
        compiler_params=_params(2), name=name)(*arrays)
    return res


def _mm_body(a_ref, w_ref, o_ref):
    o_ref[...] = _dot(a_ref[...], w_ref[...]).astype(o_ref.dtype)


def _mm_residual_body(a_ref, w_ref, x_ref, g_ref, o_ref):
    o_ref[...] = x_ref[...] + g_ref[...] * _dot(a_ref[...], w_ref[...])


def _matmul(a, w, out_dtype, *, tm, tn, body=_mm_body, name=None):
    return _tiled_call(body, a.shape[0], w.shape[1], tm, tn, [(a, 'row'), (w, 'col')],
                       [(w.shape[1], out_dtype)], name=name)


def _ada_body(c_ref, w_ref, b_ref, o_ref):
    c = c_ref[...]
    a = (c * jax.nn.sigmoid(c)).astype(BF16)
    o_ref[...] = _dot(a, w_ref[...].astype(BF16)) + b_ref[...]


def _ada_mod(c, w_ada, b_ada):
    bsz, d = c.shape
    n = w_ada.shape[1]
    c_pad = jnp.zeros((SUBLANES, d), F32).at[:bsz].set(c)
    out = _tiled_call(_ada_body, SUBLANES, n, SUBLANES, 1024,
                      [(c_pad, 'row'), (w_ada, 'col'), (b_ada.reshape(1, n), 'vec')],
                      [(n, F32)], name='ada_mod')
    return out[:bsz]


def _norm_mod_body(x_ref, g_ref, sc_ref, sh_ref, o_ref):
    x = x_ref[...]
    y = x * lax.rsqrt(jnp.mean(x * x, axis=-1, keepdims=True) + RMS_EPS) * g_ref[...]
    o_ref[...] = (y * (1.0 + sc_ref[...]) + sh_ref[...]).astype(o_ref.dtype)


def _norm_mod(x2, g, sc, sh, seq, tm=256):
    t, d = x2.shape
    bsz = t // seq
    return _tiled_call(_norm_mod_body, t, d, tm, d,
                       [(x2, 'tile'), (g.reshape(1, d), 'vec'), (sc.reshape(bsz, 1, d), 'bvec'),
                        (sh.reshape(bsz, 1, d), 'bvec')],
                       [(d, BF16)], tiles_per_batch=seq // tm, name='norm_mod')


def _head_norm(x, g):
    outs = []
    for h in range(x.shape[1] // HEAD_DIM):
        xh = x[:, h * HEAD_DIM:(h + 1) * HEAD_DIM].astype(F32)
        outs.append(xh * lax.rsqrt(jnp.mean(xh * xh, axis=-1, keepdims=True) + RMS_EPS) * g)
    return jnp.concatenate(outs, axis=1)


def _qk_norm_body(q_ref, k_ref, gq_ref, gk_ref, qo_ref, ko_ref):
    qo_ref[...] = _head_norm(q_ref[...], gq_ref[...]).astype(qo_ref.dtype)
    ko_ref[...] = _head_norm(k_ref[...], gk_ref[...]).astype(ko_ref.dtype)


def _qk_norm(pa, gq, gk, width, tm=512):
    t = pa.shape[0]
    return _tiled_call(_qk_norm_body, t, width, tm, width,
                       [(pa, ('tileoff', 0)), (pa, ('tileoff', 1)),
                        (gq.reshape(1, HEAD_DIM), 'full'), (gk.reshape(1, HEAD_DIM), 'full')],
                       [(width, BF16), (width, BF16)], name='qk_norm_a')


def _dsa_body(qb_ref, kb_ref, q_ref, iq_ref, iwt_ref, ik_ref, k_ref, vt_ref, o_ref,
              keys_ref, thr_ref, bias_ref, s_ref, p_ref, m_ref, l_ref, acc_ref, *, tq, tk, top_k, n_heads):
    qb = qb_ref[pl.program_id(1)]
    kb = kb_ref[pl.program_id(1)]
    kb_last = ((qb + 1) * tq - 1) // tk
    n_chunks = kb_last + 1
    kpos = lax.broadcasted_iota(jnp.int32, (tk, tq), 0)
    qpos = lax.broadcasted_iota(jnp.int32, (tk, tq), 1) + qb * tq
    n_acc = 4 * SUBLANES

    @pl.when(kb == 0)
    def _scores_and_threshold():
        w = iwt_ref[...] * (IDX_HEADS ** -0.5 * IDX_DIM ** -0.5)

        def score_chunk(c, carry):
            ik2 = ik_ref[pl.ds(pl.multiple_of(c * tk, tk), tk), :]
            acc = jnp.zeros((tk, tq), F32)
            for p in range(IDX_HEADS // 2):
                iq_pair = iq_ref[:, p * LANES:(p + 1) * LANES]
                for half in range(2):
                    rel = _dot_nt(ik2[:, half * LANES:(half + 1) * LANES], iq_pair)
                    h = 2 * p + half
                    acc = acc + w[h:h + 1, :] * jnp.maximum(rel, 0.0)
            bits = pltpu.bitcast(acc, jnp.int32)
            key = jnp.where(bits < 0, bits ^ 0x7FFFFFFF, bits)
            key = jnp.where(kpos + c * tk <= qpos, key, INT_MIN)
            keys_ref[c] = key
            return carry

        lax.fori_loop(0, n_chunks, score_chunk, 0)

        def count_ge(cand):
            def body(c, cnt):
                for j in range(tk // n_acc):
                    blk = keys_ref[c, j * n_acc:(j + 1) * n_acc, :]
                    cnt = cnt + jnp.where(blk >= cand, 1.0, 0.0)
                return cnt
            cnt = lax.fori_loop(0, n_chunks, body, jnp.zeros((n_acc, tq), F32))
            return jnp.sum(cnt, axis=0, keepdims=True)

        k_f = float(top_k)
        t0 = jnp.where(count_ge(jnp.zeros((1, tq), jnp.int32)) >= k_f, 0, INT_MIN).astype(jnp.int32)

        def bit_body(i, t):
            cand = t | lax.shift_left(jnp.int32(1), 30 - i)
            return jnp.where(count_ge(cand) >= k_f, cand, t)

        t = lax.fori_loop(0, 31, bit_body, t0)
        thr_ref[...] = jnp.broadcast_to(t, thr_ref.shape)
        m_ref[...] = jnp.full(m_ref.shape, NEG_BIG, F32)
        l_ref[...] = jnp.zeros(l_ref.shape, F32)
        acc_ref[...] = jnp.zeros(acc_ref.shape, F32)

    @pl.when(kb <= kb_last)
    def _attend():
        sel = (keys_ref[kb] >= thr_ref[0:1, :]) & (kpos + kb * tk <= qpos)
        bias_ref[...] = jnp.where(sel, 0.0, NEG_BIG)
        c = HEAD_DIM ** -0.5 * LOG2_E

        def raw_scores(h):
            sl = slice(h * HEAD_DIM, (h + 1) * HEAD_DIM)
            s_ref[h % 4] = _dot_nt(k_ref[:, sl], q_ref[:, sl])

        def softmax(h):
            s = s_ref[h % 4] + bias_ref[...]
            m_old = m_ref[h, 0:1, :]
            m_new = jnp.maximum(m_old, jnp.max(s, axis=0, keepdims=True))
            p = jnp.exp2((s - m_new) * c)
            alpha = jnp.exp2((m_old - m_new) * c)
            l_new = alpha * l_ref[h, 0:1, :] + jnp.sum(p, axis=0, keepdims=True)
            p_ref[h % 3] = p.astype(BF16)
            m_ref[h] = jnp.broadcast_to(m_new, (SUBLANES, tq))
            l_ref[h] = jnp.broadcast_to(l_new, (SUBLANES, tq))
            return alpha

        def accumulate(h, alpha):
            acc_ref[h] = alpha * acc_ref[h] + _dot(vt_ref[h], p_ref[h % 3])

        s_ahead, p_ahead = 3, 2
        for h in range(s_ahead):
            raw_scores(h)
        alphas = {h: softmax(h) for h in range(p_ahead)}
        for h in range(n_heads):
            if h + s_ahead < n_heads:
                raw_scores(h + s_ahead)
            if h + p_ahead < n_heads:
                alphas[h + p_ahead] = softmax(h + p_ahead)
            accumulate(h, alphas.pop(h))

    @pl.when(kb == kb_last)
    def _finish():
        for h in range(n_heads):
            sl = slice(h * HEAD_DIM, (h + 1) * HEAD_DIM)
            o_ref[:, sl] = (acc_ref[h] / l_ref[h, 0:1, :]).T.astype(o_ref.dtype)


def _dsa_attention(qn, kn, pa, ik2, iwt, vt, seq, top_k, *, tq=256, tk=512):
    t, width = qn.shape
    bsz = t // seq
    nq, nk = seq // tq, seq // tk
    n_heads = width // HEAD_DIM
    assert width == IDX_HEADS * IDX_DIM and pa.shape[1] == 4 * width
    iq_off = 3

    pairs = [(qb, kb) for qb in range(nq) for kb in range(((qb + 1) * tq - 1) // tk + 1)]
    qb_tab = jnp.asarray([p[0] for p in pairs], jnp.int32)
    kb_tab = jnp.asarray([p[1] for p in pairs], jnp.int32)

    body = functools.partial(_dsa_body, tq=tq, tk=tk, top_k=top_k, n_heads=n_heads)
    grid_spec = pltpu.PrefetchScalarGridSpec(
        num_scalar_prefetch=2, grid=(bsz, len(pairs)),
        in_specs=[
            pl.BlockSpec((tq, width), lambda b, s, qt, kt: (b * nq + qt[s], 0)),
            pl.BlockSpec((tq, width), lambda b, s, qt, kt: (b * nq + qt[s], iq_off)),
            pl.BlockSpec((None, IDX_HEADS, tq), lambda b, s, qt, kt: (b, 0, qt[s])),
            pl.BlockSpec((seq, 2 * LANES), lambda b, s, qt, kt: (b, 0)),
            pl.BlockSpec((tk, width), lambda b, s, qt, kt: (b * nk + kt[s], 0)),
            pl.BlockSpec((None, n_heads, HEAD_DIM, tk), lambda b, s, qt, kt: (b, 0, 0, kt[s])),
        ],
        out_specs=pl.BlockSpec((tq, width), lambda b, s, qt, kt: (b * nq + qt[s], 0)),
        scratch_shapes=[
            pltpu.VMEM((nk, tk, tq), jnp.int32),
            pltpu.VMEM((SUBLANES, tq), jnp.int32),
            pltpu.VMEM((tk, tq), F32),
            pltpu.VMEM((4, tk, tq), F32),
            pltpu.VMEM((3, tk, tq), BF16),
            pltpu.VMEM((n_heads, SUBLANES, tq), F32),
            pltpu.VMEM((n_heads, SUBLANES, tq), F32),
            pltpu.VMEM((n_heads, HEAD_DIM, tq), F32),
        ])
    return pl.pallas_call(
        body, grid_spec=grid_spec, out_shape=jax.ShapeDtypeStruct((t, width), BF16),
        compiler_params=_params(2), name='dsa_attention',
    )(qb_tab, kb_tab, qn, pa, iwt, ik2, kn, vt)


def _dil_body(q_ref, kc_ref, vc_ref, kp_ref, vp_ref, gq_ref, gk_ref, o_ref, lse_ref,
              kcat_ref, vcat_ref, *, band, rows, n_heads):
    n = pl.program_id(1)
    kcat_ref[0:band, :] = _head_norm(kp_ref[...], gk_ref[...]).astype(BF16)
    kcat_ref[band:band + rows, :] = _head_norm(kc_ref[...], gk_ref[...]).astype(BF16)
    vcat_ref[0:band, :] = vp_ref[...]
    vcat_ref[band:band + rows, :] = vc_ref[...]
    qi = lax.broadcasted_iota(jnp.int32, (band, 2 * band), 0)
    ki = lax.broadcasted_iota(jnp.int32, (band, 2 * band), 1)
    back = qi + band - ki
    in_band = (back >= 0) & (back <= band)
    scale = HEAD_DIM ** -0.5
    gq = gq_ref[...]

    def sub_block(j, carry):
        r0 = pl.multiple_of(j * band, band)
        bias = jnp.where(in_band & ((ki >= band) | (n * (rows // band) + j > 0)), 0.0, NEG_BIG)
        heads = [slice(h * HEAD_DIM, (h + 1) * HEAD_DIM) for h in range(n_heads)]

        def scores(sl):
            qh = q_ref[pl.ds(r0, band), sl].astype(F32)
            qh = (qh * lax.rsqrt(jnp.mean(qh * qh, axis=-1, keepdims=True) + RMS_EPS) * gq).astype(BF16)
            return _dot_nt(qh, kcat_ref[pl.ds(r0, 2 * band), sl]) * scale + bias

        s_all = [scores(sl) for sl in heads]
        m_all = [jnp.max(s, axis=1, keepdims=True) for s in s_all]
        p_all = [jnp.exp(s - m) for s, m in zip(s_all, m_all)]
        l_all = [jnp.sum(p, axis=1, keepdims=True) for p in p_all]
        for h, sl in enumerate(heads):
            o = _dot(p_all[h].astype(BF16), vcat_ref[pl.ds(r0, 2 * band), sl]) / l_all[h]
            o_ref[pl.ds(r0, band), sl] = o.astype(o_ref.dtype)
            lse_ref[pl.ds(r0, band), h:h + 1] = m_all[h] + jnp.log(l_all[h])
        return carry

    lax.fori_loop(0, rows // band, sub_block, 0)


def _dilated_group(qkv, gq, gk, n_seq, len_seq, *, band=128, rows=512):
    t, w3 = qkv.shape
    width = w3 // 3
    n_heads = width // HEAD_DIM
    rows = min(rows, len_seq)
    nblk = len_seq // rows
    sub = rows // band

    def cur(c):
        return lambda s, n: (s * nblk + n, c)

    def prev(c):
        return lambda s, n: (s * nblk * sub + jnp.maximum(n * sub - 1, 0), c)

    body = functools.partial(_dil_body, band=band, rows=rows, n_heads=n_heads)
    return pl.pallas_call(
        body, grid=(n_seq, nblk),
        in_specs=[
            pl.BlockSpec((rows, width), cur(0)), pl.BlockSpec((rows, width), cur(1)),
            pl.BlockSpec((rows, width), cur(2)),
            pl.BlockSpec((band, width), prev(1)), pl.BlockSpec((band, width), prev(2)),
            pl.BlockSpec((1, HEAD_DIM), lambda s, n: (0, 0)), pl.BlockSpec((1, HEAD_DIM), lambda s, n: (0, 0)),
        ],
        out_specs=[pl.BlockSpec((rows, width), lambda s, n: (s * nblk + n, 0)),
                   pl.BlockSpec((rows, n_heads), lambda s, n: (s * nblk + n, 0))],
        out_shape=[jax.ShapeDtypeStruct((t, width), BF16), jax.ShapeDtypeStruct((t, n_heads), F32)],
        scratch_shapes=[pltpu.VMEM((band + rows, width), BF16), pltpu.VMEM((band + rows, width), BF16)],
        compiler_params=_params(2), name='dilated_attention',
    )(qkv, qkv, qkv, qkv, qkv, gq.reshape(1, HEAD_DIM), gk.reshape(1, HEAD_DIM))


def _dil_merge_body(o0_ref, o1_ref, o2_ref, l0_ref, l1_ref, l2_ref, out_ref):
    l0, l1, l2 = l0_ref[...], l1_ref[...], l2_ref[...]
    m = jnp.maximum(jnp.maximum(l0, l1), l2)
    e0, e1, e2 = jnp.exp(l0 - m), jnp.exp(l1 - m), jnp.exp(l2 - m)
    den = e0 + e1 + e2
    a0, a1, a2 = e0 / den, e1 / den, e2 / den
    for h in range(l0.shape[1]):
        sl = slice(h * HEAD_DIM, (h + 1) * HEAD_DIM)
        out_ref[:, sl] = (a0[:, h:h + 1] * o0_ref[:, sl].astype(F32)
                          + a1[:, h:h + 1] * o1_ref[:, sl].astype(F32)
                          + a2[:, h:h + 1] * o2_ref[:, sl].astype(F32)).astype(out_ref.dtype)


def _dil_merge(outs, lses, tm=512):
    t, width = outs[0].shape
    nh = lses[0].shape[1]
    ospec = pl.BlockSpec((tm, width), lambda i: (i, 0))
    lspec = pl.BlockSpec((tm, nh), lambda i: (i, 0))
    return pl.pallas_call(
        _dil_merge_body, grid=(t // tm,), in_specs=[ospec] * 3 + [lspec] * 3, out_specs=ospec,
        out_shape=jax.ShapeDtypeStruct((t, width), BF16), compiler_params=_params(1), name='dilated_merge',
    )(*outs, *lses)


def _to_classes(a, bsz, seq, dil):
    if dil == 1:
        return a
    c = a.shape[1]
    return a.reshape(bsz, seq // dil, dil, c).transpose(0, 2, 1, 3).reshape(bsz * seq, c)


def _from_classes(a, bsz, seq, dil):
    if dil == 1:
        return a
    c = a.shape[1]
    return a.reshape(bsz, dil, seq // dil, c).transpose(0, 2, 1, 3).reshape(bsz * seq, c)


def _lru_body(x_ref, y_ref, xp_ref, cw_ref, cb_ref, wr_ref, br_ref, wi_ref, bi_ref, lam_ref, o_ref,
              carry_ref, *, tm, tiles_per_batch):
    i = pl.program_id(0)
    first = (i % tiles_per_batch) == 0
    x = x_ref[...]
    prev = jnp.where(first, 0.0, xp_ref[...])
    xs = jnp.concatenate([prev, x], axis=0)
    xc = cb_ref[...] + cw_ref[CONV_WIDTH - 1:CONV_WIDTH, :] * x
    for j in range(CONV_WIDTH - 1):
        back = CONV_WIDTH - 1 - j
        xc = xc + cw_ref[j:j + 1, :] * xs[SUBLANES - back:SUBLANES - back + tm, :]
    xcb = xc.astype(BF16)
    r = jax.nn.sigmoid(_dot(xcb, wr_ref[...]) + br_ref[...])
    ig = jax.nn.sigmoid(_dot(xcb, wi_ref[...]) + bi_ref[...])
    lam = lam_ref[...]
    softplus_neg = jnp.maximum(-lam, 0.0) + jnp.log1p(jnp.exp(-jnp.abs(lam)))
    log_a = (-LRU_C * softplus_neg) * r
    a = jnp.exp(log_a)
    b = jnp.sqrt(1.0 - a * a) * (ig * xc)
    rows = lax.broadcasted_iota(jnp.int32, a.shape, 0)
    step = 1
    while step < tm:
        a_sh = pltpu.roll(a, step, 0)
        b_sh = pltpu.roll(b, step, 0)
        valid = rows >= step
        b = jnp.where(valid, a * b_sh + b, b)
        a = jnp.where(valid, a * a_sh, a)
        step *= 2
    h0 = jnp.where(first, 0.0, carry_ref[0:1, :])
    h = a * h0 + b
    carry_ref[...] = jnp.broadcast_to(h[tm - 1:tm, :], carry_ref.shape)
    o_ref[...] = (h * jax.nn.gelu(y_ref[...])).astype(o_ref.dtype)


def _block_diag(w):
    g, n, _ = w.shape
    eye = jnp.eye(g, dtype=w.dtype)
    return (eye[:, None, :, None] * w[:, :, None, :]).reshape(g * n, g * n)


def _rg_lru(pc, conv_w, conv_b, w_r, b_r, w_i, b_i, lam, seq, tm=256):
    t, w2 = pc.shape
    w = w2 // 2
    row = lambda v: v.reshape(1, w)
    full2 = lambda shape: pl.BlockSpec(shape, lambda i: (0, 0))
    body = functools.partial(_lru_body, tm=tm, tiles_per_batch=seq // tm)
    return pl.pallas_call(
        body, grid=(t // tm,),
        in_specs=[
            pl.BlockSpec((tm, w), lambda i: (i, 0)),
            pl.BlockSpec((tm, w), lambda i: (i, 1)),
            pl.BlockSpec((SUBLANES, w), lambda i: (jnp.maximum(i * (tm // SUBLANES) - 1, 0), 0)),
            full2((CONV_WIDTH, w)), full2((1, w)), full2((w, w)), full2((1, w)), full2((w, w)),
            full2((1, w)), full2((1, w)),
        ],
        out_specs=pl.BlockSpec((tm, w), lambda i: (i, 0)),
        out_shape=jax.ShapeDtypeStruct((t, w), BF16),
        scratch_shapes=[pltpu.VMEM((SUBLANES, w), F32)],
        compiler_params=_params(1), name='rg_lru',
    )(pc, pc, pc, conv_w, row(conv_b), _block_diag(w_r).astype(BF16), row(b_r),
      _block_diag(w_i).astype(BF16), row(b_i), row(lam))


def _branch_merge_body(h_ref, oa_ref, ob_ref, oc_ref, wga_ref, wgb_ref, wgc_ref, wa_ref, wb_ref, wc_ref, o_ref):
    h = h_ref[...]
    mixed = (jax.nn.sigmoid(_dot(h, wga_ref[...])) * _dot(oa_ref[...], wa_ref[...])
             + jax.nn.sigmoid(_dot(h, wgb_ref[...])) * _dot(ob_ref[...], wb_ref[...])
             + jax.nn.sigmoid(_dot(h, wgc_ref[...])) * _dot(oc_ref[...], wc_ref[...]))
    o_ref[...] = mixed.astype(o_ref.dtype)


MOE_ROWS = 256
IDX_REC = 1024


def _router_body(a_ref, w_ref, b_ref, gsel_ref, eidx_ref, rank_ref, cnt_ref, carry_ref, *, n_experts, top_k, tm):
    i = pl.program_id(0)

    @pl.when(i == 0)
    def _():
        carry_ref[...] = jnp.zeros(carry_ref.shape, F32)

    logits = _dot(a_ref[...], w_ref[...])
    lane = lax.broadcasted_iota(jnp.int32, logits.shape, 1).astype(F32)
    scores = jax.nn.sigmoid(logits)
    cur = jnp.where(lane < n_experts, scores + b_ref[...], -jnp.inf)
    mask = jnp.zeros(logits.shape, F32)
    hits, picked_scores = [], []
    for r in range(top_k):
        m = jnp.max(cur, axis=1, keepdims=True)
        idx = jnp.min(jnp.where(cur == m, lane, float(LANES)), axis=1, keepdims=True)
        hit = lane == idx
        hits.append(hit)
        picked_scores.append(jnp.sum(jnp.where(hit, scores, 0.0), axis=1, keepdims=True))
        eidx_ref[:, r:r + 1] = idx.astype(jnp.int32)
        cur = jnp.where(hit, -jnp.inf, cur)
        mask = mask + jnp.where(hit, 1.0, 0.0)
    total = picked_scores[0]
    for r in range(1, top_k):
        total = total + picked_scores[r]
    ri = lax.broadcasted_iota(jnp.int32, (tm, tm), 0)
    ci = lax.broadcasted_iota(jnp.int32, (tm, tm), 1)
    tri = jnp.where(ri > ci, 1.0, 0.0).astype(BF16)
    before = _dot(tri, mask.astype(BF16)) + carry_ref[0:1, :]
    for r in range(top_k):
        gsel_ref[:, r:r + 1] = picked_scores[r] / total * ROUTED_SCALE
        rank_ref[:, r:r + 1] = jnp.sum(jnp.where(hits[r], before, 0.0), axis=1, keepdims=True).astype(jnp.int32)
    carry_ref[...] = carry_ref[...] + jnp.sum(mask, axis=0, keepdims=True)
    cnt_ref[...] = carry_ref[...]


def _router(h2, w_router, b_router, tm=1024):
    t, d = h2.shape
    n_exp = w_router.shape[1]
    w_r = jnp.zeros((d, LANES), BF16).at[:, :n_exp].set(w_router.astype(BF16))
    b_r = jnp.zeros((1, LANES), F32).at[0, :n_exp].set(b_router.astype(F32))
    col = lambda dt: jax.ShapeDtypeStruct((t, MOE_TOP_K), dt)
    cspec = pl.BlockSpec((tm, MOE_TOP_K), lambda i: (i, 0))
    return pl.pallas_call(
        functools.partial(_router_body, n_experts=n_exp, top_k=MOE_TOP_K, tm=tm), grid=(t // tm,),
        in_specs=[pl.BlockSpec((tm, d), lambda i: (i, 0)), pl.BlockSpec((d, LANES), lambda i: (0, 0)),
                  pl.BlockSpec((1, LANES), lambda i: (0, 0))],
        out_specs=[cspec, cspec, cspec, pl.BlockSpec((SUBLANES, LANES), lambda i: (0, 0))],
        out_shape=[col(F32), col(jnp.int32), col(jnp.int32), jax.ShapeDtypeStruct((SUBLANES, LANES), F32)],
        scratch_shapes=[pltpu.VMEM((SUBLANES, LANES), F32)],
        compiler_params=_params(1), name='router')(h2, w_r, b_r)


def _pack_bf16_pairs(lo, hi):
    lo_bits = pltpu.bitcast(lo.astype(BF16).astype(F32), jnp.uint32)
    hi_bits = pltpu.bitcast(hi.astype(BF16).astype(F32), jnp.uint32)
    return (hi_bits & jnp.uint32(0xFFFF0000)) | lax.shift_right_logical(lo_bits, jnp.uint32(16))


def _unpack_lo(u):
    return pltpu.bitcast(lax.shift_left(u, jnp.uint32(16)), F32)


def _unpack_hi(u):
    return pltpu.bitcast(u & jnp.uint32(0xFFFF0000), F32)


def _norm_mod_pack_body(x_ref, g_ref, sc_ref, sh_ref, o_ref, p_ref):
    x = x_ref[...]
    y = x * lax.rsqrt(jnp.mean(x * x, axis=-1, keepdims=True) + RMS_EPS) * g_ref[...]
    y = y * (1.0 + sc_ref[...]) + sh_ref[...]
    half = y.shape[1] // 2
    o_ref[...] = y.astype(o_ref.dtype)
    p_ref[...] = _pack_bf16_pairs(y[:, :half], y[:, half:])


def _norm_mod_pack(x2, g, sc, sh, seq, tm=256):
    t, d = x2.shape
    bsz = t // seq
    row = pl.BlockSpec((tm, d), lambda i: (i, 0))
    vec = pl.BlockSpec((1, d), lambda i: (0, 0))
    bvec = pl.BlockSpec((None, 1, d), lambda i: (i // (seq // tm), 0, 0))
    return pl.pallas_call(
        _norm_mod_pack_body, grid=(t // tm,), in_specs=[row, vec, bvec, bvec],
        out_specs=[row, pl.BlockSpec((tm, d // 2), lambda i: (i, 0))],
        out_shape=[jax.ShapeDtypeStruct((t, d), BF16), jax.ShapeDtypeStruct((t, d // 2), jnp.uint32)],
        compiler_params=_params(1), name='norm_mod_pack',
    )(x2, g.reshape(1, d), sc.reshape(bsz, 1, d), sh.reshape(bsz, 1, d))


def _moe_group_body(te_ref, nu_ref, idx_hbm, h_hbm, w1_ref, w3_ref, w2_ref, o_ref,
                    idx_smem, xbuf, isem, gsem, *, tmr, n_tiles):
    i = pl.program_id(0)
    n = nu_ref[0]
    dh = xbuf.shape[-1]
    n_groups = tmr // SUBLANES
    n_up = n_groups // 2
    n_down = n_up // 2
    kc = 2 * dh // n_up

    def idx_copy(tile, rec):
        tile = jnp.minimum(tile, n_tiles - 1)
        return pltpu.make_async_copy(idx_hbm.at[pl.ds(pl.multiple_of(tile * IDX_REC, IDX_REC), IDX_REC)],
                                     idx_smem.at[pl.ds(pl.multiple_of(rec * IDX_REC, IDX_REC), IDX_REC)],
                                     isem.at[rec])

    def row_in(tok, g, u, s):
        return pltpu.make_async_copy(h_hbm.at[lax.shift_right_logical(tok, 3), pl.ds(tok & (SUBLANES - 1), 1)],
                                     xbuf.at[s, g, pl.ds(u, 1)], gsem.at[s])

    def gather_group(g, rec, s):
        for u in range(SUBLANES):
            row_in(idx_smem[rec * IDX_REC + g * SUBLANES + u], g, u, s).start(priority=u % 2)

    def per_group(fn):
        def body(g, c):
            fn(g)
            return c
        lax.fori_loop(0, n_groups, body, 0)

    def gather_wait(s):
        per_group(lambda g: [row_in(jnp.int32(0), g, u, s).wait() for u in range(SUBLANES)])

    @pl.when(i == 0)
    def _prologue():
        for k in range(2):
            idx_copy(k, k).start()
            idx_copy(k, k).wait()
        idx_copy(2, 2).start()
        per_group(lambda g: gather_group(g, 0, 0))
        per_group(lambda g: gather_group(g, 1, 1))

    @pl.when(i < n)
    def _tile():
        s, s_in, r_in = i % 3, (i + 2) % 3, (i + 2) % 3
        idx_copy(i + 2, r_in).wait()
        gather_wait(s)
        h1 = jnp.zeros((tmr, w1_ref.shape[-1]), F32)
        h3 = jnp.zeros((tmr, w1_ref.shape[-1]), F32)
        for c in range(n_up):
            gather_group(2 * c, r_in, s_in)
            gather_group(2 * c + 1, r_in, s_in)
            pc = c % (n_up // 2)
            u = xbuf[s, :, :, pc * kc:(pc + 1) * kc].reshape(tmr, kc)
            xc = (_unpack_lo(u) if c < n_up // 2 else _unpack_hi(u)).astype(BF16)
            h1 = h1 + _dot(xc, w1_ref[c * kc:(c + 1) * kc, :].astype(BF16))
            h3 = h3 + _dot(xc, w3_ref[c * kc:(c + 1) * kc, :].astype(BF16))
        hid = ((h1 * jax.nn.sigmoid(h1)) * h3).astype(BF16)
        for c in range(n_down):
            y_lo = _dot(hid, w2_ref[:, c * kc:(c + 1) * kc].astype(BF16))
            y_hi = _dot(hid, w2_ref[:, dh + c * kc:dh + (c + 1) * kc].astype(BF16))
            o_ref[:, c * kc:(c + 1) * kc] = _pack_bf16_pairs(y_lo, y_hi)
        idx_copy(i + 3, i % 3).start()

    @pl.when(i == n)
    def _drain():
        gather_wait(i % 3)
        gather_wait((i + 1) % 3)
        idx_copy(i + 2, (i + 2) % 3).wait()


def _moe_grouped(h2p, idx, tile_expert, n_used, w1, w3, w2, layer):
    t, dh = h2p.shape
    d = 2 * dh
    n_tiles = idx.shape[0] // IDX_REC
    tmr = MOE_ROWS
    f = w1.shape[3]
    assert n_tiles >= 4
    wmap = lambda i, te, nu: (layer, te[i], 0, 0)
    grid_spec = pltpu.PrefetchScalarGridSpec(
        num_scalar_prefetch=2, grid=(n_tiles + 1,),
        in_specs=[
            pl.BlockSpec(memory_space=pl.ANY), pl.BlockSpec(memory_space=pl.ANY),
            pl.BlockSpec((None, None, d, f), wmap), pl.BlockSpec((None, None, d, f), wmap),
            pl.BlockSpec((None, None, f, d), wmap),
        ],
        out_specs=pl.BlockSpec((tmr, dh), lambda i, te, nu: (jnp.minimum(i, nu[0] - 1), 0)),
        scratch_shapes=[
            pltpu.SMEM((3 * IDX_REC,), jnp.int32),
            pltpu.VMEM((3, tmr // SUBLANES, SUBLANES, dh), jnp.uint32),
            pltpu.SemaphoreType.DMA((3,)), pltpu.SemaphoreType.DMA((3,)),
        ])
    return pl.pallas_call(
        functools.partial(_moe_group_body, tmr=tmr, n_tiles=n_tiles), grid_spec=grid_spec,
        out_shape=jax.ShapeDtypeStruct((n_tiles * tmr, dh), jnp.uint32),
        compiler_params=_params(1), name='moe_grouped',
    )(tile_expert, n_used, idx, h2p.reshape(t // SUBLANES, SUBLANES, dh), w1, w3, w2)


def _moe_plan(eidx, rank, counts, n_experts, tm_combine):
    t, k = eidx.shape
    tmr = MOE_ROWS
    n_tiles = (t * k) // tmr + n_experts
    padded = ((counts + tmr - 1) // tmr) * tmr
    ends = jnp.cumsum(padded)
    offsets = ends - padded
    n_used = (ends[-1] // tmr).astype(jnp.int32)
    pos = offsets[eidx] + rank
    tok = jnp.broadcast_to(jnp.arange(t, dtype=jnp.int32)[:, None], (t, k))
    p = jnp.arange(n_tiles * tmr, dtype=jnp.int32)
    tok_sorted = (p % tmr).at[pos.reshape(-1)].set(tok.reshape(-1)).reshape(n_tiles, tmr)
    idx = jnp.concatenate([tok_sorted, jnp.zeros((n_tiles, IDX_REC - tmr), jnp.int32)], axis=1).reshape(-1)
    tile_start = jnp.minimum(jnp.arange(n_tiles + 1, dtype=jnp.int32), n_used - 1) * tmr
    tile_expert = jnp.sum((ends[None, :] <= tile_start[:, None]).astype(jnp.int32), axis=1)
    assert tm_combine * k == IDX_REC
    pos_blocks = pos.reshape(t // tm_combine, tm_combine, k).transpose(0, 2, 1).reshape(-1)
    return idx, jnp.minimum(tile_expert, n_experts - 1), n_used.reshape(1), pos_blocks


def _moe_combine_body(pos_hbm, y_hbm, x_ref, g_ref, gsel_ref, h_ref, w1_ref, w3_ref, w2_ref, o_ref,
                      pos_smem, ybuf, isem, gsem, *, tm, n_slots):
    i = pl.program_id(0)
    n_steps = pl.num_programs(0)
    n_groups = tm // SUBLANES
    half = x_ref.shape[1] // 2

    def pos_copy(step, rec):
        step = jnp.minimum(step, n_steps - 1)
        return pltpu.make_async_copy(pos_hbm.at[pl.ds(pl.multiple_of(step * IDX_REC, IDX_REC), IDX_REC)],
                                     pos_smem.at[pl.ds(pl.multiple_of(rec * IDX_REC, IDX_REC), IDX_REC)],
                                     isem.at[rec])

    def row_in(row, r, g, u, s):
        return pltpu.make_async_copy(y_hbm.at[lax.shift_right_logical(row, 3), pl.ds(row & (SUBLANES - 1), 1)],
                                     ybuf.at[s, r, g, pl.ds(u, 1)], gsem.at[s])

    def per_group(fn):
        def body(g, c):
            fn(g)
            return c
        lax.fori_loop(0, n_groups, body, 0)

    def gather_group(g, rec, s):
        for r in range(n_slots):
            for u in range(SUBLANES):
                row_in(pos_smem[rec * IDX_REC + r * tm + g * SUBLANES + u], r, g, u, s).start(priority=u % 2)

    def gather_wait(s):
        def group(g):
            for r in range(n_slots):
                for u in range(SUBLANES):
                    row_in(jnp.int32(0), r, g, u, s).wait()
        per_group(group)

    @pl.when(i == 0)
    def _prologue():
        pos_copy(0, 0).start()
        pos_copy(0, 0).wait()
        per_group(lambda g: gather_group(g, 0, 0))
        pos_copy(1, 1).start()

    nxt, rec_next = (i + 1) % 2, (i + 1) % 2
    pos_copy(i + 1, rec_next).wait()
    groups = iter(range(n_groups))

    def issue(count):
        for _ in range(count):
            g = next(groups, None)
            if g is not None:
                gather_group(g, rec_next, nxt)

    issue(2)
    a = h_ref[...]
    h1 = _dot(a, w1_ref[...])
    issue(2)
    hid = ((h1 * jax.nn.sigmoid(h1)) * _dot(a, w3_ref[...])).astype(BF16)
    issue(2)
    acc_lo = _dot(hid, w2_ref[:, :half])
    issue(2)
    acc_hi = _dot(hid, w2_ref[:, half:])
    gather_wait(i % 2)
    pos_copy(i + 2, i % 2).start()
    gs = gsel_ref[...]
    for r in range(n_slots):
        issue(1)
        u = ybuf[i % 2, r].reshape(tm, half)
        acc_lo = acc_lo + gs[:, r:r + 1] * _unpack_lo(u)
        acc_hi = acc_hi + gs[:, r:r + 1] * _unpack_hi(u)
    issue(n_groups)
    o_ref[:, :half] = x_ref[:, :half] + g_ref[:, :half] * acc_lo
    o_ref[:, half:] = x_ref[:, half:] + g_ref[:, half:] * acc_hi

    @pl.when(i == n_steps - 1)
    def _drain():
        gather_wait(nxt)
        pos_copy(i + 2, i % 2).wait()


def _moe_combine(x2, g2, gsel, h2, ws1, ws3, ws2, y, pos_blocks, seq, tm):
    t, d = x2.shape
    bsz = t // seq
    k = gsel.shape[1]
    f = ws1.shape[1]
    dh = d // 2
    row = pl.BlockSpec((tm, d), lambda i: (i, 0))
    full = lambda shape: pl.BlockSpec(shape, lambda i: (0, 0))
    anyspace = pl.BlockSpec(memory_space=pl.ANY)
    return pl.pallas_call(
        functools.partial(_moe_combine_body, tm=tm, n_slots=k), grid=(t // tm,),
        in_specs=[anyspace, anyspace, row, pl.BlockSpec((None, 1, d), lambda i: (i // (seq // tm), 0, 0)),
                  pl.BlockSpec((tm, k), lambda i: (i, 0)), row, full((d, f)), full((d, f)), full((f, d))],
        out_specs=row, out_shape=jax.ShapeDtypeStruct((t, d), F32),
        scratch_shapes=[
            pltpu.SMEM((2 * IDX_REC,), jnp.int32),
            pltpu.VMEM((2, k, tm // SUBLANES, SUBLANES, dh), jnp.uint32),
            pltpu.SemaphoreType.DMA((2,)), pltpu.SemaphoreType.DMA((2,)),
        ],
        compiler_params=_params(1), name='moe_combine',
    )(pos_blocks, y.reshape(y.shape[0] // SUBLANES, SUBLANES, dh), x2, g2.reshape(bsz, 1, d), gsel, h2,
      ws1.astype(BF16), ws3.astype(BF16), ws2.astype(BF16))


def kernel(x, c, w_ada, b_ada, ada_layer, norm_mix, norm_ffn, w_in, q_norm_a, k_norm_a, q_norm_b,
           k_norm_b, conv_w, conv_b, w_rgate, b_rgate, w_igate, b_igate, lru_lambda, w_branch_a,
           w_branch_b, w_branch_c, w_out, w_router, b_router, w1_exp, w3_exp, w2_exp, w1_shared,
           w3_shared, w2_shared):
    bsz, seq, d = x.shape
    t = bsz * seq
    depth = w_in.shape[0]
    mixw = d // 4
    a_cols = 3 * mixw + IDX_HEADS * IDX_DIM + IDX_DIM + IDX_HEADS
    off_b = a_cols
    off_c = off_b + N_DIL * 3 * mixw
    off_g = off_c + 2 * mixw
    top_k = min(DSA_TOPK_MAX, seq // 4)
    tm = 1024 if t % 1024 == 0 else 256
    tpb = seq // tm

    mod_shared = _ada_mod(c, w_ada, b_ada).reshape(bsz, N_MOD, d)
    x2 = x.reshape(t, d)
    for l in range(depth):
        mod = mod_shared + ada_layer[l]
        sh1, sc1, g1, sh2, sc2, g2 = [mod[:, j] for j in range(N_MOD)]
        h = _norm_mod(x2, norm_mix[l], sc1, sh1, seq)
        wl = w_in[l]

        n_main = 3 * mixw + IDX_HEADS * IDX_DIM
        pa = _matmul(h, wl[:, :n_main].astype(BF16), BF16, tm=tm, tn=512, name='proj_a')
        w_small = jnp.zeros((d, LANES), BF16).at[:, :IDX_DIM + IDX_HEADS].set(
            wl[:, n_main:a_cols].astype(BF16))
        small = _matmul(h, w_small, F32, tm=tm, tn=LANES, name='proj_a_idx')
        ik = small[:, :IDX_DIM].astype(BF16)
        zeros = jnp.zeros_like(ik)
        ik2 = jnp.concatenate([ik, zeros, zeros, ik], axis=1)
        qn, kn = _qk_norm(pa, q_norm_a[l], k_norm_a[l], mixw)
        iwt = small[:, IDX_DIM:IDX_DIM + IDX_HEADS].reshape(bsz, seq, IDX_HEADS).transpose(0, 2, 1)
        vt = pa[:, 2 * mixw:3 * mixw].reshape(bsz, seq, mixw // HEAD_DIM, HEAD_DIM).transpose(0, 2, 3, 1)
        o_a = _dsa_attention(qn, kn, pa, ik2, iwt, vt, seq, top_k)

        outs, lses = [], []
        for g, (window, dil) in enumerate(DIL_PATTERNS):
            w_g = wl[:, off_b + g * 3 * mixw:off_b + (g + 1) * 3 * mixw].astype(BF16)
            qkv = _to_classes(_matmul(h, w_g, BF16, tm=tm, tn=512, name='proj_b'), bsz, seq, dil)
            og, lg = _dilated_group(qkv, q_norm_b[l, g], k_norm_b[l, g], bsz * dil, seq // dil,
                                    band=window // dil)
            outs.append(_from_classes(og, bsz, seq, dil))
            lses.append(_from_classes(lg, bsz, seq, dil))
        o_b = _dil_merge(outs, lses)

        pc = _matmul(h, wl[:, off_c:off_g].astype(BF16), F32, tm=tm, tn=512, name='proj_c')
        o_c = _rg_lru(pc, conv_w[l], conv_b[l], w_rgate[l], b_rgate[l], w_igate[l], b_igate[l],
                      lru_lambda[l], seq)

        w_gates = wl[:, off_g:].astype(BF16)
        tn_merge = 256
        nd = d // tn_merge
        mixed = _tiled_call(
            _branch_merge_body, t, d, tm, tn_merge,
            [(h, 'row'), (o_a, 'row'), (o_b, 'row'), (o_c, 'row'),
             (w_gates, ('coloff', 0)), (w_gates, ('coloff', nd)), (w_gates, ('coloff', 2 * nd)),
             (w_branch_a[l].astype(BF16), 'col'), (w_branch_b[l].astype(BF16), 'col'),
             (w_branch_c[l].astype(BF16), 'col')],
            [(d, BF16)], name='branch_merge')
        x2 = _tiled_call(
            _mm_residual_body, t, d, tm, 512,
            [(mixed, 'row'), (w_out[l].astype(BF16), 'col'), (x2, 'tile'), (g1.reshape(bsz, 1, d), 'bvec')],
            [(d, F32)], tiles_per_batch=tpb, name='out_proj')

        h2, h2p = _norm_mod_pack(x2, norm_ffn[l], sc2, sh2, seq)
        gsel, eidx, rank, cnt = _router(h2, w_router[l], b_router[l])
        tm_combine = IDX_REC // MOE_TOP_K
        idx, tile_expert, n_used, pos_blocks = _moe_plan(eidx, rank, cnt[0, :N_EXPERTS].astype(jnp.int32),
                                                        N_EXPERTS, tm_combine)
        y = _moe_grouped(h2p, idx, tile_expert, n_used, w1_exp, w3_exp, w2_exp, l)
        x2 = _moe_combine(x2, g2, gsel, h2, w1_shared[l], w3_shared[l], w2_shared[l], y, pos_blocks, seq,
                          tm_combine)
    return x2.reshape(bsz, seq, d)
```

```python
import functools

import jax
import jax.numpy as jnp
from jax import lax
from jax.experimental import pallas as pl
from jax.experimental.pallas import tpu as pltpu

HEAD_DIM = 128
A_HEADS = 8
IDX_HEADS = 16
IDX_DIM = 64
DSA_TOPK_MAX = 256
DIL_PATTERNS = ((128, 1), (512, 4), (2048, 16))
N_DIL = 3
B_HEADS = 8
LRU_BLOCKS = 16
CONV_WIDTH = 4
LRU_C = 8.0
N_EXPERTS = 64
MOE_TOP_K = 8
ROUTED_SCALE = 2.5
N_MOD = 6
RMS_EPS = 1e-6

LANES = 128
SUBLANES = 8
VMEM_LIMIT_BYTES = 56 * 1024 * 1024

NEG_BIG = -1e30
LOG2_E = 1.4426950408889634
INT_MIN = -(2 ** 31)

BF16 = jnp.bfloat16
F32 = jnp.float32


def _params(n_axes):
    return pltpu.CompilerParams(dimension_semantics=("arbitrary",) * n_axes,
                                vmem_limit_bytes=VMEM_LIMIT_BYTES)


def _dot(a, b):
    return jnp.dot(a, b, preferred_element_type=F32)


def _dot_nt(a, b):
    return lax.dot_general(a, b, (((1,), (1,)), ((), ())), preferred_element_type=F32)


def _tiled_call(body, m, n, tm, tn, ins, outs, *, tiles_per_batch=None, name=None):
    assert m % tm == 0 and n % tn == 0, (m, n, tm, tn)
    grid = (m // tm, n // tn)
    in_specs, arrays = [], []
    for arr, kind in ins:
        arrays.append(arr)
        if kind == 'row':
            in_specs.append(pl.BlockSpec((tm, arr.shape[1]), lambda i, j: (i, 0)))
        elif isinstance(kind, tuple) and kind[0] == 'rowoff':
            _, off, width = kind
            in_specs.append(pl.BlockSpec((tm, width), lambda i, j, off=off: (i, off)))
        elif kind == 'col':
            in_specs.append(pl.BlockSpec((arr.shape[0], tn), lambda i, j: (0, j)))
        elif isinstance(kind, tuple) and kind[0] == 'coloff':
            in_specs.append(pl.BlockSpec((arr.shape[0], tn), lambda i, j, off=kind[1]: (0, j + off)))
        elif kind == 'col3':
            in_specs.append(pl.BlockSpec((None, arr.shape[1], arr.shape[2]), lambda i, j: (j, 0, 0)))
        elif kind == 'tile':
            in_specs.append(pl.BlockSpec((tm, tn), lambda i, j: (i, j)))
        elif isinstance(kind, tuple) and kind[0] == 'tileoff':
            in_specs.append(pl.BlockSpec((tm, tn), lambda i, j, off=kind[1]: (i, j + off)))
        elif kind == 'vec':
            in_specs.append(pl.BlockSpec((1, tn), lambda i, j: (0, j)))
        elif kind == 'bvec':
            tpb = tiles_per_batch
            in_specs.append(pl.BlockSpec((None, 1, tn), lambda i, j, tpb=tpb: (i // tpb, 0, j)))
        elif kind == 'full':
            nd = arr.ndim
            in_specs.append(pl.BlockSpec(arr.shape, lambda i, j, nd=nd: (0,) * nd))
        else:
            raise ValueError(kind)
    out_shape, out_specs = [], []
    for n_cols, dtype in outs:
        assert (n_cols * tn) % n == 0
        w = n_cols * tn // n
        out_shape.append(jax.ShapeDtypeStruct((m, n_cols), dtype))
        out_specs.append(pl.BlockSpec((tm, w), lambda i, j: (i, j)))
    single = len(outs) == 1
    res = pl.pallas_call(
        body, grid=grid, in_specs=in_specs,
        out_specs=out_specs[0] if single else out_specs,
        out_shape=out_shape[0] if single else out_shape,
        compiler_params=_params(2), name=name)(*arrays)
    return res


def _mm_body(a_ref, w_ref, o_ref):
    o_ref[...] = _dot(a_ref[...], w_ref[...]).astype(o_ref.dtype)


def _mm_residual_body(a_ref, w_ref, x_ref, g_ref, o_ref):
    o_ref[...] = x_ref[...] + g_ref[...] * _dot(a_ref[...], w_ref[...])


def _matmul(a, w, out_dtype, *, tm, tn, body=_mm_body, name=None):
    return _tiled_call(body, a.shape[0], w.shape[1], tm, tn, [(a, 'row'), (w, 'col')],
                       [(w.shape[1], out_dtype)], name=name)


def _ada_body(c_ref, w_ref, b_ref, o_ref):
    c = c_ref[...]
    a = (c * jax.nn.sigmoid(c)).astype(BF16)
    o_ref[...] = _dot(a, w_ref[...].astype(BF16)) + b_ref[...]


def _ada_mod(c, w_ada, b_ada):
    bsz, d = c.shape
    n = w_ada.shape[1]
    c_pad = jnp.zeros((SUBLANES, d), F32).at[:bsz].set(c)
    out = _tiled_call(_ada_body, SUBLANES, n, SUBLANES, 1024,
                      [(c_pad, 'row'), (w_ada, 'col'), (b_ada.reshape(1, n), 'vec')],
                      [(n, F32)], name='ada_mod')
    return out[:bsz]


def _norm_mod_body(x_ref, g_ref, sc_ref, sh_ref, o_ref):
    x = x_ref[...]
    y = x * lax.rsqrt(jnp.mean(x * x, axis=-1, keepdims=True) + RMS_EPS) * g_ref[...]
    o_ref[...] = (y * (1.0 + sc_ref[...]) + sh_ref[...]).astype(o_ref.dtype)


def _norm_mod(x2, g, sc, sh, seq, tm=256):
    t, d = x2.shape
    bsz = t // seq
    return _tiled_call(_norm_mod_body, t, d, tm, d,
                       [(x2, 'tile'), (g.reshape(1, d), 'vec'), (sc.reshape(bsz, 1, d), 'bvec'),
                        (sh.reshape(bsz, 1, d), 'bvec')],
                       [(d, BF16)], tiles_per_batch=seq // tm, name='norm_mod')


def _head_norm(x, g):
    outs = []
    for h in range(x.shape[1] // HEAD_DIM):
        xh = x[:, h * HEAD_DIM:(h + 1) * HEAD_DIM].astype(F32)
        outs.append(xh * lax.rsqrt(jnp.mean(xh * xh, axis=-1, keepdims=True) + RMS_EPS) * g)
    return jnp.concatenate(outs, axis=1)


def _mm_qk_norm_body(a_ref, w_ref, gq_ref, gk_ref, o_ref, *, q_tiles):
    g = jnp.where(pl.program_id(1) < q_tiles, gq_ref[...], gk_ref[...])
    o_ref[...] = _head_norm(_dot(a_ref[...], w_ref[...]), g).astype(o_ref.dtype)


def _proj_qk_norm(h, w_qk, gq, gk, tm, tn=512):
    n = w_qk.shape[1]
    body = functools.partial(_mm_qk_norm_body, q_tiles=n // 2 // tn)
    return _tiled_call(body, h.shape[0], n, tm, tn,
                       [(h, 'row'), (w_qk, 'col'), (gq.reshape(1, HEAD_DIM), 'full'),
                        (gk.reshape(1, HEAD_DIM), 'full')], [(n, BF16)], name='proj_qk_norm')


def _dsa_body(qb_ref, kb_ref, q_ref, iq_ref, iwt_ref, ik_ref, k_ref, vt_ref, o_ref,
              keys_ref, thr_ref, bias_ref, s_ref, p_ref, m_ref, l_ref, acc_ref, *, tq, tk, top_k, n_heads):
    qb = qb_ref[pl.program_id(1)]
    kb = kb_ref[pl.program_id(1)]
    kb_last = ((qb + 1) * tq - 1) // tk
    n_chunks = kb_last + 1
    kpos = lax.broadcasted_iota(jnp.int32, (tk, tq), 0)
    qpos = lax.broadcasted_iota(jnp.int32, (tk, tq), 1) + qb * tq
    n_acc = 4 * SUBLANES

    @pl.when(kb == 0)
    def _scores_and_threshold():
        w = iwt_ref[...] * (IDX_HEADS ** -0.5 * IDX_DIM ** -0.5)

        def score_chunk(c, carry):
            ik2 = ik_ref[pl.ds(pl.multiple_of(c * tk, tk), tk), :]
            acc = jnp.zeros((tk, tq), F32)
            for p in range(IDX_HEADS // 2):
                iq_pair = iq_ref[:, p * LANES:(p + 1) * LANES]
                for half in range(2):
                    rel = _dot_nt(ik2[:, half * LANES:(half + 1) * LANES], iq_pair)
                    h = 2 * p + half
                    acc = acc + w[h:h + 1, :] * jnp.maximum(rel, 0.0)
            bits = pltpu.bitcast(acc, jnp.int32)
            key = jnp.where(bits < 0, bits ^ 0x7FFFFFFF, bits)
            key = jnp.where(kpos + c * tk <= qpos, key, INT_MIN)
            keys_ref[c] = key
            return carry

        lax.fori_loop(0, n_chunks, score_chunk, 0)

        def count_ge(cand):
            def body(c, cnt):
                for j in range(tk // n_acc):
                    blk = keys_ref[c, j * n_acc:(j + 1) * n_acc, :]
                    cnt = cnt + jnp.where(blk >= cand, 1.0, 0.0)
                return cnt
            cnt = lax.fori_loop(0, n_chunks, body, jnp.zeros((n_acc, tq), F32))
            return jnp.sum(cnt, axis=0, keepdims=True)

        k_f = float(top_k)
        t0 = jnp.where(count_ge(jnp.zeros((1, tq), jnp.int32)) >= k_f, 0, INT_MIN).astype(jnp.int32)

        def bit_body(i, t):
            cand = t | lax.shift_left(jnp.int32(1), 30 - i)
            return jnp.where(count_ge(cand) >= k_f, cand, t)

        t = lax.fori_loop(0, 31, bit_body, t0)
        thr_ref[...] = jnp.broadcast_to(t, thr_ref.shape)
        m_ref[...] = jnp.full(m_ref.shape, NEG_BIG, F32)
        l_ref[...] = jnp.zeros(l_ref.shape, F32)
        acc_ref[...] = jnp.zeros(acc_ref.shape, F32)

    @pl.when(kb <= kb_last)
    def _attend():
        sel = (keys_ref[kb] >= thr_ref[0:1, :]) & (kpos + kb * tk <= qpos)
        bias_ref[...] = jnp.where(sel, 0.0, NEG_BIG)
        c = HEAD_DIM ** -0.5 * LOG2_E

        def raw_scores(h):
            sl = slice(h * HEAD_DIM, (h + 1) * HEAD_DIM)
            s_ref[h % 4] = _dot_nt(k_ref[:, sl], q_ref[:, sl])

        def softmax(h):
            s = s_ref[h % 4] + bias_ref[...]
            m_old = m_ref[h, 0:1, :]
            m_new = jnp.maximum(m_old, jnp.max(s, axis=0, keepdims=True))
            p = jnp.exp2((s - m_new) * c)
            alpha = jnp.exp2((m_old - m_new) * c)
            l_new = alpha * l_ref[h, 0:1, :] + jnp.sum(p, axis=0, keepdims=True)
            p_ref[h % 3] = p.astype(BF16)
            m_ref[h] = jnp.broadcast_to(m_new, (SUBLANES, tq))
            l_ref[h] = jnp.broadcast_to(l_new, (SUBLANES, tq))
            return alpha

        def accumulate(h, alpha):
            acc_ref[h] = alpha * acc_ref[h] + _dot(vt_ref[h], p_ref[h % 3])

        s_ahead, p_ahead = 3, 2
        for h in range(s_ahead):
            raw_scores(h)
        alphas = {h: softmax(h) for h in range(p_ahead)}
        for h in range(n_heads):
            if h + s_ahead < n_heads:
                raw_scores(h + s_ahead)
            if h + p_ahead < n_heads:
                alphas[h + p_ahead] = softmax(h + p_ahead)
            accumulate(h, alphas.pop(h))

    @pl.when(kb == kb_last)
    def _finish():
        for h in range(n_heads):
            sl = slice(h * HEAD_DIM, (h + 1) * HEAD_DIM)
            o_ref[:, sl] = (acc_ref[h] / l_ref[h, 0:1, :]).T.astype(o_ref.dtype)


def _dsa_attention(pqk, pvi, ik2, iwt, vt, seq, top_k, *, tq=256, tk=512):
    t = pqk.shape[0]
    width = pqk.shape[1] // 2
    bsz = t // seq
    nq, nk = seq // tq, seq // tk
    n_heads = width // HEAD_DIM
    assert width == IDX_HEADS * IDX_DIM and pvi.shape[1] == 2 * width

    pairs = [(qb, kb) for qb in range(nq) for kb in range(((qb + 1) * tq - 1) // tk + 1)]
    qb_tab = jnp.asarray([p[0] for p in pairs], jnp.int32)
    kb_tab = jnp.asarray([p[1] for p in pairs], jnp.int32)

    body = functools.partial(_dsa_body, tq=tq, tk=tk, top_k=top_k, n_heads=n_heads)
    grid_spec = pltpu.PrefetchScalarGridSpec(
        num_scalar_prefetch=2, grid=(bsz, len(pairs)),
        in_specs=[
            pl.BlockSpec((tq, width), lambda b, s, qt, kt: (b * nq + qt[s], 0)),
            pl.BlockSpec((tq, width), lambda b, s, qt, kt: (b * nq + qt[s], 1)),
            pl.BlockSpec((None, IDX_HEADS, tq), lambda b, s, qt, kt: (b, 0, qt[s])),
            pl.BlockSpec((seq, 2 * LANES), lambda b, s, qt, kt: (b, 0)),
            pl.BlockSpec((tk, width), lambda b, s, qt, kt: (b * nk + kt[s], 1)),
            pl.BlockSpec((None, n_heads, HEAD_DIM, tk), lambda b, s, qt, kt: (b, 0, 0, kt[s])),
        ],
        out_specs=pl.BlockSpec((tq, width), lambda b, s, qt, kt: (b * nq + qt[s], 0)),
        scratch_shapes=[
            pltpu.VMEM((nk, tk, tq), jnp.int32),
            pltpu.VMEM((SUBLANES, tq), jnp.int32),
            pltpu.VMEM((tk, tq), F32),
            pltpu.VMEM((4, tk, tq), F32),
            pltpu.VMEM((3, tk, tq), BF16),
            pltpu.VMEM((n_heads, SUBLANES, tq), F32),
            pltpu.VMEM((n_heads, SUBLANES, tq), F32),
            pltpu.VMEM((n_heads, HEAD_DIM, tq), F32),
        ])
    return pl.pallas_call(
        body, grid_spec=grid_spec, out_shape=jax.ShapeDtypeStruct((t, width), BF16),
        compiler_params=_params(2), name='dsa_attention',
    )(qb_tab, kb_tab, pqk, pvi, iwt, ik2, pqk, vt)


def _dil_body(q_ref, kc_ref, vc_ref, kp_ref, vp_ref, gq_ref, gk_ref, o_ref, lse_ref,
              kcat_ref, vcat_ref, *, band, rows, n_heads):
    n = pl.program_id(1)
    kcat_ref[0:band, :] = _head_norm(kp_ref[...], gk_ref[...]).astype(BF16)
    kcat_ref[band:band + rows, :] = _head_norm(kc_ref[...], gk_ref[...]).astype(BF16)
    vcat_ref[0:band, :] = vp_ref[...]
    vcat_ref[band:band + rows, :] = vc_ref[...]
    qi = lax.broadcasted_iota(jnp.int32, (band, 2 * band), 0)
    ki = lax.broadcasted_iota(jnp.int32, (band, 2 * band), 1)
    back = qi + band - ki
    in_band = (back >= 0) & (back <= band)
    scale = HEAD_DIM ** -0.5
    gq = gq_ref[...]

    def sub_block(j, carry):
        r0 = pl.multiple_of(j * band, band)
        bias = jnp.where(in_band & ((ki >= band) | (n * (rows // band) + j > 0)), 0.0, NEG_BIG)
        heads = [slice(h * HEAD_DIM, (h + 1) * HEAD_DIM) for h in range(n_heads)]

        def scores(sl):
            qh = q_ref[pl.ds(r0, band), sl].astype(F32)
            qh = (qh * lax.rsqrt(jnp.mean(qh * qh, axis=-1, keepdims=True) + RMS_EPS) * gq).astype(BF16)
            return _dot_nt(qh, kcat_ref[pl.ds(r0, 2 * band), sl]) * scale + bias

        s_all = [scores(sl) for sl in heads]
        m_all = [jnp.max(s, axis=1, keepdims=True) for s in s_all]
        p_all = [jnp.exp(s - m) for s, m in zip(s_all, m_all)]
        l_all = [jnp.sum(p, axis=1, keepdims=True) for p in p_all]
        for h, sl in enumerate(heads):
            o = _dot(p_all[h].astype(BF16), vcat_ref[pl.ds(r0, 2 * band), sl]) / l_all[h]
            o_ref[pl.ds(r0, band), sl] = o.astype(o_ref.dtype)
            lse_ref[pl.ds(r0, band), h:h + 1] = m_all[h] + jnp.log(l_all[h])
        return carry

    lax.fori_loop(0, rows // band, sub_block, 0)


def _dilated_group(qkv, gq, gk, n_seq, len_seq, *, band=128, rows=512):
    t, w3 = qkv.shape
    width = w3 // 3
    n_heads = width // HEAD_DIM
    rows = min(rows, len_seq)
    nblk = len_seq // rows
    sub = rows // band

    def cur(c):
        return lambda s, n: (s * nblk + n, c)

    def prev(c):
        return lambda s, n: (s * nblk * sub + jnp.maximum(n * sub - 1, 0), c)

    body = functools.partial(_dil_body, band=band, rows=rows, n_heads=n_heads)
    return pl.pallas_call(
        body, grid=(n_seq, nblk),
        in_specs=[
            pl.BlockSpec((rows, width), cur(0)), pl.BlockSpec((rows, width), cur(1)),
            pl.BlockSpec((rows, width), cur(2)),
            pl.BlockSpec((band, width), prev(1)), pl.BlockSpec((band, width), prev(2)),
            pl.BlockSpec((1, HEAD_DIM), lambda s, n: (0, 0)), pl.BlockSpec((1, HEAD_DIM), lambda s, n: (0, 0)),
        ],
        out_specs=[pl.BlockSpec((rows, width), lambda s, n: (s * nblk + n, 0)),
                   pl.BlockSpec((rows, n_heads), lambda s, n: (s * nblk + n, 0))],
        out_shape=[jax.ShapeDtypeStruct((t, width), BF16), jax.ShapeDtypeStruct((t, n_heads), F32)],
        scratch_shapes=[pltpu.VMEM((band + rows, width), BF16), pltpu.VMEM((band + rows, width), BF16)],
        compiler_params=_params(2), name='dilated_attention',
    )(qkv, qkv, qkv, qkv, qkv, gq.reshape(1, HEAD_DIM), gk.reshape(1, HEAD_DIM))


def _dil_merge_body(o0_ref, o1_ref, o2_ref, l0_ref, l1_ref, l2_ref, out_ref):
    l0, l1, l2 = l0_ref[...], l1_ref[...], l2_ref[...]
    m = jnp.maximum(jnp.maximum(l0, l1), l2)
    e0, e1, e2 = jnp.exp(l0 - m), jnp.exp(l1 - m), jnp.exp(l2 - m)
    den = e0 + e1 + e2
    a0, a1, a2 = e0 / den, e1 / den, e2 / den
    for h in range(l0.shape[1]):
        sl = slice(h * HEAD_DIM, (h + 1) * HEAD_DIM)
        out_ref[:, sl] = (a0[:, h:h + 1] * o0_ref[:, sl].astype(F32)
                          + a1[:, h:h + 1] * o1_ref[:, sl].astype(F32)
                          + a2[:, h:h + 1] * o2_ref[:, sl].astype(F32)).astype(out_ref.dtype)


def _dil_merge(outs, lses, tm=512):
    t, width = outs[0].shape
    nh = lses[0].shape[1]
    ospec = pl.BlockSpec((tm, width), lambda i: (i, 0))
    lspec = pl.BlockSpec((tm, nh), lambda i: (i, 0))
    return pl.pallas_call(
        _dil_merge_body, grid=(t // tm,), in_specs=[ospec] * 3 + [lspec] * 3, out_specs=ospec,
        out_shape=jax.ShapeDtypeStruct((t, width), BF16), compiler_params=_params(1), name='dilated_merge',
    )(*outs, *lses)


def _to_classes(a, bsz, seq, dil):
    if dil == 1:
        return a
    c = a.shape[1]
    return a.reshape(bsz, seq // dil, dil, c).transpose(0, 2, 1, 3).reshape(bsz * seq, c)


def _from_classes(a, bsz, seq, dil):
    if dil == 1:
        return a
    c = a.shape[1]
    return a.reshape(bsz, dil, seq // dil, c).transpose(0, 2, 1, 3).reshape(bsz * seq, c)


def _lru_body(x_ref, y_ref, xp_ref, cw_ref, cb_ref, wr_ref, br_ref, wi_ref, bi_ref, lam_ref, o_ref,
              carry_ref, *, tm, tiles_per_batch):
    i = pl.program_id(0)
    first = (i % tiles_per_batch) == 0
    x = x_ref[...]
    prev = jnp.where(first, 0.0, xp_ref[...])
    xs = jnp.concatenate([prev, x], axis=0)
    xc = cb_ref[...] + cw_ref[CONV_WIDTH - 1:CONV_WIDTH, :] * x
    for j in range(CONV_WIDTH - 1):
        back = CONV_WIDTH - 1 - j
        xc = xc + cw_ref[j:j + 1, :] * xs[SUBLANES - back:SUBLANES - back + tm, :]
    xcb = xc.astype(BF16)
    def gate(w_ref, b_ref):
        n_chunks, cw, _ = w_ref.shape
        pre = [_dot(xcb[:, c * cw:(c + 1) * cw], w_ref[c]) for c in range(n_chunks)]
        return jax.nn.sigmoid(jnp.concatenate(pre, axis=1) + b_ref[...])

    r = gate(wr_ref, br_ref)
    ig = gate(wi_ref, bi_ref)
    lam = lam_ref[...]
    softplus_neg = jnp.maximum(-lam, 0.0) + jnp.log1p(jnp.exp(-jnp.abs(lam)))
    log_a = (-LRU_C * softplus_neg) * r
    a = jnp.exp(log_a)
    b = jnp.sqrt(1.0 - a * a) * (ig * xc)
    rows = lax.broadcasted_iota(jnp.int32, a.shape, 0)
    step = 1
    while step < tm:
        a_sh = pltpu.roll(a, step, 0)
        b_sh = pltpu.roll(b, step, 0)
        valid = rows >= step
        b = jnp.where(valid, a * b_sh + b, b)
        a = jnp.where(valid, a * a_sh, a)
        step *= 2
    h0 = jnp.where(first, 0.0, carry_ref[0:1, :])
    h = a * h0 + b
    carry_ref[...] = jnp.broadcast_to(h[tm - 1:tm, :], carry_ref.shape)
    o_ref[...] = (h * jax.nn.gelu(y_ref[...])).astype(o_ref.dtype)


MXU_WIDTH = 256


def _block_diag_chunks(w):
    g, n, _ = w.shape
    per = MXU_WIDTH // n
    wc = w.reshape(g // per, per, n, n)
    eye = jnp.eye(per, dtype=w.dtype)
    return (eye[None, :, None, :, None] * wc[:, :, :, None, :]).reshape(g // per, per * n, per * n)


def _rg_lru(pc, conv_w, conv_b, w_r, b_r, w_i, b_i, lam, seq, tm=256):
    t, w2 = pc.shape
    w = w2 // 2
    row = lambda v: v.reshape(1, w)
    full2 = lambda shape: pl.BlockSpec(shape, lambda i: (0, 0))
    chunks = pl.BlockSpec((w // MXU_WIDTH, MXU_WIDTH, MXU_WIDTH), lambda i: (0, 0, 0))
    body = functools.partial(_lru_body, tm=tm, tiles_per_batch=seq // tm)
    return pl.pallas_call(
        body, grid=(t // tm,),
        in_specs=[
            pl.BlockSpec((tm, w), lambda i: (i, 0)),
            pl.BlockSpec((tm, w), lambda i: (i, 1)),
            pl.BlockSpec((SUBLANES, w), lambda i: (jnp.maximum(i * (tm // SUBLANES) - 1, 0), 0)),
            full2((CONV_WIDTH, w)), full2((1, w)), chunks, full2((1, w)), chunks,
            full2((1, w)), full2((1, w)),
        ],
        out_specs=pl.BlockSpec((tm, w), lambda i: (i, 0)),
        out_shape=jax.ShapeDtypeStruct((t, w), BF16),
        scratch_shapes=[pltpu.VMEM((SUBLANES, w), F32)],
        compiler_params=_params(1), name='rg_lru',
    )(pc, pc, pc, conv_w, row(conv_b), _block_diag_chunks(w_r).astype(BF16), row(b_r),
      _block_diag_chunks(w_i).astype(BF16), row(b_i), row(lam))


def _branch_merge_body(h_ref, oa_ref, ob_ref, oc_ref, wga_ref, wgb_ref, wgc_ref, wa_ref, wb_ref, wc_ref, o_ref):
    h = h_ref[...]
    mixed = (jax.nn.sigmoid(_dot(h, wga_ref[...])) * _dot(oa_ref[...], wa_ref[...])
             + jax.nn.sigmoid(_dot(h, wgb_ref[...])) * _dot(ob_ref[...], wb_ref[...])
             + jax.nn.sigmoid(_dot(h, wgc_ref[...])) * _dot(oc_ref[...], wc_ref[...]))
    o_ref[...] = mixed.astype(o_ref.dtype)


MOE_ROWS = 256
IDX_REC = 1024


def _router_body(a_ref, w_ref, b_ref, gsel_ref, eidx_ref, rank_ref, cnt_ref, carry_ref, *, n_experts, top_k, tm):
    i = pl.program_id(0)

    @pl.when(i == 0)
    def _():
        carry_ref[...] = jnp.zeros(carry_ref.shape, F32)

    logits = _dot(a_ref[...], w_ref[...])
    lane = lax.broadcasted_iota(jnp.int32, logits.shape, 1).astype(F32)
    scores = jax.nn.sigmoid(logits)
    cur = jnp.where(lane < n_experts, scores + b_ref[...], -jnp.inf)
    mask = jnp.zeros(logits.shape, F32)
    hits, picked_scores = [], []
    for r in range(top_k):
        m = jnp.max(cur, axis=1, keepdims=True)
        idx = jnp.min(jnp.where(cur == m, lane, float(LANES)), axis=1, keepdims=True)
        hit = lane == idx
        hits.append(hit)
        picked_scores.append(jnp.sum(jnp.where(hit, scores, 0.0), axis=1, keepdims=True))
        eidx_ref[:, r:r + 1] = idx.astype(jnp.int32)
        cur = jnp.where(hit, -jnp.inf, cur)
        mask = mask + jnp.where(hit, 1.0, 0.0)
    total = picked_scores[0]
    for r in range(1, top_k):
        total = total + picked_scores[r]
    ri = lax.broadcasted_iota(jnp.int32, (tm, tm), 0)
    ci = lax.broadcasted_iota(jnp.int32, (tm, tm), 1)
    tri = jnp.where(ri > ci, 1.0, 0.0).astype(BF16)
    before = _dot(tri, mask.astype(BF16)) + carry_ref[0:1, :]
    for r in range(top_k):
        gsel_ref[:, r:r + 1] = picked_scores[r] / total * ROUTED_SCALE
        rank_ref[:, r:r + 1] = jnp.sum(jnp.where(hits[r], before, 0.0), axis=1, keepdims=True).astype(jnp.int32)
    carry_ref[...] = carry_ref[...] + jnp.sum(mask, axis=0, keepdims=True)
    cnt_ref[...] = carry_ref[...]


def _router(h2, w_router, b_router, tm=1024):
    t, d = h2.shape
    n_exp = w_router.shape[1]
    w_r = jnp.zeros((d, LANES), BF16).at[:, :n_exp].set(w_router.astype(BF16))
    b_r = jnp.zeros((1, LANES), F32).at[0, :n_exp].set(b_router.astype(F32))
    col = lambda dt: jax.ShapeDtypeStruct((t, MOE_TOP_K), dt)
    cspec = pl.BlockSpec((tm, MOE_TOP_K), lambda i: (i, 0))
    return pl.pallas_call(
        functools.partial(_router_body, n_experts=n_exp, top_k=MOE_TOP_K, tm=tm), grid=(t // tm,),
        in_specs=[pl.BlockSpec((tm, d), lambda i: (i, 0)), pl.BlockSpec((d, LANES), lambda i: (0, 0)),
                  pl.BlockSpec((1, LANES), lambda i: (0, 0))],
        out_specs=[cspec, cspec, cspec, pl.BlockSpec((SUBLANES, LANES), lambda i: (0, 0))],
        out_shape=[col(F32), col(jnp.int32), col(jnp.int32), jax.ShapeDtypeStruct((SUBLANES, LANES), F32)],
        scratch_shapes=[pltpu.VMEM((SUBLANES, LANES), F32)],
        compiler_params=_params(1), name='router')(h2, w_r, b_r)


def _pack_bf16_pairs(lo, hi):
    lo_bits = pltpu.bitcast(lo.astype(BF16).astype(F32), jnp.uint32)
    hi_bits = pltpu.bitcast(hi.astype(BF16).astype(F32), jnp.uint32)
    return (hi_bits & jnp.uint32(0xFFFF0000)) | lax.shift_right_logical(lo_bits, jnp.uint32(16))


def _unpack_lo(u):
    return pltpu.bitcast(lax.shift_left(u, jnp.uint32(16)), F32)


def _unpack_hi(u):
    return pltpu.bitcast(u & jnp.uint32(0xFFFF0000), F32)


def _norm_mod_pack_body(x_ref, g_ref, sc_ref, sh_ref, o_ref, p_ref):
    x = x_ref[...]
    y = x * lax.rsqrt(jnp.mean(x * x, axis=-1, keepdims=True) + RMS_EPS) * g_ref[...]
    y = y * (1.0 + sc_ref[...]) + sh_ref[...]
    half = y.shape[1] // 2
    o_ref[...] = y.astype(o_ref.dtype)
    p_ref[...] = _pack_bf16_pairs(y[:, :half], y[:, half:])


def _norm_mod_pack(x2, g, sc, sh, seq, tm=256):
    t, d = x2.shape
    bsz = t // seq
    row = pl.BlockSpec((tm, d), lambda i: (i, 0))
    vec = pl.BlockSpec((1, d), lambda i: (0, 0))
    bvec = pl.BlockSpec((None, 1, d), lambda i: (i // (seq // tm), 0, 0))
    return pl.pallas_call(
        _norm_mod_pack_body, grid=(t // tm,), in_specs=[row, vec, bvec, bvec],
        out_specs=[row, pl.BlockSpec((tm, d // 2), lambda i: (i, 0))],
        out_shape=[jax.ShapeDtypeStruct((t, d), BF16), jax.ShapeDtypeStruct((t, d // 2), jnp.uint32)],
        compiler_params=_params(1), name='norm_mod_pack',
    )(x2, g.reshape(1, d), sc.reshape(bsz, 1, d), sh.reshape(bsz, 1, d))


def _moe_group_body(te_ref, nu_ref, idx_hbm, h_hbm, w1_ref, w3_ref, w2_ref, o_ref,
                    idx_smem, xbuf, isem, gsem, *, tmr, n_tiles):
    i = pl.program_id(0)
    n = nu_ref[0]
    dh = xbuf.shape[-1]
    n_groups = tmr // SUBLANES
    n_up = n_groups // 2
    n_down = n_up // 2
    kc = 2 * dh // n_up

    def idx_copy(tile, rec):
        tile = jnp.minimum(tile, n_tiles - 1)
        return pltpu.make_async_copy(idx_hbm.at[pl.ds(pl.multiple_of(tile * IDX_REC, IDX_REC), IDX_REC)],
                                     idx_smem.at[pl.ds(pl.multiple_of(rec * IDX_REC, IDX_REC), IDX_REC)],
                                     isem.at[rec])

    def row_in(tok, g, u, s):
        return pltpu.make_async_copy(h_hbm.at[lax.shift_right_logical(tok, 3), pl.ds(tok & (SUBLANES - 1), 1)],
                                     xbuf.at[s, g, pl.ds(u, 1)], gsem.at[s])

    def gather_group(g, rec, s):
        for u in range(SUBLANES):
            row_in(idx_smem[rec * IDX_REC + g * SUBLANES + u], g, u, s).start(priority=u % 2)

    def per_group(fn):
        def body(g, c):
            fn(g)
            return c
        lax.fori_loop(0, n_groups, body, 0)

    def gather_wait(s):
        per_group(lambda g: [row_in(jnp.int32(0), g, u, s).wait() for u in range(SUBLANES)])

    @pl.when(i == 0)
    def _prologue():
        for k in range(2):
            idx_copy(k, k).start()
            idx_copy(k, k).wait()
        idx_copy(2, 2).start()
        per_group(lambda g: gather_group(g, 0, 0))
        per_group(lambda g: gather_group(g, 1, 1))

    @pl.when(i < n)
    def _tile():
        s, s_in, r_in = i % 3, (i + 2) % 3, (i + 2) % 3
        idx_copy(i + 2, r_in).wait()
        gather_wait(s)
        h1 = jnp.zeros((tmr, w1_ref.shape[-1]), F32)
        h3 = jnp.zeros((tmr, w1_ref.shape[-1]), F32)
        for c in range(n_up):
            gather_group(2 * c, r_in, s_in)
            gather_group(2 * c + 1, r_in, s_in)
            pc = c % (n_up // 2)
            u = xbuf[s, :, :, pc * kc:(pc + 1) * kc].reshape(tmr, kc)
            xc = (_unpack_lo(u) if c < n_up // 2 else _unpack_hi(u)).astype(BF16)
            h1 = h1 + _dot(xc, w1_ref[c * kc:(c + 1) * kc, :].astype(BF16))
            h3 = h3 + _dot(xc, w3_ref[c * kc:(c + 1) * kc, :].astype(BF16))
        hid = ((h1 * jax.nn.sigmoid(h1)) * h3).astype(BF16)
        for c in range(n_down):
            y_lo = _dot(hid, w2_ref[:, c * kc:(c + 1) * kc].astype(BF16))
            y_hi = _dot(hid, w2_ref[:, dh + c * kc:dh + (c + 1) * kc].astype(BF16))
            o_ref[:, c * kc:(c + 1) * kc] = _pack_bf16_pairs(y_lo, y_hi)
        idx_copy(i + 3, i % 3).start()

    @pl.when(i == n)
    def _drain():
        gather_wait(i % 3)
        gather_wait((i + 1) % 3)
        idx_copy(i + 2, (i + 2) % 3).wait()


def _moe_grouped(h2p, idx, tile_expert, n_used, w1, w3, w2, layer):
    t, dh = h2p.shape
    d = 2 * dh
    n_tiles = idx.shape[0] // IDX_REC
    tmr = MOE_ROWS
    f = w1.shape[3]
    assert n_tiles >= 4
    wmap = lambda i, te, nu: (layer, te[i], 0, 0)
    grid_spec = pltpu.PrefetchScalarGridSpec(
        num_scalar_prefetch=2, grid=(n_tiles + 1,),
        in_specs=[
            pl.BlockSpec(memory_space=pl.ANY), pl.BlockSpec(memory_space=pl.ANY),
            pl.BlockSpec((None, None, d, f), wmap), pl.BlockSpec((None, None, d, f), wmap),
            pl.BlockSpec((None, None, f, d), wmap),
        ],
        out_specs=pl.BlockSpec((tmr, dh), lambda i, te, nu: (jnp.minimum(i, nu[0] - 1), 0)),
        scratch_shapes=[
            pltpu.SMEM((3 * IDX_REC,), jnp.int32),
            pltpu.VMEM((3, tmr // SUBLANES, SUBLANES, dh), jnp.uint32),
            pltpu.SemaphoreType.DMA((3,)), pltpu.SemaphoreType.DMA((3,)),
        ])
    return pl.pallas_call(
        functools.partial(_moe_group_body, tmr=tmr, n_tiles=n_tiles), grid_spec=grid_spec,
        out_shape=jax.ShapeDtypeStruct((n_tiles * tmr, dh), jnp.uint32),
        compiler_params=_params(1), name='moe_grouped',
    )(tile_expert, n_used, idx, h2p.reshape(t // SUBLANES, SUBLANES, dh), w1, w3, w2)


def _moe_plan(eidx, rank, counts, n_experts, tm_combine):
    t, k = eidx.shape
    tmr = MOE_ROWS
    n_tiles = (t * k) // tmr + n_experts
    padded = ((counts + tmr - 1) // tmr) * tmr
    ends = jnp.cumsum(padded)
    offsets = ends - padded
    n_used = (ends[-1] // tmr).astype(jnp.int32)
    pos = offsets[eidx] + rank
    tok = jnp.broadcast_to(jnp.arange(t, dtype=jnp.int32)[:, None], (t, k))
    p = jnp.arange(n_tiles * tmr, dtype=jnp.int32)
    tok_sorted = (p % tmr).at[pos.reshape(-1)].set(tok.reshape(-1)).reshape(n_tiles, tmr)
    idx = jnp.concatenate([tok_sorted, jnp.zeros((n_tiles, IDX_REC - tmr), jnp.int32)], axis=1).reshape(-1)
    tile_start = jnp.minimum(jnp.arange(n_tiles + 1, dtype=jnp.int32), n_used - 1) * tmr
    tile_expert = jnp.sum((ends[None, :] <= tile_start[:, None]).astype(jnp.int32), axis=1)
    assert tm_combine * k == IDX_REC
    pos_blocks = pos.reshape(t // tm_combine, tm_combine, k).transpose(0, 2, 1).reshape(-1)
    return idx, jnp.minimum(tile_expert, n_experts - 1), n_used.reshape(1), pos_blocks


def _moe_combine_body(pos_hbm, y_hbm, x_ref, g_ref, gsel_ref, h_ref, w1_ref, w3_ref, w2_ref, o_ref,
                      pos_smem, ybuf, isem, gsem, *, tm, n_slots):
    i = pl.program_id(0)
    n_steps = pl.num_programs(0)
    n_groups = tm // SUBLANES
    half = x_ref.shape[1] // 2

    def pos_copy(step, rec):
        step = jnp.minimum(step, n_steps - 1)
        return pltpu.make_async_copy(pos_hbm.at[pl.ds(pl.multiple_of(step * IDX_REC, IDX_REC), IDX_REC)],
                                     pos_smem.at[pl.ds(pl.multiple_of(rec * IDX_REC, IDX_REC), IDX_REC)],
                                     isem.at[rec])

    def row_in(row, r, g, u, s):
        return pltpu.make_async_copy(y_hbm.at[lax.shift_right_logical(row, 3), pl.ds(row & (SUBLANES - 1), 1)],
                                     ybuf.at[s, r, g, pl.ds(u, 1)], gsem.at[s])

    def per_group(fn):
        def body(g, c):
            fn(g)
            return c
        lax.fori_loop(0, n_groups, body, 0)

    def gather_group(g, rec, s):
        for r in range(n_slots):
            for u in range(SUBLANES):
                row_in(pos_smem[rec * IDX_REC + r * tm + g * SUBLANES + u], r, g, u, s).start(priority=u % 2)

    def gather_wait(s):
        def group(g):
            for r in range(n_slots):
                for u in range(SUBLANES):
                    row_in(jnp.int32(0), r, g, u, s).wait()
        per_group(group)

    @pl.when(i == 0)
    def _prologue():
        pos_copy(0, 0).start()
        pos_copy(0, 0).wait()
        per_group(lambda g: gather_group(g, 0, 0))
        pos_copy(1, 1).start()

    nxt, rec_next = (i + 1) % 2, (i + 1) % 2
    pos_copy(i + 1, rec_next).wait()
    groups = iter(range(n_groups))

    def issue(count):
        for _ in range(count):
            g = next(groups, None)
            if g is not None:
                gather_group(g, rec_next, nxt)

    issue(2)
    a = h_ref[...]
    h1 = _dot(a, w1_ref[...])
    issue(2)
    hid = ((h1 * jax.nn.sigmoid(h1)) * _dot(a, w3_ref[...])).astype(BF16)
    issue(2)
    acc_lo = _dot(hid, w2_ref[:, :half])
    issue(2)
    acc_hi = _dot(hid, w2_ref[:, half:])
    gather_wait(i % 2)
    pos_copy(i + 2, i % 2).start()
    gs = gsel_ref[...]
    for r in range(n_slots):
        issue(1)
        u = ybuf[i % 2, r].reshape(tm, half)
        acc_lo = acc_lo + gs[:, r:r + 1] * _unpack_lo(u)
        acc_hi = acc_hi + gs[:, r:r + 1] * _unpack_hi(u)
    issue(n_groups)
    o_ref[:, :half] = x_ref[:, :half] + g_ref[:, :half] * acc_lo
    o_ref[:, half:] = x_ref[:, half:] + g_ref[:, half:] * acc_hi

    @pl.when(i == n_steps - 1)
    def _drain():
        gather_wait(nxt)
        pos_copy(i + 2, i % 2).wait()


def _moe_combine(x2, g2, gsel, h2, ws1, ws3, ws2, y, pos_blocks, seq, tm):
    t, d = x2.shape
    bsz = t // seq
    k = gsel.shape[1]
    f = ws1.shape[1]
    dh = d // 2
    row = pl.BlockSpec((tm, d), lambda i: (i, 0))
    full = lambda shape: pl.BlockSpec(shape, lambda i: (0, 0))
    anyspace = pl.BlockSpec(memory_space=pl.ANY)
    return pl.pallas_call(
        functools.partial(_moe_combine_body, tm=tm, n_slots=k), grid=(t // tm,),
        in_specs=[anyspace, anyspace, row, pl.BlockSpec((None, 1, d), lambda i: (i // (seq // tm), 0, 0)),
                  pl.BlockSpec((tm, k), lambda i: (i, 0)), row, full((d, f)), full((d, f)), full((f, d))],
        out_specs=row, out_shape=jax.ShapeDtypeStruct((t, d), F32),
        scratch_shapes=[
            pltpu.SMEM((2 * IDX_REC,), jnp.int32),
            pltpu.VMEM((2, k, tm // SUBLANES, SUBLANES, dh), jnp.uint32),
            pltpu.SemaphoreType.DMA((2,)), pltpu.SemaphoreType.DMA((2,)),
        ],
        compiler_params=_params(1), name='moe_combine',
    )(pos_blocks, y.reshape(y.shape[0] // SUBLANES, SUBLANES, dh), x2, g2.reshape(bsz, 1, d), gsel, h2,
      ws1.astype(BF16), ws3.astype(BF16), ws2.astype(BF16))


def kernel(x, c, w_ada, b_ada, ada_layer, norm_mix, norm_ffn, w_in, q_norm_a, k_norm_a, q_norm_b,
           k_norm_b, conv_w, conv_b, w_rgate, b_rgate, w_igate, b_igate, lru_lambda, w_branch_a,
           w_branch_b, w_branch_c, w_out, w_router, b_router, w1_exp, w3_exp, w2_exp, w1_shared,
           w3_shared, w2_shared):
    bsz, seq, d = x.shape
    t = bsz * seq
    depth = w_in.shape[0]
    mixw = d // 4
    a_cols = 3 * mixw + IDX_HEADS * IDX_DIM + IDX_DIM + IDX_HEADS
    off_b = a_cols
    off_c = off_b + N_DIL * 3 * mixw
    off_g = off_c + 2 * mixw
    top_k = min(DSA_TOPK_MAX, seq // 4)
    tm = 1024 if t % 1024 == 0 else 256
    tpb = seq // tm

    mod_shared = _ada_mod(c, w_ada, b_ada).reshape(bsz, N_MOD, d)
    x2 = x.reshape(t, d)
    for l in range(depth):
        mod = mod_shared + ada_layer[l]
        sh1, sc1, g1, sh2, sc2, g2 = [mod[:, j] for j in range(N_MOD)]
        h = _norm_mod(x2, norm_mix[l], sc1, sh1, seq)
        wl = w_in[l]

        n_main = 3 * mixw + IDX_HEADS * IDX_DIM
        pqk = _proj_qk_norm(h, wl[:, :2 * mixw].astype(BF16), q_norm_a[l], k_norm_a[l], tm)
        pvi = _matmul(h, wl[:, 2 * mixw:n_main].astype(BF16), BF16, tm=tm, tn=512, name='proj_a')
        w_small = jnp.zeros((d, LANES), BF16).at[:, :IDX_DIM + IDX_HEADS].set(
            wl[:, n_main:a_cols].astype(BF16))
        small = _matmul(h, w_small, F32, tm=tm, tn=LANES, name='proj_a_idx')
        ik = small[:, :IDX_DIM].astype(BF16)
        zeros = jnp.zeros_like(ik)
        ik2 = jnp.concatenate([ik, zeros, zeros, ik], axis=1)
        iwt = small[:, IDX_DIM:IDX_DIM + IDX_HEADS].reshape(bsz, seq, IDX_HEADS).transpose(0, 2, 1)
        vt = pvi[:, :mixw].reshape(bsz, seq, mixw // HEAD_DIM, HEAD_DIM).transpose(0, 2, 3, 1)
        o_a = _dsa_attention(pqk, pvi, ik2, iwt, vt, seq, top_k)

        outs, lses = [], []
        for g, (window, dil) in enumerate(DIL_PATTERNS):
            w_g = wl[:, off_b + g * 3 * mixw:off_b + (g + 1) * 3 * mixw].astype(BF16)
            qkv = _to_classes(_matmul(h, w_g, BF16, tm=tm, tn=512, name='proj_b'), bsz, seq, dil)
            og, lg = _dilated_group(qkv, q_norm_b[l, g], k_norm_b[l, g], bsz * dil, seq // dil,
                                    band=window // dil)
            outs.append(_from_classes(og, bsz, seq, dil))
            lses.append(_from_classes(lg, bsz, seq, dil))
        o_b = _dil_merge(outs, lses)

        pc = _matmul(h, wl[:, off_c:off_g].astype(BF16), F32, tm=tm, tn=512, name='proj_c')
        o_c = _rg_lru(pc, conv_w[l], conv_b[l], w_rgate[l], b_rgate[l], w_igate[l], b_igate[l],
                      lru_lambda[l], seq)

        w_gates = wl[:, off_g:].astype(BF16)
        tn_merge = 256
        nd = d // tn_merge
        mixed = _tiled_call(
            _branch_merge_body, t, d, tm, tn_merge,
            [(h, 'row'), (o_a, 'row'), (o_b, 'row'), (o_c, 'row'),
             (w_gates, ('coloff', 0)), (w_gates, ('coloff', nd)), (w_gates, ('coloff', 2 * nd)),
             (w_branch_a[l].astype(BF16), 'col'), (w_branch_b[l].astype(BF16), 'col'),
             (w_branch_c[l].astype(BF16), 'col')],
            [(d, BF16)], name='branch_merge')
        x2 = _tiled_call(
            _mm_residual_body, t, d, tm, 512,
            [(mixed, 'row'), (w_out[l].astype(BF16), 'col'), (x2, 'tile'), (g1.reshape(bsz, 1, d), 'bvec')],
            [(d, F32)], tiles_per_batch=tpb, name='out_proj')

        h2, h2p = _norm_mod_pack(x2, norm_ffn[l], sc2, sh2, seq)
        gsel, eidx, rank, cnt = _router(h2, w_router[l], b_router[l])
        tm_combine = IDX_REC // MOE_TOP_K
        idx, tile_expert, n_used, pos_blocks = _moe_plan(eidx, rank, cnt[0, :N_EXPERTS].astype(jnp.int32),
                                                        N_EXPERTS, tm_combine)
        y = _moe_grouped(h2p, idx, tile_expert, n_used, w1_exp, w3_exp, w2_exp, l)
        x2 = _moe_combine(x2, g2, gsel, h2, w1_shared[l], w3_shared[l], w2_shared[l], y, pos_blocks, seq,
                          tm_combine)
    return x2.reshape(bsz, seq, d)
```

```python
import functools

import jax
import jax.numpy as jnp
from jax import lax
from jax.experimental import pallas as pl
from jax.experimental.pallas import tpu as pltpu

HEAD_DIM = 128
A_HEADS = 8
IDX_HEADS = 16
IDX_DIM = 64
DSA_TOPK_MAX = 256
DIL_PATTERNS = ((128, 1), (512, 4), (2048, 16))
N_DIL = 3
B_HEADS = 8
LRU_BLOCKS = 16
CONV_WIDTH = 4
LRU_C = 8.0
N_EXPERTS = 64
MOE_TOP_K = 8
ROUTED_SCALE = 2.5
N_MOD = 6
RMS_EPS = 1e-6

LANES = 128
SUBLANES = 8
VMEM_LIMIT_BYTES = 56 * 1024 * 1024

NEG_BIG = -1e30
LOG2_E = 1.4426950408889634
INT_MIN = -(2 ** 31)

BF16 = jnp.bfloat16
F32 = jnp.float32


def _params(n_axes):
    return pltpu.CompilerParams(dimension_semantics=("arbitrary",) * n_axes,
                                vmem_limit_bytes=VMEM_LIMIT_BYTES)


def _dot(a, b):
    return jnp.dot(a, b, preferred_element_type=F32)


def _dot_nt(a, b):
    return lax.dot_general(a, b, (((1,), (1,)), ((), ())), preferred_element_type=F32)


def _tiled_call(body, m, n, tm, tn, ins, outs, *, tiles_per_batch=None, name=None):
    assert m % tm == 0 and n % tn == 0, (m, n, tm, tn)
    grid = (m // tm, n // tn)
    in_specs, arrays = [], []
    for arr, kind in ins:
        arrays.append(arr)
        if kind == 'row':
            in_specs.append(pl.BlockSpec((tm, arr.shape[1]), lambda i, j: (i, 0)))
        elif isinstance(kind, tuple) and kind[0] == 'rowoff':
            _, off, width = kind
            in_specs.append(pl.BlockSpec((tm, width), lambda i, j, off=off: (i, off)))
        elif kind == 'col':
            in_specs.append(pl.BlockSpec((arr.shape[0], tn), lambda i, j: (0, j)))
        elif isinstance(kind, tuple) and kind[0] == 'coloff':
            in_specs.append(pl.BlockSpec((arr.shape[0], tn), lambda i, j, off=kind[1]: (0, j + off)))
        elif kind == 'col3':
            in_specs.append(pl.BlockSpec((None, arr.shape[1], arr.shape[2]), lambda i, j: (j, 0, 0)))
        elif kind == 'tile':
            in_specs.append(pl.BlockSpec((tm, tn), lambda i, j: (i, j)))
        elif isinstance(kind, tuple) and kind[0] == 'tileoff':
            in_specs.append(pl.BlockSpec((tm, tn), lambda i, j, off=kind[1]: (i, j + off)))
        elif kind == 'vec':
            in_specs.append(pl.BlockSpec((1, tn), lambda i, j: (0, j)))
        elif kind == 'bvec':
            tpb = tiles_per_batch
            in_specs.append(pl.BlockSpec((None, 1, tn), lambda i, j, tpb=tpb: (i // tpb, 0, j)))
        elif kind == 'full':
            nd = arr.ndim
            in_specs.append(pl.BlockSpec(arr.shape, lambda i, j, nd=nd: (0,) * nd))
        else:
            raise ValueError(kind)
    out_shape, out_specs = [], []
    for n_cols, dtype in outs:
        assert (n_cols * tn) % n == 0
        w = n_cols * tn // n
        out_shape.append(jax.ShapeDtypeStruct((m, n_cols), dtype))
        out_specs.append(pl.BlockSpec((tm, w), lambda i, j: (i, j)))
    single = len(outs) == 1
    res = pl.pallas_call(
        body, grid=grid, in_specs=in_specs,
        out_specs=out_specs[0] if single else out_specs,
        out_shape=out_shape[0] if single else out_shape,
        compiler_params=_params(2), name=name)(*arrays)
    return res


def _mm_body(a_ref, w_ref, o_ref):
    o_ref[...] = _dot(a_ref[...], w_ref[...]).astype(o_ref.dtype)


def _mm_residual_body(a_ref, w_ref, x_ref, g_ref, o_ref):
    o_ref[...] = x_ref[...] + g_ref[...] * _dot(a_ref[...], w_ref[...])


def _matmul(a, w, out_dtype, *, tm, tn, body=_mm_body, name=None):
    return _tiled_call(body, a.shape[0], w.shape[1], tm, tn, [(a, 'row'), (w, 'col')],
                       [(w.shape[1], out_dtype)], name=name)


def _ada_body(c_ref, w_ref, b_ref, o_ref):
    c = c_ref[...]
    a = (c * jax.nn.sigmoid(c)).astype(BF16)
    o_ref[...] = _dot(a, w_ref[...].astype(BF16)) + b_ref[...]


def _ada_mod(c, w_ada, b_ada):
    bsz, d = c.shape
    n = w_ada.shape[1]
    c_pad = jnp.zeros((SUBLANES, d), F32).at[:bsz].set(c)
    out = _tiled_call(_ada_body, SUBLANES, n, SUBLANES, 1024,
                      [(c_pad, 'row'), (w_ada, 'col'), (b_ada.reshape(1, n), 'vec')],
                      [(n, F32)], name='ada_mod')
    return out[:bsz]


def _norm_mod_body(x_ref, g_ref, sc_ref, sh_ref, o_ref):
    x = x_ref[...]
    y = x * lax.rsqrt(jnp.mean(x * x, axis=-1, keepdims=True) + RMS_EPS) * g_ref[...]
    o_ref[...] = (y * (1.0 + sc_ref[...]) + sh_ref[...]).astype(o_ref.dtype)


def _norm_mod(x2, g, sc, sh, seq, tm=256):
    t, d = x2.shape
    bsz = t // seq
    return _tiled_call(_norm_mod_body, t, d, tm, d,
                       [(x2, 'tile'), (g.reshape(1, d), 'vec'), (sc.reshape(bsz, 1, d), 'bvec'),
                        (sh.reshape(bsz, 1, d), 'bvec')],
                       [(d, BF16)], tiles_per_batch=seq // tm, name='norm_mod')


def _head_norm(x, g):
    outs = []
    for h in range(x.shape[1] // HEAD_DIM):
        xh = x[:, h * HEAD_DIM:(h + 1) * HEAD_DIM].astype(F32)
        outs.append(xh * lax.rsqrt(jnp.mean(xh * xh, axis=-1, keepdims=True) + RMS_EPS) * g)
    return jnp.concatenate(outs, axis=1)


def _mm_qk_norm_body(a_ref, w_ref, gq_ref, gk_ref, o_ref, *, q_tiles):
    g = jnp.where(pl.program_id(1) < q_tiles, gq_ref[...], gk_ref[...])
    o_ref[...] = _head_norm(_dot(a_ref[...], w_ref[...]), g).astype(o_ref.dtype)


def _proj_qk_norm(h, w_qk, gq, gk, tm, tn=512):
    n = w_qk.shape[1]
    body = functools.partial(_mm_qk_norm_body, q_tiles=n // 2 // tn)
    return _tiled_call(body, h.shape[0], n, tm, tn,
                       [(h, 'row'), (w_qk, 'col'), (gq.reshape(1, HEAD_DIM), 'full'),
                        (gk.reshape(1, HEAD_DIM), 'full')], [(n, BF16)], name='proj_qk_norm')


def _dsa_body(qb_ref, kb_ref, q_ref, iq_ref, iwt_ref, ik_ref, k_ref, vt_ref, o_ref,
              keys_ref, thr_ref, bias_ref, s_ref, p_ref, m_ref, l_ref, acc_ref, *, tq, tk, top_k, n_heads):
    qb = qb_ref[pl.program_id(1)]
    kb = kb_ref[pl.program_id(1)]
    kb_last = ((qb + 1) * tq - 1) // tk
    n_chunks = kb_last + 1
    kpos = lax.broadcasted_iota(jnp.int32, (tk, tq), 0)
    qpos = lax.broadcasted_iota(jnp.int32, (tk, tq), 1) + qb * tq
    n_acc = 4 * SUBLANES

    @pl.when(kb == 0)
    def _scores_and_threshold():
        w = iwt_ref[...] * (IDX_HEADS ** -0.5 * IDX_DIM ** -0.5)

        def score_chunk(c, carry):
            ik2 = ik_ref[pl.ds(pl.multiple_of(c * tk, tk), tk), :]
            acc = jnp.zeros((tk, tq), F32)
            for p in range(IDX_HEADS // 2):
                iq_pair = iq_ref[:, p * LANES:(p + 1) * LANES]
                for half in range(2):
                    rel = _dot_nt(ik2[:, half * LANES:(half + 1) * LANES], iq_pair)
                    h = 2 * p + half
                    acc = acc + w[h:h + 1, :] * jnp.maximum(rel, 0.0)
            bits = pltpu.bitcast(acc, jnp.int32)
            key = jnp.where(bits < 0, bits ^ 0x7FFFFFFF, bits)
            key = jnp.where(kpos + c * tk <= qpos, key, INT_MIN)
            keys_ref[c] = key
            return carry

        lax.fori_loop(0, n_chunks, score_chunk, 0)

        def count_ge(cand):
            def body(c, cnt):
                for j in range(tk // n_acc):
                    blk = keys_ref[c, j * n_acc:(j + 1) * n_acc, :]
                    cnt = cnt + jnp.where(blk >= cand, 1.0, 0.0)
                return cnt
            cnt = lax.fori_loop(0, n_chunks, body, jnp.zeros((n_acc, tq), F32))
            return jnp.sum(cnt, axis=0, keepdims=True)

        k_f = float(top_k)
        t0 = jnp.where(count_ge(jnp.zeros((1, tq), jnp.int32)) >= k_f, 0, INT_MIN).astype(jnp.int32)

        def bit_body(i, t):
            cand = t | lax.shift_left(jnp.int32(1), 30 - i)
            return jnp.where(count_ge(cand) >= k_f, cand, t)

        t = lax.fori_loop(0, 31, bit_body, t0)
        thr_ref[...] = jnp.broadcast_to(t, thr_ref.shape)
        m_ref[...] = jnp.full(m_ref.shape, NEG_BIG, F32)
        l_ref[...] = jnp.zeros(l_ref.shape, F32)
        acc_ref[...] = jnp.zeros(acc_ref.shape, F32)

    @pl.when(kb <= kb_last)
    def _attend():
        sel = (keys_ref[kb] >= thr_ref[0:1, :]) & (kpos + kb * tk <= qpos)
        bias_ref[...] = jnp.where(sel, 0.0, NEG_BIG)
        c = HEAD_DIM ** -0.5 * LOG2_E

        def raw_scores(h):
            sl = slice(h * HEAD_DIM, (h + 1) * HEAD_DIM)
            s_ref[h % 4] = _dot_nt(k_ref[:, sl], q_ref[:, sl])

        def softmax(h):
            s = s_ref[h % 4] + bias_ref[...]
            m_old = m_ref[h, 0:1, :]
            m_new = jnp.maximum(m_old, jnp.max(s, axis=0, keepdims=True))
            p = jnp.exp2((s - m_new) * c)
            alpha = jnp.exp2((m_old - m_new) * c)
            l_new = alpha * l_ref[h, 0:1, :] + jnp.sum(p, axis=0, keepdims=True)
            p_ref[h % 3] = p.astype(BF16)
            m_ref[h] = jnp.broadcast_to(m_new, (SUBLANES, tq))
            l_ref[h] = jnp.broadcast_to(l_new, (SUBLANES, tq))
            return alpha

        def accumulate(h, alpha):
            acc_ref[h] = alpha * acc_ref[h] + _dot(vt_ref[h], p_ref[h % 3])

        s_ahead, p_ahead = 3, 2
        for h in range(s_ahead):
            raw_scores(h)
        alphas = {h: softmax(h) for h in range(p_ahead)}
        for h in range(n_heads):
            if h + s_ahead < n_heads:
                raw_scores(h + s_ahead)
            if h + p_ahead < n_heads:
                alphas[h + p_ahead] = softmax(h + p_ahead)
            accumulate(h, alphas.pop(h))

    @pl.when(kb == kb_last)
    def _finish():
        for h in range(n_heads):
            sl = slice(h * HEAD_DIM, (h + 1) * HEAD_DIM)
            o_ref[:, sl] = (acc_ref[h] / l_ref[h, 0:1, :]).T.astype(o_ref.dtype)


def _dsa_attention(pqk, pvi, ik2, iwt, vt, seq, top_k, *, tq=256, tk=512):
    t = pqk.shape[0]
    width = pqk.shape[1] // 2
    bsz = t // seq
    nq, nk = seq // tq, seq // tk
    n_heads = width // HEAD_DIM
    assert width == IDX_HEADS * IDX_DIM and pvi.shape[1] == 2 * width

    pairs = [(qb, kb) for qb in range(nq) for kb in range(((qb + 1) * tq - 1) // tk + 1)]
    qb_tab = jnp.asarray([p[0] for p in pairs], jnp.int32)
    kb_tab = jnp.asarray([p[1] for p in pairs], jnp.int32)

    body = functools.partial(_dsa_body, tq=tq, tk=tk, top_k=top_k, n_heads=n_heads)
    grid_spec = pltpu.PrefetchScalarGridSpec(
        num_scalar_prefetch=2, grid=(bsz, len(pairs)),
        in_specs=[
            pl.BlockSpec((tq, width), lambda b, s, qt, kt: (b * nq + qt[s], 0)),
            pl.BlockSpec((tq, width), lambda b, s, qt, kt: (b * nq + qt[s], 1)),
            pl.BlockSpec((None, IDX_HEADS, tq), lambda b, s, qt, kt: (b, 0, qt[s])),
            pl.BlockSpec((seq, 2 * LANES), lambda b, s, qt, kt: (b, 0)),
            pl.BlockSpec((tk, width), lambda b, s, qt, kt: (b * nk + kt[s], 1)),
            pl.BlockSpec((None, n_heads, HEAD_DIM, tk), lambda b, s, qt, kt: (b, 0, 0, kt[s])),
        ],
        out_specs=pl.BlockSpec((tq, width), lambda b, s, qt, kt: (b * nq + qt[s], 0)),
        scratch_shapes=[
            pltpu.VMEM((nk, tk, tq), jnp.int32),
            pltpu.VMEM((SUBLANES, tq), jnp.int32),
            pltpu.VMEM((tk, tq), F32),
            pltpu.VMEM((4, tk, tq), F32),
            pltpu.VMEM((3, tk, tq), BF16),
            pltpu.VMEM((n_heads, SUBLANES, tq), F32),
            pltpu.VMEM((n_heads, SUBLANES, tq), F32),
            pltpu.VMEM((n_heads, HEAD_DIM, tq), F32),
        ])
    return pl.pallas_call(
        body, grid_spec=grid_spec, out_shape=jax.ShapeDtypeStruct((t, width), BF16),
        compiler_params=_params(2), name='dsa_attention',
    )(qb_tab, kb_tab, pqk, pvi, iwt, ik2, pqk, vt)


def _dil_body(q_ref, kc_ref, vc_ref, kp_ref, vp_ref, gq_ref, gk_ref, o_ref, lse_ref,
              kcat_ref, vcat_ref, *, band, rows, n_heads):
    n = pl.program_id(1)
    kcat_ref[0:band, :] = _head_norm(kp_ref[...], gk_ref[...]).astype(BF16)
    kcat_ref[band:band + rows, :] = _head_norm(kc_ref[...], gk_ref[...]).astype(BF16)
    vcat_ref[0:band, :] = vp_ref[...]
    vcat_ref[band:band + rows, :] = vc_ref[...]
    qi = lax.broadcasted_iota(jnp.int32, (band, 2 * band), 0)
    ki = lax.broadcasted_iota(jnp.int32, (band, 2 * band), 1)
    back = qi + band - ki
    in_band = (back >= 0) & (back <= band)
    scale = HEAD_DIM ** -0.5
    gq = gq_ref[...]

    def sub_block(j, carry):
        r0 = pl.multiple_of(j * band, band)
        bias = jnp.where(in_band & ((ki >= band) | (n * (rows // band) + j > 0)), 0.0, NEG_BIG)
        heads = [slice(h * HEAD_DIM, (h + 1) * HEAD_DIM) for h in range(n_heads)]

        def scores(sl):
            qh = q_ref[pl.ds(r0, band), sl].astype(F32)
            qh = (qh * lax.rsqrt(jnp.mean(qh * qh, axis=-1, keepdims=True) + RMS_EPS) * gq).astype(BF16)
            return _dot_nt(qh, kcat_ref[pl.ds(r0, 2 * band), sl]) * scale + bias

        s_all = [scores(sl) for sl in heads]
        m_all = [jnp.max(s, axis=1, keepdims=True) for s in s_all]
        p_all = [jnp.exp(s - m) for s, m in zip(s_all, m_all)]
        l_all = [jnp.sum(p, axis=1, keepdims=True) for p in p_all]
        for h, sl in enumerate(heads):
            o = _dot(p_all[h].astype(BF16), vcat_ref[pl.ds(r0, 2 * band), sl]) / l_all[h]
            o_ref[pl.ds(r0, band), sl] = o.astype(o_ref.dtype)
            lse_ref[pl.ds(r0, band), h:h + 1] = m_all[h] + jnp.log(l_all[h])
        return carry

    lax.fori_loop(0, rows // band, sub_block, 0)


def _dilated_group(qkv, gq, gk, n_seq, len_seq, *, band=128, rows=512):
    t, w3 = qkv.shape
    width = w3 // 3
    n_heads = width // HEAD_DIM
    rows = min(rows, len_seq)
    nblk = len_seq // rows
    sub = rows // band

    def cur(c):
        return lambda s, n: (s * nblk + n, c)

    def prev(c):
        return lambda s, n: (s * nblk * sub + jnp.maximum(n * sub - 1, 0), c)

    body = functools.partial(_dil_body, band=band, rows=rows, n_heads=n_heads)
    return pl.pallas_call(
        body, grid=(n_seq, nblk),
        in_specs=[
            pl.BlockSpec((rows, width), cur(0)), pl.BlockSpec((rows, width), cur(1)),
            pl.BlockSpec((rows, width), cur(2)),
            pl.BlockSpec((band, width), prev(1)), pl.BlockSpec((band, width), prev(2)),
            pl.BlockSpec((1, HEAD_DIM), lambda s, n: (0, 0)), pl.BlockSpec((1, HEAD_DIM), lambda s, n: (0, 0)),
        ],
        out_specs=[pl.BlockSpec((rows, width), lambda s, n: (s * nblk + n, 0)),
                   pl.BlockSpec((rows, n_heads), lambda s, n: (s * nblk + n, 0))],
        out_shape=[jax.ShapeDtypeStruct((t, width), BF16), jax.ShapeDtypeStruct((t, n_heads), F32)],
        scratch_shapes=[pltpu.VMEM((band + rows, width), BF16), pltpu.VMEM((band + rows, width), BF16)],
        compiler_params=_params(2), name='dilated_attention',
    )(qkv, qkv, qkv, qkv, qkv, gq.reshape(1, HEAD_DIM), gk.reshape(1, HEAD_DIM))


def _dil_merge_body(o0_ref, o1_ref, o2_ref, l0_ref, l1_ref, l2_ref, out_ref):
    l0, l1, l2 = l0_ref[...], l1_ref[...], l2_ref[...]
    m = jnp.maximum(jnp.maximum(l0, l1), l2)
    e0, e1, e2 = jnp.exp(l0 - m), jnp.exp(l1 - m), jnp.exp(l2 - m)
    den = e0 + e1 + e2
    a0, a1, a2 = e0 / den, e1 / den, e2 / den
    for h in range(l0.shape[1]):
        sl = slice(h * HEAD_DIM, (h + 1) * HEAD_DIM)
        out_ref[:, sl] = (a0[:, h:h + 1] * o0_ref[:, sl].astype(F32)
                          + a1[:, h:h + 1] * o1_ref[:, sl].astype(F32)
                          + a2[:, h:h + 1] * o2_ref[:, sl].astype(F32)).astype(out_ref.dtype)


def _dil_merge(outs, lses, tm=512):
    t, width = outs[0].shape
    nh = lses[0].shape[1]
    ospec = pl.BlockSpec((tm, width), lambda i: (i, 0))
    lspec = pl.BlockSpec((tm, nh), lambda i: (i, 0))
    return pl.pallas_call(
        _dil_merge_body, grid=(t // tm,), in_specs=[ospec] * 3 + [lspec] * 3, out_specs=ospec,
        out_shape=jax.ShapeDtypeStruct((t, width), BF16), compiler_params=_params(1), name='dilated_merge',
    )(*outs, *lses)


def _to_classes(a, bsz, seq, dil):
    if dil == 1:
        return a
    c = a.shape[1]
    return a.reshape(bsz, seq // dil, dil, c).transpose(0, 2, 1, 3).reshape(bsz * seq, c)


def _from_classes(a, bsz, seq, dil):
    if dil == 1:
        return a
    c = a.shape[1]
    return a.reshape(bsz, dil, seq // dil, c).transpose(0, 2, 1, 3).reshape(bsz * seq, c)


def _lru_body(x_ref, y_ref, xp_ref, cw_ref, cb_ref, wr_ref, br_ref, wi_ref, bi_ref, lam_ref, o_ref,
              carry_ref, *, tm, tiles_per_batch):
    i = pl.program_id(0)
    first = (i % tiles_per_batch) == 0
    x = x_ref[...]
    prev = jnp.where(first, 0.0, xp_ref[...])
    xs = jnp.concatenate([prev, x], axis=0)
    xc = cb_ref[...] + cw_ref[CONV_WIDTH - 1:CONV_WIDTH, :] * x
    for j in range(CONV_WIDTH - 1):
        back = CONV_WIDTH - 1 - j
        xc = xc + cw_ref[j:j + 1, :] * xs[SUBLANES - back:SUBLANES - back + tm, :]
    xcb = xc.astype(BF16)
    def gate(w_ref, b_ref):
        n_chunks, cw, _ = w_ref.shape
        pre = [_dot(xcb[:, c * cw:(c + 1) * cw], w_ref[c]) for c in range(n_chunks)]
        return jax.nn.sigmoid(jnp.concatenate(pre, axis=1) + b_ref[...])

    r = gate(wr_ref, br_ref)
    ig = gate(wi_ref, bi_ref)
    lam = lam_ref[...]
    softplus_neg = jnp.maximum(-lam, 0.0) + jnp.log1p(jnp.exp(-jnp.abs(lam)))
    log_a = (-LRU_C * softplus_neg) * r
    a = jnp.exp(log_a)
    b = jnp.sqrt(1.0 - a * a) * (ig * xc)
    rows = lax.broadcasted_iota(jnp.int32, a.shape, 0)
    step = 1
    while step < tm:
        a_sh = pltpu.roll(a, step, 0)
        b_sh = pltpu.roll(b, step, 0)
        valid = rows >= step
        b = jnp.where(valid, a * b_sh + b, b)
        a = jnp.where(valid, a * a_sh, a)
        step *= 2
    h0 = jnp.where(first, 0.0, carry_ref[0:1, :])
    h = a * h0 + b
    carry_ref[...] = jnp.broadcast_to(h[tm - 1:tm, :], carry_ref.shape)
    o_ref[...] = (h * jax.nn.gelu(y_ref[...])).astype(o_ref.dtype)


MXU_WIDTH = 256


def _block_diag_chunks(w):
    g, n, _ = w.shape
    per = MXU_WIDTH // n
    wc = w.reshape(g // per, per, n, n)
    eye = jnp.eye(per, dtype=w.dtype)
    return (eye[None, :, None, :, None] * wc[:, :, :, None, :]).reshape(g // per, per * n, per * n)


def _rg_lru(pc, conv_w, conv_b, w_r, b_r, w_i, b_i, lam, seq, tm=256):
    t, w2 = pc.shape
    w = w2 // 2
    row = lambda v: v.reshape(1, w)
    full2 = lambda shape: pl.BlockSpec(shape, lambda i: (0, 0))
    chunks = pl.BlockSpec((w // MXU_WIDTH, MXU_WIDTH, MXU_WIDTH), lambda i: (0, 0, 0))
    body = functools.partial(_lru_body, tm=tm, tiles_per_batch=seq // tm)
    return pl.pallas_call(
        body, grid=(t // tm,),
        in_specs=[
            pl.BlockSpec((tm, w), lambda i: (i, 0)),
            pl.BlockSpec((tm, w), lambda i: (i, 1)),
            pl.BlockSpec((SUBLANES, w), lambda i: (jnp.maximum(i * (tm // SUBLANES) - 1, 0), 0)),
            full2((CONV_WIDTH, w)), full2((1, w)), chunks, full2((1, w)), chunks,
            full2((1, w)), full2((1, w)),
        ],
        out_specs=pl.BlockSpec((tm, w), lambda i: (i, 0)),
        out_shape=jax.ShapeDtypeStruct((t, w), BF16),
        scratch_shapes=[pltpu.VMEM((SUBLANES, w), F32)],
        compiler_params=_params(1), name='rg_lru',
    )(pc, pc, pc, conv_w, row(conv_b), _block_diag_chunks(w_r).astype(BF16), row(b_r),
      _block_diag_chunks(w_i).astype(BF16), row(b_i), row(lam))


def _branch_merge_body(h_ref, oa_ref, ob_ref, oc_ref, wga_ref, wgb_ref, wgc_ref, wa_ref, wb_ref, wc_ref, o_ref):
    h = h_ref[...]
    mixed = (jax.nn.sigmoid(_dot(h, wga_ref[...])) * _dot(oa_ref[...], wa_ref[...])
             + jax.nn.sigmoid(_dot(h, wgb_ref[...])) * _dot(ob_ref[...], wb_ref[...])
             + jax.nn.sigmoid(_dot(h, wgc_ref[...])) * _dot(oc_ref[...], wc_ref[...]))
    o_ref[...] = mixed.astype(o_ref.dtype)


MOE_ROWS = 256
IDX_REC = 1024


def _router_body(a_ref, w_ref, b_ref, gsel_ref, eidx_ref, rank_ref, cnt_ref, carry_ref, *, n_experts, top_k, tm):
    i = pl.program_id(0)

    @pl.when(i == 0)
    def _():
        carry_ref[...] = jnp.zeros(carry_ref.shape, F32)

    logits = _dot(a_ref[...], w_ref[...])
    lane = lax.broadcasted_iota(jnp.int32, logits.shape, 1).astype(F32)
    scores = jax.nn.sigmoid(logits)
    cur = jnp.where(lane < n_experts, scores + b_ref[...], -jnp.inf)
    mask = jnp.zeros(logits.shape, F32)
    hits, picked_scores = [], []
    for r in range(top_k):
        m = jnp.max(cur, axis=1, keepdims=True)
        idx = jnp.min(jnp.where(cur == m, lane, float(LANES)), axis=1, keepdims=True)
        hit = lane == idx
        hits.append(hit)
        picked_scores.append(jnp.sum(jnp.where(hit, scores, 0.0), axis=1, keepdims=True))
        eidx_ref[:, r:r + 1] = idx.astype(jnp.int32)
        cur = jnp.where(hit, -jnp.inf, cur)
        mask = mask + jnp.where(hit, 1.0, 0.0)
    total = picked_scores[0]
    for r in range(1, top_k):
        total = total + picked_scores[r]
    ri = lax.broadcasted_iota(jnp.int32, (tm, tm), 0)
    ci = lax.broadcasted_iota(jnp.int32, (tm, tm), 1)
    tri = jnp.where(ri > ci, 1.0, 0.0).astype(BF16)
    before = _dot(tri, mask.astype(BF16)) + carry_ref[0:1, :]
    for r in range(top_k):
        gsel_ref[:, r:r + 1] = picked_scores[r] / total * ROUTED_SCALE
        rank_ref[:, r:r + 1] = jnp.sum(jnp.where(hits[r], before, 0.0), axis=1, keepdims=True).astype(jnp.int32)
    carry_ref[...] = carry_ref[...] + jnp.sum(mask, axis=0, keepdims=True)
    cnt_ref[...] = carry_ref[...]


def _router(h2, w_router, b_router, tm=1024):
    t, d = h2.shape
    n_exp = w_router.shape[1]
    w_r = jnp.zeros((d, LANES), BF16).at[:, :n_exp].set(w_router.astype(BF16))
    b_r = jnp.zeros((1, LANES), F32).at[0, :n_exp].set(b_router.astype(F32))
    col = lambda dt: jax.ShapeDtypeStruct((t, MOE_TOP_K), dt)
    cspec = pl.BlockSpec((tm, MOE_TOP_K), lambda i: (i, 0))
    return pl.pallas_call(
        functools.partial(_router_body, n_experts=n_exp, top_k=MOE_TOP_K, tm=tm), grid=(t // tm,),
        in_specs=[pl.BlockSpec((tm, d), lambda i: (i, 0)), pl.BlockSpec((d, LANES), lambda i: (0, 0)),
                  pl.BlockSpec((1, LANES), lambda i: (0, 0))],
        out_specs=[cspec, cspec, cspec, pl.BlockSpec((SUBLANES, LANES), lambda i: (0, 0))],
        out_shape=[col(F32), col(jnp.int32), col(jnp.int32), jax.ShapeDtypeStruct((SUBLANES, LANES), F32)],
        scratch_shapes=[pltpu.VMEM((SUBLANES, LANES), F32)],
        compiler_params=_params(1), name='router')(h2, w_r, b_r)


def _pack_bf16_pairs(lo, hi):
    lo_bits = pltpu.bitcast(lo.astype(BF16).astype(F32), jnp.uint32)
    hi_bits = pltpu.bitcast(hi.astype(BF16).astype(F32), jnp.uint32)
    return (hi_bits & jnp.uint32(0xFFFF0000)) | lax.shift_right_logical(lo_bits, jnp.uint32(16))


def _unpack_lo(u):
    return pltpu.bitcast(lax.shift_left(u, jnp.uint32(16)), F32)


def _unpack_hi(u):
    return pltpu.bitcast(u & jnp.uint32(0xFFFF0000), F32)


def _norm_mod_pack_body(x_ref, g_ref, sc_ref, sh_ref, o_ref, p_ref):
    x = x_ref[...]
    y = x * lax.rsqrt(jnp.mean(x * x, axis=-1, keepdims=True) + RMS_EPS) * g_ref[...]
    y = y * (1.0 + sc_ref[...]) + sh_ref[...]
    half = y.shape[1] // 2
    o_ref[...] = y.astype(o_ref.dtype)
    p_ref[...] = _pack_bf16_pairs(y[:, :half], y[:, half:])


def _norm_mod_pack(x2, g, sc, sh, seq, tm=256):
    t, d = x2.shape
    bsz = t // seq
    row = pl.BlockSpec((tm, d), lambda i: (i, 0))
    vec = pl.BlockSpec((1, d), lambda i: (0, 0))
    bvec = pl.BlockSpec((None, 1, d), lambda i: (i // (seq // tm), 0, 0))
    return pl.pallas_call(
        _norm_mod_pack_body, grid=(t // tm,), in_specs=[row, vec, bvec, bvec],
        out_specs=[row, pl.BlockSpec((tm, d // 2), lambda i: (i, 0))],
        out_shape=[jax.ShapeDtypeStruct((t, d), BF16), jax.ShapeDtypeStruct((t, d // 2), jnp.uint32)],
        compiler_params=_params(1), name='norm_mod_pack',
    )(x2, g.reshape(1, d), sc.reshape(bsz, 1, d), sh.reshape(bsz, 1, d))


def _moe_group_body(te_ref, nu_ref, idx_hbm, h_hbm, w1_ref, w3_ref, w2_ref, o_ref,
                    idx_smem, xbuf, isem, gsem, *, tmr, n_tiles):
    i = pl.program_id(0)
    n = nu_ref[0]
    dh = xbuf.shape[-1]
    n_groups = tmr // SUBLANES
    n_up = n_groups // 2
    n_down = n_up // 2
    kc = 2 * dh // n_up

    def idx_copy(tile, rec):
        tile = jnp.minimum(tile, n_tiles - 1)
        return pltpu.make_async_copy(idx_hbm.at[pl.ds(pl.multiple_of(tile * IDX_REC, IDX_REC), IDX_REC)],
                                     idx_smem.at[pl.ds(pl.multiple_of(rec * IDX_REC, IDX_REC), IDX_REC)],
                                     isem.at[rec])

    def row_in(tok, g, u, s):
        return pltpu.make_async_copy(h_hbm.at[lax.shift_right_logical(tok, 3), pl.ds(tok & (SUBLANES - 1), 1)],
                                     xbuf.at[s, g, pl.ds(u, 1)], gsem.at[s])

    def gather_group(g, rec, s):
        for u in range(SUBLANES):
            row_in(idx_smem[rec * IDX_REC + g * SUBLANES + u], g, u, s).start(priority=u % 2)

    def per_group(fn):
        def body(g, c):
            fn(g)
            return c
        lax.fori_loop(0, n_groups, body, 0)

    def gather_wait(s):
        per_group(lambda g: [row_in(jnp.int32(0), g, u, s).wait() for u in range(SUBLANES)])

    @pl.when(i == 0)
    def _prologue():
        for k in range(2):
            idx_copy(k, k).start()
            idx_copy(k, k).wait()
        idx_copy(2, 2).start()
        per_group(lambda g: gather_group(g, 0, 0))
        per_group(lambda g: gather_group(g, 1, 1))

    @pl.when(i < n)
    def _tile():
        s, s_in, r_in = i % 3, (i + 2) % 3, (i + 2) % 3
        idx_copy(i + 2, r_in).wait()
        gather_wait(s)
        h1 = jnp.zeros((tmr, w1_ref.shape[-1]), F32)
        h3 = jnp.zeros((tmr, w1_ref.shape[-1]), F32)
        for c in range(n_up):
            gather_group(2 * c, r_in, s_in)
            gather_group(2 * c + 1, r_in, s_in)
            pc = c % (n_up // 2)
            u = xbuf[s, :, :, pc * kc:(pc + 1) * kc].reshape(tmr, kc)
            xc = (_unpack_lo(u) if c < n_up // 2 else _unpack_hi(u)).astype(BF16)
            h1 = h1 + _dot(xc, w1_ref[c * kc:(c + 1) * kc, :].astype(BF16))
            h3 = h3 + _dot(xc, w3_ref[c * kc:(c + 1) * kc, :].astype(BF16))
        hid = ((h1 * jax.nn.sigmoid(h1)) * h3).astype(BF16)
        for c in range(n_down):
            y_lo = _dot(hid, w2_ref[:, c * kc:(c + 1) * kc].astype(BF16))
            y_hi = _dot(hid, w2_ref[:, dh + c * kc:dh + (c + 1) * kc].astype(BF16))
            o_ref[:, c * kc:(c + 1) * kc] = _pack_bf16_pairs(y_lo, y_hi)
        idx_copy(i + 3, i % 3).start()

    @pl.when(i == n)
    def _drain():
        gather_wait(i % 3)
        gather_wait((i + 1) % 3)
        idx_copy(i + 2, (i + 2) % 3).wait()


def _moe_grouped(h2p, idx, tile_expert, n_used, w1, w3, w2, layer):
    t, dh = h2p.shape
    d = 2 * dh
    n_tiles = idx.shape[0] // IDX_REC
    tmr = MOE_ROWS
    f = w1.shape[3]
    assert n_tiles >= 4
    wmap = lambda i, te, nu: (layer, te[i], 0, 0)
    grid_spec = pltpu.PrefetchScalarGridSpec(
        num_scalar_prefetch=2, grid=(n_tiles + 1,),
        in_specs=[
            pl.BlockSpec(memory_space=pl.ANY), pl.BlockSpec(memory_space=pl.ANY),
            pl.BlockSpec((None, None, d, f), wmap), pl.BlockSpec((None, None, d, f), wmap),
            pl.BlockSpec((None, None, f, d), wmap),
        ],
        out_specs=pl.BlockSpec((tmr, dh), lambda i, te, nu: (jnp.minimum(i, nu[0] - 1), 0)),
        scratch_shapes=[
            pltpu.SMEM((3 * IDX_REC,), jnp.int32),
            pltpu.VMEM((3, tmr // SUBLANES, SUBLANES, dh), jnp.uint32),
            pltpu.SemaphoreType.DMA((3,)), pltpu.SemaphoreType.DMA((3,)),
        ])
    return pl.pallas_call(
        functools.partial(_moe_group_body, tmr=tmr, n_tiles=n_tiles), grid_spec=grid_spec,
        out_shape=jax.ShapeDtypeStruct((n_tiles * tmr, dh), jnp.uint32),
        compiler_params=_params(1), name='moe_grouped',
    )(tile_expert, n_used, idx, h2p.reshape(t // SUBLANES, SUBLANES, dh), w1, w3, w2)


def _moe_plan(eidx, rank, counts, n_experts, tm_combine):
    t, k = eidx.shape
    tmr = MOE_ROWS
    n_tiles = (t * k) // tmr + n_experts
    padded = ((counts + tmr - 1) // tmr) * tmr
    ends = jnp.cumsum(padded)
    offsets = ends - padded
    n_used = (ends[-1] // tmr).astype(jnp.int32)
    pos = offsets[eidx] + rank
    tok = jnp.broadcast_to(jnp.arange(t, dtype=jnp.int32)[:, None], (t, k))
    p = jnp.arange(n_tiles * tmr, dtype=jnp.int32)
    tok_sorted = (p % tmr).at[pos.reshape(-1)].set(tok.reshape(-1), unique_indices=True,
                                                  mode='promise_in_bounds').reshape(n_tiles, tmr)
    idx = jnp.concatenate([tok_sorted, jnp.zeros((n_tiles, IDX_REC - tmr), jnp.int32)], axis=1).reshape(-1)
    tile_start = jnp.minimum(jnp.arange(n_tiles + 1, dtype=jnp.int32), n_used - 1) * tmr
    tile_expert = jnp.sum((ends[None, :] <= tile_start[:, None]).astype(jnp.int32), axis=1)
    assert tm_combine * k == IDX_REC
    pos_blocks = pos.reshape(t // tm_combine, tm_combine, k).transpose(0, 2, 1).reshape(-1)
    return idx, jnp.minimum(tile_expert, n_experts - 1), n_used.reshape(1), pos_blocks


def _moe_combine_body(pos_hbm, y_hbm, x_ref, g_ref, gsel_ref, h_ref, w1_ref, w3_ref, w2_ref, o_ref,
                      pos_smem, ybuf, isem, gsem, *, tm, n_slots):
    i = pl.program_id(0)
    n_steps = pl.num_programs(0)
    n_groups = tm // SUBLANES
    half = x_ref.shape[1] // 2

    def pos_copy(step, rec):
        step = jnp.minimum(step, n_steps - 1)
        return pltpu.make_async_copy(pos_hbm.at[pl.ds(pl.multiple_of(step * IDX_REC, IDX_REC), IDX_REC)],
                                     pos_smem.at[pl.ds(pl.multiple_of(rec * IDX_REC, IDX_REC), IDX_REC)],
                                     isem.at[rec])

    def row_in(row, r, g, u, s):
        return pltpu.make_async_copy(y_hbm.at[lax.shift_right_logical(row, 3), pl.ds(row & (SUBLANES - 1), 1)],
                                     ybuf.at[s, r, g, pl.ds(u, 1)], gsem.at[s])

    def per_group(fn):
        def body(g, c):
            fn(g)
            return c
        lax.fori_loop(0, n_groups, body, 0)

    def gather_group(g, rec, s):
        for r in range(n_slots):
            for u in range(SUBLANES):
                row_in(pos_smem[rec * IDX_REC + r * tm + g * SUBLANES + u], r, g, u, s).start(priority=u % 2)

    def gather_wait(s):
        def group(g):
            for r in range(n_slots):
                for u in range(SUBLANES):
                    row_in(jnp.int32(0), r, g, u, s).wait()
        per_group(group)

    @pl.when(i == 0)
    def _prologue():
        pos_copy(0, 0).start()
        pos_copy(0, 0).wait()
        per_group(lambda g: gather_group(g, 0, 0))
        pos_copy(1, 1).start()

    nxt, rec_next = (i + 1) % 2, (i + 1) % 2
    pos_copy(i + 1, rec_next).wait()
    groups = iter(range(n_groups))

    def issue(count):
        for _ in range(count):
            g = next(groups, None)
            if g is not None:
                gather_group(g, rec_next, nxt)

    issue(2)
    a = h_ref[...]
    h1 = _dot(a, w1_ref[...])
    issue(2)
    hid = ((h1 * jax.nn.sigmoid(h1)) * _dot(a, w3_ref[...])).astype(BF16)
    issue(2)
    acc_lo = _dot(hid, w2_ref[:, :half])
    issue(2)
    acc_hi = _dot(hid, w2_ref[:, half:])
    gather_wait(i % 2)
    pos_copy(i + 2, i % 2).start()
    gs = gsel_ref[...]
    for r in range(n_slots):
        issue(1)
        u = ybuf[i % 2, r].reshape(tm, half)
        acc_lo = acc_lo + gs[:, r:r + 1] * _unpack_lo(u)
        acc_hi = acc_hi + gs[:, r:r + 1] * _unpack_hi(u)
    issue(n_groups)
    o_ref[:, :half] = x_ref[:, :half] + g_ref[:, :half] * acc_lo
    o_ref[:, half:] = x_ref[:, half:] + g_ref[:, half:] * acc_hi

    @pl.when(i == n_steps - 1)
    def _drain():
        gather_wait(nxt)
        pos_copy(i + 2, i % 2).wait()


def _moe_combine(x2, g2, gsel, h2, ws1, ws3, ws2, y, pos_blocks, seq, tm):
    t, d = x2.shape
    bsz = t // seq
    k = gsel.shape[1]
    f = ws1.shape[1]
    dh = d // 2
    row = pl.BlockSpec((tm, d), lambda i: (i, 0))
    full = lambda shape: pl.BlockSpec(shape, lambda i: (0, 0))
    anyspace = pl.BlockSpec(memory_space=pl.ANY)
    return pl.pallas_call(
        functools.partial(_moe_combine_body, tm=tm, n_slots=k), grid=(t // tm,),
        in_specs=[anyspace, anyspace, row, pl.BlockSpec((None, 1, d), lambda i: (i // (seq // tm), 0, 0)),
                  pl.BlockSpec((tm, k), lambda i: (i, 0)), row, full((d, f)), full((d, f)), full((f, d))],
        out_specs=row, out_shape=jax.ShapeDtypeStruct((t, d), F32),
        scratch_shapes=[
            pltpu.SMEM((2 * IDX_REC,), jnp.int32),
            pltpu.VMEM((2, k, tm // SUBLANES, SUBLANES, dh), jnp.uint32),
            pltpu.SemaphoreType.DMA((2,)), pltpu.SemaphoreType.DMA((2,)),
        ],
        compiler_params=_params(1), name='moe_combine',
    )(pos_blocks, y.reshape(y.shape[0] // SUBLANES, SUBLANES, dh), x2, g2.reshape(bsz, 1, d), gsel, h2,
      ws1.astype(BF16), ws3.astype(BF16), ws2.astype(BF16))


def kernel(x, c, w_ada, b_ada, ada_layer, norm_mix, norm_ffn, w_in, q_norm_a, k_norm_a, q_norm_b,
           k_norm_b, conv_w, conv_b, w_rgate, b_rgate, w_igate, b_igate, lru_lambda, w_branch_a,
           w_branch_b, w_branch_c, w_out, w_router, b_router, w1_exp, w3_exp, w2_exp, w1_shared,
           w3_shared, w2_shared):
    bsz, seq, d = x.shape
    t = bsz * seq
    depth = w_in.shape[0]
    mixw = d // 4
    a_cols = 3 * mixw + IDX_HEADS * IDX_DIM + IDX_DIM + IDX_HEADS
    off_b = a_cols
    off_c = off_b + N_DIL * 3 * mixw
    off_g = off_c + 2 * mixw
    top_k = min(DSA_TOPK_MAX, seq // 4)
    tm = 1024 if t % 1024 == 0 else 256
    tpb = seq // tm

    mod_shared = _ada_mod(c, w_ada, b_ada).reshape(bsz, N_MOD, d)
    x2 = x.reshape(t, d)
    for l in range(depth):
        mod = mod_shared + ada_layer[l]
        sh1, sc1, g1, sh2, sc2, g2 = [mod[:, j] for j in range(N_MOD)]
        h = _norm_mod(x2, norm_mix[l], sc1, sh1, seq)
        wl = w_in[l]

        n_main = 3 * mixw + IDX_HEADS * IDX_DIM
        pqk = _proj_qk_norm(h, wl[:, :2 * mixw].astype(BF16), q_norm_a[l], k_norm_a[l], tm)
        pvi = _matmul(h, wl[:, 2 * mixw:n_main].astype(BF16), BF16, tm=tm, tn=512, name='proj_a')
        w_small = jnp.zeros((d, LANES), BF16).at[:, :IDX_DIM + IDX_HEADS].set(
            wl[:, n_main:a_cols].astype(BF16))
        small = _matmul(h, w_small, F32, tm=tm, tn=LANES, name='proj_a_idx')
        ik = small[:, :IDX_DIM].astype(BF16)
        zeros = jnp.zeros_like(ik)
        ik2 = jnp.concatenate([ik, zeros, zeros, ik], axis=1)
        iwt = small[:, IDX_DIM:IDX_DIM + IDX_HEADS].reshape(bsz, seq, IDX_HEADS).transpose(0, 2, 1)
        vt = pvi[:, :mixw].reshape(bsz, seq, mixw // HEAD_DIM, HEAD_DIM).transpose(0, 2, 3, 1)
        o_a = _dsa_attention(pqk, pvi, ik2, iwt, vt, seq, top_k)

        outs, lses = [], []
        for g, (window, dil) in enumerate(DIL_PATTERNS):
            w_g = wl[:, off_b + g * 3 * mixw:off_b + (g + 1) * 3 * mixw].astype(BF16)
            qkv = _to_classes(_matmul(h, w_g, BF16, tm=tm, tn=512, name='proj_b'), bsz, seq, dil)
            og, lg = _dilated_group(qkv, q_norm_b[l, g], k_norm_b[l, g], bsz * dil, seq // dil,
                                    band=window // dil)
            outs.append(_from_classes(og, bsz, seq, dil))
            lses.append(_from_classes(lg, bsz, seq, dil))
        o_b = _dil_merge(outs, lses)

        pc = _matmul(h, wl[:, off_c:off_g].astype(BF16), F32, tm=tm, tn=512, name='proj_c')
        o_c = _rg_lru(pc, conv_w[l], conv_b[l], w_rgate[l], b_rgate[l], w_igate[l], b_igate[l],
                      lru_lambda[l], seq)

        w_gates = wl[:, off_g:].astype(BF16)
        tn_merge = 256
        nd = d // tn_merge
        mixed = _tiled_call(
            _branch_merge_body, t, d, tm, tn_merge,
            [(h, 'row'), (o_a, 'row'), (o_b, 'row'), (o_c, 'row'),
             (w_gates, ('coloff', 0)), (w_gates, ('coloff', nd)), (w_gates, ('coloff', 2 * nd)),
             (w_branch_a[l].astype(BF16), 'col'), (w_branch_b[l].astype(BF16), 'col'),
             (w_branch_c[l].astype(BF16), 'col')],
            [(d, BF16)], name='branch_merge')
        x2 = _tiled_call(
            _mm_residual_body, t, d, tm, 512,
            [(mixed, 'row'), (w_out[l].astype(BF16), 'col'), (x2, 'tile'), (g1.reshape(bsz, 1, d), 'bvec')],
            [(d, F32)], tiles_per_batch=tpb, name='out_proj')

        h2, h2p = _norm_mod_pack(x2, norm_ffn[l], sc2, sh2, seq)
        gsel, eidx, rank, cnt = _router(h2, w_router[l], b_router[l])
        tm_combine = IDX_REC // MOE_TOP_K
        idx, tile_expert, n_used, pos_blocks = _moe_plan(eidx, rank, cnt[0, :N_EXPERTS].astype(jnp.int32),
                                                        N_EXPERTS, tm_combine)
        y = _moe_grouped(h2p, idx, tile_expert, n_used, w1_exp, w3_exp, w2_exp, l)
        x2 = _moe_combine(x2, g2, gsel, h2, w1_shared[l], w3_shared[l], w2_shared[l], y, pos_blocks, seq,
                          tm_combine)
    return x2.reshape(bsz, seq, d)
```

```python
import functools

import jax
import jax.numpy as jnp
from jax import lax
from jax.experimental import pallas as pl
from jax.experimental.pallas import tpu as pltpu

HEAD_DIM = 128
A_HEADS = 8
IDX_HEADS = 16
IDX_DIM = 64
DSA_TOPK_MAX = 256
DIL_PATTERNS = ((128, 1), (512, 4), (2048, 16))
N_DIL = 3
B_HEADS = 8
LRU_BLOCKS = 16
CONV_WIDTH = 4
LRU_C = 8.0
N_EXPERTS = 64
MOE_TOP_K = 8
ROUTED_SCALE = 2.5
N_MOD = 6
RMS_EPS = 1e-6

LANES = 128
SUBLANES = 8
VMEM_LIMIT_BYTES = 56 * 1024 * 1024

NEG_BIG = -1e30
LOG2_E = 1.4426950408889634
INT_MIN = -(2 ** 31)

BF16 = jnp.bfloat16
F32 = jnp.float32


def _params(n_axes):
    return pltpu.CompilerParams(dimension_semantics=("arbitrary",) * n_axes,
                                vmem_limit_bytes=VMEM_LIMIT_BYTES)


def _dot(a, b):
    return jnp.dot(a, b, preferred_element_type=F32)


def _dot_nt(a, b):
    return lax.dot_general(a, b, (((1,), (1,)), ((), ())), preferred_element_type=F32)


def _tiled_call(body, m, n, tm, tn, ins, outs, *, tiles_per_batch=None, name=None):
    assert m % tm == 0 and n % tn == 0, (m, n, tm, tn)
    grid = (m // tm, n // tn)
    in_specs, arrays = [], []
    for arr, kind in ins:
        arrays.append(arr)
        if kind == 'row':
            in_specs.append(pl.BlockSpec((tm, arr.shape[1]), lambda i, j: (i, 0)))
        elif isinstance(kind, tuple) and kind[0] == 'rowoff':
            _, off, width = kind
            in_specs.append(pl.BlockSpec((tm, width), lambda i, j, off=off: (i, off)))
        elif kind == 'col':
            in_specs.append(pl.BlockSpec((arr.shape[0], tn), lambda i, j: (0, j)))
        elif isinstance(kind, tuple) and kind[0] == 'coloff':
            in_specs.append(pl.BlockSpec((arr.shape[0], tn), lambda i, j, off=kind[1]: (0, j + off)))
        elif kind == 'col3':
            in_specs.append(pl.BlockSpec((None, arr.shape[1], arr.shape[2]), lambda i, j: (j, 0, 0)))
        elif kind == 'tile':
            in_specs.append(pl.BlockSpec((tm, tn), lambda i, j: (i, j)))
        elif isinstance(kind, tuple) and kind[0] == 'tileoff':
            in_specs.append(pl.BlockSpec((tm, tn), lambda i, j, off=kind[1]: (i, j + off)))
        elif kind == 'vec':
            in_specs.append(pl.BlockSpec((1, tn), lambda i, j: (0, j)))
        elif kind == 'bvec':
            tpb = tiles_per_batch
            in_specs.append(pl.BlockSpec((None, 1, tn), lambda i, j, tpb=tpb: (i // tpb, 0, j)))
        elif kind == 'full':
            nd = arr.ndim
            in_specs.append(pl.BlockSpec(arr.shape, lambda i, j, nd=nd: (0,) * nd))
        else:
            raise ValueError(kind)
    out_shape, out_specs = [], []
    for n_cols, dtype in outs:
        assert (n_cols * tn) % n == 0
        w = n_cols * tn // n
        out_shape.append(jax.ShapeDtypeStruct((m, n_cols), dtype))
        out_specs.append(pl.BlockSpec((tm, w), lambda i, j: (i, j)))
    single = len(outs) == 1
    res = pl.pallas_call(
        body, grid=grid, in_specs=in_specs,
        out_specs=out_specs[0] if single else out_specs,
        out_shape=out_shape[0] if single else out_shape,
        compiler_params=_params(2), name=name)(*arrays)
    return res


def _mm_body(a_ref, w_ref, o_ref):
    o_ref[...] = _dot(a_ref[...], w_ref[...]).astype(o_ref.dtype)


def _mm_residual_body(a_ref, w_ref, x_ref, g_ref, o_ref):
    o_ref[...] = x_ref[...] + g_ref[...] * _dot(a_ref[...], w_ref[...])


def _matmul(a, w, out_dtype, *, tm, tn, body=_mm_body, name=None):
    return _tiled_call(body, a.shape[0], w.shape[1], tm, tn, [(a, 'row'), (w, 'col')],
                       [(w.shape[1], out_dtype)], name=name)


def _ada_body(c_ref, w_ref, b_ref, o_ref):
    c = c_ref[...]
    a = (c * jax.nn.sigmoid(c)).astype(BF16)
    o_ref[...] = _dot(a, w_ref[...].astype(BF16)) + b_ref[...]


def _ada_mod(c, w_ada, b_ada):
    bsz, d = c.shape
    n = w_ada.shape[1]
    c_pad = jnp.zeros((SUBLANES, d), F32).at[:bsz].set(c)
    out = _tiled_call(_ada_body, SUBLANES, n, SUBLANES, 1024,
                      [(c_pad, 'row'), (w_ada, 'col'), (b_ada.reshape(1, n), 'vec')],
                      [(n, F32)], name='ada_mod')
    return out[:bsz]


def _norm_mod_body(x_ref, g_ref, sc_ref, sh_ref, o_ref):
    x = x_ref[...]
    y = x * lax.rsqrt(jnp.mean(x * x, axis=-1, keepdims=True) + RMS_EPS) * g_ref[...]
    o_ref[...] = (y * (1.0 + sc_ref[...]) + sh_ref[...]).astype(o_ref.dtype)


def _norm_mod(x2, g, sc, sh, seq, tm=256):
    t, d = x2.shape
    bsz = t // seq
    return _tiled_call(_norm_mod_body, t, d, tm, d,
                       [(x2, 'tile'), (g.reshape(1, d), 'vec'), (sc.reshape(bsz, 1, d), 'bvec'),
                        (sh.reshape(bsz, 1, d), 'bvec')],
                       [(d, BF16)], tiles_per_batch=seq // tm, name='norm_mod')


def _head_norm(x, g):
    outs = []
    for h in range(x.shape[1] // HEAD_DIM):
        xh = x[:, h * HEAD_DIM:(h + 1) * HEAD_DIM].astype(F32)
        outs.append(xh * lax.rsqrt(jnp.mean(xh * xh, axis=-1, keepdims=True) + RMS_EPS) * g)
    return jnp.concatenate(outs, axis=1)


def _mm_qk_norm_body(a_ref, w_ref, gq_ref, gk_ref, o_ref, *, q_tiles):
    g = jnp.where(pl.program_id(1) < q_tiles, gq_ref[...], gk_ref[...])
    o_ref[...] = _head_norm(_dot(a_ref[...], w_ref[...]), g).astype(o_ref.dtype)


def _proj_qk_norm(h, w_qk, gq, gk, tm, tn=512):
    n = w_qk.shape[1]
    body = functools.partial(_mm_qk_norm_body, q_tiles=n // 2 // tn)
    return _tiled_call(body, h.shape[0], n, tm, tn,
                       [(h, 'row'), (w_qk, 'col'), (gq.reshape(1, HEAD_DIM), 'full'),
                        (gk.reshape(1, HEAD_DIM), 'full')], [(n, BF16)], name='proj_qk_norm')


def _dsa_body(qb_ref, kb_ref, q_ref, iq_ref, iwt_ref, ik_ref, k_ref, vt_ref, o_ref,
              keys_ref, thr_ref, bias_ref, s_ref, p_ref, m_ref, l_ref, acc_ref, *, tq, tk, top_k, n_heads):
    qb = qb_ref[pl.program_id(1)]
    kb = kb_ref[pl.program_id(1)]
    kb_last = ((qb + 1) * tq - 1) // tk
    n_chunks = kb_last + 1
    kpos = lax.broadcasted_iota(jnp.int32, (tk, tq), 0)
    qpos = lax.broadcasted_iota(jnp.int32, (tk, tq), 1) + qb * tq
    n_acc = 4 * SUBLANES

    @pl.when(kb == 0)
    def _scores_and_threshold():
        w = iwt_ref[...] * (IDX_HEADS ** -0.5 * IDX_DIM ** -0.5)

        def score_chunk(c, carry):
            ik2 = ik_ref[pl.ds(pl.multiple_of(c * tk, tk), tk), :]
            acc = jnp.zeros((tk, tq), F32)
            for p in range(IDX_HEADS // 2):
                iq_pair = iq_ref[:, p * LANES:(p + 1) * LANES]
                for half in range(2):
                    rel = _dot_nt(ik2[:, half * LANES:(half + 1) * LANES], iq_pair)
                    h = 2 * p + half
                    acc = acc + w[h:h + 1, :] * jnp.maximum(rel, 0.0)
            bits = pltpu.bitcast(acc, jnp.int32)
            key = jnp.where(bits < 0, bits ^ 0x7FFFFFFF, bits)
            key = jnp.where(kpos + c * tk <= qpos, key, INT_MIN)
            keys_ref[c] = key
            return carry

        lax.fori_loop(0, n_chunks, score_chunk, 0)

        def count_ge(cand):
            def body(c, cnt):
                for j in range(tk // n_acc):
                    blk = keys_ref[c, j * n_acc:(j + 1) * n_acc, :]
                    cnt = cnt + jnp.where(blk >= cand, 1.0, 0.0)
                return cnt
            cnt = lax.fori_loop(0, n_chunks, body, jnp.zeros((n_acc, tq), F32))
            return jnp.sum(cnt, axis=0, keepdims=True)

        k_f = float(top_k)
        t0 = jnp.where(count_ge(jnp.zeros((1, tq), jnp.int32)) >= k_f, 0, INT_MIN).astype(jnp.int32)

        def bit_body(i, t):
            cand = t | lax.shift_left(jnp.int32(1), 30 - i)
            return jnp.where(count_ge(cand) >= k_f, cand, t)

        t = lax.fori_loop(0, 31, bit_body, t0)
        thr_ref[...] = jnp.broadcast_to(t, thr_ref.shape)
        m_ref[...] = jnp.full(m_ref.shape, NEG_BIG, F32)
        l_ref[...] = jnp.zeros(l_ref.shape, F32)
        acc_ref[...] = jnp.zeros(acc_ref.shape, F32)

    @pl.when(kb <= kb_last)
    def _attend():
        sel = (keys_ref[kb] >= thr_ref[0:1, :]) & (kpos + kb * tk <= qpos)
        bias_ref[...] = jnp.where(sel, 0.0, NEG_BIG)
        c = HEAD_DIM ** -0.5 * LOG2_E

        def raw_scores(h):
            sl = slice(h * HEAD_DIM, (h + 1) * HEAD_DIM)
            s_ref[h % 4] = _dot_nt(k_ref[:, sl], q_ref[:, sl])

        def softmax(h):
            s = s_ref[h % 4] + bias_ref[...]
            m_old = m_ref[h, 0:1, :]
            m_new = jnp.maximum(m_old, jnp.max(s, axis=0, keepdims=True))
            p = jnp.exp2((s - m_new) * c)
            alpha = jnp.exp2((m_old - m_new) * c)
            l_new = alpha * l_ref[h, 0:1, :] + jnp.sum(p, axis=0, keepdims=True)
            p_ref[h % 3] = p.astype(BF16)
            m_ref[h] = jnp.broadcast_to(m_new, (SUBLANES, tq))
            l_ref[h] = jnp.broadcast_to(l_new, (SUBLANES, tq))
            return alpha

        def accumulate(h, alpha):
            acc_ref[h] = alpha * acc_ref[h] + _dot(vt_ref[h], p_ref[h % 3])

        s_ahead, p_ahead = 3, 2
        for h in range(s_ahead):
            raw_scores(h)
        alphas = {h: softmax(h) for h in range(p_ahead)}
        for h in range(n_heads):
            if h + s_ahead < n_heads:
                raw_scores(h + s_ahead)
            if h + p_ahead < n_heads:
                alphas[h + p_ahead] = softmax(h + p_ahead)
            accumulate(h, alphas.pop(h))

    @pl.when(kb == kb_last)
    def _finish():
        for h in range(n_heads):
            sl = slice(h * HEAD_DIM, (h + 1) * HEAD_DIM)
            o_ref[:, sl] = (acc_ref[h] / l_ref[h, 0:1, :]).T.astype(o_ref.dtype)


def _dsa_attention(pqk, pvi, ik2, iwt, vt, seq, top_k, *, tq=256, tk=512):
    t = pqk.shape[0]
    width = pqk.shape[1] // 2
    bsz = t // seq
    nq, nk = seq // tq, seq // tk
    n_heads = width // HEAD_DIM
    assert width == IDX_HEADS * IDX_DIM and pvi.shape[1] == 2 * width

    pairs = [(qb, kb) for qb in range(nq) for kb in range(((qb + 1) * tq - 1) // tk + 1)]
    qb_tab = jnp.asarray([p[0] for p in pairs], jnp.int32)
    kb_tab = jnp.asarray([p[1] for p in pairs], jnp.int32)

    body = functools.partial(_dsa_body, tq=tq, tk=tk, top_k=top_k, n_heads=n_heads)
    grid_spec = pltpu.PrefetchScalarGridSpec(
        num_scalar_prefetch=2, grid=(bsz, len(pairs)),
        in_specs=[
            pl.BlockSpec((tq, width), lambda b, s, qt, kt: (b * nq + qt[s], 0)),
            pl.BlockSpec((tq, width), lambda b, s, qt, kt: (b * nq + qt[s], 1)),
            pl.BlockSpec((None, IDX_HEADS, tq), lambda b, s, qt, kt: (b, 0, qt[s])),
            pl.BlockSpec((seq, 2 * LANES), lambda b, s, qt, kt: (b, 0)),
            pl.BlockSpec((tk, width), lambda b, s, qt, kt: (b * nk + kt[s], 1)),
            pl.BlockSpec((None, n_heads, HEAD_DIM, tk), lambda b, s, qt, kt: (b, 0, 0, kt[s])),
        ],
        out_specs=pl.BlockSpec((tq, width), lambda b, s, qt, kt: (b * nq + qt[s], 0)),
        scratch_shapes=[
            pltpu.VMEM((nk, tk, tq), jnp.int32),
            pltpu.VMEM((SUBLANES, tq), jnp.int32),
            pltpu.VMEM((tk, tq), F32),
            pltpu.VMEM((4, tk, tq), F32),
            pltpu.VMEM((3, tk, tq), BF16),
            pltpu.VMEM((n_heads, SUBLANES, tq), F32),
            pltpu.VMEM((n_heads, SUBLANES, tq), F32),
            pltpu.VMEM((n_heads, HEAD_DIM, tq), F32),
        ])
    return pl.pallas_call(
        body, grid_spec=grid_spec, out_shape=jax.ShapeDtypeStruct((t, width), BF16),
        compiler_params=_params(2), name='dsa_attention',
    )(qb_tab, kb_tab, pqk, pvi, iwt, ik2, pqk, vt)


def _dil_body(q_ref, kc_ref, vc_ref, kp_ref, vp_ref, gq_ref, gk_ref, o_ref, lse_ref,
              kcat_ref, vcat_ref, *, band, rows, n_heads):
    n = pl.program_id(1)
    kcat_ref[0:band, :] = _head_norm(kp_ref[...], gk_ref[...]).astype(BF16)
    kcat_ref[band:band + rows, :] = _head_norm(kc_ref[...], gk_ref[...]).astype(BF16)
    vcat_ref[0:band, :] = vp_ref[...]
    vcat_ref[band:band + rows, :] = vc_ref[...]
    qi = lax.broadcasted_iota(jnp.int32, (band, 2 * band), 0)
    ki = lax.broadcasted_iota(jnp.int32, (band, 2 * band), 1)
    back = qi + band - ki
    in_band = (back >= 0) & (back <= band)
    scale = HEAD_DIM ** -0.5
    gq = gq_ref[...]

    def sub_block(j, carry):
        r0 = pl.multiple_of(j * band, band)
        bias = jnp.where(in_band & ((ki >= band) | (n * (rows // band) + j > 0)), 0.0, NEG_BIG)
        heads = [slice(h * HEAD_DIM, (h + 1) * HEAD_DIM) for h in range(n_heads)]

        def scores(sl):
            qh = q_ref[pl.ds(r0, band), sl].astype(F32)
            qh = (qh * lax.rsqrt(jnp.mean(qh * qh, axis=-1, keepdims=True) + RMS_EPS) * gq).astype(BF16)
            return _dot_nt(qh, kcat_ref[pl.ds(r0, 2 * band), sl]) * scale + bias

        s_all = [scores(sl) for sl in heads]
        m_all = [jnp.max(s, axis=1, keepdims=True) for s in s_all]
        p_all = [jnp.exp(s - m) for s, m in zip(s_all, m_all)]
        l_all = [jnp.sum(p, axis=1, keepdims=True) for p in p_all]
        for h, sl in enumerate(heads):
            o = _dot(p_all[h].astype(BF16), vcat_ref[pl.ds(r0, 2 * band), sl]) / l_all[h]
            o_ref[pl.ds(r0, band), sl] = o.astype(o_ref.dtype)
            lse_ref[pl.ds(r0, band), h:h + 1] = m_all[h] + jnp.log(l_all[h])
        return carry

    lax.fori_loop(0, rows // band, sub_block, 0)


def _dilated_group(qkv, gq, gk, n_seq, len_seq, *, band=128, rows=512):
    t, w3 = qkv.shape
    width = w3 // 3
    n_heads = width // HEAD_DIM
    rows = min(rows, len_seq)
    nblk = len_seq // rows
    sub = rows // band

    def cur(c):
        return lambda s, n: (s * nblk + n, c)

    def prev(c):
        return lambda s, n: (s * nblk * sub + jnp.maximum(n * sub - 1, 0), c)

    body = functools.partial(_dil_body, band=band, rows=rows, n_heads=n_heads)
    return pl.pallas_call(
        body, grid=(n_seq, nblk),
        in_specs=[
            pl.BlockSpec((rows, width), cur(0)), pl.BlockSpec((rows, width), cur(1)),
            pl.BlockSpec((rows, width), cur(2)),
            pl.BlockSpec((band, width), prev(1)), pl.BlockSpec((band, width), prev(2)),
            pl.BlockSpec((1, HEAD_DIM), lambda s, n: (0, 0)), pl.BlockSpec((1, HEAD_DIM), lambda s, n: (0, 0)),
        ],
        out_specs=[pl.BlockSpec((rows, width), lambda s, n: (s * nblk + n, 0)),
                   pl.BlockSpec((rows, n_heads), lambda s, n: (s * nblk + n, 0))],
        out_shape=[jax.ShapeDtypeStruct((t, width), BF16), jax.ShapeDtypeStruct((t, n_heads), F32)],
        scratch_shapes=[pltpu.VMEM((band + rows, width), BF16), pltpu.VMEM((band + rows, width), BF16)],
        compiler_params=_params(2), name='dilated_attention',
    )(qkv, qkv, qkv, qkv, qkv, gq.reshape(1, HEAD_DIM), gk.reshape(1, HEAD_DIM))


def _dil_merge_body(o0_ref, o1_ref, o2_ref, l0_ref, l1_ref, l2_ref, out_ref):
    l0, l1, l2 = l0_ref[...], l1_ref[...], l2_ref[...]
    m = jnp.maximum(jnp.maximum(l0, l1), l2)
    e0, e1, e2 = jnp.exp(l0 - m), jnp.exp(l1 - m), jnp.exp(l2 - m)
    den = e0 + e1 + e2
    a0, a1, a2 = e0 / den, e1 / den, e2 / den
    for h in range(l0.shape[1]):
        sl = slice(h * HEAD_DIM, (h + 1) * HEAD_DIM)
        out_ref[:, sl] = (a0[:, h:h + 1] * o0_ref[:, sl].astype(F32)
                          + a1[:, h:h + 1] * o1_ref[:, sl].astype(F32)
                          + a2[:, h:h + 1] * o2_ref[:, sl].astype(F32)).astype(out_ref.dtype)


def _dil_merge(outs, lses, tm=512):
    t, width = outs[0].shape
    nh = lses[0].shape[1]
    ospec = pl.BlockSpec((tm, width), lambda i: (i, 0))
    lspec = pl.BlockSpec((tm, nh), lambda i: (i, 0))
    return pl.pallas_call(
        _dil_merge_body, grid=(t // tm,), in_specs=[ospec] * 3 + [lspec] * 3, out_specs=ospec,
        out_shape=jax.ShapeDtypeStruct((t, width), BF16), compiler_params=_params(1), name='dilated_merge',
    )(*outs, *lses)


def _to_classes(a, bsz, seq, dil):
    if dil == 1:
        return a
    c = a.shape[1]
    return a.reshape(bsz, seq // dil, dil, c).transpose(0, 2, 1, 3).reshape(bsz * seq, c)


def _from_classes(a, bsz, seq, dil):
    if dil == 1:
        return a
    c = a.shape[1]
    return a.reshape(bsz, dil, seq // dil, c).transpose(0, 2, 1, 3).reshape(bsz * seq, c)


def _lru_body(x_ref, y_ref, xp_ref, cw_ref, cb_ref, wr_ref, br_ref, wi_ref, bi_ref, lam_ref, o_ref,
              carry_ref, *, tm, tiles_per_batch):
    i = pl.program_id(0)
    first = (i % tiles_per_batch) == 0
    x = x_ref[...]
    prev = jnp.where(first, 0.0, xp_ref[...])
    xs = jnp.concatenate([prev, x], axis=0)
    xc = cb_ref[...] + cw_ref[CONV_WIDTH - 1:CONV_WIDTH, :] * x
    for j in range(CONV_WIDTH - 1):
        back = CONV_WIDTH - 1 - j
        xc = xc + cw_ref[j:j + 1, :] * xs[SUBLANES - back:SUBLANES - back + tm, :]
    xcb = xc.astype(BF16)
    def gate(w_ref, b_ref):
        n_chunks, cw, _ = w_ref.shape
        pre = [_dot(xcb[:, c * cw:(c + 1) * cw], w_ref[c]) for c in range(n_chunks)]
        return jax.nn.sigmoid(jnp.concatenate(pre, axis=1) + b_ref[...])

    r = gate(wr_ref, br_ref)
    ig = gate(wi_ref, bi_ref)
    lam = lam_ref[...]
    softplus_neg = jnp.maximum(-lam, 0.0) + jnp.log1p(jnp.exp(-jnp.abs(lam)))
    log_a = (-LRU_C * softplus_neg) * r
    a = jnp.exp(log_a)
    b = jnp.sqrt(1.0 - a * a) * (ig * xc)
    rows = lax.broadcasted_iota(jnp.int32, a.shape, 0)
    step = 1
    while step < tm:
        a_sh = pltpu.roll(a, step, 0)
        b_sh = pltpu.roll(b, step, 0)
        valid = rows >= step
        b = jnp.where(valid, a * b_sh + b, b)
        a = jnp.where(valid, a * a_sh, a)
        step *= 2
    h0 = jnp.where(first, 0.0, carry_ref[0:1, :])
    h = a * h0 + b
    carry_ref[...] = jnp.broadcast_to(h[tm - 1:tm, :], carry_ref.shape)
    o_ref[...] = (h * jax.nn.gelu(y_ref[...])).astype(o_ref.dtype)


MXU_WIDTH = 256


def _block_diag_chunks(w):
    g, n, _ = w.shape
    per = MXU_WIDTH // n
    wc = w.reshape(g // per, per, n, n)
    eye = jnp.eye(per, dtype=w.dtype)
    return (eye[None, :, None, :, None] * wc[:, :, :, None, :]).reshape(g // per, per * n, per * n)


def _rg_lru(pc, conv_w, conv_b, w_r, b_r, w_i, b_i, lam, seq, tm=256):
    t, w2 = pc.shape
    w = w2 // 2
    row = lambda v: v.reshape(1, w)
    full2 = lambda shape: pl.BlockSpec(shape, lambda i: (0, 0))
    chunks = pl.BlockSpec((w // MXU_WIDTH, MXU_WIDTH, MXU_WIDTH), lambda i: (0, 0, 0))
    body = functools.partial(_lru_body, tm=tm, tiles_per_batch=seq // tm)
    return pl.pallas_call(
        body, grid=(t // tm,),
        in_specs=[
            pl.BlockSpec((tm, w), lambda i: (i, 0)),
            pl.BlockSpec((tm, w), lambda i: (i, 1)),
            pl.BlockSpec((SUBLANES, w), lambda i: (jnp.maximum(i * (tm // SUBLANES) - 1, 0), 0)),
            full2((CONV_WIDTH, w)), full2((1, w)), chunks, full2((1, w)), chunks,
            full2((1, w)), full2((1, w)),
        ],
        out_specs=pl.BlockSpec((tm, w), lambda i: (i, 0)),
        out_shape=jax.ShapeDtypeStruct((t, w), BF16),
        scratch_shapes=[pltpu.VMEM((SUBLANES, w), F32)],
        compiler_params=_params(1), name='rg_lru',
    )(pc, pc, pc, conv_w, row(conv_b), _block_diag_chunks(w_r).astype(BF16), row(b_r),
      _block_diag_chunks(w_i).astype(BF16), row(b_i), row(lam))


def _branch_merge_body(h_ref, oa_ref, ob_ref, oc_ref, wga_ref, wgb_ref, wgc_ref, wa_ref, wb_ref, wc_ref, o_ref):
    h = h_ref[...]
    mixed = (jax.nn.sigmoid(_dot(h, wga_ref[...])) * _dot(oa_ref[...], wa_ref[...])
             + jax.nn.sigmoid(_dot(h, wgb_ref[...])) * _dot(ob_ref[...], wb_ref[...])
             + jax.nn.sigmoid(_dot(h, wgc_ref[...])) * _dot(oc_ref[...], wc_ref[...]))
    o_ref[...] = mixed.astype(o_ref.dtype)


MOE_ROWS = 256
IDX_REC = 1024


def _router_body(a_ref, w_ref, b_ref, gsel_ref, eidx_ref, rank_ref, cnt_ref, carry_ref, *, n_experts, top_k, tm):
    i = pl.program_id(0)

    @pl.when(i == 0)
    def _():
        carry_ref[...] = jnp.zeros(carry_ref.shape, F32)

    logits = _dot(a_ref[...], w_ref[...])
    lane = lax.broadcasted_iota(jnp.int32, logits.shape, 1).astype(F32)
    scores = jax.nn.sigmoid(logits)
    cur = jnp.where(lane < n_experts, scores + b_ref[...], -jnp.inf)
    mask = jnp.zeros(logits.shape, F32)
    hits, picked_scores = [], []
    for r in range(top_k):
        m = jnp.max(cur, axis=1, keepdims=True)
        idx = jnp.min(jnp.where(cur == m, lane, float(LANES)), axis=1, keepdims=True)
        hit = lane == idx
        hits.append(hit)
        picked_scores.append(jnp.sum(jnp.where(hit, scores, 0.0), axis=1, keepdims=True))
        eidx_ref[:, r:r + 1] = idx.astype(jnp.int32)
        cur = jnp.where(hit, -jnp.inf, cur)
        mask = mask + jnp.where(hit, 1.0, 0.0)
    total = picked_scores[0]
    for r in range(1, top_k):
        total = total + picked_scores[r]
    ri = lax.broadcasted_iota(jnp.int32, (tm, tm), 0)
    ci = lax.broadcasted_iota(jnp.int32, (tm, tm), 1)
    tri = jnp.where(ri > ci, 1.0, 0.0).astype(BF16)
    before = _dot(tri, mask.astype(BF16)) + carry_ref[0:1, :]
    for r in range(top_k):
        gsel_ref[:, r:r + 1] = picked_scores[r] / total * ROUTED_SCALE
        rank_ref[:, r:r + 1] = jnp.sum(jnp.where(hits[r], before, 0.0), axis=1, keepdims=True).astype(jnp.int32)
    carry_ref[...] = carry_ref[...] + jnp.sum(mask, axis=0, keepdims=True)
    cnt_ref[...] = carry_ref[...]


def _router(h2, w_router, b_router, tm=1024):
    t, d = h2.shape
    n_exp = w_router.shape[1]
    w_r = jnp.zeros((d, LANES), BF16).at[:, :n_exp].set(w_router.astype(BF16))
    b_r = jnp.zeros((1, LANES), F32).at[0, :n_exp].set(b_router.astype(F32))
    col = lambda dt: jax.ShapeDtypeStruct((t, MOE_TOP_K), dt)
    cspec = pl.BlockSpec((tm, MOE_TOP_K), lambda i: (i, 0))
    return pl.pallas_call(
        functools.partial(_router_body, n_experts=n_exp, top_k=MOE_TOP_K, tm=tm), grid=(t // tm,),
        in_specs=[pl.BlockSpec((tm, d), lambda i: (i, 0)), pl.BlockSpec((d, LANES), lambda i: (0, 0)),
                  pl.BlockSpec((1, LANES), lambda i: (0, 0))],
        out_specs=[cspec, cspec, cspec, pl.BlockSpec((SUBLANES, LANES), lambda i: (0, 0))],
        out_shape=[col(F32), col(jnp.int32), col(jnp.int32), jax.ShapeDtypeStruct((SUBLANES, LANES), F32)],
        scratch_shapes=[pltpu.VMEM((SUBLANES, LANES), F32)],
        compiler_params=_params(1), name='router')(h2, w_r, b_r)


def _pack_bf16_pairs(lo, hi):
    lo_bits = pltpu.bitcast(lo.astype(BF16).astype(F32), jnp.uint32)
    hi_bits = pltpu.bitcast(hi.astype(BF16).astype(F32), jnp.uint32)
    return (hi_bits & jnp.uint32(0xFFFF0000)) | lax.shift_right_logical(lo_bits, jnp.uint32(16))


def _unpack_lo(u):
    return pltpu.bitcast(lax.shift_left(u, jnp.uint32(16)), F32)


def _unpack_hi(u):
    return pltpu.bitcast(u & jnp.uint32(0xFFFF0000), F32)


def _norm_mod_pack_body(x_ref, g_ref, sc_ref, sh_ref, o_ref, p_ref):
    x = x_ref[...]
    y = x * lax.rsqrt(jnp.mean(x * x, axis=-1, keepdims=True) + RMS_EPS) * g_ref[...]
    y = y * (1.0 + sc_ref[...]) + sh_ref[...]
    half = y.shape[1] // 2
    o_ref[...] = y.astype(o_ref.dtype)
    p_ref[...] = _pack_bf16_pairs(y[:, :half], y[:, half:])


def _norm_mod_pack(x2, g, sc, sh, seq, tm=256):
    t, d = x2.shape
    bsz = t // seq
    row = pl.BlockSpec((tm, d), lambda i: (i, 0))
    vec = pl.BlockSpec((1, d), lambda i: (0, 0))
    bvec = pl.BlockSpec((None, 1, d), lambda i: (i // (seq // tm), 0, 0))
    return pl.pallas_call(
        _norm_mod_pack_body, grid=(t // tm,), in_specs=[row, vec, bvec, bvec],
        out_specs=[row, pl.BlockSpec((tm, d // 2), lambda i: (i, 0))],
        out_shape=[jax.ShapeDtypeStruct((t, d), BF16), jax.ShapeDtypeStruct((t, d // 2), jnp.uint32)],
        compiler_params=_params(1), name='norm_mod_pack',
    )(x2, g.reshape(1, d), sc.reshape(bsz, 1, d), sh.reshape(bsz, 1, d))


def _moe_group_body(te_ref, nu_ref, idx_hbm, h_hbm, w1_ref, w3_ref, w2_ref, o_ref,
                    idx_smem, xbuf, isem, gsem, *, tmr, n_tiles):
    i = pl.program_id(0)
    n = nu_ref[0]
    dh = xbuf.shape[-1]
    n_groups = tmr // SUBLANES
    n_up = n_groups // 2
    n_down = n_up // 2
    kc = 2 * dh // n_up

    def idx_copy(tile, rec):
        tile = jnp.minimum(tile, n_tiles - 1)
        return pltpu.make_async_copy(idx_hbm.at[pl.ds(pl.multiple_of(tile * IDX_REC, IDX_REC), IDX_REC)],
                                     idx_smem.at[pl.ds(pl.multiple_of(rec * IDX_REC, IDX_REC), IDX_REC)],
                                     isem.at[rec])

    def row_in(tok, g, u, s):
        return pltpu.make_async_copy(h_hbm.at[lax.shift_right_logical(tok, 3), pl.ds(tok & (SUBLANES - 1), 1)],
                                     xbuf.at[s, g, pl.ds(u, 1)], gsem.at[s])

    def gather_group(g, rec, s):
        for u in range(SUBLANES):
            row_in(idx_smem[rec * IDX_REC + g * SUBLANES + u], g, u, s).start(priority=u % 2)

    def per_group(fn):
        def body(g, c):
            fn(g)
            return c
        lax.fori_loop(0, n_groups, body, 0)

    def gather_wait(s):
        per_group(lambda g: [row_in(jnp.int32(0), g, u, s).wait() for u in range(SUBLANES)])

    @pl.when(i == 0)
    def _prologue():
        for k in range(2):
            idx_copy(k, k).start()
            idx_copy(k, k).wait()
        idx_copy(2, 2).start()
        per_group(lambda g: gather_group(g, 0, 0))
        per_group(lambda g: gather_group(g, 1, 1))

    @pl.when(i < n)
    def _tile():
        s, s_in, r_in = i % 3, (i + 2) % 3, (i + 2) % 3
        idx_copy(i + 2, r_in).wait()
        gather_wait(s)
        h1 = jnp.zeros((tmr, w1_ref.shape[-1]), F32)
        h3 = jnp.zeros((tmr, w1_ref.shape[-1]), F32)
        for c in range(n_up):
            gather_group(2 * c, r_in, s_in)
            gather_group(2 * c + 1, r_in, s_in)
            pc = c % (n_up // 2)
            u = xbuf[s, :, :, pc * kc:(pc + 1) * kc].reshape(tmr, kc)
            xc = (_unpack_lo(u) if c < n_up // 2 else _unpack_hi(u)).astype(BF16)
            h1 = h1 + _dot(xc, w1_ref[c * kc:(c + 1) * kc, :].astype(BF16))
            h3 = h3 + _dot(xc, w3_ref[c * kc:(c + 1) * kc, :].astype(BF16))
        hid = ((h1 * jax.nn.sigmoid(h1)) * h3).astype(BF16)
        for c in range(n_down):
            y_lo = _dot(hid, w2_ref[:, c * kc:(c + 1) * kc].astype(BF16))
            y_hi = _dot(hid, w2_ref[:, dh + c * kc:dh + (c + 1) * kc].astype(BF16))
            o_ref[:, c * kc:(c + 1) * kc] = _pack_bf16_pairs(y_lo, y_hi)
        idx_copy(i + 3, i % 3).start()

    @pl.when(i == n)
    def _drain():
        gather_wait(i % 3)
        gather_wait((i + 1) % 3)
        idx_copy(i + 2, (i + 2) % 3).wait()


def _moe_grouped(h2p, idx, tile_expert, n_used, w1, w3, w2, layer):
    t, dh = h2p.shape
    d = 2 * dh
    n_tiles = idx.shape[0] // IDX_REC
    tmr = MOE_ROWS
    f = w1.shape[3]
    assert n_tiles >= 4
    wmap = lambda i, te, nu: (layer, te[i], 0, 0)
    grid_spec = pltpu.PrefetchScalarGridSpec(
        num_scalar_prefetch=2, grid=(n_tiles + 1,),
        in_specs=[
            pl.BlockSpec(memory_space=pl.ANY), pl.BlockSpec(memory_space=pl.ANY),
            pl.BlockSpec((None, None, d, f), wmap), pl.BlockSpec((None, None, d, f), wmap),
            pl.BlockSpec((None, None, f, d), wmap),
        ],
        out_specs=pl.BlockSpec((tmr, dh), lambda i, te, nu: (jnp.minimum(i, nu[0] - 1), 0)),
        scratch_shapes=[
            pltpu.SMEM((3 * IDX_REC,), jnp.int32),
            pltpu.VMEM((3, tmr // SUBLANES, SUBLANES, dh), jnp.uint32),
            pltpu.SemaphoreType.DMA((3,)), pltpu.SemaphoreType.DMA((3,)),
        ])
    return pl.pallas_call(
        functools.partial(_moe_group_body, tmr=tmr, n_tiles=n_tiles), grid_spec=grid_spec,
        out_shape=jax.ShapeDtypeStruct((n_tiles * tmr, dh), jnp.uint32),
        compiler_params=_params(1), name='moe_grouped',
    )(tile_expert, n_used, idx, h2p.reshape(t // SUBLANES, SUBLANES, dh), w1, w3, w2)


def _moe_plan(eidx, rank, counts, n_experts, tm_combine):
    t, k = eidx.shape
    tmr = MOE_ROWS
    n_tiles = (t * k) // tmr + n_experts
    padded = ((counts + tmr - 1) // tmr) * tmr
    ends = jnp.cumsum(padded)
    offsets = ends - padded
    n_used = (ends[-1] // tmr).astype(jnp.int32)
    pos = offsets[eidx] + rank
    tok = jnp.broadcast_to(jnp.arange(t, dtype=jnp.int32)[:, None], (t, k))
    j = jnp.arange(tmr, dtype=jnp.int32)[None, :]
    pad_pos = jnp.where(j < (padded - counts)[:, None], (offsets + counts)[:, None] + j,
                        n_tiles * tmr + jnp.arange(n_experts, dtype=jnp.int32)[:, None] * tmr + j)
    keys = jnp.concatenate([pos.reshape(-1), pad_pos.reshape(-1)])
    vals = jnp.concatenate([tok.reshape(-1), jnp.broadcast_to(j, (n_experts, tmr)).reshape(-1)])
    tok_sorted = lax.sort((keys, vals), num_keys=1)[1].reshape(n_tiles, tmr)
    idx = jnp.concatenate([tok_sorted, jnp.zeros((n_tiles, IDX_REC - tmr), jnp.int32)], axis=1).reshape(-1)
    tile_start = jnp.minimum(jnp.arange(n_tiles + 1, dtype=jnp.int32), n_used - 1) * tmr
    tile_expert = jnp.sum((ends[None, :] <= tile_start[:, None]).astype(jnp.int32), axis=1)
    assert tm_combine * k == IDX_REC
    pos_blocks = pos.reshape(t // tm_combine, tm_combine, k).transpose(0, 2, 1).reshape(-1)
    return idx, jnp.minimum(tile_expert, n_experts - 1), n_used.reshape(1), pos_blocks


def _moe_combine_body(pos_hbm, y_hbm, x_ref, g_ref, gsel_ref, h_ref, w1_ref, w3_ref, w2_ref, o_ref,
                      pos_smem, ybuf, isem, gsem, *, tm, n_slots):
    i = pl.program_id(0)
    n_steps = pl.num_programs(0)
    n_groups = tm // SUBLANES
    half = x_ref.shape[1] // 2

    def pos_copy(step, rec):
        step = jnp.minimum(step, n_steps - 1)
        return pltpu.make_async_copy(pos_hbm.at[pl.ds(pl.multiple_of(step * IDX_REC, IDX_REC), IDX_REC)],
                                     pos_smem.at[pl.ds(pl.multiple_of(rec * IDX_REC, IDX_REC), IDX_REC)],
                                     isem.at[rec])

    def row_in(row, r, g, u, s):
        return pltpu.make_async_copy(y_hbm.at[lax.shift_right_logical(row, 3), pl.ds(row & (SUBLANES - 1), 1)],
                                     ybuf.at[s, r, g, pl.ds(u, 1)], gsem.at[s])

    def per_group(fn):
        def body(g, c):
            fn(g)
            return c
        lax.fori_loop(0, n_groups, body, 0)

    def gather_group(g, rec, s):
        for r in range(n_slots):
            for u in range(SUBLANES):
                row_in(pos_smem[rec * IDX_REC + r * tm + g * SUBLANES + u], r, g, u, s).start(priority=u % 2)

    def gather_wait(s):
        def group(g):
            for r in range(n_slots):
                for u in range(SUBLANES):
                    row_in(jnp.int32(0), r, g, u, s).wait()
        per_group(group)

    @pl.when(i == 0)
    def _prologue():
        pos_copy(0, 0).start()
        pos_copy(0, 0).wait()
        per_group(lambda g: gather_group(g, 0, 0))
        pos_copy(1, 1).start()

    nxt, rec_next = (i + 1) % 2, (i + 1) % 2
    pos_copy(i + 1, rec_next).wait()
    groups = iter(range(n_groups))

    def issue(count):
        for _ in range(count):
            g = next(groups, None)
            if g is not None:
                gather_group(g, rec_next, nxt)

    issue(2)
    a = h_ref[...]
    h1 = _dot(a, w1_ref[...])
    issue(2)
    hid = ((h1 * jax.nn.sigmoid(h1)) * _dot(a, w3_ref[...])).astype(BF16)
    issue(2)
    acc_lo = _dot(hid, w2_ref[:, :half])
    issue(2)
    acc_hi = _dot(hid, w2_ref[:, half:])
    gather_wait(i % 2)
    pos_copy(i + 2, i % 2).start()
    gs = gsel_ref[...]
    for r in range(n_slots):
        issue(1)
        u = ybuf[i % 2, r].reshape(tm, half)
        acc_lo = acc_lo + gs[:, r:r + 1] * _unpack_lo(u)
        acc_hi = acc_hi + gs[:, r:r + 1] * _unpack_hi(u)
    issue(n_groups)
    o_ref[:, :half] = x_ref[:, :half] + g_ref[:, :half] * acc_lo
    o_ref[:, half:] = x_ref[:, half:] + g_ref[:, half:] * acc_hi

    @pl.when(i == n_steps - 1)
    def _drain():
        gather_wait(nxt)
        pos_copy(i + 2, i % 2).wait()


def _moe_combine(x2, g2, gsel, h2, ws1, ws3, ws2, y, pos_blocks, seq, tm):
    t, d = x2.shape
    bsz = t // seq
    k = gsel.shape[1]
    f = ws1.shape[1]
    dh = d // 2
    row = pl.BlockSpec((tm, d), lambda i: (i, 0))
    full = lambda shape: pl.BlockSpec(shape, lambda i: (0, 0))
    anyspace = pl.BlockSpec(memory_space=pl.ANY)
    return pl.pallas_call(
        functools.partial(_moe_combine_body, tm=tm, n_slots=k), grid=(t // tm,),
        in_specs=[anyspace, anyspace, row, pl.BlockSpec((None, 1, d), lambda i: (i // (seq // tm), 0, 0)),
                  pl.BlockSpec((tm, k), lambda i: (i, 0)), row, full((d, f)), full((d, f)), full((f, d))],
        out_specs=row, out_shape=jax.ShapeDtypeStruct((t, d), F32),
        scratch_shapes=[
            pltpu.SMEM((2 * IDX_REC,), jnp.int32),
            pltpu.VMEM((2, k, tm // SUBLANES, SUBLANES, dh), jnp.uint32),
            pltpu.SemaphoreType.DMA((2,)), pltpu.SemaphoreType.DMA((2,)),
        ],
        compiler_params=_params(1), name='moe_combine',
    )(pos_blocks, y.reshape(y.shape[0] // SUBLANES, SUBLANES, dh), x2, g2.reshape(bsz, 1, d), gsel, h2,
      ws1.astype(BF16), ws3.astype(BF16), ws2.astype(BF16))


def kernel(x, c, w_ada, b_ada, ada_layer, norm_mix, norm_ffn, w_in, q_norm_a, k_norm_a, q_norm_b,
           k_norm_b, conv_w, conv_b, w_rgate, b_rgate, w_igate, b_igate, lru_lambda, w_branch_a,
           w_branch_b, w_branch_c, w_out, w_router, b_router, w1_exp, w3_exp, w2_exp, w1_shared,
           w3_shared, w2_shared):
    bsz, seq, d = x.shape
    t = bsz * seq
    depth = w_in.shape[0]
    mixw = d // 4
    a_cols = 3 * mixw + IDX_HEADS * IDX_DIM + IDX_DIM + IDX_HEADS
    off_b = a_cols
    off_c = off_b + N_DIL * 3 * mixw
    off_g = off_c + 2 * mixw
    top_k = min(DSA_TOPK_MAX, seq // 4)
    tm = 1024 if t % 1024 == 0 else 256
    tpb = seq // tm

    mod_shared = _ada_mod(c, w_ada, b_ada).reshape(bsz, N_MOD, d)
    x2 = x.reshape(t, d)
    for l in range(depth):
        mod = mod_shared + ada_layer[l]
        sh1, sc1, g1, sh2, sc2, g2 = [mod[:, j] for j in range(N_MOD)]
        h = _norm_mod(x2, norm_mix[l], sc1, sh1, seq)
        wl = w_in[l]

        n_main = 3 * mixw + IDX_HEADS * IDX_DIM
        pqk = _proj_qk_norm(h, wl[:, :2 * mixw].astype(BF16), q_norm_a[l], k_norm_a[l], tm)
        pvi = _matmul(h, wl[:, 2 * mixw:n_main].astype(BF16), BF16, tm=tm, tn=512, name='proj_a')
        w_small = jnp.zeros((d, LANES), BF16).at[:, :IDX_DIM + IDX_HEADS].set(
            wl[:, n_main:a_cols].astype(BF16))
        small = _matmul(h, w_small, F32, tm=tm, tn=LANES, name='proj_a_idx')
        ik = small[:, :IDX_DIM].astype(BF16)
        zeros = jnp.zeros_like(ik)
        ik2 = jnp.concatenate([ik, zeros, zeros, ik], axis=1)
        iwt = small[:, IDX_DIM:IDX_DIM + IDX_HEADS].reshape(bsz, seq, IDX_HEADS).transpose(0, 2, 1)
        vt = pvi[:, :mixw].reshape(bsz, seq, mixw // HEAD_DIM, HEAD_DIM).transpose(0, 2, 3, 1)
        o_a = _dsa_attention(pqk, pvi, ik2, iwt, vt, seq, top_k)

        outs, lses = [], []
        for g, (window, dil) in enumerate(DIL_PATTERNS):
            w_g = wl[:, off_b + g * 3 * mixw:off_b + (g + 1) * 3 * mixw].astype(BF16)
            qkv = _to_classes(_matmul(h, w_g, BF16, tm=tm, tn=512, name='proj_b'), bsz, seq, dil)
            og, lg = _dilated_group(qkv, q_norm_b[l, g], k_norm_b[l, g], bsz * dil, seq // dil,
                                    band=window // dil)
            outs.append(_from_classes(og, bsz, seq, dil))
            lses.append(_from_classes(lg, bsz, seq, dil))
        o_b = _dil_merge(outs, lses)

        pc = _matmul(h, wl[:, off_c:off_g].astype(BF16), F32, tm=tm, tn=512, name='proj_c')
        o_c = _rg_lru(pc, conv_w[l], conv_b[l], w_rgate[l], b_rgate[l], w_igate[l], b_igate[l],
                      lru_lambda[l], seq)

        w_gates = wl[:, off_g:].astype(BF16)
        tn_merge = 256
        nd = d // tn_merge
        mixed = _tiled_call(
            _branch_merge_body, t, d, tm, tn_merge,
            [(h, 'row'), (o_a, 'row'), (o_b, 'row'), (o_c, 'row'),
             (w_gates, ('coloff', 0)), (w_gates, ('coloff', nd)), (w_gates, ('coloff', 2 * nd)),
             (w_branch_a[l].astype(BF16), 'col'), (w_branch_b[l].astype(BF16), 'col'),
             (w_branch_c[l].astype(BF16), 'col')],
            [(d, BF16)], name='branch_merge')
        x2 = _tiled_call(
            _mm_residual_body, t, d, tm, 512,
            [(mixed, 'row'), (w_out[l].astype(BF16), 'col'), (x2, 'tile'), (g1.reshape(bsz, 1, d), 'bvec')],
            [(d, F32)], tiles_per_batch=tpb, name='out_proj')

        h2, h2p = _norm_mod_pack(x2, norm_ffn[l], sc2, sh2, seq)
        gsel, eidx, rank, cnt = _router(h2, w_router[l], b_router[l])
        tm_combine = IDX_REC // MOE_TOP_K
        idx, tile_expert, n_used, pos_blocks = _moe_plan(eidx, rank, cnt[0, :N_EXPERTS].astype(jnp.int32),
                                                        N_EXPERTS, tm_combine)
        y = _moe_grouped(h2p, idx, tile_expert, n_used, w1_exp, w3_exp, w2_exp, l)
        x2 = _moe_combine(x2, g2, gsel, h2, w1_shared[l], w3_shared[l], w2_shared[l], y, pos_blocks, seq,
                          tm_combine)
    return x2.reshape(bsz, seq, d)
```

```python
import functools

import jax
import jax.numpy as jnp
from jax import lax
from jax.experimental import pallas as pl
from jax.experimental.pallas import tpu as pltpu

HEAD_DIM = 128
A_HEADS = 8
IDX_HEADS = 16
IDX_DIM = 64
DSA_TOPK_MAX = 256
DIL_PATTERNS = ((128, 1), (512, 4), (2048, 16))
N_DIL = 3
B_HEADS = 8
LRU_BLOCKS = 16
CONV_WIDTH = 4
LRU_C = 8.0
N_EXPERTS = 64
MOE_TOP_K = 8
ROUTED_SCALE = 2.5
N_MOD = 6
RMS_EPS = 1e-6

LANES = 128
SUBLANES = 8
VMEM_LIMIT_BYTES = 56 * 1024 * 1024

NEG_BIG = -1e30
LOG2_E = 1.4426950408889634
INT_MIN = -(2 ** 31)

BF16 = jnp.bfloat16
F32 = jnp.float32


def _params(n_axes):
    return pltpu.CompilerParams(dimension_semantics=("arbitrary",) * n_axes,
                                vmem_limit_bytes=VMEM_LIMIT_BYTES)


def _dot(a, b):
    return jnp.dot(a, b, preferred_element_type=F32)


def _dot_nt(a, b):
    return lax.dot_general(a, b, (((1,), (1,)), ((), ())), preferred_element_type=F32)


def _tiled_call(body, m, n, tm, tn, ins, outs, *, tiles_per_batch=None, name=None):
    assert m % tm == 0 and n % tn == 0, (m, n, tm, tn)
    grid = (m // tm, n // tn)
    in_specs, arrays = [], []
    for arr, kind in ins:
        arrays.append(arr)
        if kind == 'row':
            in_specs.append(pl.BlockSpec((tm, arr.shape[1]), lambda i, j: (i, 0)))
        elif isinstance(kind, tuple) and kind[0] == 'rowoff':
            _, off, width = kind
            in_specs.append(pl.BlockSpec((tm, width), lambda i, j, off=off: (i, off)))
        elif kind == 'col':
            in_specs.append(pl.BlockSpec((arr.shape[0], tn), lambda i, j: (0, j)))
        elif isinstance(kind, tuple) and kind[0] == 'coloff':
            in_specs.append(pl.BlockSpec((arr.shape[0], tn), lambda i, j, off=kind[1]: (0, j + off)))
        elif kind == 'col3':
            in_specs.append(pl.BlockSpec((None, arr.shape[1], arr.shape[2]), lambda i, j: (j, 0, 0)))
        elif kind == 'tile':
            in_specs.append(pl.BlockSpec((tm, tn), lambda i, j: (i, j)))
        elif isinstance(kind, tuple) and kind[0] == 'tileoff':
            in_specs.append(pl.BlockSpec((tm, tn), lambda i, j, off=kind[1]: (i, j + off)))
        elif kind == 'vec':
            in_specs.append(pl.BlockSpec((1, tn), lambda i, j: (0, j)))
        elif kind == 'bvec':
            tpb = tiles_per_batch
            in_specs.append(pl.BlockSpec((None, 1, tn), lambda i, j, tpb=tpb: (i // tpb, 0, j)))
        elif kind == 'full':
            nd = arr.ndim
            in_specs.append(pl.BlockSpec(arr.shape, lambda i, j, nd=nd: (0,) * nd))
        else:
            raise ValueError(kind)
    out_shape, out_specs = [], []
    for n_cols, dtype in outs:
        assert (n_cols * tn) % n == 0
        w = n_cols * tn // n
        out_shape.append(jax.ShapeDtypeStruct((m, n_cols), dtype))
        out_specs.append(pl.BlockSpec((tm, w), lambda i, j: (i, j)))
    single = len(outs) == 1
    res = pl.pallas_call(
        body, grid=grid, in_specs=in_specs,
        out_specs=out_specs[0] if single else out_specs,
        out_shape=out_shape[0] if single else out_shape,
        compiler_params=_params(2), name=name)(*arrays)
    return res


def _mm_body(a_ref, w_ref, o_ref):
    o_ref[...] = _dot(a_ref[...], w_ref[...]).astype(o_ref.dtype)


def _mm_residual_body(a_ref, w_ref, x_ref, g_ref, o_ref):
    o_ref[...] = x_ref[...] + g_ref[...] * _dot(a_ref[...], w_ref[...])


def _matmul(a, w, out_dtype, *, tm, tn, body=_mm_body, name=None):
    return _tiled_call(body, a.shape[0], w.shape[1], tm, tn, [(a, 'row'), (w, 'col')],
                       [(w.shape[1], out_dtype)], name=name)


def _ada_body(c_ref, w_ref, b_ref, o_ref):
    c = c_ref[...]
    a = (c * jax.nn.sigmoid(c)).astype(BF16)
    o_ref[...] = _dot(a, w_ref[...].astype(BF16)) + b_ref[...]


def _ada_mod(c, w_ada, b_ada):
    bsz, d = c.shape
    n = w_ada.shape[1]
    c_pad = jnp.zeros((SUBLANES, d), F32).at[:bsz].set(c)
    out = _tiled_call(_ada_body, SUBLANES, n, SUBLANES, 1024,
                      [(c_pad, 'row'), (w_ada, 'col'), (b_ada.reshape(1, n), 'vec')],
                      [(n, F32)], name='ada_mod')
    return out[:bsz]


def _norm_mod_body(x_ref, g_ref, sc_ref, sh_ref, o_ref):
    x = x_ref[...]
    y = x * lax.rsqrt(jnp.mean(x * x, axis=-1, keepdims=True) + RMS_EPS) * g_ref[...]
    o_ref[...] = (y * (1.0 + sc_ref[...]) + sh_ref[...]).astype(o_ref.dtype)


def _norm_mod(x2, g, sc, sh, seq, tm=256):
    t, d = x2.shape
    bsz = t // seq
    return _tiled_call(_norm_mod_body, t, d, tm, d,
                       [(x2, 'tile'), (g.reshape(1, d), 'vec'), (sc.reshape(bsz, 1, d), 'bvec'),
                        (sh.reshape(bsz, 1, d), 'bvec')],
                       [(d, BF16)], tiles_per_batch=seq // tm, name='norm_mod')


def _head_norm(x, g):
    outs = []
    for h in range(x.shape[1] // HEAD_DIM):
        xh = x[:, h * HEAD_DIM:(h + 1) * HEAD_DIM].astype(F32)
        outs.append(xh * lax.rsqrt(jnp.mean(xh * xh, axis=-1, keepdims=True) + RMS_EPS) * g)
    return jnp.concatenate(outs, axis=1)


def _mm_qk_norm_body(a_ref, w_ref, gq_ref, gk_ref, o_ref, *, q_tiles):
    g = jnp.where(pl.program_id(1) < q_tiles, gq_ref[...], gk_ref[...])
    o_ref[...] = _head_norm(_dot(a_ref[...], w_ref[...]), g).astype(o_ref.dtype)


def _proj_qk_norm(h, w_qk, gq, gk, tm, tn=512):
    n = w_qk.shape[1]
    body = functools.partial(_mm_qk_norm_body, q_tiles=n // 2 // tn)
    return _tiled_call(body, h.shape[0], n, tm, tn,
                       [(h, 'row'), (w_qk, 'col'), (gq.reshape(1, HEAD_DIM), 'full'),
                        (gk.reshape(1, HEAD_DIM), 'full')], [(n, BF16)], name='proj_qk_norm')


def _dsa_body(qb_ref, kb_ref, q_ref, iq_ref, iwt_ref, ik_ref, k_ref, vt_ref, o_ref,
              keys_ref, thr_ref, bias_ref, s_ref, p_ref, m_ref, l_ref, acc_ref, *, tq, tk, top_k, n_heads):
    qb = qb_ref[pl.program_id(1)]
    kb = kb_ref[pl.program_id(1)]
    kb_last = ((qb + 1) * tq - 1) // tk
    n_chunks = kb_last + 1
    kpos = lax.broadcasted_iota(jnp.int32, (tk, tq), 0)
    qpos = lax.broadcasted_iota(jnp.int32, (tk, tq), 1) + qb * tq
    n_acc = 4 * SUBLANES

    @pl.when(kb == 0)
    def _scores_and_threshold():
        w = iwt_ref[...] * (IDX_HEADS ** -0.5 * IDX_DIM ** -0.5)

        def score_chunk(c, carry):
            ik2 = ik_ref[pl.ds(pl.multiple_of(c * tk, tk), tk), :]
            acc = jnp.zeros((tk, tq), F32)
            for p in range(IDX_HEADS // 2):
                iq_pair = iq_ref[:, p * LANES:(p + 1) * LANES]
                for half in range(2):
                    rel = _dot_nt(ik2[:, half * LANES:(half + 1) * LANES], iq_pair)
                    h = 2 * p + half
                    acc = acc + w[h:h + 1, :] * jnp.maximum(rel, 0.0)
            bits = pltpu.bitcast(acc, jnp.int32)
            key = jnp.where(bits < 0, bits ^ 0x7FFFFFFF, bits)
            key = jnp.where(kpos + c * tk <= qpos, key, INT_MIN)
            keys_ref[c] = key
            return carry

        lax.fori_loop(0, n_chunks, score_chunk, 0)

        def count_ge(cand):
            def body(c, cnt):
                for j in range(tk // n_acc):
                    blk = keys_ref[c, j * n_acc:(j + 1) * n_acc, :]
                    cnt = cnt + jnp.where(blk >= cand, 1.0, 0.0)
                return cnt
            cnt = lax.fori_loop(0, n_chunks, body, jnp.zeros((n_acc, tq), F32))
            return jnp.sum(cnt, axis=0, keepdims=True)

        k_f = float(top_k)
        t0 = jnp.where(count_ge(jnp.zeros((1, tq), jnp.int32)) >= k_f, 0, INT_MIN).astype(jnp.int32)

        def bit_body(i, t):
            cand = t | lax.shift_left(jnp.int32(1), 30 - i)
            return jnp.where(count_ge(cand) >= k_f, cand, t)

        t = lax.fori_loop(0, 31, bit_body, t0)
        thr_ref[...] = jnp.broadcast_to(t, thr_ref.shape)
        m_ref[...] = jnp.full(m_ref.shape, NEG_BIG, F32)
        l_ref[...] = jnp.zeros(l_ref.shape, F32)
        acc_ref[...] = jnp.zeros(acc_ref.shape, F32)

    @pl.when(kb <= kb_last)
    def _attend():
        sel = (keys_ref[kb] >= thr_ref[0:1, :]) & (kpos + kb * tk <= qpos)
        bias_ref[...] = jnp.where(sel, 0.0, NEG_BIG)
        c = HEAD_DIM ** -0.5 * LOG2_E

        def raw_scores(h):
            sl = slice(h * HEAD_DIM, (h + 1) * HEAD_DIM)
            s_ref[h % 4] = _dot_nt(k_ref[:, sl], q_ref[:, sl])

        def softmax(h):
            s = s_ref[h % 4] + bias_ref[...]
            m_old = m_ref[h, 0:1, :]
            m_new = jnp.maximum(m_old, jnp.max(s, axis=0, keepdims=True))
            p = jnp.exp2((s - m_new) * c)
            alpha = jnp.exp2((m_old - m_new) * c)
            l_new = alpha * l_ref[h, 0:1, :] + jnp.sum(p, axis=0, keepdims=True)
            p_ref[h % 3] = p.astype(BF16)
            m_ref[h] = jnp.broadcast_to(m_new, (SUBLANES, tq))
            l_ref[h] = jnp.broadcast_to(l_new, (SUBLANES, tq))
            return alpha

        def accumulate(h, alpha):
            acc_ref[h] = alpha * acc_ref[h] + _dot(vt_ref[h], p_ref[h % 3])

        s_ahead, p_ahead = 3, 2
        for h in range(s_ahead):
            raw_scores(h)
        alphas = {h: softmax(h) for h in range(p_ahead)}
        for h in range(n_heads):
            if h + s_ahead < n_heads:
                raw_scores(h + s_ahead)
            if h + p_ahead < n_heads:
                alphas[h + p_ahead] = softmax(h + p_ahead)
            accumulate(h, alphas.pop(h))

    @pl.when(kb == kb_last)
    def _finish():
        for h in range(n_heads):
            sl = slice(h * HEAD_DIM, (h + 1) * HEAD_DIM)
            o_ref[:, sl] = (acc_ref[h] / l_ref[h, 0:1, :]).T.astype(o_ref.dtype)


def _dsa_attention(pqk, pvi, ik2, iwt, vt, seq, top_k, *, tq=256, tk=512):
    t = pqk.shape[0]
    width = pqk.shape[1] // 2
    bsz = t // seq
    nq, nk = seq // tq, seq // tk
    n_heads = width // HEAD_DIM
    assert width == IDX_HEADS * IDX_DIM and pvi.shape[1] == 2 * width

    pairs = [(qb, kb) for qb in range(nq) for kb in range(((qb + 1) * tq - 1) // tk + 1)]
    qb_tab = jnp.asarray([p[0] for p in pairs], jnp.int32)
    kb_tab = jnp.asarray([p[1] for p in pairs], jnp.int32)

    body = functools.partial(_dsa_body, tq=tq, tk=tk, top_k=top_k, n_heads=n_heads)
    grid_spec = pltpu.PrefetchScalarGridSpec(
        num_scalar_prefetch=2, grid=(bsz, len(pairs)),
        in_specs=[
            pl.BlockSpec((tq, width), lambda b, s, qt, kt: (b * nq + qt[s], 0)),
            pl.BlockSpec((tq, width), lambda b, s, qt, kt: (b * nq + qt[s], 1)),
            pl.BlockSpec((None, IDX_HEADS, tq), lambda b, s, qt, kt: (b, 0, qt[s])),
            pl.BlockSpec((seq, 2 * LANES), lambda b, s, qt, kt: (b, 0)),
            pl.BlockSpec((tk, width), lambda b, s, qt, kt: (b * nk + kt[s], 1)),
            pl.BlockSpec((None, n_heads, HEAD_DIM, tk), lambda b, s, qt, kt: (b, 0, 0, kt[s])),
        ],
        out_specs=pl.BlockSpec((tq, width), lambda b, s, qt, kt: (b * nq + qt[s], 0)),
        scratch_shapes=[
            pltpu.VMEM((nk, tk, tq), jnp.int32),
            pltpu.VMEM((SUBLANES, tq), jnp.int32),
            pltpu.VMEM((tk, tq), F32),
            pltpu.VMEM((4, tk, tq), F32),
            pltpu.VMEM((3, tk, tq), BF16),
            pltpu.VMEM((n_heads, SUBLANES, tq), F32),
            pltpu.VMEM((n_heads, SUBLANES, tq), F32),
            pltpu.VMEM((n_heads, HEAD_DIM, tq), F32),
        ])
    return pl.pallas_call(
        body, grid_spec=grid_spec, out_shape=jax.ShapeDtypeStruct((t, width), BF16),
        compiler_params=_params(2), name='dsa_attention',
    )(qb_tab, kb_tab, pqk, pvi, iwt, ik2, pqk, vt)


def _dil_body(q_ref, kc_ref, vc_ref, kp_ref, vp_ref, gq_ref, gk_ref, o_ref, lse_ref,
              kcat_ref, vcat_ref, *, band, rows, n_heads):
    n = pl.program_id(1)
    @pl.when(n == 0)
    def _():
        kcat_ref[0:band, :] = jnp.zeros((band, kcat_ref.shape[1]), BF16)
        vcat_ref[0:band, :] = jnp.zeros((band, vcat_ref.shape[1]), BF16)

    @pl.when(n > 0)
    def _():
        kcat_ref[0:band, :] = kcat_ref[rows:rows + band, :]
        vcat_ref[0:band, :] = vcat_ref[rows:rows + band, :]

    kcat_ref[band:band + rows, :] = _head_norm(kc_ref[...], gk_ref[...]).astype(BF16)
    vcat_ref[band:band + rows, :] = vc_ref[...]
    qi = lax.broadcasted_iota(jnp.int32, (band, 2 * band), 0)
    ki = lax.broadcasted_iota(jnp.int32, (band, 2 * band), 1)
    back = qi + band - ki
    in_band = (back >= 0) & (back <= band)
    scale = HEAD_DIM ** -0.5
    gq = gq_ref[...]

    def sub_block(j, carry):
        r0 = pl.multiple_of(j * band, band)
        bias = jnp.where(in_band & ((ki >= band) | (n * (rows // band) + j > 0)), 0.0, NEG_BIG)
        heads = [slice(h * HEAD_DIM, (h + 1) * HEAD_DIM) for h in range(n_heads)]

        def scores(sl):
            qh = q_ref[pl.ds(r0, band), sl].astype(F32)
            qh = (qh * lax.rsqrt(jnp.mean(qh * qh, axis=-1, keepdims=True) + RMS_EPS) * gq).astype(BF16)
            return _dot_nt(qh, kcat_ref[pl.ds(r0, 2 * band), sl]) * scale + bias

        s_all = [scores(sl) for sl in heads]
        m_all = [jnp.max(s, axis=1, keepdims=True) for s in s_all]
        p_all = [jnp.exp(s - m) for s, m in zip(s_all, m_all)]
        l_all = [jnp.sum(p, axis=1, keepdims=True) for p in p_all]
        for h, sl in enumerate(heads):
            o = _dot(p_all[h].astype(BF16), vcat_ref[pl.ds(r0, 2 * band), sl]) / l_all[h]
            o_ref[pl.ds(r0, band), sl] = o.astype(o_ref.dtype)
            lse_ref[pl.ds(r0, band), h:h + 1] = m_all[h] + jnp.log(l_all[h])
        return carry

    lax.fori_loop(0, rows // band, sub_block, 0)


def _dilated_group(qkv, gq, gk, n_seq, len_seq, *, band=128, rows=512):
    t, w3 = qkv.shape
    width = w3 // 3
    n_heads = width // HEAD_DIM
    rows = min(rows, len_seq)
    nblk = len_seq // rows
    sub = rows // band

    def cur(c):
        return lambda s, n: (s * nblk + n, c)

    def prev(c):
        return lambda s, n: (s * nblk * sub + jnp.maximum(n * sub - 1, 0), c)

    body = functools.partial(_dil_body, band=band, rows=rows, n_heads=n_heads)
    return pl.pallas_call(
        body, grid=(n_seq, nblk),
        in_specs=[
            pl.BlockSpec((rows, width), cur(0)), pl.BlockSpec((rows, width), cur(1)),
            pl.BlockSpec((rows, width), cur(2)),
            pl.BlockSpec((band, width), prev(1)), pl.BlockSpec((band, width), prev(2)),
            pl.BlockSpec((1, HEAD_DIM), lambda s, n: (0, 0)), pl.BlockSpec((1, HEAD_DIM), lambda s, n: (0, 0)),
        ],
        out_specs=[pl.BlockSpec((rows, width), lambda s, n: (s * nblk + n, 0)),
                   pl.BlockSpec((rows, n_heads), lambda s, n: (s * nblk + n, 0))],
        out_shape=[jax.ShapeDtypeStruct((t, width), BF16), jax.ShapeDtypeStruct((t, n_heads), F32)],
        scratch_shapes=[pltpu.VMEM((band + rows, width), BF16), pltpu.VMEM((band + rows, width), BF16)],
        compiler_params=_params(2), name='dilated_attention',
    )(qkv, qkv, qkv, qkv, qkv, gq.reshape(1, HEAD_DIM), gk.reshape(1, HEAD_DIM))


def _dil_merge_body(o0_ref, o1_ref, o2_ref, l0_ref, l1_ref, l2_ref, out_ref):
    l0, l1, l2 = l0_ref[...], l1_ref[...], l2_ref[...]
    m = jnp.maximum(jnp.maximum(l0, l1), l2)
    e0, e1, e2 = jnp.exp(l0 - m), jnp.exp(l1 - m), jnp.exp(l2 - m)
    den = e0 + e1 + e2
    a0, a1, a2 = e0 / den, e1 / den, e2 / den
    for h in range(l0.shape[1]):
        sl = slice(h * HEAD_DIM, (h + 1) * HEAD_DIM)
        out_ref[:, sl] = (a0[:, h:h + 1] * o0_ref[:, sl].astype(F32)
                          + a1[:, h:h + 1] * o1_ref[:, sl].astype(F32)
                          + a2[:, h:h + 1] * o2_ref[:, sl].astype(F32)).astype(out_ref.dtype)


def _dil_merge(outs, lses, tm=512):
    t, width = outs[0].shape
    nh = lses[0].shape[1]
    ospec = pl.BlockSpec((tm, width), lambda i: (i, 0))
    lspec = pl.BlockSpec((tm, nh), lambda i: (i, 0))
    return pl.pallas_call(
        _dil_merge_body, grid=(t // tm,), in_specs=[ospec] * 3 + [lspec] * 3, out_specs=ospec,
        out_shape=jax.ShapeDtypeStruct((t, width), BF16), compiler_params=_params(1), name='dilated_merge',
    )(*outs, *lses)


def _to_classes(a, bsz, seq, dil):
    if dil == 1:
        return a
    c = a.shape[1]
    return a.reshape(bsz, seq // dil, dil, c).transpose(0, 2, 1, 3).reshape(bsz * seq, c)


def _from_classes(a, bsz, seq, dil):
    if dil == 1:
        return a
    c = a.shape[1]
    return a.reshape(bsz, dil, seq // dil, c).transpose(0, 2, 1, 3).reshape(bsz * seq, c)


def _lru_body(x_ref, y_ref, xp_ref, cw_ref, cb_ref, wr_ref, br_ref, wi_ref, bi_ref, lam_ref, o_ref,
              carry_ref, *, tm, tiles_per_batch):
    i = pl.program_id(0)
    first = (i % tiles_per_batch) == 0
    x = x_ref[...]
    prev = jnp.where(first, 0.0, xp_ref[...])
    xs = jnp.concatenate([prev, x], axis=0)
    xc = cb_ref[...] + cw_ref[CONV_WIDTH - 1:CONV_WIDTH, :] * x
    for j in range(CONV_WIDTH - 1):
        back = CONV_WIDTH - 1 - j
        xc = xc + cw_ref[j:j + 1, :] * xs[SUBLANES - back:SUBLANES - back + tm, :]
    xcb = xc.astype(BF16)
    def gate(w_ref, b_ref):
        n_chunks, cw, _ = w_ref.shape
        pre = [_dot(xcb[:, c * cw:(c + 1) * cw], w_ref[c]) for c in range(n_chunks)]
        return jax.nn.sigmoid(jnp.concatenate(pre, axis=1) + b_ref[...])

    r = gate(wr_ref, br_ref)
    ig = gate(wi_ref, bi_ref)
    lam = lam_ref[...]
    softplus_neg = jnp.maximum(-lam, 0.0) + jnp.log1p(jnp.exp(-jnp.abs(lam)))
    log_a = (-LRU_C * softplus_neg) * r
    a = jnp.exp(log_a)
    b = jnp.sqrt(1.0 - a * a) * (ig * xc)
    rows = lax.broadcasted_iota(jnp.int32, a.shape, 0)
    step = 1
    while step < tm:
        a_sh = pltpu.roll(a, step, 0)
        b_sh = pltpu.roll(b, step, 0)
        valid = rows >= step
        b = jnp.where(valid, a * b_sh + b, b)
        a = jnp.where(valid, a * a_sh, a)
        step *= 2
    h0 = jnp.where(first, 0.0, carry_ref[0:1, :])
    h = a * h0 + b
    carry_ref[...] = jnp.broadcast_to(h[tm - 1:tm, :], carry_ref.shape)
    o_ref[...] = (h * jax.nn.gelu(y_ref[...])).astype(o_ref.dtype)


MXU_WIDTH = 256


def _block_diag_chunks(w):
    g, n, _ = w.shape
    per = MXU_WIDTH // n
    wc = w.reshape(g // per, per, n, n)
    eye = jnp.eye(per, dtype=w.dtype)
    return (eye[None, :, None, :, None] * wc[:, :, :, None, :]).reshape(g // per, per * n, per * n)


def _rg_lru(pc, conv_w, conv_b, w_r, b_r, w_i, b_i, lam, seq, tm=256):
    t, w2 = pc.shape
    w = w2 // 2
    row = lambda v: v.reshape(1, w)
    full2 = lambda shape: pl.BlockSpec(shape, lambda i: (0, 0))
    chunks = pl.BlockSpec((w // MXU_WIDTH, MXU_WIDTH, MXU_WIDTH), lambda i: (0, 0, 0))
    body = functools.partial(_lru_body, tm=tm, tiles_per_batch=seq // tm)
    return pl.pallas_call(
        body, grid=(t // tm,),
        in_specs=[
            pl.BlockSpec((tm, w), lambda i: (i, 0)),
            pl.BlockSpec((tm, w), lambda i: (i, 1)),
            pl.BlockSpec((SUBLANES, w), lambda i: (jnp.maximum(i * (tm // SUBLANES) - 1, 0), 0)),
            full2((CONV_WIDTH, w)), full2((1, w)), chunks, full2((1, w)), chunks,
            full2((1, w)), full2((1, w)),
        ],
        out_specs=pl.BlockSpec((tm, w), lambda i: (i, 0)),
        out_shape=jax.ShapeDtypeStruct((t, w), BF16),
        scratch_shapes=[pltpu.VMEM((SUBLANES, w), F32)],
        compiler_params=_params(1), name='rg_lru',
    )(pc, pc, pc, conv_w, row(conv_b), _block_diag_chunks(w_r).astype(BF16), row(b_r),
      _block_diag_chunks(w_i).astype(BF16), row(b_i), row(lam))


def _branch_merge_body(h_ref, oa_ref, ob_ref, oc_ref, wga_ref, wgb_ref, wgc_ref, wa_ref, wb_ref, wc_ref, o_ref):
    h = h_ref[...]
    mixed = (jax.nn.sigmoid(_dot(h, wga_ref[...])) * _dot(oa_ref[...], wa_ref[...])
             + jax.nn.sigmoid(_dot(h, wgb_ref[...])) * _dot(ob_ref[...], wb_ref[...])
             + jax.nn.sigmoid(_dot(h, wgc_ref[...])) * _dot(oc_ref[...], wc_ref[...]))
    o_ref[...] = mixed.astype(o_ref.dtype)


MOE_ROWS = 256
IDX_REC = 1024


def _router_body(a_ref, w_ref, b_ref, gsel_ref, eidx_ref, rank_ref, cnt_ref, carry_ref, *, n_experts, top_k, tm):
    i = pl.program_id(0)

    @pl.when(i == 0)
    def _():
        carry_ref[...] = jnp.zeros(carry_ref.shape, F32)

    logits = _dot(a_ref[...], w_ref[...])
    lane = lax.broadcasted_iota(jnp.int32, logits.shape, 1).astype(F32)
    scores = jax.nn.sigmoid(logits)
    cur = jnp.where(lane < n_experts, scores + b_ref[...], -jnp.inf)
    mask = jnp.zeros(logits.shape, F32)
    hits, picked_scores = [], []
    for r in range(top_k):
        m = jnp.max(cur, axis=1, keepdims=True)
        idx = jnp.min(jnp.where(cur == m, lane, float(LANES)), axis=1, keepdims=True)
        hit = lane == idx
        hits.append(hit)
        picked_scores.append(jnp.sum(jnp.where(hit, scores, 0.0), axis=1, keepdims=True))
        eidx_ref[:, r:r + 1] = idx.astype(jnp.int32)
        cur = jnp.where(hit, -jnp.inf, cur)
        mask = mask + jnp.where(hit, 1.0, 0.0)
    total = picked_scores[0]
    for r in range(1, top_k):
        total = total + picked_scores[r]
    ri = lax.broadcasted_iota(jnp.int32, (tm, tm), 0)
    ci = lax.broadcasted_iota(jnp.int32, (tm, tm), 1)
    tri = jnp.where(ri > ci, 1.0, 0.0).astype(BF16)
    before = _dot(tri, mask.astype(BF16)) + carry_ref[0:1, :]
    for r in range(top_k):
        gsel_ref[:, r:r + 1] = picked_scores[r] / total * ROUTED_SCALE
        rank_ref[:, r:r + 1] = jnp.sum(jnp.where(hits[r], before, 0.0), axis=1, keepdims=True).astype(jnp.int32)
    carry_ref[...] = carry_ref[...] + jnp.sum(mask, axis=0, keepdims=True)
    cnt_ref[...] = carry_ref[...]


def _router(h2, w_router, b_router, tm=1024):
    t, d = h2.shape
    n_exp = w_router.shape[1]
    w_r = jnp.zeros((d, LANES), BF16).at[:, :n_exp].set(w_router.astype(BF16))
    b_r = jnp.zeros((1, LANES), F32).at[0, :n_exp].set(b_router.astype(F32))
    col = lambda dt: jax.ShapeDtypeStruct((t, MOE_TOP_K), dt)
    cspec = pl.BlockSpec((tm, MOE_TOP_K), lambda i: (i, 0))
    return pl.pallas_call(
        functools.partial(_router_body, n_experts=n_exp, top_k=MOE_TOP_K, tm=tm), grid=(t // tm,),
        in_specs=[pl.BlockSpec((tm, d), lambda i: (i, 0)), pl.BlockSpec((d, LANES), lambda i: (0, 0)),
                  pl.BlockSpec((1, LANES), lambda i: (0, 0))],
        out_specs=[cspec, cspec, cspec, pl.BlockSpec((SUBLANES, LANES), lambda i: (0, 0))],
        out_shape=[col(F32), col(jnp.int32), col(jnp.int32), jax.ShapeDtypeStruct((SUBLANES, LANES), F32)],
        scratch_shapes=[pltpu.VMEM((SUBLANES, LANES), F32)],
        compiler_params=_params(1), name='router')(h2, w_r, b_r)


def _pack_bf16_pairs(lo, hi):
    lo_bits = pltpu.bitcast(lo.astype(BF16).astype(F32), jnp.uint32)
    hi_bits = pltpu.bitcast(hi.astype(BF16).astype(F32), jnp.uint32)
    return (hi_bits & jnp.uint32(0xFFFF0000)) | lax.shift_right_logical(lo_bits, jnp.uint32(16))


def _unpack_lo(u):
    return pltpu.bitcast(lax.shift_left(u, jnp.uint32(16)), F32)


def _unpack_hi(u):
    return pltpu.bitcast(u & jnp.uint32(0xFFFF0000), F32)


def _norm_mod_pack_body(x_ref, g_ref, sc_ref, sh_ref, o_ref, p_ref):
    x = x_ref[...]
    y = x * lax.rsqrt(jnp.mean(x * x, axis=-1, keepdims=True) + RMS_EPS) * g_ref[...]
    y = y * (1.0 + sc_ref[...]) + sh_ref[...]
    half = y.shape[1] // 2
    o_ref[...] = y.astype(o_ref.dtype)
    p_ref[...] = _pack_bf16_pairs(y[:, :half], y[:, half:])


def _norm_mod_pack(x2, g, sc, sh, seq, tm=256):
    t, d = x2.shape
    bsz = t // seq
    row = pl.BlockSpec((tm, d), lambda i: (i, 0))
    vec = pl.BlockSpec((1, d), lambda i: (0, 0))
    bvec = pl.BlockSpec((None, 1, d), lambda i: (i // (seq // tm), 0, 0))
    return pl.pallas_call(
        _norm_mod_pack_body, grid=(t // tm,), in_specs=[row, vec, bvec, bvec],
        out_specs=[row, pl.BlockSpec((tm, d // 2), lambda i: (i, 0))],
        out_shape=[jax.ShapeDtypeStruct((t, d), BF16), jax.ShapeDtypeStruct((t, d // 2), jnp.uint32)],
        compiler_params=_params(1), name='norm_mod_pack',
    )(x2, g.reshape(1, d), sc.reshape(bsz, 1, d), sh.reshape(bsz, 1, d))


def _moe_group_body(te_ref, nu_ref, idx_hbm, h_hbm, w1_ref, w3_ref, w2_ref, o_ref,
                    idx_smem, xbuf, isem, gsem, *, tmr, n_tiles):
    i = pl.program_id(0)
    n = nu_ref[0]
    dh = xbuf.shape[-1]
    n_groups = tmr // SUBLANES
    n_up = n_groups // 2
    n_down = n_up // 2
    kc = 2 * dh // n_up

    def idx_copy(tile, rec):
        tile = jnp.minimum(tile, n_tiles - 1)
        return pltpu.make_async_copy(idx_hbm.at[pl.ds(pl.multiple_of(tile * IDX_REC, IDX_REC), IDX_REC)],
                                     idx_smem.at[pl.ds(pl.multiple_of(rec * IDX_REC, IDX_REC), IDX_REC)],
                                     isem.at[rec])

    def row_in(tok, g, u, s):
        return pltpu.make_async_copy(h_hbm.at[lax.shift_right_logical(tok, 3), pl.ds(tok & (SUBLANES - 1), 1)],
                                     xbuf.at[s, g, pl.ds(u, 1)], gsem.at[s])

    def gather_group(g, rec, s):
        for u in range(SUBLANES):
            row_in(idx_smem[rec * IDX_REC + g * SUBLANES + u], g, u, s).start(priority=u % 2)

    def per_group(fn):
        def body(g, c):
            fn(g)
            return c
        lax.fori_loop(0, n_groups, body, 0)

    def gather_wait(s):
        per_group(lambda g: [row_in(jnp.int32(0), g, u, s).wait() for u in range(SUBLANES)])

    @pl.when(i == 0)
    def _prologue():
        for k in range(2):
            idx_copy(k, k).start()
            idx_copy(k, k).wait()
        idx_copy(2, 2).start()
        per_group(lambda g: gather_group(g, 0, 0))
        per_group(lambda g: gather_group(g, 1, 1))

    @pl.when(i < n)
    def _tile():
        s, s_in, r_in = i % 3, (i + 2) % 3, (i + 2) % 3
        idx_copy(i + 2, r_in).wait()
        gather_wait(s)
        h1 = jnp.zeros((tmr, w1_ref.shape[-1]), F32)
        h3 = jnp.zeros((tmr, w1_ref.shape[-1]), F32)
        for c in range(n_up):
            gather_group(2 * c, r_in, s_in)
            gather_group(2 * c + 1, r_in, s_in)
            pc = c % (n_up // 2)
            u = xbuf[s, :, :, pc * kc:(pc + 1) * kc].reshape(tmr, kc)
            xc = (_unpack_lo(u) if c < n_up // 2 else _unpack_hi(u)).astype(BF16)
            h1 = h1 + _dot(xc, w1_ref[c * kc:(c + 1) * kc, :].astype(BF16))
            h3 = h3 + _dot(xc, w3_ref[c * kc:(c + 1) * kc, :].astype(BF16))
        hid = ((h1 * jax.nn.sigmoid(h1)) * h3).astype(BF16)
        for c in range(n_down):
            y_lo = _dot(hid, w2_ref[:, c * kc:(c + 1) * kc].astype(BF16))
            y_hi = _dot(hid, w2_ref[:, dh + c * kc:dh + (c + 1) * kc].astype(BF16))
            o_ref[:, c * kc:(c + 1) * kc] = _pack_bf16_pairs(y_lo, y_hi)
        idx_copy(i + 3, i % 3).start()

    @pl.when(i == n)
    def _drain():
        gather_wait(i % 3)
        gather_wait((i + 1) % 3)
        idx_copy(i + 2, (i + 2) % 3).wait()


def _moe_grouped(h2p, idx, tile_expert, n_used, w1, w3, w2, layer):
    t, dh = h2p.shape
    d = 2 * dh
    n_tiles = idx.shape[0] // IDX_REC
    tmr = MOE_ROWS
    f = w1.shape[3]
    assert n_tiles >= 4
    wmap = lambda i, te, nu: (layer, te[i], 0, 0)
    grid_spec = pltpu.PrefetchScalarGridSpec(
        num_scalar_prefetch=2, grid=(n_tiles + 1,),
        in_specs=[
            pl.BlockSpec(memory_space=pl.ANY), pl.BlockSpec(memory_space=pl.ANY),
            pl.BlockSpec((None, None, d, f), wmap), pl.BlockSpec((None, None, d, f), wmap),
            pl.BlockSpec((None, None, f, d), wmap),
        ],
        out_specs=pl.BlockSpec((tmr, dh), lambda i, te, nu: (jnp.minimum(i, nu[0] - 1), 0)),
        scratch_shapes=[
            pltpu.SMEM((3 * IDX_REC,), jnp.int32),
            pltpu.VMEM((3, tmr // SUBLANES, SUBLANES, dh), jnp.uint32),
            pltpu.SemaphoreType.DMA((3,)), pltpu.SemaphoreType.DMA((3,)),
        ])
    return pl.pallas_call(
        functools.partial(_moe_group_body, tmr=tmr, n_tiles=n_tiles), grid_spec=grid_spec,
        out_shape=jax.ShapeDtypeStruct((n_tiles * tmr, dh), jnp.uint32),
        compiler_params=_params(1), name='moe_grouped',
    )(tile_expert, n_used, idx, h2p.reshape(t // SUBLANES, SUBLANES, dh), w1, w3, w2)


def _moe_plan(eidx, rank, counts, n_experts, tm_combine):
    t, k = eidx.shape
    tmr = MOE_ROWS
    n_tiles = (t * k) // tmr + n_experts
    padded = ((counts + tmr - 1) // tmr) * tmr
    ends = jnp.cumsum(padded)
    offsets = ends - padded
    n_used = (ends[-1] // tmr).astype(jnp.int32)
    pos = offsets[eidx] + rank
    tok = jnp.broadcast_to(jnp.arange(t, dtype=jnp.int32)[:, None], (t, k))
    j = jnp.arange(tmr, dtype=jnp.int32)[None, :]
    pad_pos = jnp.where(j < (padded - counts)[:, None], (offsets + counts)[:, None] + j,
                        n_tiles * tmr + jnp.arange(n_experts, dtype=jnp.int32)[:, None] * tmr + j)
    keys = jnp.concatenate([pos.reshape(-1), pad_pos.reshape(-1)])
    vals = jnp.concatenate([tok.reshape(-1), jnp.broadcast_to(j, (n_experts, tmr)).reshape(-1)])
    tok_sorted = lax.sort((keys, vals), num_keys=1)[1].reshape(n_tiles, tmr)
    idx = jnp.concatenate([tok_sorted, jnp.zeros((n_tiles, IDX_REC - tmr), jnp.int32)], axis=1).reshape(-1)
    tile_start = jnp.minimum(jnp.arange(n_tiles + 1, dtype=jnp.int32), n_used - 1) * tmr
    tile_expert = jnp.sum((ends[None, :] <= tile_start[:, None]).astype(jnp.int32), axis=1)
    assert tm_combine * k == IDX_REC
    pos_blocks = pos.reshape(t // tm_combine, tm_combine, k).transpose(0, 2, 1).reshape(-1)
    return idx, jnp.minimum(tile_expert, n_experts - 1), n_used.reshape(1), pos_blocks


def _moe_combine_body(pos_hbm, y_hbm, x_ref, g_ref, gsel_ref, h_ref, w1_ref, w3_ref, w2_ref, o_ref,
                      pos_smem, ybuf, isem, gsem, *, tm, n_slots):
    i = pl.program_id(0)
    n_steps = pl.num_programs(0)
    n_groups = tm // SUBLANES
    half = x_ref.shape[1] // 2

    def pos_copy(step, rec):
        step = jnp.minimum(step, n_steps - 1)
        return pltpu.make_async_copy(pos_hbm.at[pl.ds(pl.multiple_of(step * IDX_REC, IDX_REC), IDX_REC)],
                                     pos_smem.at[pl.ds(pl.multiple_of(rec * IDX_REC, IDX_REC), IDX_REC)],
                                     isem.at[rec])

    def row_in(row, r, g, u, s):
        return pltpu.make_async_copy(y_hbm.at[lax.shift_right_logical(row, 3), pl.ds(row & (SUBLANES - 1), 1)],
                                     ybuf.at[s, r, g, pl.ds(u, 1)], gsem.at[s])

    def per_group(fn):
        def body(g, c):
            fn(g)
            return c
        lax.fori_loop(0, n_groups, body, 0)

    def gather_group(g, rec, s):
        for r in range(n_slots):
            for u in range(SUBLANES):
                row_in(pos_smem[rec * IDX_REC + r * tm + g * SUBLANES + u], r, g, u, s).start(priority=u % 2)

    def gather_wait(s):
        def group(g):
            for r in range(n_slots):
                for u in range(SUBLANES):
                    row_in(jnp.int32(0), r, g, u, s).wait()
        per_group(group)

    @pl.when(i == 0)
    def _prologue():
        pos_copy(0, 0).start()
        pos_copy(0, 0).wait()
        per_group(lambda g: gather_group(g, 0, 0))
        pos_copy(1, 1).start()

    nxt, rec_next = (i + 1) % 2, (i + 1) % 2
    pos_copy(i + 1, rec_next).wait()
    groups = iter(range(n_groups))

    def issue(count):
        for _ in range(count):
            g = next(groups, None)
            if g is not None:
                gather_group(g, rec_next, nxt)

    issue(2)
    a = h_ref[...]
    h1 = _dot(a, w1_ref[...])
    issue(2)
    hid = ((h1 * jax.nn.sigmoid(h1)) * _dot(a, w3_ref[...])).astype(BF16)
    issue(2)
    acc_lo = _dot(hid, w2_ref[:, :half])
    issue(2)
    acc_hi = _dot(hid, w2_ref[:, half:])
    gather_wait(i % 2)
    pos_copy(i + 2, i % 2).start()
    gs = gsel_ref[...]
    for r in range(n_slots):
        issue(1)
        u = ybuf[i % 2, r].reshape(tm, half)
        acc_lo = acc_lo + gs[:, r:r + 1] * _unpack_lo(u)
        acc_hi = acc_hi + gs[:, r:r + 1] * _unpack_hi(u)
    issue(n_groups)
    o_ref[:, :half] = x_ref[:, :half] + g_ref[:, :half] * acc_lo
    o_ref[:, half:] = x_ref[:, half:] + g_ref[:, half:] * acc_hi

    @pl.when(i == n_steps - 1)
    def _drain():
        gather_wait(nxt)
        pos_copy(i + 2, i % 2).wait()


def _moe_combine(x2, g2, gsel, h2, ws1, ws3, ws2, y, pos_blocks, seq, tm):
    t, d = x2.shape
    bsz = t // seq
    k = gsel.shape[1]
    f = ws1.shape[1]
    dh = d // 2
    row = pl.BlockSpec((tm, d), lambda i: (i, 0))
    full = lambda shape: pl.BlockSpec(shape, lambda i: (0, 0))
    anyspace = pl.BlockSpec(memory_space=pl.ANY)
    return pl.pallas_call(
        functools.partial(_moe_combine_body, tm=tm, n_slots=k), grid=(t // tm,),
        in_specs=[anyspace, anyspace, row, pl.BlockSpec((None, 1, d), lambda i: (i // (seq // tm), 0, 0)),
                  pl.BlockSpec((tm, k), lambda i: (i, 0)), row, full((d, f)), full((d, f)), full((f, d))],
        out_specs=row, out_shape=jax.ShapeDtypeStruct((t, d), F32),
        scratch_shapes=[
            pltpu.SMEM((2 * IDX_REC,), jnp.int32),
            pltpu.VMEM((2, k, tm // SUBLANES, SUBLANES, dh), jnp.uint32),
            pltpu.SemaphoreType.DMA((2,)), pltpu.SemaphoreType.DMA((2,)),
        ],
        compiler_params=_params(1), name='moe_combine',
    )(pos_blocks, y.reshape(y.shape[0] // SUBLANES, SUBLANES, dh), x2, g2.reshape(bsz, 1, d), gsel, h2,
      ws1.astype(BF16), ws3.astype(BF16), ws2.astype(BF16))


def kernel(x, c, w_ada, b_ada, ada_layer, norm_mix, norm_ffn, w_in, q_norm_a, k_norm_a, q_norm_b,
           k_norm_b, conv_w, conv_b, w_rgate, b_rgate, w_igate, b_igate, lru_lambda, w_branch_a,
           w_branch_b, w_branch_c, w_out, w_router, b_router, w1_exp, w3_exp, w2_exp, w1_shared,
           w3_shared, w2_shared):
    bsz, seq, d = x.shape
    t = bsz * seq
    depth = w_in.shape[0]
    mixw = d // 4
    a_cols = 3 * mixw + IDX_HEADS * IDX_DIM + IDX_DIM + IDX_HEADS
    off_b = a_cols
    off_c = off_b + N_DIL * 3 * mixw
    off_g = off_c + 2 * mixw
    top_k = min(DSA_TOPK_MAX, seq // 4)
    tm = 1024 if t % 1024 == 0 else 256
    tpb = seq // tm

    mod_shared = _ada_mod(c, w_ada, b_ada).reshape(bsz, N_MOD, d)
    x2 = x.reshape(t, d)
    for l in range(depth):
        mod = mod_shared + ada_layer[l]
        sh1, sc1, g1, sh2, sc2, g2 = [mod[:, j] for j in range(N_MOD)]
        h = _norm_mod(x2, norm_mix[l], sc1, sh1, seq)
        wl = w_in[l]

        n_main = 3 * mixw + IDX_HEADS * IDX_DIM
        pqk = _proj_qk_norm(h, wl[:, :2 * mixw].astype(BF16), q_norm_a[l], k_norm_a[l], tm)
        pvi = _matmul(h, wl[:, 2 * mixw:n_main].astype(BF16), BF16, tm=tm, tn=512, name='proj_a')
        w_small = jnp.zeros((d, LANES), BF16).at[:, :IDX_DIM + IDX_HEADS].set(
            wl[:, n_main:a_cols].astype(BF16))
        small = _matmul(h, w_small, F32, tm=tm, tn=LANES, name='proj_a_idx')
        ik = small[:, :IDX_DIM].astype(BF16)
        zeros = jnp.zeros_like(ik)
        ik2 = jnp.concatenate([ik, zeros, zeros, ik], axis=1)
        iwt = small[:, IDX_DIM:IDX_DIM + IDX_HEADS].reshape(bsz, seq, IDX_HEADS).transpose(0, 2, 1)
        vt = pvi[:, :mixw].reshape(bsz, seq, mixw // HEAD_DIM, HEAD_DIM).transpose(0, 2, 3, 1)
        o_a = _dsa_attention(pqk, pvi, ik2, iwt, vt, seq, top_k)

        outs, lses = [], []
        for g, (window, dil) in enumerate(DIL_PATTERNS):
            w_g = wl[:, off_b + g * 3 * mixw:off_b + (g + 1) * 3 * mixw].astype(BF16)
            qkv = _to_classes(_matmul(h, w_g, BF16, tm=tm, tn=512, name='proj_b'), bsz, seq, dil)
            og, lg = _dilated_group(qkv, q_norm_b[l, g], k_norm_b[l, g], bsz * dil, seq // dil,
                                    band=window // dil)
            outs.append(_from_classes(og, bsz, seq, dil))
            lses.append(_from_classes(lg, bsz, seq, dil))
        o_b = _dil_merge(outs, lses)

        pc = _matmul(h, wl[:, off_c:off_g].astype(BF16), F32, tm=tm, tn=512, name='proj_c')
        o_c = _rg_lru(pc, conv_w[l], conv_b[l], w_rgate[l], b_rgate[l], w_igate[l], b_igate[l],
                      lru_lambda[l], seq)

        w_gates = wl[:, off_g:].astype(BF16)
        tn_merge = 256
        nd = d // tn_merge
        mixed = _tiled_call(
            _branch_merge_body, t, d, tm, tn_merge,
            [(h, 'row'), (o_a, 'row'), (o_b, 'row'), (o_c, 'row'),
             (w_gates, ('coloff', 0)), (w_gates, ('coloff', nd)), (w_gates, ('coloff', 2 * nd)),
             (w_branch_a[l].astype(BF16), 'col'), (w_branch_b[l].astype(BF16), 'col'),
             (w_branch_c[l].astype(BF16), 'col')],
            [(d, BF16)], name='branch_merge')
        x2 = _tiled_call(
            _mm_residual_body, t, d, tm, 512,
            [(mixed, 'row'), (w_out[l].astype(BF16), 'col'), (x2, 'tile'), (g1.reshape(bsz, 1, d), 'bvec')],
            [(d, F32)], tiles_per_batch=tpb, name='out_proj')

        h2, h2p = _norm_mod_pack(x2, norm_ffn[l], sc2, sh2, seq)
        gsel, eidx, rank, cnt = _router(h2, w_router[l], b_router[l])
        tm_combine = IDX_REC // MOE_TOP_K
        idx, tile_expert, n_used, pos_blocks = _moe_plan(eidx, rank, cnt[0, :N_EXPERTS].astype(jnp.int32),
                                                        N_EXPERTS, tm_combine)
        y = _moe_grouped(h2p, idx, tile_expert, n_used, w1_exp, w3_exp, w2_exp, l)
        x2 = _moe_combine(x2, g2, gsel, h2, w1_shared[l], w3_shared[l], w2_shared[l], y, pos_blocks, seq,
                          tm_combine)
    return x2.reshape(bsz, seq, d)
```
